```python
import jax, jax.numpy as jnp
from jax import lax
import numpy as np

D_MODEL = 1024
BATCH = 8
SEQ = 16384
DEPTH = 4

N_EVEN = (DEPTH + 1) // 2
N_ODD = DEPTH // 2
D_FF = 2816

POOL_WINDOWS = (2, 4, 8, 16)
POOL_GROUPS = len(POOL_WINDOWS)
POOL_DIM = D_MODEL // 2
POOL_GROUP_DIM = POOL_DIM // POOL_GROUPS

MLA_HEADS = 8
QK_NOPE_DIM = D_MODEL // 16
QK_ROPE_DIM = D_MODEL // 32
QK_HEAD_DIM = QK_NOPE_DIM + QK_ROPE_DIM
V_HEAD_DIM = D_MODEL // 16
Q_LORA_RANK = 3 * D_MODEL // 8
KV_LORA_RANK = D_MODEL // 4
ROPE_THETA = 10000.0
Q_BLOCK = 128

MIX_IN_EVEN = POOL_DIM + Q_LORA_RANK + KV_LORA_RANK + QK_ROPE_DIM
MIX_OUT_EVEN = POOL_DIM + MLA_HEADS * V_HEAD_DIM

CONV_DIM = D_MODEL
CONV_WIDTH = 3

NORM_EPS = 1e-6

kernel_name = "hybrid_pool_mla_shortconv_macaron"


def rms_norm(x, g):
    xf = x.astype(jnp.float32)
    y = xf * lax.rsqrt(jnp.mean(xf * xf, axis=-1, keepdims=True) + NORM_EPS)
    return (y * g.astype(jnp.float32)).astype(x.dtype)


def swiglu(x, w_gate, w_up, w_down):
    return (jax.nn.silu(x @ w_gate) * (x @ w_up)) @ w_down


def rotary(x, cos, sin):
    x1, x2 = jnp.split(x, 2, axis=-1)
    c = cos[None, :, None, :]
    s = sin[None, :, None, :]
    return jnp.concatenate([x1 * c - x2 * s, x2 * c + x1 * s], axis=-1)


def causal_multiscale_pool(u):
    s = u.shape[1]
    uf = u.astype(jnp.float32)
    cs = jnp.pad(jnp.cumsum(uf, axis=1), ((0, 0), (1, 0), (0, 0), (0, 0)))
    t = jnp.arange(1, s + 1)
    means = []
    for g, w in enumerate(POOL_WINDOWS):
        lo = jnp.maximum(t - w, 0)
        wsum = cs[:, 1:, g] - cs[:, lo, g]
        cnt = jnp.minimum(t, w).astype(jnp.float32)[None, :, None]
        means.append(wsum / cnt)
    mean = jnp.stack(means, axis=2)
    return (mean - uf).astype(u.dtype)


def causal_attention(q, k, v):
    b, s, h, dqk = q.shape
    dv = v.shape[-1]
    scale = dqk ** -0.5
    kpos = jnp.arange(s)

    def block(i):
        start = i * Q_BLOCK
        qb = lax.dynamic_slice_in_dim(q, start, Q_BLOCK, axis=1)
        sc = jnp.einsum('bqhd,bkhd->bhqk', qb, k,
                        preferred_element_type=jnp.float32) * scale
        qpos = start + jnp.arange(Q_BLOCK)
        sc = jnp.where(kpos[None, :] <= qpos[:, None], sc, -jnp.inf)
        p = jax.nn.softmax(sc, axis=-1)
        return jnp.einsum('bhqk,bkhd->bqhd', p.astype(v.dtype), v)

    out = lax.map(block, jnp.arange(s // Q_BLOCK))
    return jnp.moveaxis(out, 0, 1).reshape(b, s, h, dv)


def pool_mla_mixer(hn, cos, sin, w_in, q_a_norm, w_q_up, kv_a_norm, w_kv_up,
                   q_head_norm, k_head_norm, w_pool, pool_scale, w_out):
    b, s, _ = hn.shape
    z = hn @ w_in
    c1 = POOL_DIM
    c2 = c1 + Q_LORA_RANK
    c3 = c2 + KV_LORA_RANK
    u, q_lat, kv_lat, k_rope = jnp.split(z, [c1, c2, c3], axis=-1)

    u = u.reshape(b, s, POOL_GROUPS, POOL_GROUP_DIM)
    pooled = causal_multiscale_pool(u)
    pool_out = jnp.einsum('bsgc,gcd->bsgd', pooled, w_pool).reshape(b, s, POOL_DIM) * pool_scale

    q = (rms_norm(q_lat, q_a_norm) @ w_q_up).reshape(b, s, MLA_HEADS, QK_HEAD_DIM)
    kv = (rms_norm(kv_lat, kv_a_norm) @ w_kv_up).reshape(b, s, MLA_HEADS, QK_NOPE_DIM + V_HEAD_DIM)
    k_nope, v = jnp.split(kv, [QK_NOPE_DIM], axis=-1)
    k_rope_h = jnp.broadcast_to(k_rope[:, :, None, :], (b, s, MLA_HEADS, QK_ROPE_DIM))
    k = jnp.concatenate([k_nope, k_rope_h], axis=-1)
    q = rms_norm(q, q_head_norm)
    k = rms_norm(k, k_head_norm)
    q = jnp.concatenate([q[..., :QK_NOPE_DIM], rotary(q[..., QK_NOPE_DIM:], cos, sin)], axis=-1)
    k = jnp.concatenate([k[..., :QK_NOPE_DIM], rotary(k[..., QK_NOPE_DIM:], cos, sin)], axis=-1)
    attn = causal_attention(q, k, v).reshape(b, s, MLA_HEADS * V_HEAD_DIM)

    return jnp.concatenate([pool_out, attn], axis=-1) @ w_out


def gated_conv_mixer(hn, w_in, conv_w, w_out):
    gb, gc, hh = jnp.split(hn @ w_in, 3, axis=-1)
    u = gc * hh
    s = u.shape[1]
    up = jnp.pad(u, ((0, 0), (CONV_WIDTH - 1, 0), (0, 0)))
    y = conv_w[0] * up[:, 0:s]
    for j in range(1, CONV_WIDTH):
        y = y + conv_w[j] * up[:, j:j + s]
    return (gb * y) @ w_out


def _fwd_setup_inputs(seed: int = 0) -> dict:
    key = jax.random.key(seed)
    ks = iter(jax.random.split(key, 32))

    def dense(shape, fan_in):
        return jax.random.normal(next(ks), shape, jnp.float32) * (fan_in ** -0.5)

    def gain(shape):
        return 1.0 + 0.05 * jax.random.normal(next(ks), shape, jnp.float32)

    return {
        "x": jax.random.normal(next(ks), (BATCH, SEQ, D_MODEL), jnp.float32),
        "ffn1_norm": gain((DEPTH, D_MODEL)),
        "ffn1_w_gate": dense((DEPTH, D_MODEL, D_FF), D_MODEL),
        "ffn1_w_up": dense((DEPTH, D_MODEL, D_FF), D_MODEL),
        "ffn1_w_down": dense((DEPTH, D_FF, D_MODEL), D_FF),
        "mix_norm": gain((DEPTH, D_MODEL)),
        "ffn2_norm": gain((DEPTH, D_MODEL)),
        "ffn2_w_gate": dense((DEPTH, D_MODEL, D_FF), D_MODEL),
        "ffn2_w_up": dense((DEPTH, D_MODEL, D_FF), D_MODEL),
        "ffn2_w_down": dense((DEPTH, D_FF, D_MODEL), D_FF),
        "a_w_in": dense((N_EVEN, D_MODEL, MIX_IN_EVEN), D_MODEL),
        "a_q_a_norm": gain((N_EVEN, Q_LORA_RANK)),
        "a_w_q_up": dense((N_EVEN, Q_LORA_RANK, MLA_HEADS * QK_HEAD_DIM), Q_LORA_RANK),
        "a_kv_a_norm": gain((N_EVEN, KV_LORA_RANK)),
        "a_w_kv_up": dense((N_EVEN, KV_LORA_RANK, MLA_HEADS * (QK_NOPE_DIM + V_HEAD_DIM)), KV_LORA_RANK),
        "a_q_head_norm": gain((N_EVEN, QK_HEAD_DIM)),
        "a_k_head_norm": gain((N_EVEN, QK_HEAD_DIM)),
        "a_w_pool": dense((N_EVEN, POOL_GROUPS, POOL_GROUP_DIM, POOL_GROUP_DIM), POOL_GROUP_DIM),
        "a_pool_scale": gain((N_EVEN, POOL_DIM)),
        "a_w_out": dense((N_EVEN, MIX_OUT_EVEN, D_MODEL), MIX_OUT_EVEN),
        "c_w_in": dense((N_ODD, D_MODEL, 3 * CONV_DIM), D_MODEL),
        "c_conv_w": dense((N_ODD, CONV_WIDTH, CONV_DIM), CONV_WIDTH),
        "c_w_out": dense((N_ODD, CONV_DIM, D_MODEL), CONV_DIM),
    }


def _fwd_reference(x, ffn1_norm, ffn1_w_gate, ffn1_w_up, ffn1_w_down, mix_norm,
              ffn2_norm, ffn2_w_gate, ffn2_w_up, ffn2_w_down,
              a_w_in, a_q_a_norm, a_w_q_up, a_kv_a_norm, a_w_kv_up,
              a_q_head_norm, a_k_head_norm, a_w_pool, a_pool_scale, a_w_out,
              c_w_in, c_conv_w, c_w_out):
    s = x.shape[1]
    pos = jnp.arange(s, dtype=jnp.float32)
    inv_freq = ROPE_THETA ** (-jnp.arange(0, QK_ROPE_DIM, 2, dtype=jnp.float32) / QK_ROPE_DIM)
    ang = pos[:, None] * inv_freq[None, :]
    cos = jnp.cos(ang).astype(x.dtype)
    sin = jnp.sin(ang).astype(x.dtype)

    for layer in range(DEPTH):
        x = x + 0.5 * swiglu(rms_norm(x, ffn1_norm[layer]),
                             ffn1_w_gate[layer], ffn1_w_up[layer], ffn1_w_down[layer])
        hn = rms_norm(x, mix_norm[layer])
        i = layer // 2
        if layer % 2 == 0:
            x = x + pool_mla_mixer(hn, cos, sin, a_w_in[i], a_q_a_norm[i], a_w_q_up[i],
                                   a_kv_a_norm[i], a_w_kv_up[i], a_q_head_norm[i],
                                   a_k_head_norm[i], a_w_pool[i], a_pool_scale[i], a_w_out[i])
        else:
            x = x + gated_conv_mixer(hn, c_w_in[i], c_conv_w[i], c_w_out[i])
        x = x + 0.5 * swiglu(rms_norm(x, ffn2_norm[layer]),
                             ffn2_w_gate[layer], ffn2_w_up[layer], ffn2_w_down[layer])
    return x


import jax as _jax
import jax.numpy as _jnp

TWIN_FORMAT = 'train_step'
FWD_PARAMS = ['x', 'ffn1_norm', 'ffn1_w_gate', 'ffn1_w_up', 'ffn1_w_down', 'mix_norm', 'ffn2_norm', 'ffn2_w_gate', 'ffn2_w_up', 'ffn2_w_down', 'a_w_in', 'a_q_a_norm', 'a_w_q_up', 'a_kv_a_norm', 'a_w_kv_up', 'a_q_head_norm', 'a_k_head_norm', 'a_w_pool', 'a_pool_scale', 'a_w_out', 'c_w_in', 'c_conv_w', 'c_w_out']
TWIN_WEIGHTS = ['ffn1_norm', 'ffn1_w_gate', 'ffn1_w_up', 'ffn1_w_down', 'mix_norm', 'ffn2_norm', 'ffn2_w_gate', 'ffn2_w_up', 'ffn2_w_down', 'a_w_in', 'a_q_a_norm', 'a_w_q_up', 'a_kv_a_norm', 'a_w_kv_up', 'a_q_head_norm', 'a_k_head_norm', 'a_w_pool', 'a_pool_scale', 'a_w_out', 'c_w_in', 'c_conv_w', 'c_w_out']
TWIN_DIFF_INPUT = 'x'
TWIN_INPUTS = ['x', 'ffn1_norm', 'ffn1_w_gate', 'ffn1_w_up', 'ffn1_w_down', 'mix_norm', 'ffn2_norm', 'ffn2_w_gate', 'ffn2_w_up', 'ffn2_w_down', 'a_w_in', 'a_q_a_norm', 'a_w_q_up', 'a_kv_a_norm', 'a_w_kv_up', 'a_q_head_norm', 'a_k_head_norm', 'a_w_pool', 'a_pool_scale', 'a_w_out', 'c_w_in', 'c_conv_w', 'c_w_out', 'loss_target', 'm_ffn1_norm', 'm_ffn1_w_gate', 'm_ffn1_w_up', 'm_ffn1_w_down', 'm_mix_norm', 'm_ffn2_norm', 'm_ffn2_w_gate', 'm_ffn2_w_up', 'm_ffn2_w_down', 'm_a_w_in', 'm_a_q_a_norm', 'm_a_w_q_up', 'm_a_kv_a_norm', 'm_a_w_kv_up', 'm_a_q_head_norm', 'm_a_k_head_norm', 'm_a_w_pool', 'm_a_pool_scale', 'm_a_w_out', 'm_c_w_in', 'm_c_conv_w', 'm_c_w_out', 'v_ffn1_norm', 'v_ffn1_w_gate', 'v_ffn1_w_up', 'v_ffn1_w_down', 'v_mix_norm', 'v_ffn2_norm', 'v_ffn2_w_gate', 'v_ffn2_w_up', 'v_ffn2_w_down', 'v_a_w_in', 'v_a_q_a_norm', 'v_a_w_q_up', 'v_a_kv_a_norm', 'v_a_w_kv_up', 'v_a_q_head_norm', 'v_a_k_head_norm', 'v_a_w_pool', 'v_a_pool_scale', 'v_a_w_out', 'v_c_w_in', 'v_c_conv_w', 'v_c_w_out']
TWIN_OUTPUTS = ['loss', 'grad_x', 'grad_ffn1_norm', 'grad_ffn1_w_gate', 'grad_ffn1_w_up', 'grad_ffn1_w_down', 'grad_mix_norm', 'grad_ffn2_norm', 'grad_ffn2_w_gate', 'grad_ffn2_w_up', 'grad_ffn2_w_down', 'grad_a_w_in', 'grad_a_q_a_norm', 'grad_a_w_q_up', 'grad_a_kv_a_norm', 'grad_a_w_kv_up', 'grad_a_q_head_norm', 'grad_a_k_head_norm', 'grad_a_w_pool', 'grad_a_pool_scale', 'grad_a_w_out', 'grad_c_w_in', 'grad_c_conv_w', 'grad_c_w_out', 'delta_ffn1_norm', 'delta_ffn1_w_gate', 'delta_ffn1_w_up', 'delta_ffn1_w_down', 'delta_mix_norm', 'delta_ffn2_norm', 'delta_ffn2_w_gate', 'delta_ffn2_w_up', 'delta_ffn2_w_down', 'delta_a_w_in', 'delta_a_q_a_norm', 'delta_a_w_q_up', 'delta_a_kv_a_norm', 'delta_a_w_kv_up', 'delta_a_q_head_norm', 'delta_a_k_head_norm', 'delta_a_w_pool', 'delta_a_pool_scale', 'delta_a_w_out', 'delta_c_w_in', 'delta_c_conv_w', 'delta_c_w_out', 'new_m_ffn1_norm', 'new_m_ffn1_w_gate', 'new_m_ffn1_w_up', 'new_m_ffn1_w_down', 'new_m_mix_norm', 'new_m_ffn2_norm', 'new_m_ffn2_w_gate', 'new_m_ffn2_w_up', 'new_m_ffn2_w_down', 'new_m_a_w_in', 'new_m_a_q_a_norm', 'new_m_a_w_q_up', 'new_m_a_kv_a_norm', 'new_m_a_w_kv_up', 'new_m_a_q_head_norm', 'new_m_a_k_head_norm', 'new_m_a_w_pool', 'new_m_a_pool_scale', 'new_m_a_w_out', 'new_m_c_w_in', 'new_m_c_conv_w', 'new_m_c_w_out', 'new_v_ffn1_norm', 'new_v_ffn1_w_gate', 'new_v_ffn1_w_up', 'new_v_ffn1_w_down', 'new_v_mix_norm', 'new_v_ffn2_norm', 'new_v_ffn2_w_gate', 'new_v_ffn2_w_up', 'new_v_ffn2_w_down', 'new_v_a_w_in', 'new_v_a_q_a_norm', 'new_v_a_w_q_up', 'new_v_a_kv_a_norm', 'new_v_a_w_kv_up', 'new_v_a_q_head_norm', 'new_v_a_k_head_norm', 'new_v_a_w_pool', 'new_v_a_pool_scale', 'new_v_a_w_out', 'new_v_c_w_in', 'new_v_c_conv_w', 'new_v_c_w_out']
TWIN_LEAF_KINDS = {'loss': 'loss', 'grad_x': 'grad_x', 'grad_ffn1_norm': 'grad_w', 'grad_ffn1_w_gate': 'grad_w', 'grad_ffn1_w_up': 'grad_w', 'grad_ffn1_w_down': 'grad_w', 'grad_mix_norm': 'grad_w', 'grad_ffn2_norm': 'grad_w', 'grad_ffn2_w_gate': 'grad_w', 'grad_ffn2_w_up': 'grad_w', 'grad_ffn2_w_down': 'grad_w', 'grad_a_w_in': 'grad_w', 'grad_a_q_a_norm': 'grad_w', 'grad_a_w_q_up': 'grad_w', 'grad_a_kv_a_norm': 'grad_w', 'grad_a_w_kv_up': 'grad_w', 'grad_a_q_head_norm': 'grad_w', 'grad_a_k_head_norm': 'grad_w', 'grad_a_w_pool': 'grad_w', 'grad_a_pool_scale': 'grad_w', 'grad_a_w_out': 'grad_w', 'grad_c_w_in': 'grad_w', 'grad_c_conv_w': 'grad_w', 'grad_c_w_out': 'grad_w', 'delta_ffn1_norm': 'delta_w', 'delta_ffn1_w_gate': 'delta_w', 'delta_ffn1_w_up': 'delta_w', 'delta_ffn1_w_down': 'delta_w', 'delta_mix_norm': 'delta_w', 'delta_ffn2_norm': 'delta_w', 'delta_ffn2_w_gate': 'delta_w', 'delta_ffn2_w_up': 'delta_w', 'delta_ffn2_w_down': 'delta_w', 'delta_a_w_in': 'delta_w', 'delta_a_q_a_norm': 'delta_w', 'delta_a_w_q_up': 'delta_w', 'delta_a_kv_a_norm': 'delta_w', 'delta_a_w_kv_up': 'delta_w', 'delta_a_q_head_norm': 'delta_w', 'delta_a_k_head_norm': 'delta_w', 'delta_a_w_pool': 'delta_w', 'delta_a_pool_scale': 'delta_w', 'delta_a_w_out': 'delta_w', 'delta_c_w_in': 'delta_w', 'delta_c_conv_w': 'delta_w', 'delta_c_w_out': 'delta_w', 'new_m_ffn1_norm': 'new_m', 'new_m_ffn1_w_gate': 'new_m', 'new_m_ffn1_w_up': 'new_m', 'new_m_ffn1_w_down': 'new_m', 'new_m_mix_norm': 'new_m', 'new_m_ffn2_norm': 'new_m', 'new_m_ffn2_w_gate': 'new_m', 'new_m_ffn2_w_up': 'new_m', 'new_m_ffn2_w_down': 'new_m', 'new_m_a_w_in': 'new_m', 'new_m_a_q_a_norm': 'new_m', 'new_m_a_w_q_up': 'new_m', 'new_m_a_kv_a_norm': 'new_m', 'new_m_a_w_kv_up': 'new_m', 'new_m_a_q_head_norm': 'new_m', 'new_m_a_k_head_norm': 'new_m', 'new_m_a_w_pool': 'new_m', 'new_m_a_pool_scale': 'new_m', 'new_m_a_w_out': 'new_m', 'new_m_c_w_in': 'new_m', 'new_m_c_conv_w': 'new_m', 'new_m_c_w_out': 'new_m', 'new_v_ffn1_norm': 'new_v', 'new_v_ffn1_w_gate': 'new_v', 'new_v_ffn1_w_up': 'new_v', 'new_v_ffn1_w_down': 'new_v', 'new_v_mix_norm': 'new_v', 'new_v_ffn2_norm': 'new_v', 'new_v_ffn2_w_gate': 'new_v', 'new_v_ffn2_w_up': 'new_v', 'new_v_ffn2_w_down': 'new_v', 'new_v_a_w_in': 'new_v', 'new_v_a_q_a_norm': 'new_v', 'new_v_a_w_q_up': 'new_v', 'new_v_a_kv_a_norm': 'new_v', 'new_v_a_w_kv_up': 'new_v', 'new_v_a_q_head_norm': 'new_v', 'new_v_a_k_head_norm': 'new_v', 'new_v_a_w_pool': 'new_v', 'new_v_a_pool_scale': 'new_v', 'new_v_a_w_out': 'new_v', 'new_v_c_w_in': 'new_v', 'new_v_c_conv_w': 'new_v', 'new_v_c_w_out': 'new_v'}


def _forward(args):
    return _fwd_reference(*[args[k] for k in FWD_PARAMS])


def _output_shape():
    def fwd():
        inp = _fwd_setup_inputs(0)
        return _fwd_reference(*[inp[k] for k in FWD_PARAMS])
    out = _jax.eval_shape(fwd)
    return out.shape, out.dtype

N_MICROBATCH = 1
ADAM_LR = 0.001
ADAM_B1 = 0.9
ADAM_B2 = 0.999
ADAM_EPS = 1e-08
ADAM_WD = 0.01
ADAM_STEP = 10
PER_EXAMPLE_BATCH_AXIS = {'x': 0, 'loss_target': 0}
SHARED_INPUTS = []
_WEIGHT_DTYPES = {'ffn1_norm': _jnp.float32, 'ffn1_w_gate': _jnp.float32, 'ffn1_w_up': _jnp.float32, 'ffn1_w_down': _jnp.float32, 'mix_norm': _jnp.float32, 'ffn2_norm': _jnp.float32, 'ffn2_w_gate': _jnp.float32, 'ffn2_w_up': _jnp.float32, 'ffn2_w_down': _jnp.float32, 'a_w_in': _jnp.float32, 'a_q_a_norm': _jnp.float32, 'a_w_q_up': _jnp.float32, 'a_kv_a_norm': _jnp.float32, 'a_w_kv_up': _jnp.float32, 'a_q_head_norm': _jnp.float32, 'a_k_head_norm': _jnp.float32, 'a_w_pool': _jnp.float32, 'a_pool_scale': _jnp.float32, 'a_w_out': _jnp.float32, 'c_w_in': _jnp.float32, 'c_conv_w': _jnp.float32, 'c_w_out': _jnp.float32}
MOMENT_SCALE = {'ffn1_norm': 2.206601e+01, 'ffn1_w_gate': 6.489223e-01, 'ffn1_w_up': 6.938934e-01, 'ffn1_w_down': 1.158672e+00, 'mix_norm': 2.826578e+02, 'ffn2_norm': 2.344329e+01, 'ffn2_w_gate': 4.741634e-01, 'ffn2_w_up': 5.492926e-01, 'ffn2_w_down': 9.106698e-01, 'a_w_in': 3.438906e+00, 'a_q_a_norm': 6.936810e-01, 'a_w_q_up': 4.967934e-01, 'a_kv_a_norm': 2.158309e+00, 'a_w_kv_up': 6.503102e-01, 'a_q_head_norm': 2.719296e+00, 'a_k_head_norm': 2.714676e+00, 'a_w_pool': 8.270443e+00, 'a_pool_scale': 1.023832e+02, 'a_w_out': 4.417069e+00, 'c_w_in': 3.943647e+00, 'c_conv_w': 7.448082e+01, 'c_w_out': 3.791820e+00}


def _to_microbatches(a, axis):
    t = _jnp.moveaxis(a, axis, 0)
    t = t.reshape((N_MICROBATCH, t.shape[0] // N_MICROBATCH) + t.shape[1:])
    return _jnp.moveaxis(t, 1, axis + 1)


def setup_inputs(seed: int = 0) -> dict:
    inp = _fwd_setup_inputs(seed)
    key = _jax.random.fold_in(_jax.random.key(seed), 7919)
    shape, _ = _output_shape()
    out = dict(inp)
    out["loss_target"] = _jax.random.normal(_jax.random.fold_in(key, 0), shape, _jnp.float32)
    for i, name in enumerate(TWIN_WEIGHTS):
        w = inp[name].astype(_jnp.float32)
        if MOMENT_SCALE is None:
            s = _jnp.sqrt(_jnp.mean(_jnp.square(w)) + 1e-30)
        else:
            s = MOMENT_SCALE[name]
        km, kv = _jax.random.split(_jax.random.fold_in(key, i + 1))
        out[name] = w
        out["m_" + name] = s * _jax.random.normal(km, w.shape, _jnp.float32)
        out["v_" + name] = (s * s) * _jax.random.uniform(kv, w.shape, _jnp.float32, 0.5, 1.5)
    if N_MICROBATCH > 1:
        for name, axis in PER_EXAMPLE_BATCH_AXIS.items():
            out[name] = _to_microbatches(out[name], axis)
    return {'x': out['x'], 'ffn1_norm': out['ffn1_norm'], 'ffn1_w_gate': out['ffn1_w_gate'], 'ffn1_w_up': out['ffn1_w_up'], 'ffn1_w_down': out['ffn1_w_down'], 'mix_norm': out['mix_norm'], 'ffn2_norm': out['ffn2_norm'], 'ffn2_w_gate': out['ffn2_w_gate'], 'ffn2_w_up': out['ffn2_w_up'], 'ffn2_w_down': out['ffn2_w_down'], 'a_w_in': out['a_w_in'], 'a_q_a_norm': out['a_q_a_norm'], 'a_w_q_up': out['a_w_q_up'], 'a_kv_a_norm': out['a_kv_a_norm'], 'a_w_kv_up': out['a_w_kv_up'], 'a_q_head_norm': out['a_q_head_norm'], 'a_k_head_norm': out['a_k_head_norm'], 'a_w_pool': out['a_w_pool'], 'a_pool_scale': out['a_pool_scale'], 'a_w_out': out['a_w_out'], 'c_w_in': out['c_w_in'], 'c_conv_w': out['c_conv_w'], 'c_w_out': out['c_w_out'], 'loss_target': out['loss_target'], 'm_ffn1_norm': out['m_ffn1_norm'], 'm_ffn1_w_gate': out['m_ffn1_w_gate'], 'm_ffn1_w_up': out['m_ffn1_w_up'], 'm_ffn1_w_down': out['m_ffn1_w_down'], 'm_mix_norm': out['m_mix_norm'], 'm_ffn2_norm': out['m_ffn2_norm'], 'm_ffn2_w_gate': out['m_ffn2_w_gate'], 'm_ffn2_w_up': out['m_ffn2_w_up'], 'm_ffn2_w_down': out['m_ffn2_w_down'], 'm_a_w_in': out['m_a_w_in'], 'm_a_q_a_norm': out['m_a_q_a_norm'], 'm_a_w_q_up': out['m_a_w_q_up'], 'm_a_kv_a_norm': out['m_a_kv_a_norm'], 'm_a_w_kv_up': out['m_a_w_kv_up'], 'm_a_q_head_norm': out['m_a_q_head_norm'], 'm_a_k_head_norm': out['m_a_k_head_norm'], 'm_a_w_pool': out['m_a_w_pool'], 'm_a_pool_scale': out['m_a_pool_scale'], 'm_a_w_out': out['m_a_w_out'], 'm_c_w_in': out['m_c_w_in'], 'm_c_conv_w': out['m_c_conv_w'], 'm_c_w_out': out['m_c_w_out'], 'v_ffn1_norm': out['v_ffn1_norm'], 'v_ffn1_w_gate': out['v_ffn1_w_gate'], 'v_ffn1_w_up': out['v_ffn1_w_up'], 'v_ffn1_w_down': out['v_ffn1_w_down'], 'v_mix_norm': out['v_mix_norm'], 'v_ffn2_norm': out['v_ffn2_norm'], 'v_ffn2_w_gate': out['v_ffn2_w_gate'], 'v_ffn2_w_up': out['v_ffn2_w_up'], 'v_ffn2_w_down': out['v_ffn2_w_down'], 'v_a_w_in': out['v_a_w_in'], 'v_a_q_a_norm': out['v_a_q_a_norm'], 'v_a_w_q_up': out['v_a_w_q_up'], 'v_a_kv_a_norm': out['v_a_kv_a_norm'], 'v_a_w_kv_up': out['v_a_w_kv_up'], 'v_a_q_head_norm': out['v_a_q_head_norm'], 'v_a_k_head_norm': out['v_a_k_head_norm'], 'v_a_w_pool': out['v_a_w_pool'], 'v_a_pool_scale': out['v_a_pool_scale'], 'v_a_w_out': out['v_a_w_out'], 'v_c_w_in': out['v_c_w_in'], 'v_c_conv_w': out['v_c_conv_w'], 'v_c_w_out': out['v_c_w_out']}


def _loss(weights, diff, rest, loss_target):
    with _jax.named_scope("forward"):
        args = {**rest, TWIN_DIFF_INPUT: diff, **{k: w.astype(_WEIGHT_DTYPES[k]) for k, w in weights.items()}}
        y = _forward(args)
    with _jax.named_scope("loss_head"):
        err = _jnp.square(y.astype(_jnp.float32) - loss_target)
        return 0.5 * _jnp.sum(_jnp.mean(err, axis=-1)) if err.ndim else 0.5 * err


def _adamw(w, g, m, v):
    m = ADAM_B1 * m + (1.0 - ADAM_B1) * g
    v = ADAM_B2 * v + (1.0 - ADAM_B2) * _jnp.square(g)
    m_hat = m / (1.0 - ADAM_B1 ** ADAM_STEP)
    v_hat = v / (1.0 - ADAM_B2 ** ADAM_STEP)
    delta = -ADAM_LR * (m_hat / (_jnp.sqrt(v_hat) + ADAM_EPS) + ADAM_WD * w)
    return delta, m, v


def reference(x, ffn1_norm, ffn1_w_gate, ffn1_w_up, ffn1_w_down, mix_norm, ffn2_norm, ffn2_w_gate, ffn2_w_up, ffn2_w_down, a_w_in, a_q_a_norm, a_w_q_up, a_kv_a_norm, a_w_kv_up, a_q_head_norm, a_k_head_norm, a_w_pool, a_pool_scale, a_w_out, c_w_in, c_conv_w, c_w_out, loss_target, m_ffn1_norm, m_ffn1_w_gate, m_ffn1_w_up, m_ffn1_w_down, m_mix_norm, m_ffn2_norm, m_ffn2_w_gate, m_ffn2_w_up, m_ffn2_w_down, m_a_w_in, m_a_q_a_norm, m_a_w_q_up, m_a_kv_a_norm, m_a_w_kv_up, m_a_q_head_norm, m_a_k_head_norm, m_a_w_pool, m_a_pool_scale, m_a_w_out, m_c_w_in, m_c_conv_w, m_c_w_out, v_ffn1_norm, v_ffn1_w_gate, v_ffn1_w_up, v_ffn1_w_down, v_mix_norm, v_ffn2_norm, v_ffn2_w_gate, v_ffn2_w_up, v_ffn2_w_down, v_a_w_in, v_a_q_a_norm, v_a_w_q_up, v_a_kv_a_norm, v_a_w_kv_up, v_a_q_head_norm, v_a_k_head_norm, v_a_w_pool, v_a_pool_scale, v_a_w_out, v_c_w_in, v_c_conv_w, v_c_w_out):
    given = dict(x=x, ffn1_norm=ffn1_norm, ffn1_w_gate=ffn1_w_gate, ffn1_w_up=ffn1_w_up, ffn1_w_down=ffn1_w_down, mix_norm=mix_norm, ffn2_norm=ffn2_norm, ffn2_w_gate=ffn2_w_gate, ffn2_w_up=ffn2_w_up, ffn2_w_down=ffn2_w_down, a_w_in=a_w_in, a_q_a_norm=a_q_a_norm, a_w_q_up=a_w_q_up, a_kv_a_norm=a_kv_a_norm, a_w_kv_up=a_w_kv_up, a_q_head_norm=a_q_head_norm, a_k_head_norm=a_k_head_norm, a_w_pool=a_w_pool, a_pool_scale=a_pool_scale, a_w_out=a_w_out, c_w_in=c_w_in, c_conv_w=c_conv_w, c_w_out=c_w_out, loss_target=loss_target, m_ffn1_norm=m_ffn1_norm, m_ffn1_w_gate=m_ffn1_w_gate, m_ffn1_w_up=m_ffn1_w_up, m_ffn1_w_down=m_ffn1_w_down, m_mix_norm=m_mix_norm, m_ffn2_norm=m_ffn2_norm, m_ffn2_w_gate=m_ffn2_w_gate, m_ffn2_w_up=m_ffn2_w_up, m_ffn2_w_down=m_ffn2_w_down, m_a_w_in=m_a_w_in, m_a_q_a_norm=m_a_q_a_norm, m_a_w_q_up=m_a_w_q_up, m_a_kv_a_norm=m_a_kv_a_norm, m_a_w_kv_up=m_a_w_kv_up, m_a_q_head_norm=m_a_q_head_norm, m_a_k_head_norm=m_a_k_head_norm, m_a_w_pool=m_a_w_pool, m_a_pool_scale=m_a_pool_scale, m_a_w_out=m_a_w_out, m_c_w_in=m_c_w_in, m_c_conv_w=m_c_conv_w, m_c_w_out=m_c_w_out, v_ffn1_norm=v_ffn1_norm, v_ffn1_w_gate=v_ffn1_w_gate, v_ffn1_w_up=v_ffn1_w_up, v_ffn1_w_down=v_ffn1_w_down, v_mix_norm=v_mix_norm, v_ffn2_norm=v_ffn2_norm, v_ffn2_w_gate=v_ffn2_w_gate, v_ffn2_w_up=v_ffn2_w_up, v_ffn2_w_down=v_ffn2_w_down, v_a_w_in=v_a_w_in, v_a_q_a_norm=v_a_q_a_norm, v_a_w_q_up=v_a_w_q_up, v_a_kv_a_norm=v_a_kv_a_norm, v_a_w_kv_up=v_a_w_kv_up, v_a_q_head_norm=v_a_q_head_norm, v_a_k_head_norm=v_a_k_head_norm, v_a_w_pool=v_a_w_pool, v_a_pool_scale=v_a_pool_scale, v_a_w_out=v_a_w_out, v_c_w_in=v_c_w_in, v_c_conv_w=v_c_conv_w, v_c_w_out=v_c_w_out)
    weights = {n: given[n] for n in TWIN_WEIGHTS}
    shared = {n: given[n] for n in SHARED_INPUTS}
    per_example = {n: given[n] for n in ['x']}
    grad_fn = _jax.value_and_grad(_loss, argnums=(0, 1))

    def one_microbatch(ex, loss_target):
        ex = dict(ex)
        diff = ex.pop(TWIN_DIFF_INPUT)
        return grad_fn(weights, diff, {**shared, **ex}, loss_target)

    if N_MICROBATCH == 1:
        loss, (grad_w, grad_x) = one_microbatch(per_example, given["loss_target"])
    else:
        def body(carry, xs):
            loss_sum, grad_sum = carry
            l_k, (gw_k, gx_k) = one_microbatch(xs[0], xs[1])
            with _jax.named_scope("update"):
                return (loss_sum + l_k, _jax.tree.map(_jnp.add, grad_sum, gw_k)), gx_k

        init = (_jnp.zeros((), _jnp.float32), _jax.tree.map(_jnp.zeros_like, weights))
        (loss, grad_w), grad_x = _jax.lax.scan(body, init, (per_example, given["loss_target"]))
    with _jax.named_scope("update"):
        delta_w, new_m, new_v = {}, {}, {}
        for n in TWIN_WEIGHTS:
            delta_w[n], new_m[n], new_v[n] = _adamw(weights[n], grad_w[n], given["m_" + n], given["v_" + n])
    return (loss, grad_x, *[grad_w[n] for n in TWIN_WEIGHTS], *[delta_w[n] for n in TWIN_WEIGHTS],
            *[new_m[n] for n in TWIN_WEIGHTS], *[new_v[n] for n in TWIN_WEIGHTS])
```

```python
import functools

import numpy as np
import jax
import jax.numpy as jnp
from jax import lax
from jax.experimental import pallas as pl
from jax.experimental.pallas import tpu as pltpu

BF, F32 = jnp.bfloat16, jnp.float32
MESH = pl.DeviceIdType.MESH
AXES = ("x", "y", "c")

NORM_EPS = 1e-6
DEPTH = 4
HEADS = 8
HEAD_SLOT = 128
QK_DIM, NOPE_DIM, ROPE_DIM, V_DIM = 96, 64, 32, 64
POOL_WINDOWS = (2, 4, 8, 16)
POOL_DIM, POOL_GROUP = 512, 128
Q_RANK, KV_RANK = 384, 256
ROPE_THETA = 10000.0
HALO = 16
ATTN_SCALE = QK_DIM ** -0.5

ADAM_LR, ADAM_B1, ADAM_B2, ADAM_EPS, ADAM_WD, ADAM_STEP = 0.001, 0.9, 0.999, 1e-08, 0.01, 10

TM_FFN, TF_FFN = 1024, 256
TM_MIX_FWD, TM_MIX_BWD = 512, 256
TM_CONV_FWD, TM_CONV_BWD = 256, 256
TQ_ATTN = 512
TM_MM = 512
TK_TN, BM_TN, BN_TN = 512, 1024, 1536
FLAT_COLS = 1024
FLAT_ROW_ALIGN = 256
TR_FLAT = 256
VMEM_LIMIT = 56 * 1024 * 1024

WEIGHTS = ['ffn1_norm', 'ffn1_w_gate', 'ffn1_w_up', 'ffn1_w_down', 'mix_norm', 'ffn2_norm', 'ffn2_w_gate',
           'ffn2_w_up', 'ffn2_w_down', 'a_w_in', 'a_q_a_norm', 'a_w_q_up', 'a_kv_a_norm', 'a_w_kv_up',
           'a_q_head_norm', 'a_k_head_norm', 'a_w_pool', 'a_pool_scale', 'a_w_out', 'c_w_in', 'c_conv_w',
           'c_w_out']
COL_SHARDED = ('ffn1_w_gate', 'ffn1_w_up', 'ffn2_w_gate', 'ffn2_w_up', 'a_w_in', 'a_w_q_up', 'a_w_kv_up',
               'c_w_in', 'c_conv_w')
ROW_SHARDED = ('ffn1_w_down', 'ffn2_w_down', 'a_w_out', 'c_w_out')
SHARDED = tuple(n for n in WEIGHTS if n in COL_SHARDED or n in ROW_SHARDED)
REPLICATED = tuple(n for n in WEIGHTS if n not in SHARDED)
INPUTS = ['x'] + WEIGHTS + ['loss_target'] + ['m_' + n for n in WEIGHTS] + ['v_' + n for n in WEIGHTS]


def _dot(a, b):
    return jnp.dot(a, b, preferred_element_type=F32)


def _dot_nt(a, b):
    return lax.dot_general(a, b, (((1,), (1,)), ((), ())), preferred_element_type=F32)


def _dot_tn(a, b):
    return lax.dot_general(a, b, (((0,), (0,)), ((), ())), preferred_element_type=F32)


def _params(*sem):
    return pltpu.CompilerParams(dimension_semantics=sem or None, vmem_limit_bytes=VMEM_LIMIT)


def _tile(n, cap, unit):
    if n <= cap:
        return n
    best = None
    for t in range(unit, cap + 1, unit):
        if n % t == 0:
            best = t
    assert best is not None, (n, cap, unit)
    return best


def _rms(x, width=None):
    ms = jnp.sum(x * x, axis=-1, keepdims=True) * (1.0 / (width or x.shape[-1]))
    r = lax.rsqrt(ms + NORM_EPS)
    return x * r, r


def _rms_bwd(a, xhat, r, width=None):
    return r * (a - xhat * (jnp.sum(a * xhat, axis=-1, keepdims=True) * (1.0 / (width or a.shape[-1]))))


def _colsum(a):
    return jnp.sum(a, axis=0, keepdims=True)


def _accumulate(ref, first, value):
    @pl.when(first)
    def _():
        ref[...] = value

    @pl.when(jnp.logical_not(first))
    def _():
        ref[...] += value


def _rot_half(v):
    lane = lax.broadcasted_iota(jnp.int32, v.shape, 1)
    rot = jnp.where(lane < NOPE_DIM + ROPE_DIM // 2, -pltpu.roll(v, HEAD_SLOT - ROPE_DIM // 2, 1),
                    pltpu.roll(v, ROPE_DIM // 2, 1))
    return jnp.where((lane >= NOPE_DIM) & (lane < QK_DIM), rot, 0.0)


def _rope(v, cos, sin):
    return v * cos + _rot_half(v) * sin


def _rope_bwd(d, cos, sin):
    return d * cos - _rot_half(d * sin)


def ffn_fwd(x, g, wg, wu, wd):
    T, D = x.shape
    F = wg.shape[1]
    tm, tf = _tile(T, TM_FFN, 8), _tile(F, TF_FFN, 128)
    nf = F // tf

    def body(x_ref, g_ref, wg_ref, wu_ref, wd_ref, y_ref, gg_ref, uu_ref, n_sc, acc):
        f = pl.program_id(1)

        @pl.when(f == 0)
        def _():
            xh, _ = _rms(x_ref[...])
            n_sc[...] = (xh * g_ref[...]).astype(BF)
            acc[...] = jnp.zeros_like(acc)

        n = n_sc[...]
        gg = _dot(n, wg_ref[...])
        uu = _dot(n, wu_ref[...])
        gg_ref[...] = gg.astype(BF)
        uu_ref[...] = uu.astype(BF)
        h = gg * jax.nn.sigmoid(gg) * uu
        acc[...] += _dot(h.astype(BF), wd_ref[...])

        @pl.when(f == nf - 1)
        def _():
            y_ref[...] = x_ref[...] + 0.5 * acc[...]

    return pl.pallas_call(
        body, name="ffn_fwd", grid=(T // tm, nf),
        in_specs=[pl.BlockSpec((tm, D), lambda i, f: (i, 0)), pl.BlockSpec((1, D), lambda i, f: (0, 0)),
                  pl.BlockSpec((D, tf), lambda i, f: (0, f)), pl.BlockSpec((D, tf), lambda i, f: (0, f)),
                  pl.BlockSpec((tf, D), lambda i, f: (f, 0))],
        out_specs=[pl.BlockSpec((tm, D), lambda i, f: (i, 0)), pl.BlockSpec((tm, tf), lambda i, f: (i, f)),
                   pl.BlockSpec((tm, tf), lambda i, f: (i, f))],
        out_shape=[jax.ShapeDtypeStruct((T, D), F32), jax.ShapeDtypeStruct((T, F), BF),
                   jax.ShapeDtypeStruct((T, F), BF)],
        scratch_shapes=[pltpu.VMEM((tm, D), BF), pltpu.VMEM((tm, D), F32)],
        compiler_params=_params("arbitrary", "arbitrary"),
    )(x, g, wg, wu, wd)


def ffn_bwd(x, dy, g, gg, uu, wd_t, wg_t, wu_t):
    T, D = x.shape
    F = gg.shape[1]
    tm, tf = _tile(T, TM_FFN, 8), _tile(F, TF_FFN, 128)
    nf = F // tf

    def body(x_ref, dy_ref, g_ref, gg_ref, uu_ref, wdt_ref, wgt_ref, wut_ref,
             dx_ref, n_ref, dyh_ref, h_ref, dg_ref, du_ref, dgn_ref, acc):
        i, f = pl.program_id(0), pl.program_id(1)

        @pl.when(f == 0)
        def _():
            xh, _ = _rms(x_ref[...])
            n_ref[...] = (xh * g_ref[...]).astype(BF)
            dyh_ref[...] = (0.5 * dy_ref[...]).astype(BF)
            acc[...] = jnp.zeros_like(acc)

        dh = _dot(dyh_ref[...], wdt_ref[...])
        gv = gg_ref[...].astype(F32)
        uv = uu_ref[...].astype(F32)
        sg = jax.nn.sigmoid(gv)
        silu = gv * sg
        h_ref[...] = (silu * uv).astype(BF)
        d_up = (dh * silu).astype(BF)
        d_gate = (dh * uv * (sg * (1.0 + gv * (1.0 - sg)))).astype(BF)
        du_ref[...] = d_up
        dg_ref[...] = d_gate
        acc[...] += _dot(d_gate, wgt_ref[...]) + _dot(d_up, wut_ref[...])

        @pl.when(f == nf - 1)
        def _():
            xh, r = _rms(x_ref[...])
            dn = acc[...]
            dx_ref[...] = dy_ref[...] + _rms_bwd(dn * g_ref[...], xh, r)
            _accumulate(dgn_ref, i == 0, _colsum(dn * xh))

    tok = lambda i, f: (i, 0)
    chunk = lambda i, f: (i, f)
    return pl.pallas_call(
        body, name="ffn_bwd", grid=(T // tm, nf),
        in_specs=[pl.BlockSpec((tm, D), tok), pl.BlockSpec((tm, D), tok), pl.BlockSpec((1, D), lambda i, f: (0, 0)),
                  pl.BlockSpec((tm, tf), chunk), pl.BlockSpec((tm, tf), chunk),
                  pl.BlockSpec((D, tf), lambda i, f: (0, f)), pl.BlockSpec((tf, D), lambda i, f: (f, 0)),
                  pl.BlockSpec((tf, D), lambda i, f: (f, 0))],
        out_specs=[pl.BlockSpec((tm, D), tok), pl.BlockSpec((tm, D), tok), pl.BlockSpec((tm, D), tok),
                   pl.BlockSpec((tm, tf), chunk), pl.BlockSpec((tm, tf), chunk), pl.BlockSpec((tm, tf), chunk),
                   pl.BlockSpec((1, D), lambda i, f: (0, 0))],
        out_shape=[jax.ShapeDtypeStruct((T, D), F32), jax.ShapeDtypeStruct((T, D), BF),
                   jax.ShapeDtypeStruct((T, D), BF), jax.ShapeDtypeStruct((T, F), BF),
                   jax.ShapeDtypeStruct((T, F), BF), jax.ShapeDtypeStruct((T, F), BF),
                   jax.ShapeDtypeStruct((1, D), F32)],
        scratch_shapes=[pltpu.VMEM((tm, D), F32)],
        compiler_params=_params("arbitrary", "arbitrary"),
    )(x, dy, g, gg, uu, wd_t, wg_t, wu_t)


def mm_tn(a, b):
    T, M = a.shape
    N = b.shape[1]
    tk, bm, bn = _tile(T, TK_TN, 16), _tile(M, BM_TN, 128), _tile(N, BN_TN, 128)

    def body(a_ref, b_ref, o_ref):
        part = _dot_tn(a_ref[...].astype(BF), b_ref[...].astype(BF))
        _accumulate(o_ref, pl.program_id(2) == 0, part)

    return pl.pallas_call(
        body, name="mm_tn", grid=(M // bm, N // bn, T // tk),
        in_specs=[pl.BlockSpec((tk, bm), lambda i, j, k: (k, i)), pl.BlockSpec((tk, bn), lambda i, j, k: (k, j))],
        out_specs=pl.BlockSpec((bm, bn), lambda i, j, k: (i, j)),
        out_shape=jax.ShapeDtypeStruct((M, N), F32),
        compiler_params=_params("arbitrary", "arbitrary", "arbitrary"),
    )(a, b)


def mm_multi(pairs, res=None, out_dtype=F32):
    T = pairs[0][0].shape[0]
    N = pairs[0][1].shape[1]
    tm = _tile(T, TM_MM, 16)
    n = len(pairs)

    def body(*refs):
        o_ref = refs[-1]
        acc = refs[2 * n][...] if res is not None else None
        for k in range(n):
            part = _dot(refs[k][...].astype(BF), refs[n + k][...])
            acc = part if acc is None else acc + part
        o_ref[...] = acc.astype(out_dtype)

    ins = [a for a, _ in pairs] + [w for _, w in pairs]
    specs = [pl.BlockSpec((tm, a.shape[1]), lambda i: (i, 0)) for a, _ in pairs]
    specs += [pl.BlockSpec(w.shape, lambda i: (0, 0)) for _, w in pairs]
    if res is not None:
        ins.append(res)
        specs.append(pl.BlockSpec((tm, N), lambda i: (i, 0)))
    return pl.pallas_call(
        body, name="mm_multi", grid=(T // tm,), in_specs=specs,
        out_specs=pl.BlockSpec((tm, N), lambda i: (i, 0)),
        out_shape=jax.ShapeDtypeStruct((T, N), out_dtype),
        compiler_params=_params("arbitrary"),
    )(*ins)


def _causal_mask(t, q_major):
    r = lax.broadcasted_iota(jnp.int32, (t, t), 0)
    c = lax.broadcasted_iota(jnp.int32, (t, t), 1)
    return (c <= r) if q_major else (r <= c)


def attn_fwd(q, k, v):
    T = q.shape[0]
    t = _tile(T, TQ_ATTN, 128)
    nq = T // t

    def body(q_ref, k_ref, v_ref, o_ref, lse_ref):
        i = pl.program_id(1)
        qv = q_ref[...]

        def step(j, carry, masked):
            m, l, acc = carry
            rows = pl.ds(pl.multiple_of(j * t, t), t)
            s = _dot_nt(qv, k_ref[rows, :])
            if masked:
                s = jnp.where(_causal_mask(t, True), s, -jnp.inf)
            m2 = jnp.maximum(m, jnp.max(s, axis=-1, keepdims=True))
            p = jnp.exp(s - m2)
            a = jnp.exp(m - m2)
            return m2, a * l + jnp.sum(p, axis=-1, keepdims=True), a * acc + _dot(p.astype(BF), v_ref[rows, :])

        init = (jnp.full((t, 1), -1e30, F32), jnp.zeros((t, 1), F32), jnp.zeros((t, HEAD_SLOT), F32))
        carry = lax.fori_loop(0, i, lambda j, c: step(j, c, False), init)
        m, l, acc = step(i, carry, True)
        o_ref[...] = (acc / l).astype(BF)
        lse_ref[...] = m + jnp.log(l)

    return pl.pallas_call(
        body, name="attn_fwd", grid=(HEADS, nq),
        in_specs=[pl.BlockSpec((t, HEAD_SLOT), lambda h, i: (i, h)), pl.BlockSpec((T, HEAD_SLOT), lambda h, i: (0, h)),
                  pl.BlockSpec((T, HEAD_SLOT), lambda h, i: (0, h))],
        out_specs=[pl.BlockSpec((t, HEAD_SLOT), lambda h, i: (i, h)), pl.BlockSpec((None, t, 1), lambda h, i: (h, i, 0))],
        out_shape=[jax.ShapeDtypeStruct((T, HEADS * HEAD_SLOT), BF), jax.ShapeDtypeStruct((HEADS, T, 1), F32)],
        compiler_params=_params("arbitrary", "arbitrary"),
    )(q, k, v)


def attn_bwd_dq(q, k, v, o, do, lse):
    T = q.shape[0]
    t = _tile(T, TQ_ATTN, 128)
    nq = T // t

    def body(q_ref, k_ref, v_ref, o_ref, do_ref, lse_ref, dq_ref, delta_ref):
        i = pl.program_id(1)
        qv, dov, lsev = q_ref[...], do_ref[...], lse_ref[...]
        delta = jnp.sum(dov.astype(F32) * o_ref[...].astype(F32), axis=-1, keepdims=True)
        delta_ref[...] = delta

        def step(j, dq, masked):
            rows = pl.ds(pl.multiple_of(j * t, t), t)
            kv = k_ref[rows, :]
            s = _dot_nt(qv, kv)
            if masked:
                s = jnp.where(_causal_mask(t, True), s, -jnp.inf)
            p = jnp.exp(s - lsev)
            ds = p * (_dot_nt(dov, v_ref[rows, :]) - delta)
            return dq + _dot(ds.astype(BF), kv)

        dq = lax.fori_loop(0, i, lambda j, c: step(j, c, False), jnp.zeros((t, HEAD_SLOT), F32))
        dq_ref[...] = step(i, dq, True)

    blk = pl.BlockSpec((t, HEAD_SLOT), lambda h, i: (i, h))
    full = pl.BlockSpec((T, HEAD_SLOT), lambda h, i: (0, h))
    col = pl.BlockSpec((None, t, 1), lambda h, i: (h, i, 0))
    return pl.pallas_call(
        body, name="attn_bwd_dq", grid=(HEADS, nq),
        in_specs=[blk, full, full, blk, blk, col], out_specs=[blk, col],
        out_shape=[jax.ShapeDtypeStruct((T, HEADS * HEAD_SLOT), F32), jax.ShapeDtypeStruct((HEADS, T, 1), F32)],
        compiler_params=_params("arbitrary", "arbitrary"),
    )(q, k, v, o, do, lse)


def attn_bwd_dkv(q, k, v, do, lse_rows, delta_rows):
    T = q.shape[0]
    t = _tile(T, TQ_ATTN, 128)
    nq = T // t

    def body(q_ref, k_ref, v_ref, do_ref, lse_ref, delta_ref, dk_ref, dv_ref):
        j = pl.program_id(1)
        kv, vv = k_ref[...], v_ref[...]

        def step(i, carry, masked):
            dk, dv = carry
            rows = pl.ds(pl.multiple_of(i * t, t), t)
            qv, dov = q_ref[rows, :], do_ref[rows, :]
            st = _dot_nt(kv, qv)
            if masked:
                st = jnp.where(_causal_mask(t, False), st, -jnp.inf)
            pt = jnp.exp(st - lse_ref[pl.ds(i, 1), :])
            dst = pt * (_dot_nt(vv, dov) - delta_ref[pl.ds(i, 1), :])
            return dk + _dot(dst.astype(BF), qv), dv + _dot(pt.astype(BF), dov)

        zero = jnp.zeros((t, HEAD_SLOT), F32)
        carry = step(j, (zero, zero), True)
        dk, dv = lax.fori_loop(j + 1, nq, lambda i, c: step(i, c, False), carry)
        dk_ref[...] = dk
        dv_ref[...] = dv

    blk = pl.BlockSpec((t, HEAD_SLOT), lambda h, j: (j, h))
    full = pl.BlockSpec((T, HEAD_SLOT), lambda h, j: (0, h))
    rows = pl.BlockSpec((None, nq, t), lambda h, j: (h, 0, 0))
    return pl.pallas_call(
        body, name="attn_bwd_dkv", grid=(HEADS, nq),
        in_specs=[full, blk, blk, full, rows, rows], out_specs=[blk, blk],
        out_shape=[jax.ShapeDtypeStruct((T, HEADS * HEAD_SLOT), F32)] * 2,
        compiler_params=_params("arbitrary", "arbitrary"),
    )(q, k, v, do, lse_rows, delta_rows)


def _prev_halo(tm):
    return lambda i: (jnp.maximum(i * (tm // HALO) - 1, 0), 0)


def _next_halo(tm, T):
    return lambda i: (jnp.minimum((i + 1) * (tm // HALO), T // HALO - 1), 0)


def _inv_count(row0, n, w):
    t = row0 + lax.broadcasted_iota(jnp.int32, (n, 1), 0)
    return 1.0 / jnp.minimum(t + 1, w).astype(F32)


def _pool_fwd(u_prev, u, row0):
    tm = u.shape[0]
    out = []
    for g, w in enumerate(POOL_WINDOWS):
        lanes = slice(g * POOL_GROUP, (g + 1) * POOL_GROUP)
        ue = jnp.concatenate([u_prev[:, lanes], u[:, lanes]], axis=0)
        s, step = ue, 1
        while step < w:
            s = s + pltpu.roll(s, step, 0)
            step *= 2
        out.append(s[HALO:, :] * _inv_count(row0, tm, w) - u[:, lanes])
    return out


def _pool_bwd(dp, dp_next, row0):
    tm = dp[0].shape[0]
    out = []
    for g, w in enumerate(POOL_WINDOWS):
        e = jnp.concatenate([dp[g] * _inv_count(row0, tm, w), dp_next[g] * (1.0 / w)], axis=0)
        n = tm + HALO
        s, step = e, 1
        while step < w:
            s = s + pltpu.roll(s, n - step, 0)
            step *= 2
        out.append(s[:tm, :] - dp[g])
    return out


def _mixa_front(x, xp, first, row0, g_ref, win_ref, qan_ref, wq_ref, kvan_ref, wkn_ref):
    xh, r = _rms(x)
    hn = (xh * g_ref[...]).astype(BF)
    z = _dot(hn, win_ref[...])
    xph, _ = _rms(xp)
    u_prev = _dot((xph * g_ref[...]).astype(BF), win_ref[:, :POOL_DIM]) * jnp.where(first, 0.0, 1.0)
    u = z[:, :POOL_DIM]
    pooled = _pool_fwd(u_prev, u, row0)
    c1, c2 = POOL_DIM + Q_RANK, POOL_DIM + Q_RANK + KV_RANK
    qh, rq = _rms(z[:, POOL_DIM:c1])
    nq = (qh * qan_ref[...]).astype(BF)
    kh, rk = _rms(z[:, c1:c2])
    nkv = (kh * kvan_ref[...]).astype(BF)
    qraw = _dot(nq, wq_ref[...])
    kraw = _dot(nkv, wkn_ref[...])
    krope = z[:, c2:c2 + HEAD_SLOT]
    return dict(xh=xh, r=r, hn=hn, pooled=pooled, qh=qh, rq=rq, nq=nq, kh=kh, rk=rk, nkv=nkv,
                qraw=qraw, kraw=kraw, krope=krope)


def mixa_pre_fwd(x, g, win, qan, wq, kvan, wkn, wv, qhn, khn, wpool, pscale, cos, sin):
    T, D = x.shape
    tm = _tile(T, TM_MIX_FWD, HALO)
    HS = HEADS * HEAD_SLOT

    def body(x_ref, xp_ref, g_ref, win_ref, qan_ref, wq_ref, kvan_ref, wkn_ref, wv_ref, qhn_ref, khn_ref,
             wpool_ref, pscale_ref, cos_ref, sin_ref, q_ref, k_ref, v_ref, po_ref):
        i = pl.program_id(0)
        a = _mixa_front(x_ref[...], xp_ref[...], i == 0, i * tm, g_ref, win_ref, qan_ref, wq_ref, kvan_ref, wkn_ref)
        for gi in range(len(POOL_WINDOWS)):
            lanes = slice(gi * POOL_GROUP, (gi + 1) * POOL_GROUP)
            po = _dot(a["pooled"][gi].astype(BF), wpool_ref[gi]) * pscale_ref[:, lanes]
            po_ref[:, lanes] = po.astype(BF)
        cosv, sinv = cos_ref[...], sin_ref[...]
        v_ref[...] = _dot(a["nkv"], wv_ref[...]).astype(BF)
        for h in range(HEADS):
            lanes = slice(h * HEAD_SLOT, (h + 1) * HEAD_SLOT)
            qn, _ = _rms(a["qraw"][:, lanes], QK_DIM)
            q_ref[:, lanes] = (_rope(qn * qhn_ref[...], cosv, sinv) * ATTN_SCALE).astype(BF)
            kn, _ = _rms(a["kraw"][:, lanes] + a["krope"], QK_DIM)
            k_ref[:, lanes] = _rope(kn * khn_ref[...], cosv, sinv).astype(BF)

    tok = lambda w: pl.BlockSpec((tm, w), lambda i: (i, 0))
    whole = lambda arr: pl.BlockSpec(arr.shape, lambda i: (0,) * arr.ndim)
    return pl.pallas_call(
        body, name="mixa_pre_fwd", grid=(T // tm,),
        in_specs=[tok(D), pl.BlockSpec((HALO, D), _prev_halo(tm))] + [whole(a) for a in
                  (g, win, qan, wq, kvan, wkn, wv, qhn, khn, wpool, pscale)] + [tok(HEAD_SLOT), tok(HEAD_SLOT)],
        out_specs=[tok(HS), tok(HS), tok(HS), tok(POOL_DIM)],
        out_shape=[jax.ShapeDtypeStruct((T, HS), BF)] * 3 + [jax.ShapeDtypeStruct((T, POOL_DIM), BF)],
        compiler_params=_params("arbitrary"),
    )(x, x, g, win, qan, wq, kvan, wkn, wv, qhn, khn, wpool, pscale, cos, sin)


def mixa_pre_bwd(x, dy, dq, dk, dv, dpo, g, win, win_t, qan, wq, wq_t, kvan, wkn, wkn_t, wv_t, qhn, khn,
                 wpool, wpool_t, pscale, cos, sin):
    T, D = x.shape
    tm = _tile(T, TM_MIX_BWD, HALO)
    HS = HEADS * HEAD_SLOT
    ZW = win.shape[1]
    nt = T // tm

    def body(x_ref, xp_ref, dy_ref, dq_ref, dk_ref, dv_ref, dpo_ref, dpon_ref, g_ref, win_ref, wint_ref, qan_ref,
             wq_ref, wqt_ref, kvan_ref, wkn_ref, wknt_ref, wvt_ref, qhn_ref, khn_ref, wpool_ref, wpoolt_ref,
             pscale_ref, cos_ref, sin_ref,
             dx_ref, hn_ref, dz_ref, nq_ref, dqraw_ref, nkv_ref, dkraw_ref, pooled_ref, dps_ref,
             dg_ref, dqan_ref, dkvan_ref, dqhn_ref, dkhn_ref, dpscale_ref):
        i = pl.program_id(0)
        first = i == 0
        a = _mixa_front(x_ref[...], xp_ref[...], first, i * tm, g_ref, win_ref, qan_ref, wq_ref, kvan_ref, wkn_ref)
        cosv, sinv = cos_ref[...], sin_ref[...]
        hn_ref[...] = a["hn"]
        nq_ref[...] = a["nq"]
        nkv_ref[...] = a["nkv"]

        has_next = jnp.where(i == nt - 1, 0.0, 1.0)
        dpool, dpool_next, dpscale = [], [], []
        for gi in range(len(POOL_WINDOWS)):
            lanes = slice(gi * POOL_GROUP, (gi + 1) * POOL_GROUP)
            pooled = a["pooled"][gi].astype(BF)
            pooled_ref[:, lanes] = pooled
            dpo_g = dpo_ref[:, lanes]
            dpscale.append(_colsum(dpo_g * _dot(pooled, wpool_ref[gi])))
            dps = (dpo_g * pscale_ref[:, lanes]).astype(BF)
            dps_ref[:, lanes] = dps
            dpool.append(_dot(dps, wpoolt_ref[gi]))
            dps_n = (dpon_ref[:, lanes] * pscale_ref[:, lanes] * has_next).astype(BF)
            dpool_next.append(_dot(dps_n, wpoolt_ref[gi]))
        du = jnp.concatenate(_pool_bwd(dpool, dpool_next, i * tm), axis=1)
        _accumulate(dpscale_ref, first, jnp.concatenate(dpscale, axis=1))

        dqhn = jnp.zeros((1, HEAD_SLOT), F32)
        dkhn = jnp.zeros((1, HEAD_SLOT), F32)
        dkrope = jnp.zeros((tm, HEAD_SLOT), F32)
        for h in range(HEADS):
            lanes = slice(h * HEAD_SLOT, (h + 1) * HEAD_SLOT)
            qhat, rq = _rms(a["qraw"][:, lanes], QK_DIM)
            dqn = _rope_bwd(dq_ref[:, lanes] * ATTN_SCALE, cosv, sinv)
            dqhn = dqhn + _colsum(dqn * qhat)
            dqraw_ref[:, lanes] = _rms_bwd(dqn * qhn_ref[...], qhat, rq, QK_DIM).astype(BF)
            khat, rk = _rms(a["kraw"][:, lanes] + a["krope"], QK_DIM)
            dkn = _rope_bwd(dk_ref[:, lanes], cosv, sinv)
            dkhn = dkhn + _colsum(dkn * khat)
            dkraw = _rms_bwd(dkn * khn_ref[...], khat, rk, QK_DIM)
            dkrope = dkrope + dkraw
            dkraw_ref[:, lanes] = dkraw.astype(BF)
        _accumulate(dqhn_ref, first, dqhn)
        _accumulate(dkhn_ref, first, dkhn)

        dnq = _dot(dqraw_ref[...], wqt_ref[...])
        _accumulate(dqan_ref, first, _colsum(dnq * a["qh"]))
        dql = _rms_bwd(dnq * qan_ref[...], a["qh"], a["rq"])
        dnkv = _dot(dkraw_ref[...], wknt_ref[...]) + _dot(dv_ref[...].astype(BF), wvt_ref[...])
        _accumulate(dkvan_ref, first, _colsum(dnkv * a["kh"]))
        dkvl = _rms_bwd(dnkv * kvan_ref[...], a["kh"], a["rk"])

        dz = jnp.concatenate([du, dql, dkvl, dkrope], axis=1).astype(BF)
        dz_ref[...] = dz
        dhn = _dot(dz, wint_ref[...])
        _accumulate(dg_ref, first, _colsum(dhn * a["xh"]))
        dx_ref[...] = dy_ref[...] + _rms_bwd(dhn * g_ref[...], a["xh"], a["r"])

    tok = lambda w: pl.BlockSpec((tm, w), lambda i: (i, 0))
    whole = lambda arr: pl.BlockSpec(arr.shape, lambda i: (0,) * arr.ndim)
    row = lambda w: pl.BlockSpec((1, w), lambda i: (0, 0))
    weights = (g, win, win_t, qan, wq, wq_t, kvan, wkn, wkn_t, wv_t, qhn, khn, wpool, wpool_t, pscale)
    return pl.pallas_call(
        body, name="mixa_pre_bwd", grid=(nt,),
        in_specs=[tok(D), pl.BlockSpec((HALO, D), _prev_halo(tm)), tok(D), tok(HS), tok(HS), tok(HS), tok(POOL_DIM),
                  pl.BlockSpec((HALO, POOL_DIM), _next_halo(tm, T))] + [whole(a) for a in weights]
                 + [tok(HEAD_SLOT), tok(HEAD_SLOT)],
        out_specs=[tok(D), tok(D), tok(ZW), tok(Q_RANK), tok(HS), tok(KV_RANK), tok(HS), tok(POOL_DIM), tok(POOL_DIM),
                   row(D), row(Q_RANK), row(KV_RANK), row(HEAD_SLOT), row(HEAD_SLOT), row(POOL_DIM)],
        out_shape=[jax.ShapeDtypeStruct((T, D), F32), jax.ShapeDtypeStruct((T, D), BF),
                   jax.ShapeDtypeStruct((T, ZW), BF), jax.ShapeDtypeStruct((T, Q_RANK), BF),
                   jax.ShapeDtypeStruct((T, HS), BF), jax.ShapeDtypeStruct((T, KV_RANK), BF),
                   jax.ShapeDtypeStruct((T, HS), BF), jax.ShapeDtypeStruct((T, POOL_DIM), BF),
                   jax.ShapeDtypeStruct((T, POOL_DIM), BF),
                   jax.ShapeDtypeStruct((1, D), F32), jax.ShapeDtypeStruct((1, Q_RANK), F32),
                   jax.ShapeDtypeStruct((1, KV_RANK), F32), jax.ShapeDtypeStruct((1, HEAD_SLOT), F32),
                   jax.ShapeDtypeStruct((1, HEAD_SLOT), F32), jax.ShapeDtypeStruct((1, POOL_DIM), F32)],
        compiler_params=_params("arbitrary"),
    )(x, x, dy, dq, dk, dv, dpo, dpo, *weights, cos, sin)


def _conv_taps(u_prev, u, cw_ref):
    ue = jnp.concatenate([u_prev, u], axis=0)
    u1 = pltpu.roll(ue, 1, 0)[HALO:, :]
    u2 = pltpu.roll(ue, 2, 0)[HALO:, :]
    return cw_ref[0:1, :] * u2 + cw_ref[1:2, :] * u1 + cw_ref[2:3, :] * u, u1, u2


def mixc_fwd(x, g, win, cw, wout):
    T, D = x.shape
    tm = _tile(T, TM_CONV_FWD, HALO)

    def body(x_ref, xp_ref, g_ref, win_ref, cw_ref, wout_ref, y_ref, z_ref):
        i = pl.program_id(0)
        xv = x_ref[...]
        xh, _ = _rms(xv)
        z = _dot((xh * g_ref[...]).astype(BF), win_ref[...])
        z_ref[...] = z.astype(BF)
        xph, _ = _rms(xp_ref[...])
        zp = _dot((xph * g_ref[...]).astype(BF), win_ref[:, D:])
        u_prev = zp[:, :D] * zp[:, D:] * jnp.where(i == 0, 0.0, 1.0)
        conv, _, _ = _conv_taps(u_prev, z[:, D:2 * D] * z[:, 2 * D:], cw_ref)
        y_ref[...] = xv + _dot((z[:, :D] * conv).astype(BF), wout_ref[...])

    tok = lambda w: pl.BlockSpec((tm, w), lambda i: (i, 0))
    whole = lambda arr: pl.BlockSpec(arr.shape, lambda i: (0,) * arr.ndim)
    return pl.pallas_call(
        body, name="mixc_fwd", grid=(T // tm,),
        in_specs=[tok(D), pl.BlockSpec((HALO, D), _prev_halo(tm)), whole(g), whole(win), whole(cw), whole(wout)],
        out_specs=[tok(D), tok(3 * D)],
        out_shape=[jax.ShapeDtypeStruct((T, D), F32), jax.ShapeDtypeStruct((T, 3 * D), BF)],
        compiler_params=_params("arbitrary"),
    )(x, x, g, win, cw, wout)


def mixc_bwd(x, dy, z, g, win_t, cw, wout_t):
    T, D = x.shape
    tm = _tile(T, TM_CONV_BWD, HALO)
    nt = T // tm

    def body(x_ref, dy_ref, dyn_ref, z_ref, zp_ref, zn_ref, g_ref, wint_ref, cw_ref, woutt_ref,
             dx_ref, hn_ref, dz_ref, v_ref, dcw_ref, dg_ref):
        i = pl.program_id(0)
        first = i == 0
        xh, r = _rms(x_ref[...])
        hn_ref[...] = (xh * g_ref[...]).astype(BF)
        zv = z_ref[...].astype(F32)
        gb, gc, hh = zv[:, :D], zv[:, D:2 * D], zv[:, 2 * D:]
        u = gc * hh
        zp = zp_ref[...].astype(F32)
        u_prev = zp[:, D:2 * D] * zp[:, 2 * D:] * jnp.where(first, 0.0, 1.0)
        conv, u1, u2 = _conv_taps(u_prev, u, cw_ref)
        v_ref[...] = (gb * conv).astype(BF)

        dv = _dot(dy_ref[...].astype(BF), woutt_ref[...])
        dconv = dv * gb
        dv_next = _dot(dyn_ref[...].astype(BF), woutt_ref[...])
        dconv_next = dv_next * zn_ref[:, :D].astype(F32) * jnp.where(i == nt - 1, 0.0, 1.0)
        de = jnp.concatenate([dconv, dconv_next], axis=0)
        n = tm + HALO
        du = (cw_ref[2:3, :] * dconv + cw_ref[1:2, :] * pltpu.roll(de, n - 1, 0)[:tm, :]
              + cw_ref[0:1, :] * pltpu.roll(de, n - 2, 0)[:tm, :])
        for tap, shifted in enumerate((u2, u1, u)):
            _accumulate(dcw_ref.at[tap:tap + 1, :], first, _colsum(dconv * shifted))
        dz = jnp.concatenate([dv * conv, du * hh, du * gc], axis=1).astype(BF)
        dz_ref[...] = dz
        dhn = _dot(dz, wint_ref[...])
        _accumulate(dg_ref, first, _colsum(dhn * xh))
        dx_ref[...] = dy_ref[...] + _rms_bwd(dhn * g_ref[...], xh, r)

    tok = lambda w: pl.BlockSpec((tm, w), lambda i: (i, 0))
    whole = lambda arr: pl.BlockSpec(arr.shape, lambda i: (0,) * arr.ndim)
    return pl.pallas_call(
        body, name="mixc_bwd", grid=(nt,),
        in_specs=[tok(D), tok(D), pl.BlockSpec((HALO, D), _next_halo(tm, T)), tok(3 * D),
                  pl.BlockSpec((HALO, 3 * D), _prev_halo(tm)), pl.BlockSpec((HALO, 3 * D), _next_halo(tm, T)),
                  whole(g), whole(win_t), whole(cw), whole(wout_t)],
        out_specs=[tok(D), tok(D), tok(3 * D), tok(D), pl.BlockSpec((3, D), lambda i: (0, 0)),
                   pl.BlockSpec((1, D), lambda i: (0, 0))],
        out_shape=[jax.ShapeDtypeStruct((T, D), F32), jax.ShapeDtypeStruct((T, D), BF),
                   jax.ShapeDtypeStruct((T, 3 * D), BF), jax.ShapeDtypeStruct((T, D), BF),
                   jax.ShapeDtypeStruct((3, D), F32), jax.ShapeDtypeStruct((1, D), F32)],
        compiler_params=_params("arbitrary"),
    )(x, dy, dy, z, z, z, g, win_t, cw, wout_t)


def loss_head(y, target):
    T, D = y.shape
    tm = _tile(T, TM_MM, 8)

    def body(y_ref, t_ref, sum_ref, dy_ref):
        err = y_ref[...] - t_ref[...]
        dy_ref[...] = err * (1.0 / D)
        part = jnp.sum(jnp.sum(err * err, axis=-1, keepdims=True) * (1.0 / D), axis=0, keepdims=True)
        _accumulate(sum_ref, pl.program_id(0) == 0, jnp.broadcast_to(part, sum_ref.shape))

    return pl.pallas_call(
        body, name="loss_head", grid=(T // tm,),
        in_specs=[pl.BlockSpec((tm, D), lambda i: (i, 0))] * 2,
        out_specs=[pl.BlockSpec((8, 128), lambda i: (0, 0)), pl.BlockSpec((tm, D), lambda i: (i, 0))],
        out_shape=[jax.ShapeDtypeStruct((8, 128), F32), jax.ShapeDtypeStruct((T, D), F32)],
        compiler_params=_params("arbitrary"),
    )(y, target)


def adamw(w, g, m, v):
    R, C = w.shape
    tr = _tile(R, TR_FLAT, 8)

    def body(w_ref, g_ref, m_ref, v_ref, d_ref, m2_ref, v2_ref):
        gv = g_ref[...]
        m2 = ADAM_B1 * m_ref[...] + (1.0 - ADAM_B1) * gv
        v2 = ADAM_B2 * v_ref[...] + (1.0 - ADAM_B2) * (gv * gv)
        m2_ref[...] = m2
        v2_ref[...] = v2
        m_hat = m2 / (1.0 - ADAM_B1 ** ADAM_STEP)
        v_hat = v2 / (1.0 - ADAM_B2 ** ADAM_STEP)
        d_ref[...] = -ADAM_LR * (m_hat / (jnp.sqrt(v_hat) + ADAM_EPS) + ADAM_WD * w_ref[...])

    spec = pl.BlockSpec((tr, C), lambda i: (i, 0))
    return pl.pallas_call(
        body, name="adamw", grid=(R // tr,), in_specs=[spec] * 4, out_specs=[spec] * 3,
        out_shape=[jax.ShapeDtypeStruct((R, C), F32)] * 3,
        compiler_params=_params("arbitrary"),
    )(w, g, m, v)


def sum_slots(a):
    S, R, C = a.shape
    tr = _tile(R, TR_FLAT // 2, 16)

    def body(a_ref, o_ref):
        acc = a_ref[0].astype(F32)
        for s in range(1, S):
            acc = acc + a_ref[s].astype(F32)
        o_ref[...] = acc

    return pl.pallas_call(
        body, name="sum_slots", grid=(R // tr,),
        in_specs=[pl.BlockSpec((S, tr, C), lambda i: (0, i, 0))],
        out_specs=pl.BlockSpec((tr, C), lambda i: (i, 0)),
        out_shape=jax.ShapeDtypeStruct((R, C), F32),
        compiler_params=_params("arbitrary"),
    )(a)


ANY = pl.BlockSpec(memory_space=pl.ANY)


def _place():
    return lax.axis_index("x"), lax.axis_index("y"), lax.axis_index("c")


def allgather_shards(w):
    R, C = w.shape
    half = R // 2

    def body(w_ref, out_ref, send_sems, recv_sems, local_sem):
        x, y, c = _place()
        sibling = (x, y, 1 - c)
        chips = [(1 - x, y), (x, 1 - y), (1 - x, 1 - y)]

        def rows(px, py, pc):
            return out_ref.at[2 * px + py, pl.ds(pc * half, half), :]

        def copy(k, block, to, src=None):
            return pltpu.make_async_remote_copy(
                src_ref=rows(*block) if src is None else src, dst_ref=rows(*block),
                send_sem=send_sems.at[k], recv_sem=recv_sems.at[k], device_id=to, device_id_type=MESH)

        mine = pltpu.make_async_copy(w_ref, out_ref.at[2 * x + y], local_sem)
        mine.start()
        first = [copy(j, (x, y, c), (*chip, c), src=w_ref.at[pl.ds(c * half, half), :])
                 for j, chip in enumerate(chips)]
        for cp in first:
            cp.start()
        passed = [copy(3 + j, (*chip, c), sibling) for j, chip in enumerate(chips)]
        for j, chip in enumerate(chips):
            copy(j, (*chip, c), (x, y, c)).wait_recv()
            passed[j].start()
        for j, chip in enumerate(chips):
            copy(3 + j, (*chip, 1 - c), (x, y, c)).wait_recv()
        for cp in first + passed:
            cp.wait_send()
        mine.wait()

    return pl.pallas_call(
        body, name="allgather_shards", in_specs=[ANY], out_specs=ANY,
        out_shape=jax.ShapeDtypeStruct((4, R, C), w.dtype),
        scratch_shapes=[pltpu.SemaphoreType.DMA((6,)), pltpu.SemaphoreType.DMA((6,)), pltpu.SemaphoreType.DMA],
    )(w)


def exchange_partials(grads, small):
    _, R, C = grads.shape
    half = R // 2
    Rs = small.shape[0]

    def body(g_ref, s_ref, land_ref, sland_ref, send_sems, recv_sems, local_sems):
        x, y, c = _place()
        me = 4 * x + 2 * y + c
        peers = []
        for mask in range(1, 8):
            mx, my, mc = (mask >> 2) & 1, (mask >> 1) & 1, mask & 1
            peers.append(((1 - x) if mx else x, (1 - y) if my else y, (1 - c) if mc else c))

        def piece(px, py, pc):
            return g_ref.at[2 * px + py, pl.ds(pc * half, half), :]

        def big(k, sender, to):
            return pltpu.make_async_remote_copy(
                src_ref=piece(*to), dst_ref=land_ref.at[sender], send_sem=send_sems.at[k], recv_sem=recv_sems.at[k],
                device_id=to, device_id_type=MESH)

        def little(k, sender, to):
            return pltpu.make_async_remote_copy(
                src_ref=s_ref, dst_ref=sland_ref.at[sender], send_sem=send_sems.at[7 + k],
                recv_sem=recv_sems.at[7 + k], device_id=to, device_id_type=MESH)

        own_big = pltpu.make_async_copy(piece(x, y, c), land_ref.at[me], local_sems.at[0])
        own_small = pltpu.make_async_copy(s_ref, sland_ref.at[me], local_sems.at[1])
        own_big.start()
        own_small.start()
        sends = []
        for k, peer in enumerate(peers):
            sends += [little(k, me, peer), big(k, me, peer)]
        for cp in sends:
            cp.start()
        for k, (px, py, pc) in enumerate(peers):
            sender = 4 * px + 2 * py + pc
            little(k, sender, (x, y, c)).wait_recv()
            big(k, sender, (x, y, c)).wait_recv()
        for cp in sends:
            cp.wait_send()
        own_big.wait()
        own_small.wait()

    return pl.pallas_call(
        body, name="exchange_partials", in_specs=[ANY, ANY], out_specs=[ANY, ANY],
        out_shape=[jax.ShapeDtypeStruct((8, half, C), grads.dtype), jax.ShapeDtypeStruct((8, Rs, C), small.dtype)],
        scratch_shapes=[pltpu.SemaphoreType.DMA((14,)), pltpu.SemaphoreType.DMA((14,)), pltpu.SemaphoreType.DMA((2,))],
    )(grads, small)


def share_with_sibling(part):
    half, C = part.shape

    def body(p_ref, out_ref, send_sem, recv_sem, local_sem):
        x, y, c = _place()

        def rows(pc):
            return out_ref.at[pl.ds(pc * half, half), :]

        own = pltpu.make_async_copy(p_ref, rows(c), local_sem)
        own.start()
        send = pltpu.make_async_remote_copy(src_ref=p_ref, dst_ref=rows(c), send_sem=send_sem, recv_sem=recv_sem,
                                            device_id=(x, y, 1 - c), device_id_type=MESH)
        send.start()
        pltpu.make_async_remote_copy(src_ref=p_ref, dst_ref=rows(1 - c), send_sem=send_sem, recv_sem=recv_sem,
                                     device_id=(x, y, c), device_id_type=MESH).wait_recv()
        send.wait_send()
        own.wait()

    return pl.pallas_call(
        body, name="share_with_sibling", in_specs=[ANY], out_specs=ANY,
        out_shape=jax.ShapeDtypeStruct((2 * half, C), part.dtype),
        scratch_shapes=[pltpu.SemaphoreType.DMA, pltpu.SemaphoreType.DMA, pltpu.SemaphoreType.DMA],
    )(part)


FLAT_SEG = 16 * FLAT_COLS


def _seg_rows(n):
    return -(-n // FLAT_SEG) * 16


def _flat_rows(sizes):
    rows = sum(_seg_rows(n) for n in sizes)
    return -(-rows // FLAT_ROW_ALIGN) * FLAT_ROW_ALIGN


def pack_flat(arrays, lead=()):
    sizes = [int(np.prod(a.shape[len(lead):])) for a in arrays]
    total = _flat_rows(sizes)
    parts, used = [], 0
    for a, n in zip(arrays, sizes):
        rows = _seg_rows(n)
        flat = a.reshape(*lead, n)
        flat = jnp.pad(flat, [(0, 0)] * len(lead) + [(0, rows * FLAT_COLS - n)])
        parts.append(flat.reshape(*lead, rows, FLAT_COLS))
        used += rows
    if total > used:
        parts.append(jnp.zeros((*lead, total - used, FLAT_COLS), arrays[0].dtype))
    return jnp.concatenate(parts, axis=len(lead))


def unpack_flat(flat, shapes, lead=()):
    out, r0 = [], 0
    for shp in shapes:
        n = int(np.prod(shp))
        rows = _seg_rows(n)
        seg = flat[..., r0:r0 + rows, :].reshape(*lead, rows * FLAT_COLS)[..., :n]
        out.append(seg.reshape(*lead, *shp))
        r0 += rows
    return out


def _f32_bits_as(a, dtype):
    return lax.bitcast_convert_type(a, dtype).reshape(*a.shape[:-1], -1)


def _f32_from_bits(a):
    k = 4 // a.dtype.itemsize
    if k > 1:
        a = a.reshape(*a.shape[:-1], a.shape[-1] // k, k)
    return lax.bitcast_convert_type(a, F32)


def _join_shards(name, a):
    if name in COL_SHARDED:
        return jnp.transpose(a, (1, 2, 0, 3)).reshape(a.shape[1], a.shape[2], 4 * a.shape[3])
    return jnp.transpose(a, (1, 0, 2, 3)).reshape(a.shape[1], 4 * a.shape[2], a.shape[3])


def _split_shards(name, a):
    L, K, N = a.shape
    if name in COL_SHARDED:
        return jnp.transpose(a.reshape(L, K, 4, N // 4), (2, 0, 1, 3))
    return jnp.transpose(a.reshape(L, 4, K // 4, N), (1, 0, 2, 3))


def _pad_heads(a, width):
    a = a.reshape(*a.shape[:-1], HEADS, width)
    a = jnp.pad(a, [(0, 0)] * (a.ndim - 1) + [(0, HEAD_SLOT - width)])
    return a.reshape(*a.shape[:-2], HEADS * HEAD_SLOT)


def _unpad_heads(a, width):
    a = a.reshape(*a.shape[:-1], HEADS, HEAD_SLOT)[..., :width]
    return a.reshape(*a.shape[:-2], HEADS * width)


def _rope_tables(T):
    pos = jnp.arange(T, dtype=F32)
    inv_freq = ROPE_THETA ** (-jnp.arange(0, ROPE_DIM, 2, dtype=F32) / ROPE_DIM)
    ang = pos[:, None] * inv_freq[None, :]
    cos, sin = jnp.cos(ang), jnp.sin(ang)
    pad = HEAD_SLOT - QK_DIM
    cos_t = jnp.concatenate([jnp.ones((T, NOPE_DIM), F32), cos, cos, jnp.zeros((T, pad), F32)], axis=1)
    sin_t = jnp.concatenate([jnp.zeros((T, NOPE_DIM), F32), sin, sin, jnp.zeros((T, pad), F32)], axis=1)
    return cos_t, sin_t


def _even_weights(W, i):
    c3 = POOL_DIM + Q_RANK + KV_RANK
    w_in = W['a_w_in'][i]
    D = w_in.shape[0]
    rope_cols = jnp.concatenate([jnp.zeros((D, NOPE_DIM), BF), w_in[:, c3:], jnp.zeros((D, HEAD_SLOT - QK_DIM), BF)], axis=1)
    win = jnp.concatenate([w_in[:, :c3], rope_cols], axis=1)
    wq = _pad_heads(W['a_w_q_up'][i], QK_DIM)
    kv = W['a_w_kv_up'][i].reshape(KV_RANK, HEADS, NOPE_DIM + V_DIM)
    wkn = _pad_heads(kv[:, :, :NOPE_DIM].reshape(KV_RANK, HEADS * NOPE_DIM), NOPE_DIM)
    wv = _pad_heads(kv[:, :, NOPE_DIM:].reshape(KV_RANK, HEADS * V_DIM), V_DIM)
    w_out = W['a_w_out'][i]
    wo_pool = w_out[:POOL_DIM]
    wo_attn = _pad_heads(w_out[POOL_DIM:].T, V_DIM).T
    wpool = W['a_w_pool'][i]
    pad = lambda a: jnp.pad(a, (0, HEAD_SLOT - QK_DIM))[None, :]
    return dict(win=win, win_t=win.T, wq=wq, wq_t=wq.T, wkn=wkn, wkn_t=wkn.T, wv=wv, wv_t=wv.T,
                wo_pool=wo_pool, wo_pool_t=wo_pool.T, wo_attn=wo_attn, wo_attn_t=wo_attn.T,
                wpool=wpool, wpool_t=jnp.transpose(wpool, (0, 2, 1)),
                qan=W['a_q_a_norm'][i][None, :], kvan=W['a_kv_a_norm'][i][None, :],
                qhn=pad(W['a_q_head_norm'][i]), khn=pad(W['a_k_head_norm'][i]),
                pscale=W['a_pool_scale'][i][None, :], g=W['mix_norm'][2 * i][None, :])


def kernel(*args):
    p = dict(zip(INPUTS, args))
    x0 = p['x'][0]
    target = p['loss_target'][0]
    T, D = x0.shape

    shard_shapes = [p[n].shape for n in SHARDED]

    wire = [_f32_bits_as(p[n], BF) if n == 'c_conv_w' else p[n].astype(BF) for n in SHARDED]
    gathered = allgather_shards(pack_flat(wire))
    W = {}
    for n, a in zip(SHARDED, unpack_flat(gathered, [a.shape for a in wire], lead=(4,))):
        W[n] = _join_shards(n, _f32_from_bits(a) if n == 'c_conv_w' else a)
    for n in REPLICATED:
        W[n] = p[n]
    W['a_w_pool'] = p['a_w_pool'].astype(BF)
    cos, sin = _rope_tables(T)

    def ffn_weights(pre, l):
        wg, wu, wd = W[pre + '_w_gate'][l], W[pre + '_w_up'][l], W[pre + '_w_down'][l]
        return dict(g=W[pre + '_norm'][l][None, :], wg=wg, wu=wu, wd=wd, wg_t=wg.T, wu_t=wu.T, wd_t=wd.T)

    saved = []
    x = x0
    for l in range(DEPTH):
        s = dict(x0=x)
        f1 = ffn_weights('ffn1', l)
        x, s['g1'], s['u1'] = ffn_fwd(x, f1['g'], f1['wg'], f1['wu'], f1['wd'])
        s['x1'] = x
        if l % 2 == 0:
            e = _even_weights(W, l // 2)
            s['q'], s['k'], s['v'], s['po'] = mixa_pre_fwd(
                x, e['g'], e['win'], e['qan'], e['wq'], e['kvan'], e['wkn'], e['wv'], e['qhn'], e['khn'],
                e['wpool'], e['pscale'], cos, sin)
            s['o'], s['lse'] = attn_fwd(s['q'], s['k'], s['v'])
            x = mm_multi([(s['po'], e['wo_pool']), (s['o'], e['wo_attn'])], res=x)
        else:
            i = l // 2
            x, s['z'] = mixc_fwd(x, W['mix_norm'][l][None, :], W['c_w_in'][i], W['c_conv_w'][i].astype(F32),
                                 W['c_w_out'][i])
        s['x2'] = x
        f2 = ffn_weights('ffn2', l)
        x, s['g2'], s['u2'] = ffn_fwd(x, f2['g'], f2['wg'], f2['wu'], f2['wd'])
        saved.append(s)

    loss_sum, dy = loss_head(x, target)
    loss = lax.psum(0.5 * loss_sum[0, 0], AXES)

    G = {n: [None] * p[n].shape[0] for n in WEIGHTS}

    def ffn_back(pre, l, x_in, gg, uu, dy):
        f = ffn_weights(pre, l)
        dx, n, dyh, h, dgate, dup, dgn = ffn_bwd(x_in, dy, f['g'], gg, uu, f['wd_t'], f['wg_t'], f['wu_t'])
        G[pre + '_norm'][l] = dgn[0]
        G[pre + '_w_gate'][l] = mm_tn(n, dgate)
        G[pre + '_w_up'][l] = mm_tn(n, dup)
        G[pre + '_w_down'][l] = mm_tn(h, dyh)
        return dx

    t_attn = _tile(T, TQ_ATTN, 128)
    for l in reversed(range(DEPTH)):
        s = saved[l]
        dy = ffn_back('ffn2', l, s['x2'], s['g2'], s['u2'], dy)
        i = l // 2
        if l % 2 == 0:
            e = _even_weights(W, i)
            G['a_w_out'][i] = jnp.concatenate(
                [mm_tn(s['po'], dy), _unpad_heads(mm_tn(s['o'], dy).T, V_DIM).T], axis=0)
            dpo = mm_multi([(dy, e['wo_pool_t'])])
            do = mm_multi([(dy, e['wo_attn_t'])], out_dtype=BF)
            dq, delta = attn_bwd_dq(s['q'], s['k'], s['v'], s['o'], do, s['lse'])
            as_rows = lambda a: a.reshape(HEADS, T // t_attn, t_attn)
            dk, dv = attn_bwd_dkv(s['q'], s['k'], s['v'], do, as_rows(s['lse']), as_rows(delta))
            (dy, hn, dz, nq, dqraw, nkv, dkraw, pooled, dps, dg, dqan, dkvan, dqhn, dkhn, dpscale) = mixa_pre_bwd(
                s['x1'], dy, dq, dk, dv, dpo, e['g'], e['win'], e['win_t'], e['qan'], e['wq'], e['wq_t'], e['kvan'],
                e['wkn'], e['wkn_t'], e['wv_t'], e['qhn'], e['khn'], e['wpool'], e['wpool_t'], e['pscale'], cos, sin)
            c3 = POOL_DIM + Q_RANK + KV_RANK
            dwin = mm_tn(hn, dz)
            G['a_w_in'][i] = jnp.concatenate([dwin[:, :c3], dwin[:, c3 + NOPE_DIM:c3 + QK_DIM]], axis=1)
            G['a_w_q_up'][i] = _unpad_heads(mm_tn(nq, dqraw), QK_DIM)
            dwkn = _unpad_heads(mm_tn(nkv, dkraw), NOPE_DIM).reshape(KV_RANK, HEADS, NOPE_DIM)
            dwv = _unpad_heads(mm_tn(nkv, dv), V_DIM).reshape(KV_RANK, HEADS, V_DIM)
            G['a_w_kv_up'][i] = jnp.concatenate([dwkn, dwv], axis=2).reshape(KV_RANK, HEADS * (NOPE_DIM + V_DIM))
            dwp = mm_tn(pooled, dps)
            G['a_w_pool'][i] = jnp.stack([dwp[g * POOL_GROUP:(g + 1) * POOL_GROUP, g * POOL_GROUP:(g + 1) * POOL_GROUP]
                                          for g in range(len(POOL_WINDOWS))])
            G['mix_norm'][l] = dg[0]
            G['a_q_a_norm'][i] = dqan[0]
            G['a_kv_a_norm'][i] = dkvan[0]
            G['a_q_head_norm'][i] = dqhn[0, :QK_DIM]
            G['a_k_head_norm'][i] = dkhn[0, :QK_DIM]
            G['a_pool_scale'][i] = dpscale[0]
        else:
            w_in, w_out = W['c_w_in'][i], W['c_w_out'][i]
            dy_in = dy
            dy, hn, dz, gated, dcw, dg = mixc_bwd(s['x1'], dy, s['z'], W['mix_norm'][l][None, :], w_in.T,
                                                  W['c_conv_w'][i].astype(F32), w_out.T)
            G['c_w_in'][i] = mm_tn(hn, dz)
            G['c_w_out'][i] = mm_tn(gated, dy_in)
            G['c_conv_w'][i] = dcw
            G['mix_norm'][l] = dg[0]
        dy = ffn_back('ffn1', l, s['x0'], s['g1'], s['u1'], dy)
    grad_x = dy[None]

    G = {n: jnp.stack(v) for n, v in G.items()}
    partial_big = pack_flat([_split_shards(n, G[n]) for n in SHARDED], lead=(4,)).astype(BF)
    small_shapes = [p[n].shape for n in REPLICATED]
    partial_small = pack_flat([G[n] for n in REPLICATED])
    land, sland = exchange_partials(partial_big, partial_small)
    g_big = share_with_sibling(sum_slots(land))
    g_small = sum_slots(sland)

    outs = {}
    for names, shapes, g_flat in ((SHARDED, shard_shapes, g_big), (REPLICATED, small_shapes, g_small)):
        flat = lambda pre: pack_flat([p[pre + n] for n in names])
        delta, m2, v2 = adamw(flat(''), g_flat, flat('m_'), flat('v_'))
        for kind, arr in (('grad_', g_flat), ('delta_', delta), ('new_m_', m2), ('new_v_', v2)):
            for n, a in zip(names, unpack_flat(arr, shapes)):
                outs[kind + n] = a
    return (loss, grad_x, *[outs[k + n] for k in ('grad_', 'delta_', 'new_m_', 'new_v_') for n in WEIGHTS])
```

```python
import functools

import numpy as np
import jax
import jax.numpy as jnp
from jax import lax
from jax.experimental import pallas as pl
from jax.experimental.pallas import tpu as pltpu

BF, F32 = jnp.bfloat16, jnp.float32
MESH = pl.DeviceIdType.MESH
AXES = ("x", "y", "c")

NORM_EPS = 1e-6
DEPTH = 4
HEADS = 8
HEAD_SLOT = 128
QK_DIM, NOPE_DIM, ROPE_DIM, V_DIM = 96, 64, 32, 64
POOL_WINDOWS = (2, 4, 8, 16)
POOL_DIM, POOL_GROUP = 512, 128
Q_RANK, KV_RANK = 384, 256
ROPE_THETA = 10000.0
HALO = 16
ATTN_SCALE = QK_DIM ** -0.5

ADAM_LR, ADAM_B1, ADAM_B2, ADAM_EPS, ADAM_WD, ADAM_STEP = 0.001, 0.9, 0.999, 1e-08, 0.01, 10

TM_FFN, TF_FFN = 1024, 256
TM_MIX_FWD, TM_MIX_BWD = 512, 256
TM_CONV_FWD, TM_CONV_BWD = 256, 256
TQ_ATTN = 512
TM_MM = 512
TK_TN, BM_TN, BN_TN = 2048, 1024, 1536
FLAT_COLS = 1024
FLAT_ROW_ALIGN = 1024
SIBLING_CHUNKS = 16
GATHER_CHUNKS = 8
TR_FLAT = 256
VMEM_LIMIT = 56 * 1024 * 1024

WEIGHTS = ['ffn1_norm', 'ffn1_w_gate', 'ffn1_w_up', 'ffn1_w_down', 'mix_norm', 'ffn2_norm', 'ffn2_w_gate',
           'ffn2_w_up', 'ffn2_w_down', 'a_w_in', 'a_q_a_norm', 'a_w_q_up', 'a_kv_a_norm', 'a_w_kv_up',
           'a_q_head_norm', 'a_k_head_norm', 'a_w_pool', 'a_pool_scale', 'a_w_out', 'c_w_in', 'c_conv_w',
           'c_w_out']
COL_SHARDED = ('ffn1_w_gate', 'ffn1_w_up', 'ffn2_w_gate', 'ffn2_w_up', 'a_w_in', 'a_w_q_up', 'a_w_kv_up',
               'c_w_in', 'c_conv_w')
ROW_SHARDED = ('ffn1_w_down', 'ffn2_w_down', 'a_w_out', 'c_w_out')
SHARDED = tuple(n for n in WEIGHTS if n in COL_SHARDED or n in ROW_SHARDED)
REPLICATED = tuple(n for n in WEIGHTS if n not in SHARDED)
INPUTS = ['x'] + WEIGHTS + ['loss_target'] + ['m_' + n for n in WEIGHTS] + ['v_' + n for n in WEIGHTS]


def _dot(a, b):
    return jnp.dot(a, b, preferred_element_type=F32)


def _dot_nt(a, b):
    return lax.dot_general(a, b, (((1,), (1,)), ((), ())), preferred_element_type=F32)


def _dot_tn(a, b):
    return lax.dot_general(a, b, (((0,), (0,)), ((), ())), preferred_element_type=F32)


def _params(*sem):
    return pltpu.CompilerParams(dimension_semantics=sem or None, vmem_limit_bytes=VMEM_LIMIT)


def _tile(n, cap, unit):
    if n <= cap:
        return n
    best = None
    for t in range(unit, cap + 1, unit):
        if n % t == 0:
            best = t
    assert best is not None, (n, cap, unit)
    return best


def _rms(x, width=None):
    ms = jnp.sum(x * x, axis=-1, keepdims=True) * (1.0 / (width or x.shape[-1]))
    r = lax.rsqrt(ms + NORM_EPS)
    return x * r, r


def _rms_bwd(a, xhat, r, width=None):
    return r * (a - xhat * (jnp.sum(a * xhat, axis=-1, keepdims=True) * (1.0 / (width or a.shape[-1]))))


def _colsum(a):
    return jnp.sum(a, axis=0, keepdims=True)


def _accumulate(ref, first, value):
    @pl.when(first)
    def _():
        ref[...] = value

    @pl.when(jnp.logical_not(first))
    def _():
        ref[...] += value


def _rot_half(v):
    lane = lax.broadcasted_iota(jnp.int32, v.shape, 1)
    rot = jnp.where(lane < NOPE_DIM + ROPE_DIM // 2, -pltpu.roll(v, HEAD_SLOT - ROPE_DIM // 2, 1),
                    pltpu.roll(v, ROPE_DIM // 2, 1))
    return jnp.where((lane >= NOPE_DIM) & (lane < QK_DIM), rot, 0.0)


def _rope(v, cos, sin):
    return v * cos + _rot_half(v) * sin


def _rope_bwd(d, cos, sin):
    return d * cos - _rot_half(d * sin)


def ffn_fwd(x, g, wg, wu, wd):
    T, D = x.shape
    F = wg.shape[1]
    tm, tf = _tile(T, TM_FFN, 8), _tile(F, TF_FFN, 128)
    nf = F // tf

    def body(x_ref, g_ref, wg_ref, wu_ref, wd_ref, y_ref, gg_ref, uu_ref, n_sc, acc):
        f = pl.program_id(1)

        @pl.when(f == 0)
        def _():
            xh, _ = _rms(x_ref[...])
            n_sc[...] = (xh * g_ref[...]).astype(BF)
            acc[...] = jnp.zeros_like(acc)

        n = n_sc[...]
        gg = _dot(n, wg_ref[...])
        uu = _dot(n, wu_ref[...])
        gg_ref[...] = gg.astype(BF)
        uu_ref[...] = uu.astype(BF)
        h = gg * jax.nn.sigmoid(gg) * uu
        acc[...] += _dot(h.astype(BF), wd_ref[...])

        @pl.when(f == nf - 1)
        def _():
            y_ref[...] = x_ref[...] + 0.5 * acc[...]

    return pl.pallas_call(
        body, name="ffn_fwd", grid=(T // tm, nf),
        in_specs=[pl.BlockSpec((tm, D), lambda i, f: (i, 0)), pl.BlockSpec((1, D), lambda i, f: (0, 0)),
                  pl.BlockSpec((D, tf), lambda i, f: (0, f)), pl.BlockSpec((D, tf), lambda i, f: (0, f)),
                  pl.BlockSpec((tf, D), lambda i, f: (f, 0))],
        out_specs=[pl.BlockSpec((tm, D), lambda i, f: (i, 0)), pl.BlockSpec((tm, tf), lambda i, f: (i, f)),
                   pl.BlockSpec((tm, tf), lambda i, f: (i, f))],
        out_shape=[jax.ShapeDtypeStruct((T, D), F32), jax.ShapeDtypeStruct((T, F), BF),
                   jax.ShapeDtypeStruct((T, F), BF)],
        scratch_shapes=[pltpu.VMEM((tm, D), BF), pltpu.VMEM((tm, D), F32)],
        compiler_params=_params("arbitrary", "arbitrary"),
    )(x, g, wg, wu, wd)


def ffn_bwd(x, dy, g, gg, uu, wd_t, wg_t, wu_t):
    T, D = x.shape
    F = gg.shape[1]
    tm, tf = _tile(T, TM_FFN, 8), _tile(F, TF_FFN, 128)
    nf = F // tf

    def body(x_ref, dy_ref, g_ref, gg_ref, uu_ref, wdt_ref, wgt_ref, wut_ref,
             dx_ref, n_ref, dyh_ref, h_ref, dg_ref, du_ref, dgn_ref, acc):
        i, f = pl.program_id(0), pl.program_id(1)

        @pl.when(f == 0)
        def _():
            xh, _ = _rms(x_ref[...])
            n_ref[...] = (xh * g_ref[...]).astype(BF)
            dyh_ref[...] = (0.5 * dy_ref[...]).astype(BF)
            acc[...] = jnp.zeros_like(acc)

        dh = _dot(dyh_ref[...], wdt_ref[...])
        gv = gg_ref[...].astype(F32)
        uv = uu_ref[...].astype(F32)
        sg = jax.nn.sigmoid(gv)
        silu = gv * sg
        h_ref[...] = (silu * uv).astype(BF)
        d_up = (dh * silu).astype(BF)
        d_gate = (dh * uv * (sg * (1.0 + gv * (1.0 - sg)))).astype(BF)
        du_ref[...] = d_up
        dg_ref[...] = d_gate
        acc[...] += _dot(d_gate, wgt_ref[...]) + _dot(d_up, wut_ref[...])

        @pl.when(f == nf - 1)
        def _():
            xh, r = _rms(x_ref[...])
            dn = acc[...]
            dx_ref[...] = dy_ref[...] + _rms_bwd(dn * g_ref[...], xh, r)
            _accumulate(dgn_ref, i == 0, _colsum(dn * xh))

    tok = lambda i, f: (i, 0)
    chunk = lambda i, f: (i, f)
    return pl.pallas_call(
        body, name="ffn_bwd", grid=(T // tm, nf),
        in_specs=[pl.BlockSpec((tm, D), tok), pl.BlockSpec((tm, D), tok), pl.BlockSpec((1, D), lambda i, f: (0, 0)),
                  pl.BlockSpec((tm, tf), chunk), pl.BlockSpec((tm, tf), chunk),
                  pl.BlockSpec((D, tf), lambda i, f: (0, f)), pl.BlockSpec((tf, D), lambda i, f: (f, 0)),
                  pl.BlockSpec((tf, D), lambda i, f: (f, 0))],
        out_specs=[pl.BlockSpec((tm, D), tok), pl.BlockSpec((tm, D), tok), pl.BlockSpec((tm, D), tok),
                   pl.BlockSpec((tm, tf), chunk), pl.BlockSpec((tm, tf), chunk), pl.BlockSpec((tm, tf), chunk),
                   pl.BlockSpec((1, D), lambda i, f: (0, 0))],
        out_shape=[jax.ShapeDtypeStruct((T, D), F32), jax.ShapeDtypeStruct((T, D), BF),
                   jax.ShapeDtypeStruct((T, D), BF), jax.ShapeDtypeStruct((T, F), BF),
                   jax.ShapeDtypeStruct((T, F), BF), jax.ShapeDtypeStruct((T, F), BF),
                   jax.ShapeDtypeStruct((1, D), F32)],
        scratch_shapes=[pltpu.VMEM((tm, D), F32)],
        compiler_params=_params("arbitrary", "arbitrary"),
    )(x, dy, g, gg, uu, wd_t, wg_t, wu_t)


def mm_tn(a, b):
    T, M = a.shape
    N = b.shape[1]
    tk, bm, bn = _tile(T, TK_TN, 16), _tile(M, BM_TN, 128), _tile(N, BN_TN, 128)

    def body(a_ref, b_ref, o_ref):
        part = _dot_tn(a_ref[...].astype(BF), b_ref[...].astype(BF))
        _accumulate(o_ref, pl.program_id(2) == 0, part)

    return pl.pallas_call(
        body, name="mm_tn", grid=(M // bm, N // bn, T // tk),
        in_specs=[pl.BlockSpec((tk, bm), lambda i, j, k: (k, i)), pl.BlockSpec((tk, bn), lambda i, j, k: (k, j))],
        out_specs=pl.BlockSpec((bm, bn), lambda i, j, k: (i, j)),
        out_shape=jax.ShapeDtypeStruct((M, N), F32),
        compiler_params=_params("arbitrary", "arbitrary", "arbitrary"),
    )(a, b)


def mm_multi(pairs, res=None, out_dtype=F32):
    T = pairs[0][0].shape[0]
    N = pairs[0][1].shape[1]
    tm = _tile(T, TM_MM, 16)
    n = len(pairs)

    def body(*refs):
        o_ref = refs[-1]
        acc = refs[2 * n][...] if res is not None else None
        for k in range(n):
            part = _dot(refs[k][...].astype(BF), refs[n + k][...])
            acc = part if acc is None else acc + part
        o_ref[...] = acc.astype(out_dtype)

    ins = [a for a, _ in pairs] + [w for _, w in pairs]
    specs = [pl.BlockSpec((tm, a.shape[1]), lambda i: (i, 0)) for a, _ in pairs]
    specs += [pl.BlockSpec(w.shape, lambda i: (0, 0)) for _, w in pairs]
    if res is not None:
        ins.append(res)
        specs.append(pl.BlockSpec((tm, N), lambda i: (i, 0)))
    return pl.pallas_call(
        body, name="mm_multi", grid=(T // tm,), in_specs=specs,
        out_specs=pl.BlockSpec((tm, N), lambda i: (i, 0)),
        out_shape=jax.ShapeDtypeStruct((T, N), out_dtype),
        compiler_params=_params("arbitrary"),
    )(*ins)


def _causal_mask(t, q_major):
    r = lax.broadcasted_iota(jnp.int32, (t, t), 0)
    c = lax.broadcasted_iota(jnp.int32, (t, t), 1)
    return (c <= r) if q_major else (r <= c)


def attn_fwd(q, k, v):
    T = q.shape[0]
    t = _tile(T, TQ_ATTN, 128)
    nq = T // t

    def body(q_ref, k_ref, v_ref, o_ref, lse_ref):
        i = pl.program_id(1)
        qv = q_ref[...]

        def block(j):
            return pl.ds(pl.multiple_of(j * t, t), t)

        def update(s, j, m, l, acc):
            m2 = jnp.maximum(m, jnp.max(s, axis=-1, keepdims=True))
            p = jnp.exp(s - m2)
            scale = jnp.exp(m - m2)
            return (m2, scale * l + jnp.sum(p, axis=-1, keepdims=True),
                    scale * acc + _dot(p.astype(BF), v_ref[block(j), :]))

        def scores(j):
            return _dot_nt(qv, k_ref[block(j), :])

        def diagonal(carry):
            return update(jnp.where(_causal_mask(t, True), scores(i), -jnp.inf), i, *carry)

        def pair(j, carry, last_is_diagonal):
            s0, s1 = scores(j), scores(j + 1)
            if last_is_diagonal:
                s1 = jnp.where(_causal_mask(t, True), s1, -jnp.inf)
            return update(s1, j + 1, *update(s0, j, *carry))

        init = (jnp.full((t, 1), -1e30, F32), jnp.zeros((t, 1), F32), jnp.zeros((t, HEAD_SLOT), F32))
        carry = lax.fori_loop(0, i // 2, lambda jj, c: pair(2 * jj, c, False), init)
        m, l, acc = lax.cond(i % 2 == 1, lambda c: pair(i - 1, c, True), diagonal, carry)
        o_ref[...] = (acc / l).astype(BF)
        lse_ref[...] = m + jnp.log(l)

    return pl.pallas_call(
        body, name="attn_fwd", grid=(HEADS, nq),
        in_specs=[pl.BlockSpec((t, HEAD_SLOT), lambda h, i: (i, h)), pl.BlockSpec((T, HEAD_SLOT), lambda h, i: (0, h)),
                  pl.BlockSpec((T, HEAD_SLOT), lambda h, i: (0, h))],
        out_specs=[pl.BlockSpec((t, HEAD_SLOT), lambda h, i: (i, h)), pl.BlockSpec((None, t, 1), lambda h, i: (h, i, 0))],
        out_shape=[jax.ShapeDtypeStruct((T, HEADS * HEAD_SLOT), BF), jax.ShapeDtypeStruct((HEADS, T, 1), F32)],
        compiler_params=_params("arbitrary", "arbitrary"),
    )(q, k, v)


def attn_bwd_dq(q, k, v, o, do, lse):
    T = q.shape[0]
    t = _tile(T, TQ_ATTN, 128)
    nq = T // t

    def body(q_ref, k_ref, v_ref, o_ref, do_ref, lse_ref, dq_ref, delta_ref):
        i = pl.program_id(1)
        qv, dov, lsev = q_ref[...], do_ref[...], lse_ref[...]
        delta = jnp.sum(dov.astype(F32) * o_ref[...].astype(F32), axis=-1, keepdims=True)
        delta_ref[...] = delta

        def block(j):
            return pl.ds(pl.multiple_of(j * t, t), t)

        def scores(j, masked):
            s = _dot_nt(qv, k_ref[block(j), :])
            return jnp.where(_causal_mask(t, True), s, -jnp.inf) if masked else s

        def contribution(s, j):
            p = jnp.exp(s - lsev)
            ds = p * (_dot_nt(dov, v_ref[block(j), :]) - delta)
            return _dot(ds.astype(BF), k_ref[block(j), :])

        def pair(j, dq, last_is_diagonal):
            s0, s1 = scores(j, False), scores(j + 1, last_is_diagonal)
            return dq + contribution(s0, j) + contribution(s1, j + 1)

        dq = lax.fori_loop(0, i // 2, lambda jj, c: pair(2 * jj, c, False), jnp.zeros((t, HEAD_SLOT), F32))
        dq_ref[...] = lax.cond(i % 2 == 1, lambda c: pair(i - 1, c, True),
                               lambda c: c + contribution(scores(i, True), i), dq)

    blk = pl.BlockSpec((t, HEAD_SLOT), lambda h, i: (i, h))
    full = pl.BlockSpec((T, HEAD_SLOT), lambda h, i: (0, h))
    col = pl.BlockSpec((None, t, 1), lambda h, i: (h, i, 0))
    return pl.pallas_call(
        body, name="attn_bwd_dq", grid=(HEADS, nq),
        in_specs=[blk, full, full, blk, blk, col], out_specs=[blk, col],
        out_shape=[jax.ShapeDtypeStruct((T, HEADS * HEAD_SLOT), F32), jax.ShapeDtypeStruct((HEADS, T, 1), F32)],
        compiler_params=_params("arbitrary", "arbitrary"),
    )(q, k, v, o, do, lse)


def attn_bwd_dkv(q, k, v, do, lse_rows, delta_rows):
    T = q.shape[0]
    t = _tile(T, TQ_ATTN, 128)
    nq = T // t

    def body(q_ref, k_ref, v_ref, do_ref, lse_ref, delta_ref, dk_ref, dv_ref):
        j = pl.program_id(1)
        kv, vv = k_ref[...], v_ref[...]

        def block(i):
            return pl.ds(pl.multiple_of(i * t, t), t)

        def scores(i, masked):
            st = _dot_nt(kv, q_ref[block(i), :])
            return jnp.where(_causal_mask(t, False), st, -jnp.inf) if masked else st

        def add(carry, st, i):
            dk, dv = carry
            qv, dov = q_ref[block(i), :], do_ref[block(i), :]
            pt = jnp.exp(st - lse_ref[pl.ds(i, 1), :])
            dst = pt * (_dot_nt(vv, dov) - delta_ref[pl.ds(i, 1), :])
            return dk + _dot(dst.astype(BF), qv), dv + _dot(pt.astype(BF), dov)

        def pair(i, carry):
            s0, s1 = scores(i, False), scores(i + 1, False)
            return add(add(carry, s0, i), s1, i + 1)

        zero = jnp.zeros((t, HEAD_SLOT), F32)
        carry = add((zero, zero), scores(j, True), j)
        rest = nq - 1 - j
        carry = lax.fori_loop(0, rest // 2, lambda ii, c: pair(j + 1 + 2 * ii, c), carry)
        dk, dv = lax.cond(rest % 2 == 1, lambda c: add(c, scores(nq - 1, False), nq - 1), lambda c: c, carry)
        dk_ref[...] = dk
        dv_ref[...] = dv

    blk = pl.BlockSpec((t, HEAD_SLOT), lambda h, j: (j, h))
    full = pl.BlockSpec((T, HEAD_SLOT), lambda h, j: (0, h))
    rows = pl.BlockSpec((None, nq, t), lambda h, j: (h, 0, 0))
    return pl.pallas_call(
        body, name="attn_bwd_dkv", grid=(HEADS, nq),
        in_specs=[full, blk, blk, full, rows, rows], out_specs=[blk, blk],
        out_shape=[jax.ShapeDtypeStruct((T, HEADS * HEAD_SLOT), F32)] * 2,
        compiler_params=_params("arbitrary", "arbitrary"),
    )(q, k, v, do, lse_rows, delta_rows)


def _prev_halo(tm):
    return lambda i: (jnp.maximum(i * (tm // HALO) - 1, 0), 0)


def _next_halo(tm, T):
    return lambda i: (jnp.minimum((i + 1) * (tm // HALO), T // HALO - 1), 0)


def _inv_count(row0, n, w):
    t = row0 + lax.broadcasted_iota(jnp.int32, (n, 1), 0)
    return 1.0 / jnp.minimum(t + 1, w).astype(F32)


def _pool_fwd(u_prev, u, row0):
    tm = u.shape[0]
    out = []
    for g, w in enumerate(POOL_WINDOWS):
        lanes = slice(g * POOL_GROUP, (g + 1) * POOL_GROUP)
        ue = jnp.concatenate([u_prev[:, lanes], u[:, lanes]], axis=0)
        s, step = ue, 1
        while step < w:
            s = s + pltpu.roll(s, step, 0)
            step *= 2
        out.append(s[HALO:, :] * _inv_count(row0, tm, w) - u[:, lanes])
    return out


def _pool_bwd(dp, dp_next, row0):
    tm = dp[0].shape[0]
    out = []
    for g, w in enumerate(POOL_WINDOWS):
        e = jnp.concatenate([dp[g] * _inv_count(row0, tm, w), dp_next[g] * (1.0 / w)], axis=0)
        n = tm + HALO
        s, step = e, 1
        while step < w:
            s = s + pltpu.roll(s, n - step, 0)
            step *= 2
        out.append(s[:tm, :] - dp[g])
    return out


def _mixa_front(x, xp, first, row0, g_ref, win_ref, qan_ref, wq_ref, kvan_ref, wkn_ref):
    xh, r = _rms(x)
    hn = (xh * g_ref[...]).astype(BF)
    z = _dot(hn, win_ref[...])
    xph, _ = _rms(xp)
    u_prev = _dot((xph * g_ref[...]).astype(BF), win_ref[:, :POOL_DIM]) * jnp.where(first, 0.0, 1.0)
    u = z[:, :POOL_DIM]
    pooled = _pool_fwd(u_prev, u, row0)
    c1, c2 = POOL_DIM + Q_RANK, POOL_DIM + Q_RANK + KV_RANK
    qh, rq = _rms(z[:, POOL_DIM:c1])
    nq = (qh * qan_ref[...]).astype(BF)
    kh, rk = _rms(z[:, c1:c2])
    nkv = (kh * kvan_ref[...]).astype(BF)
    qraw = _dot(nq, wq_ref[...])
    kraw = _dot(nkv, wkn_ref[...])
    krope = z[:, c2:c2 + HEAD_SLOT]
    return dict(xh=xh, r=r, hn=hn, pooled=pooled, qh=qh, rq=rq, nq=nq, kh=kh, rk=rk, nkv=nkv,
                qraw=qraw, kraw=kraw, krope=krope)


def mixa_pre_fwd(x, g, win, qan, wq, kvan, wkn, wv, qhn, khn, wpool, pscale, cos, sin):
    T, D = x.shape
    tm = _tile(T, TM_MIX_FWD, HALO)
    HS = HEADS * HEAD_SLOT

    def body(x_ref, xp_ref, g_ref, win_ref, qan_ref, wq_ref, kvan_ref, wkn_ref, wv_ref, qhn_ref, khn_ref,
             wpool_ref, pscale_ref, cos_ref, sin_ref, q_ref, k_ref, v_ref, po_ref):
        i = pl.program_id(0)
        a = _mixa_front(x_ref[...], xp_ref[...], i == 0, i * tm, g_ref, win_ref, qan_ref, wq_ref, kvan_ref, wkn_ref)
        for gi in range(len(POOL_WINDOWS)):
            lanes = slice(gi * POOL_GROUP, (gi + 1) * POOL_GROUP)
            po = _dot(a["pooled"][gi].astype(BF), wpool_ref[gi]) * pscale_ref[:, lanes]
            po_ref[:, lanes] = po.astype(BF)
        cosv, sinv = cos_ref[...], sin_ref[...]
        v_ref[...] = _dot(a["nkv"], wv_ref[...]).astype(BF)
        for h in range(HEADS):
            lanes = slice(h * HEAD_SLOT, (h + 1) * HEAD_SLOT)
            qn, _ = _rms(a["qraw"][:, lanes], QK_DIM)
            q_ref[:, lanes] = (_rope(qn * qhn_ref[...], cosv, sinv) * ATTN_SCALE).astype(BF)
            kn, _ = _rms(a["kraw"][:, lanes] + a["krope"], QK_DIM)
            k_ref[:, lanes] = _rope(kn * khn_ref[...], cosv, sinv).astype(BF)

    tok = lambda w: pl.BlockSpec((tm, w), lambda i: (i, 0))
    whole = lambda arr: pl.BlockSpec(arr.shape, lambda i: (0,) * arr.ndim)
    return pl.pallas_call(
        body, name="mixa_pre_fwd", grid=(T // tm,),
        in_specs=[tok(D), pl.BlockSpec((HALO, D), _prev_halo(tm))] + [whole(a) for a in
                  (g, win, qan, wq, kvan, wkn, wv, qhn, khn, wpool, pscale)] + [tok(HEAD_SLOT), tok(HEAD_SLOT)],
        out_specs=[tok(HS), tok(HS), tok(HS), tok(POOL_DIM)],
        out_shape=[jax.ShapeDtypeStruct((T, HS), BF)] * 3 + [jax.ShapeDtypeStruct((T, POOL_DIM), BF)],
        compiler_params=_params("arbitrary"),
    )(x, x, g, win, qan, wq, kvan, wkn, wv, qhn, khn, wpool, pscale, cos, sin)


def mixa_pre_bwd(x, dy, dq, dk, dv, dpo, g, win, win_t, qan, wq, wq_t, kvan, wkn, wkn_t, wv_t, qhn, khn,
                 wpool, wpool_t, pscale, cos, sin):
    T, D = x.shape
    tm = _tile(T, TM_MIX_BWD, HALO)
    HS = HEADS * HEAD_SLOT
    ZW = win.shape[1]
    nt = T // tm

    def body(x_ref, xp_ref, dy_ref, dq_ref, dk_ref, dv_ref, dpo_ref, dpon_ref, g_ref, win_ref, wint_ref, qan_ref,
             wq_ref, wqt_ref, kvan_ref, wkn_ref, wknt_ref, wvt_ref, qhn_ref, khn_ref, wpool_ref, wpoolt_ref,
             pscale_ref, cos_ref, sin_ref,
             dx_ref, hn_ref, dz_ref, nq_ref, dqraw_ref, nkv_ref, dkraw_ref, pooled_ref, dps_ref,
             dg_ref, dqan_ref, dkvan_ref, dqhn_ref, dkhn_ref, dpscale_ref):
        i = pl.program_id(0)
        first = i == 0
        a = _mixa_front(x_ref[...], xp_ref[...], first, i * tm, g_ref, win_ref, qan_ref, wq_ref, kvan_ref, wkn_ref)
        cosv, sinv = cos_ref[...], sin_ref[...]
        hn_ref[...] = a["hn"]
        nq_ref[...] = a["nq"]
        nkv_ref[...] = a["nkv"]

        has_next = jnp.where(i == nt - 1, 0.0, 1.0)
        dpool, dpool_next, dpscale = [], [], []
        for gi in range(len(POOL_WINDOWS)):
            lanes = slice(gi * POOL_GROUP, (gi + 1) * POOL_GROUP)
            pooled = a["pooled"][gi].astype(BF)
            pooled_ref[:, lanes] = pooled
            dpo_g = dpo_ref[:, lanes]
            dpscale.append(_colsum(dpo_g * _dot(pooled, wpool_ref[gi])))
            dps = (dpo_g * pscale_ref[:, lanes]).astype(BF)
            dps_ref[:, lanes] = dps
            dpool.append(_dot(dps, wpoolt_ref[gi]))
            dps_n = (dpon_ref[:, lanes] * pscale_ref[:, lanes] * has_next).astype(BF)
            dpool_next.append(_dot(dps_n, wpoolt_ref[gi]))
        du = jnp.concatenate(_pool_bwd(dpool, dpool_next, i * tm), axis=1)
        _accumulate(dpscale_ref, first, jnp.concatenate(dpscale, axis=1))

        dqhn = jnp.zeros((1, HEAD_SLOT), F32)
        dkhn = jnp.zeros((1, HEAD_SLOT), F32)
        dkrope = jnp.zeros((tm, HEAD_SLOT), F32)
        for h in range(HEADS):
            lanes = slice(h * HEAD_SLOT, (h + 1) * HEAD_SLOT)
            qhat, rq = _rms(a["qraw"][:, lanes], QK_DIM)
            dqn = _rope_bwd(dq_ref[:, lanes] * ATTN_SCALE, cosv, sinv)
            dqhn = dqhn + _colsum(dqn * qhat)
            dqraw_ref[:, lanes] = _rms_bwd(dqn * qhn_ref[...], qhat, rq, QK_DIM).astype(BF)
            khat, rk = _rms(a["kraw"][:, lanes] + a["krope"], QK_DIM)
            dkn = _rope_bwd(dk_ref[:, lanes], cosv, sinv)
            dkhn = dkhn + _colsum(dkn * khat)
            dkraw = _rms_bwd(dkn * khn_ref[...], khat, rk, QK_DIM)
            dkrope = dkrope + dkraw
            dkraw_ref[:, lanes] = dkraw.astype(BF)
        _accumulate(dqhn_ref, first, dqhn)
        _accumulate(dkhn_ref, first, dkhn)

        dnq = _dot(dqraw_ref[...], wqt_ref[...])
        _accumulate(dqan_ref, first, _colsum(dnq * a["qh"]))
        dql = _rms_bwd(dnq * qan_ref[...], a["qh"], a["rq"])
        dnkv = _dot(dkraw_ref[...], wknt_ref[...]) + _dot(dv_ref[...].astype(BF), wvt_ref[...])
        _accumulate(dkvan_ref, first, _colsum(dnkv * a["kh"]))
        dkvl = _rms_bwd(dnkv * kvan_ref[...], a["kh"], a["rk"])

        dz = jnp.concatenate([du, dql, dkvl, dkrope], axis=1).astype(BF)
        dz_ref[...] = dz
        dhn = _dot(dz, wint_ref[...])
        _accumulate(dg_ref, first, _colsum(dhn * a["xh"]))
        dx_ref[...] = dy_ref[...] + _rms_bwd(dhn * g_ref[...], a["xh"], a["r"])

    tok = lambda w: pl.BlockSpec((tm, w), lambda i: (i, 0))
    whole = lambda arr: pl.BlockSpec(arr.shape, lambda i: (0,) * arr.ndim)
    row = lambda w: pl.BlockSpec((1, w), lambda i: (0, 0))
    weights = (g, win, win_t, qan, wq, wq_t, kvan, wkn, wkn_t, wv_t, qhn, khn, wpool, wpool_t, pscale)
    return pl.pallas_call(
        body, name="mixa_pre_bwd", grid=(nt,),
        in_specs=[tok(D), pl.BlockSpec((HALO, D), _prev_halo(tm)), tok(D), tok(HS), tok(HS), tok(HS), tok(POOL_DIM),
                  pl.BlockSpec((HALO, POOL_DIM), _next_halo(tm, T))] + [whole(a) for a in weights]
                 + [tok(HEAD_SLOT), tok(HEAD_SLOT)],
        out_specs=[tok(D), tok(D), tok(ZW), tok(Q_RANK), tok(HS), tok(KV_RANK), tok(HS), tok(POOL_DIM), tok(POOL_DIM),
                   row(D), row(Q_RANK), row(KV_RANK), row(HEAD_SLOT), row(HEAD_SLOT), row(POOL_DIM)],
        out_shape=[jax.ShapeDtypeStruct((T, D), F32), jax.ShapeDtypeStruct((T, D), BF),
                   jax.ShapeDtypeStruct((T, ZW), BF), jax.ShapeDtypeStruct((T, Q_RANK), BF),
                   jax.ShapeDtypeStruct((T, HS), BF), jax.ShapeDtypeStruct((T, KV_RANK), BF),
                   jax.ShapeDtypeStruct((T, HS), BF), jax.ShapeDtypeStruct((T, POOL_DIM), BF),
                   jax.ShapeDtypeStruct((T, POOL_DIM), BF),
                   jax.ShapeDtypeStruct((1, D), F32), jax.ShapeDtypeStruct((1, Q_RANK), F32),
                   jax.ShapeDtypeStruct((1, KV_RANK), F32), jax.ShapeDtypeStruct((1, HEAD_SLOT), F32),
                   jax.ShapeDtypeStruct((1, HEAD_SLOT), F32), jax.ShapeDtypeStruct((1, POOL_DIM), F32)],
        compiler_params=_params("arbitrary"),
    )(x, x, dy, dq, dk, dv, dpo, dpo, *weights, cos, sin)


def _conv_taps(u_prev, u, cw_ref):
    ue = jnp.concatenate([u_prev, u], axis=0)
    u1 = pltpu.roll(ue, 1, 0)[HALO:, :]
    u2 = pltpu.roll(ue, 2, 0)[HALO:, :]
    return cw_ref[0:1, :] * u2 + cw_ref[1:2, :] * u1 + cw_ref[2:3, :] * u, u1, u2


def mixc_fwd(x, g, win, cw, wout):
    T, D = x.shape
    tm = _tile(T, TM_CONV_FWD, HALO)

    def body(x_ref, xp_ref, g_ref, win_ref, cw_ref, wout_ref, y_ref, z_ref):
        i = pl.program_id(0)
        xv = x_ref[...]
        xh, _ = _rms(xv)
        z = _dot((xh * g_ref[...]).astype(BF), win_ref[...])
        z_ref[...] = z.astype(BF)
        xph, _ = _rms(xp_ref[...])
        zp = _dot((xph * g_ref[...]).astype(BF), win_ref[:, D:])
        u_prev = zp[:, :D] * zp[:, D:] * jnp.where(i == 0, 0.0, 1.0)
        conv, _, _ = _conv_taps(u_prev, z[:, D:2 * D] * z[:, 2 * D:], cw_ref)
        y_ref[...] = xv + _dot((z[:, :D] * conv).astype(BF), wout_ref[...])

    tok = lambda w: pl.BlockSpec((tm, w), lambda i: (i, 0))
    whole = lambda arr: pl.BlockSpec(arr.shape, lambda i: (0,) * arr.ndim)
    return pl.pallas_call(
        body, name="mixc_fwd", grid=(T // tm,),
        in_specs=[tok(D), pl.BlockSpec((HALO, D), _prev_halo(tm)), whole(g), whole(win), whole(cw), whole(wout)],
        out_specs=[tok(D), tok(3 * D)],
        out_shape=[jax.ShapeDtypeStruct((T, D), F32), jax.ShapeDtypeStruct((T, 3 * D), BF)],
        compiler_params=_params("arbitrary"),
    )(x, x, g, win, cw, wout)


def mixc_bwd(x, dy, z, g, win_t, cw, wout_t):
    T, D = x.shape
    tm = _tile(T, TM_CONV_BWD, HALO)
    nt = T // tm

    def body(x_ref, dy_ref, dyn_ref, z_ref, zp_ref, zn_ref, g_ref, wint_ref, cw_ref, woutt_ref,
             dx_ref, hn_ref, dz_ref, v_ref, dcw_ref, dg_ref):
        i = pl.program_id(0)
        first = i == 0
        xh, r = _rms(x_ref[...])
        hn_ref[...] = (xh * g_ref[...]).astype(BF)
        zv = z_ref[...].astype(F32)
        gb, gc, hh = zv[:, :D], zv[:, D:2 * D], zv[:, 2 * D:]
        u = gc * hh
        zp = zp_ref[...].astype(F32)
        u_prev = zp[:, D:2 * D] * zp[:, 2 * D:] * jnp.where(first, 0.0, 1.0)
        conv, u1, u2 = _conv_taps(u_prev, u, cw_ref)
        v_ref[...] = (gb * conv).astype(BF)

        dv = _dot(dy_ref[...].astype(BF), woutt_ref[...])
        dconv = dv * gb
        dv_next = _dot(dyn_ref[...].astype(BF), woutt_ref[...])
        dconv_next = dv_next * zn_ref[:, :D].astype(F32) * jnp.where(i == nt - 1, 0.0, 1.0)
        de = jnp.concatenate([dconv, dconv_next], axis=0)
        n = tm + HALO
        du = (cw_ref[2:3, :] * dconv + cw_ref[1:2, :] * pltpu.roll(de, n - 1, 0)[:tm, :]
              + cw_ref[0:1, :] * pltpu.roll(de, n - 2, 0)[:tm, :])
        for tap, shifted in enumerate((u2, u1, u)):
            _accumulate(dcw_ref.at[tap:tap + 1, :], first, _colsum(dconv * shifted))
        dz = jnp.concatenate([dv * conv, du * hh, du * gc], axis=1).astype(BF)
        dz_ref[...] = dz
        dhn = _dot(dz, wint_ref[...])
        _accumulate(dg_ref, first, _colsum(dhn * xh))
        dx_ref[...] = dy_ref[...] + _rms_bwd(dhn * g_ref[...], xh, r)

    tok = lambda w: pl.BlockSpec((tm, w), lambda i: (i, 0))
    whole = lambda arr: pl.BlockSpec(arr.shape, lambda i: (0,) * arr.ndim)
    return pl.pallas_call(
        body, name="mixc_bwd", grid=(nt,),
        in_specs=[tok(D), tok(D), pl.BlockSpec((HALO, D), _next_halo(tm, T)), tok(3 * D),
                  pl.BlockSpec((HALO, 3 * D), _prev_halo(tm)), pl.BlockSpec((HALO, 3 * D), _next_halo(tm, T)),
                  whole(g), whole(win_t), whole(cw), whole(wout_t)],
        out_specs=[tok(D), tok(D), tok(3 * D), tok(D), pl.BlockSpec((3, D), lambda i: (0, 0)),
                   pl.BlockSpec((1, D), lambda i: (0, 0))],
        out_shape=[jax.ShapeDtypeStruct((T, D), F32), jax.ShapeDtypeStruct((T, D), BF),
                   jax.ShapeDtypeStruct((T, 3 * D), BF), jax.ShapeDtypeStruct((T, D), BF),
                   jax.ShapeDtypeStruct((3, D), F32), jax.ShapeDtypeStruct((1, D), F32)],
        compiler_params=_params("arbitrary"),
    )(x, dy, dy, z, z, z, g, win_t, cw, wout_t)


def loss_head(y, target):
    T, D = y.shape
    tm = _tile(T, TM_MM, 8)

    def body(y_ref, t_ref, sum_ref, dy_ref):
        err = y_ref[...] - t_ref[...]
        dy_ref[...] = err * (1.0 / D)
        part = jnp.sum(jnp.sum(err * err, axis=-1, keepdims=True) * (1.0 / D), axis=0, keepdims=True)
        _accumulate(sum_ref, pl.program_id(0) == 0, jnp.broadcast_to(part, sum_ref.shape))

    return pl.pallas_call(
        body, name="loss_head", grid=(T // tm,),
        in_specs=[pl.BlockSpec((tm, D), lambda i: (i, 0))] * 2,
        out_specs=[pl.BlockSpec((8, 128), lambda i: (0, 0)), pl.BlockSpec((tm, D), lambda i: (i, 0))],
        out_shape=[jax.ShapeDtypeStruct((8, 128), F32), jax.ShapeDtypeStruct((T, D), F32)],
        compiler_params=_params("arbitrary"),
    )(y, target)


def adamw(w, g, m, v):
    R, C = w.shape
    tr = _tile(R, TR_FLAT, 8)

    def body(w_ref, g_ref, m_ref, v_ref, d_ref, m2_ref, v2_ref):
        gv = g_ref[...]
        m2 = ADAM_B1 * m_ref[...] + (1.0 - ADAM_B1) * gv
        v2 = ADAM_B2 * v_ref[...] + (1.0 - ADAM_B2) * (gv * gv)
        m2_ref[...] = m2
        v2_ref[...] = v2
        m_hat = m2 / (1.0 - ADAM_B1 ** ADAM_STEP)
        v_hat = v2 / (1.0 - ADAM_B2 ** ADAM_STEP)
        d_ref[...] = -ADAM_LR * (m_hat / (jnp.sqrt(v_hat) + ADAM_EPS) + ADAM_WD * w_ref[...])

    spec = pl.BlockSpec((tr, C), lambda i: (i, 0))
    return pl.pallas_call(
        body, name="adamw", grid=(R // tr,), in_specs=[spec] * 4, out_specs=[spec] * 3,
        out_shape=[jax.ShapeDtypeStruct((R, C), F32)] * 3,
        compiler_params=_params("arbitrary"),
    )(w, g, m, v)


def sum_slots(a):
    S, R, C = a.shape
    tr = _tile(R, TR_FLAT // 2, 16)

    def body(a_ref, o_ref):
        acc = a_ref[0].astype(F32)
        for s in range(1, S):
            acc = acc + a_ref[s].astype(F32)
        o_ref[...] = acc

    return pl.pallas_call(
        body, name="sum_slots", grid=(R // tr,),
        in_specs=[pl.BlockSpec((S, tr, C), lambda i: (0, i, 0))],
        out_specs=pl.BlockSpec((tr, C), lambda i: (i, 0)),
        out_shape=jax.ShapeDtypeStruct((R, C), F32),
        compiler_params=_params("arbitrary"),
    )(a)


ANY = pl.BlockSpec(memory_space=pl.ANY)


def _place():
    return lax.axis_index("x"), lax.axis_index("y"), lax.axis_index("c")


def allgather_shards(w):
    R, C = w.shape
    half = R // 2
    n = GATHER_CHUNKS if half % (16 * GATHER_CHUNKS) == 0 else 1
    cr = half // n

    def body(w_ref, out_ref, send_sems, recv_sems, local_sem):
        x, y, c = _place()
        sibling = (x, y, 1 - c)
        chips = [(1 - x, y), (x, 1 - y), (1 - x, 1 - y)]

        def rows(px, py, pc, q):
            return out_ref.at[2 * px + py, pl.ds(pc * half + q * cr, cr), :]

        def copy(k, block, q, to, src=None):
            return pltpu.make_async_remote_copy(
                src_ref=rows(*block, q) if src is None else src, dst_ref=rows(*block, q),
                send_sem=send_sems.at[k * n + q], recv_sem=recv_sems.at[k * n + q], device_id=to, device_id_type=MESH)

        mine = pltpu.make_async_copy(w_ref, out_ref.at[2 * x + y], local_sem)
        mine.start()
        first = [copy(j, (x, y, c), q, (*chip, c), src=w_ref.at[pl.ds(c * half + q * cr, cr), :])
                 for q in range(n) for j, chip in enumerate(chips)]
        for cp in first:
            cp.start()
        passed = []
        for q in range(n):
            for j, chip in enumerate(chips):
                copy(j, (*chip, c), q, (x, y, c)).wait_recv()
                passed.append(copy(3 + j, (*chip, c), q, sibling))
                passed[-1].start()
        for q in range(n):
            for j, chip in enumerate(chips):
                copy(3 + j, (*chip, 1 - c), q, (x, y, c)).wait_recv()
        for cp in first + passed:
            cp.wait_send()
        mine.wait()

    return pl.pallas_call(
        body, name="allgather_shards", in_specs=[ANY], out_specs=ANY,
        out_shape=jax.ShapeDtypeStruct((4, R, C), w.dtype),
        scratch_shapes=[pltpu.SemaphoreType.DMA((6 * n,)), pltpu.SemaphoreType.DMA((6 * n,)), pltpu.SemaphoreType.DMA],
    )(w)


def exchange_partials(grads, small):
    _, R, C = grads.shape
    half = R // 2
    Rs = small.shape[0]

    def body(g_ref, s_ref, land_ref, sland_ref, send_sems, recv_sems, local_sems):
        x, y, c = _place()
        me = 4 * x + 2 * y + c
        peers = []
        for mask in range(1, 8):
            mx, my, mc = (mask >> 2) & 1, (mask >> 1) & 1, mask & 1
            peers.append(((1 - x) if mx else x, (1 - y) if my else y, (1 - c) if mc else c))

        def piece(px, py, pc):
            return g_ref.at[2 * px + py, pl.ds(pc * half, half), :]

        def big(k, sender, to):
            return pltpu.make_async_remote_copy(
                src_ref=piece(*to), dst_ref=land_ref.at[sender], send_sem=send_sems.at[k], recv_sem=recv_sems.at[k],
                device_id=to, device_id_type=MESH)

        def little(k, sender, to):
            return pltpu.make_async_remote_copy(
                src_ref=s_ref, dst_ref=sland_ref.at[sender], send_sem=send_sems.at[7 + k],
                recv_sem=recv_sems.at[7 + k], device_id=to, device_id_type=MESH)

        own_big = pltpu.make_async_copy(piece(x, y, c), land_ref.at[me], local_sems.at[0])
        own_small = pltpu.make_async_copy(s_ref, sland_ref.at[me], local_sems.at[1])
        own_big.start()
        own_small.start()
        sends = []
        for k, peer in enumerate(peers):
            sends += [little(k, me, peer), big(k, me, peer)]
        for cp in sends:
            cp.start()
        for k, (px, py, pc) in enumerate(peers):
            sender = 4 * px + 2 * py + pc
            little(k, sender, (x, y, c)).wait_recv()
            big(k, sender, (x, y, c)).wait_recv()
        for cp in sends:
            cp.wait_send()
        own_big.wait()
        own_small.wait()

    return pl.pallas_call(
        body, name="exchange_partials", in_specs=[ANY, ANY], out_specs=[ANY, ANY],
        out_shape=[jax.ShapeDtypeStruct((8, half, C), grads.dtype), jax.ShapeDtypeStruct((8, Rs, C), small.dtype)],
        scratch_shapes=[pltpu.SemaphoreType.DMA((14,)), pltpu.SemaphoreType.DMA((14,)), pltpu.SemaphoreType.DMA((2,))],
    )(grads, small)


def share_with_sibling(part):
    half, C = part.shape
    n = SIBLING_CHUNKS if half % (16 * SIBLING_CHUNKS) == 0 else 1
    cr = half // n

    def body(p_ref, out_ref, send_sem, recv_sem, local_sem):
        x, y, c = _place()

        def rows(pc):
            return out_ref.at[pl.ds(pc * half, half), :]

        own = pltpu.make_async_copy(p_ref, rows(c), local_sem)
        own.start()
        for q in range(n):
            pltpu.make_async_remote_copy(
                src_ref=p_ref.at[pl.ds(q * cr, cr), :], dst_ref=out_ref.at[pl.ds(c * half + q * cr, cr), :],
                send_sem=send_sem, recv_sem=recv_sem, device_id=(x, y, 1 - c), device_id_type=MESH).start()
        everything = pltpu.make_async_remote_copy(src_ref=p_ref, dst_ref=rows(1 - c), send_sem=send_sem,
                                                  recv_sem=recv_sem, device_id=(x, y, c), device_id_type=MESH)
        everything.wait_recv()
        everything.wait_send()
        own.wait()

    return pl.pallas_call(
        body, name="share_with_sibling", in_specs=[ANY], out_specs=ANY,
        out_shape=jax.ShapeDtypeStruct((2 * half, C), part.dtype),
        scratch_shapes=[pltpu.SemaphoreType.DMA, pltpu.SemaphoreType.DMA, pltpu.SemaphoreType.DMA],
    )(part)


FLAT_SEG = 16 * FLAT_COLS


def _seg_rows(n):
    return -(-n // FLAT_SEG) * 16


def _flat_rows(sizes):
    rows = sum(_seg_rows(n) for n in sizes)
    return -(-rows // FLAT_ROW_ALIGN) * FLAT_ROW_ALIGN


def pack_flat(arrays, lead=()):
    sizes = [int(np.prod(a.shape[len(lead):])) for a in arrays]
    total = _flat_rows(sizes)
    parts, used = [], 0
    for a, n in zip(arrays, sizes):
        rows = _seg_rows(n)
        flat = a.reshape(*lead, n)
        flat = jnp.pad(flat, [(0, 0)] * len(lead) + [(0, rows * FLAT_COLS - n)])
        parts.append(flat.reshape(*lead, rows, FLAT_COLS))
        used += rows
    if total > used:
        parts.append(jnp.zeros((*lead, total - used, FLAT_COLS), arrays[0].dtype))
    return jnp.concatenate(parts, axis=len(lead))


def unpack_flat(flat, shapes, lead=()):
    out, r0 = [], 0
    for shp in shapes:
        n = int(np.prod(shp))
        rows = _seg_rows(n)
        seg = flat[..., r0:r0 + rows, :].reshape(*lead, rows * FLAT_COLS)[..., :n]
        out.append(seg.reshape(*lead, *shp))
        r0 += rows
    return out


def _f32_bits_as(a, dtype):
    return lax.bitcast_convert_type(a, dtype).reshape(*a.shape[:-1], -1)


def _f32_from_bits(a):
    k = 4 // a.dtype.itemsize
    if k > 1:
        a = a.reshape(*a.shape[:-1], a.shape[-1] // k, k)
    return lax.bitcast_convert_type(a, F32)


def _join_shards(name, a):
    if name in COL_SHARDED:
        return jnp.transpose(a, (1, 2, 0, 3)).reshape(a.shape[1], a.shape[2], 4 * a.shape[3])
    return jnp.transpose(a, (1, 0, 2, 3)).reshape(a.shape[1], 4 * a.shape[2], a.shape[3])


def _split_shards(name, a):
    L, K, N = a.shape
    if name in COL_SHARDED:
        return jnp.transpose(a.reshape(L, K, 4, N // 4), (2, 0, 1, 3))
    return jnp.transpose(a.reshape(L, 4, K // 4, N), (1, 0, 2, 3))


def _pad_heads(a, width):
    a = a.reshape(*a.shape[:-1], HEADS, width)
    a = jnp.pad(a, [(0, 0)] * (a.ndim - 1) + [(0, HEAD_SLOT - width)])
    return a.reshape(*a.shape[:-2], HEADS * HEAD_SLOT)


def _unpad_heads(a, width):
    a = a.reshape(*a.shape[:-1], HEADS, HEAD_SLOT)[..., :width]
    return a.reshape(*a.shape[:-2], HEADS * width)


def _rope_tables(T):
    pos = jnp.arange(T, dtype=F32)
    inv_freq = ROPE_THETA ** (-jnp.arange(0, ROPE_DIM, 2, dtype=F32) / ROPE_DIM)
    ang = pos[:, None] * inv_freq[None, :]
    cos, sin = jnp.cos(ang), jnp.sin(ang)
    pad = HEAD_SLOT - QK_DIM
    cos_t = jnp.concatenate([jnp.ones((T, NOPE_DIM), F32), cos, cos, jnp.zeros((T, pad), F32)], axis=1)
    sin_t = jnp.concatenate([jnp.zeros((T, NOPE_DIM), F32), sin, sin, jnp.zeros((T, pad), F32)], axis=1)
    return cos_t, sin_t


def _even_weights(W, i):
    c3 = POOL_DIM + Q_RANK + KV_RANK
    w_in = W['a_w_in'][i]
    D = w_in.shape[0]
    rope_cols = jnp.concatenate([jnp.zeros((D, NOPE_DIM), BF), w_in[:, c3:], jnp.zeros((D, HEAD_SLOT - QK_DIM), BF)], axis=1)
    win = jnp.concatenate([w_in[:, :c3], rope_cols], axis=1)
    wq = _pad_heads(W['a_w_q_up'][i], QK_DIM)
    kv = W['a_w_kv_up'][i].reshape(KV_RANK, HEADS, NOPE_DIM + V_DIM)
    wkn = _pad_heads(kv[:, :, :NOPE_DIM].reshape(KV_RANK, HEADS * NOPE_DIM), NOPE_DIM)
    wv = _pad_heads(kv[:, :, NOPE_DIM:].reshape(KV_RANK, HEADS * V_DIM), V_DIM)
    w_out = W['a_w_out'][i]
    wo_pool = w_out[:POOL_DIM]
    wo_attn = _pad_heads(w_out[POOL_DIM:].T, V_DIM).T
    wpool = W['a_w_pool'][i]
    pad = lambda a: jnp.pad(a, (0, HEAD_SLOT - QK_DIM))[None, :]
    return dict(win=win, win_t=win.T, wq=wq, wq_t=wq.T, wkn=wkn, wkn_t=wkn.T, wv=wv, wv_t=wv.T,
                wo_pool=wo_pool, wo_pool_t=wo_pool.T, wo_attn=wo_attn, wo_attn_t=wo_attn.T,
                wpool=wpool, wpool_t=jnp.transpose(wpool, (0, 2, 1)),
                qan=W['a_q_a_norm'][i][None, :], kvan=W['a_kv_a_norm'][i][None, :],
                qhn=pad(W['a_q_head_norm'][i]), khn=pad(W['a_k_head_norm'][i]),
                pscale=W['a_pool_scale'][i][None, :], g=W['mix_norm'][2 * i][None, :])


def kernel(*args):
    p = dict(zip(INPUTS, args))
    x0 = p['x'][0]
    target = p['loss_target'][0]
    T, D = x0.shape

    shard_shapes = [p[n].shape for n in SHARDED]

    wire = [_f32_bits_as(p[n], BF) if n == 'c_conv_w' else p[n].astype(BF) for n in SHARDED]
    gathered = allgather_shards(pack_flat(wire))
    W = {}
    for n, a in zip(SHARDED, unpack_flat(gathered, [a.shape for a in wire], lead=(4,))):
        W[n] = _join_shards(n, _f32_from_bits(a) if n == 'c_conv_w' else a)
    for n in REPLICATED:
        W[n] = p[n]
    W['a_w_pool'] = p['a_w_pool'].astype(BF)
    cos, sin = _rope_tables(T)

    def ffn_weights(pre, l):
        wg, wu, wd = W[pre + '_w_gate'][l], W[pre + '_w_up'][l], W[pre + '_w_down'][l]
        return dict(g=W[pre + '_norm'][l][None, :], wg=wg, wu=wu, wd=wd, wg_t=wg.T, wu_t=wu.T, wd_t=wd.T)

    saved = []
    x = x0
    for l in range(DEPTH):
        s = dict(x0=x)
        f1 = ffn_weights('ffn1', l)
        x, s['g1'], s['u1'] = ffn_fwd(x, f1['g'], f1['wg'], f1['wu'], f1['wd'])
        s['x1'] = x
        if l % 2 == 0:
            e = _even_weights(W, l // 2)
            s['q'], s['k'], s['v'], s['po'] = mixa_pre_fwd(
                x, e['g'], e['win'], e['qan'], e['wq'], e['kvan'], e['wkn'], e['wv'], e['qhn'], e['khn'],
                e['wpool'], e['pscale'], cos, sin)
            s['o'], s['lse'] = attn_fwd(s['q'], s['k'], s['v'])
            x = mm_multi([(s['po'], e['wo_pool']), (s['o'], e['wo_attn'])], res=x)
        else:
            i = l // 2
            x, s['z'] = mixc_fwd(x, W['mix_norm'][l][None, :], W['c_w_in'][i], W['c_conv_w'][i].astype(F32),
                                 W['c_w_out'][i])
        s['x2'] = x
        f2 = ffn_weights('ffn2', l)
        x, s['g2'], s['u2'] = ffn_fwd(x, f2['g'], f2['wg'], f2['wu'], f2['wd'])
        saved.append(s)

    loss_sum, dy = loss_head(x, target)
    loss = lax.psum(0.5 * loss_sum[0, 0], AXES)

    G = {n: [None] * p[n].shape[0] for n in WEIGHTS}

    def ffn_back(pre, l, x_in, gg, uu, dy):
        f = ffn_weights(pre, l)
        dx, n, dyh, h, dgate, dup, dgn = ffn_bwd(x_in, dy, f['g'], gg, uu, f['wd_t'], f['wg_t'], f['wu_t'])
        G[pre + '_norm'][l] = dgn[0]
        G[pre + '_w_gate'][l] = mm_tn(n, dgate)
        G[pre + '_w_up'][l] = mm_tn(n, dup)
        G[pre + '_w_down'][l] = mm_tn(dyh, h).T
        return dx

    t_attn = _tile(T, TQ_ATTN, 128)
    for l in reversed(range(DEPTH)):
        s = saved[l]
        dy = ffn_back('ffn2', l, s['x2'], s['g2'], s['u2'], dy)
        i = l // 2
        if l % 2 == 0:
            e = _even_weights(W, i)
            G['a_w_out'][i] = jnp.concatenate(
                [mm_tn(s['po'], dy), _unpad_heads(mm_tn(s['o'], dy).T, V_DIM).T], axis=0)
            dpo = mm_multi([(dy, e['wo_pool_t'])])
            do = mm_multi([(dy, e['wo_attn_t'])], out_dtype=BF)
            dq, delta = attn_bwd_dq(s['q'], s['k'], s['v'], s['o'], do, s['lse'])
            as_rows = lambda a: a.reshape(HEADS, T // t_attn, t_attn)
            dk, dv = attn_bwd_dkv(s['q'], s['k'], s['v'], do, as_rows(s['lse']), as_rows(delta))
            (dy, hn, dz, nq, dqraw, nkv, dkraw, pooled, dps, dg, dqan, dkvan, dqhn, dkhn, dpscale) = mixa_pre_bwd(
                s['x1'], dy, dq, dk, dv, dpo, e['g'], e['win'], e['win_t'], e['qan'], e['wq'], e['wq_t'], e['kvan'],
                e['wkn'], e['wkn_t'], e['wv_t'], e['qhn'], e['khn'], e['wpool'], e['wpool_t'], e['pscale'], cos, sin)
            c3 = POOL_DIM + Q_RANK + KV_RANK
            dwin = mm_tn(hn, dz)
            G['a_w_in'][i] = jnp.concatenate([dwin[:, :c3], dwin[:, c3 + NOPE_DIM:c3 + QK_DIM]], axis=1)
            G['a_w_q_up'][i] = _unpad_heads(mm_tn(nq, dqraw), QK_DIM)
            dwkn = _unpad_heads(mm_tn(nkv, dkraw), NOPE_DIM).reshape(KV_RANK, HEADS, NOPE_DIM)
            dwv = _unpad_heads(mm_tn(nkv, dv), V_DIM).reshape(KV_RANK, HEADS, V_DIM)
            G['a_w_kv_up'][i] = jnp.concatenate([dwkn, dwv], axis=2).reshape(KV_RANK, HEADS * (NOPE_DIM + V_DIM))
            dwp = mm_tn(pooled, dps)
            G['a_w_pool'][i] = jnp.stack([dwp[g * POOL_GROUP:(g + 1) * POOL_GROUP, g * POOL_GROUP:(g + 1) * POOL_GROUP]
                                          for g in range(len(POOL_WINDOWS))])
            G['mix_norm'][l] = dg[0]
            G['a_q_a_norm'][i] = dqan[0]
            G['a_kv_a_norm'][i] = dkvan[0]
            G['a_q_head_norm'][i] = dqhn[0, :QK_DIM]
            G['a_k_head_norm'][i] = dkhn[0, :QK_DIM]
            G['a_pool_scale'][i] = dpscale[0]
        else:
            w_in, w_out = W['c_w_in'][i], W['c_w_out'][i]
            dy_in = dy
            dy, hn, dz, gated, dcw, dg = mixc_bwd(s['x1'], dy, s['z'], W['mix_norm'][l][None, :], w_in.T,
                                                  W['c_conv_w'][i].astype(F32), w_out.T)
            G['c_w_in'][i] = mm_tn(hn, dz)
            G['c_w_out'][i] = mm_tn(gated, dy_in)
            G['c_conv_w'][i] = dcw
            G['mix_norm'][l] = dg[0]
        dy = ffn_back('ffn1', l, s['x0'], s['g1'], s['u1'], dy)
    grad_x = dy[None]

    G = {n: jnp.stack(v) for n, v in G.items()}
    partial_big = pack_flat([_split_shards(n, G[n]) for n in SHARDED], lead=(4,)).astype(BF)
    small_shapes = [p[n].shape for n in REPLICATED]
    partial_small = pack_flat([G[n] for n in REPLICATED])
    land, sland = exchange_partials(partial_big, partial_small)
    g_big = share_with_sibling(sum_slots(land))
    g_small = sum_slots(sland)

    outs = {}
    for names, shapes, g_flat in ((SHARDED, shard_shapes, g_big), (REPLICATED, small_shapes, g_small)):
        flat = lambda pre: pack_flat([p[pre + n] for n in names])
        delta, m2, v2 = adamw(flat(''), g_flat, flat('m_'), flat('v_'))
        for kind, arr in (('grad_', g_flat), ('delta_', delta), ('new_m_', m2), ('new_v_', v2)):
            for n, a in zip(names, unpack_flat(arr, shapes)):
                outs[kind + n] = a
    return (loss, grad_x, *[outs[k + n] for k in ('grad_', 'delta_', 'new_m_', 'new_v_') for n in WEIGHTS])
```

```python
import functools

import numpy as np
import jax
import jax.numpy as jnp
from jax import lax
from jax.experimental import pallas as pl
from jax.experimental.pallas import tpu as pltpu

BF, F32 = jnp.bfloat16, jnp.float32
MESH = pl.DeviceIdType.MESH
AXES = ("x", "y", "c")

NORM_EPS = 1e-6
DEPTH = 4
HEADS = 8
HEAD_SLOT = 128
QK_DIM, NOPE_DIM, ROPE_DIM, V_DIM = 96, 64, 32, 64
POOL_WINDOWS = (2, 4, 8, 16)
POOL_DIM, POOL_GROUP = 512, 128
Q_RANK, KV_RANK = 384, 256
ROPE_THETA = 10000.0
HALO = 16
ATTN_SCALE = QK_DIM ** -0.5

ADAM_LR, ADAM_B1, ADAM_B2, ADAM_EPS, ADAM_WD, ADAM_STEP = 0.001, 0.9, 0.999, 1e-08, 0.01, 10

TM_FFN, TF_FFN = 1024, 256
TM_MIX_FWD, TM_MIX_BWD = 512, 256
TM_CONV_FWD, TM_CONV_BWD = 256, 256
TQ_ATTN = 512
TM_MM = 512
TK_TN, BM_TN, BN_TN = 2048, 1024, 1536
FLAT_COLS = 1024
FLAT_ROW_ALIGN = 1024
SIBLING_CHUNKS = 16
LOCAL_CHUNKS = 16
GATHER_CHUNKS = 8
TR_FLAT = 256
VMEM_LIMIT = 56 * 1024 * 1024

WEIGHTS = ['ffn1_norm', 'ffn1_w_gate', 'ffn1_w_up', 'ffn1_w_down', 'mix_norm', 'ffn2_norm', 'ffn2_w_gate',
           'ffn2_w_up', 'ffn2_w_down', 'a_w_in', 'a_q_a_norm', 'a_w_q_up', 'a_kv_a_norm', 'a_w_kv_up',
           'a_q_head_norm', 'a_k_head_norm', 'a_w_pool', 'a_pool_scale', 'a_w_out', 'c_w_in', 'c_conv_w',
           'c_w_out']
COL_SHARDED = ('ffn1_w_gate', 'ffn1_w_up', 'ffn2_w_gate', 'ffn2_w_up', 'a_w_in', 'a_w_q_up', 'a_w_kv_up',
               'c_w_in', 'c_conv_w')
ROW_SHARDED = ('ffn1_w_down', 'ffn2_w_down', 'a_w_out', 'c_w_out')
SHARDED = tuple(n for n in WEIGHTS if n in COL_SHARDED or n in ROW_SHARDED)
REPLICATED = tuple(n for n in WEIGHTS if n not in SHARDED)
INPUTS = ['x'] + WEIGHTS + ['loss_target'] + ['m_' + n for n in WEIGHTS] + ['v_' + n for n in WEIGHTS]


def _dot(a, b):
    return jnp.dot(a, b, preferred_element_type=F32)


def _dot_nt(a, b):
    return lax.dot_general(a, b, (((1,), (1,)), ((), ())), preferred_element_type=F32)


def _dot_tn(a, b):
    return lax.dot_general(a, b, (((0,), (0,)), ((), ())), preferred_element_type=F32)


def _params(*sem):
    return pltpu.CompilerParams(dimension_semantics=sem or None, vmem_limit_bytes=VMEM_LIMIT)


def _tile(n, cap, unit):
    if n <= cap:
        return n
    best = None
    for t in range(unit, cap + 1, unit):
        if n % t == 0:
            best = t
    assert best is not None, (n, cap, unit)
    return best


def _rms(x, width=None):
    ms = jnp.sum(x * x, axis=-1, keepdims=True) * (1.0 / (width or x.shape[-1]))
    r = lax.rsqrt(ms + NORM_EPS)
    return x * r, r


def _rms_bwd(a, xhat, r, width=None):
    return r * (a - xhat * (jnp.sum(a * xhat, axis=-1, keepdims=True) * (1.0 / (width or a.shape[-1]))))


def _colsum(a):
    return jnp.sum(a, axis=0, keepdims=True)


def _accumulate(ref, first, value):
    @pl.when(first)
    def _():
        ref[...] = value

    @pl.when(jnp.logical_not(first))
    def _():
        ref[...] += value


def _rot_half(v):
    lane = lax.broadcasted_iota(jnp.int32, v.shape, 1)
    rot = jnp.where(lane < NOPE_DIM + ROPE_DIM // 2, -pltpu.roll(v, HEAD_SLOT - ROPE_DIM // 2, 1),
                    pltpu.roll(v, ROPE_DIM // 2, 1))
    return jnp.where((lane >= NOPE_DIM) & (lane < QK_DIM), rot, 0.0)


def _rope(v, cos, sin):
    return v * cos + _rot_half(v) * sin


def _rope_bwd(d, cos, sin):
    return d * cos - _rot_half(d * sin)


def ffn_fwd(x, g, wg, wu, wd):
    T, D = x.shape
    F = wg.shape[1]
    tm, tf = _tile(T, TM_FFN, 8), _tile(F, TF_FFN, 128)
    nf = F // tf

    def body(x_ref, g_ref, wg_ref, wu_ref, wd_ref, y_ref, gg_ref, uu_ref, n_sc, acc):
        f = pl.program_id(1)

        @pl.when(f == 0)
        def _():
            xh, _ = _rms(x_ref[...])
            n_sc[...] = (xh * g_ref[...]).astype(BF)
            acc[...] = jnp.zeros_like(acc)

        n = n_sc[...]
        gg = _dot(n, wg_ref[...])
        uu = _dot(n, wu_ref[...])
        gg_ref[...] = gg.astype(BF)
        uu_ref[...] = uu.astype(BF)
        h = gg * jax.nn.sigmoid(gg) * uu
        acc[...] += _dot(h.astype(BF), wd_ref[...])

        @pl.when(f == nf - 1)
        def _():
            y_ref[...] = x_ref[...] + 0.5 * acc[...]

    return pl.pallas_call(
        body, name="ffn_fwd", grid=(T // tm, nf),
        in_specs=[pl.BlockSpec((tm, D), lambda i, f: (i, 0)), pl.BlockSpec((1, D), lambda i, f: (0, 0)),
                  pl.BlockSpec((D, tf), lambda i, f: (0, f)), pl.BlockSpec((D, tf), lambda i, f: (0, f)),
                  pl.BlockSpec((tf, D), lambda i, f: (f, 0))],
        out_specs=[pl.BlockSpec((tm, D), lambda i, f: (i, 0)), pl.BlockSpec((tm, tf), lambda i, f: (i, f)),
                   pl.BlockSpec((tm, tf), lambda i, f: (i, f))],
        out_shape=[jax.ShapeDtypeStruct((T, D), F32), jax.ShapeDtypeStruct((T, F), BF),
                   jax.ShapeDtypeStruct((T, F), BF)],
        scratch_shapes=[pltpu.VMEM((tm, D), BF), pltpu.VMEM((tm, D), F32)],
        compiler_params=_params("arbitrary", "arbitrary"),
    )(x, g, wg, wu, wd)


def ffn_bwd(x, dy, g, gg, uu, wd_t, wg_t, wu_t):
    T, D = x.shape
    F = gg.shape[1]
    tm, tf = _tile(T, TM_FFN, 8), _tile(F, TF_FFN, 128)
    nf = F // tf

    def body(x_ref, dy_ref, g_ref, gg_ref, uu_ref, wdt_ref, wgt_ref, wut_ref,
             dx_ref, n_ref, dyh_ref, h_ref, dg_ref, du_ref, dgn_ref, acc):
        i, f = pl.program_id(0), pl.program_id(1)

        @pl.when(f == 0)
        def _():
            xh, _ = _rms(x_ref[...])
            n_ref[...] = (xh * g_ref[...]).astype(BF)
            dyh_ref[...] = (0.5 * dy_ref[...]).astype(BF)
            acc[...] = jnp.zeros_like(acc)

        dh = _dot(dyh_ref[...], wdt_ref[...])
        gv = gg_ref[...].astype(F32)
        uv = uu_ref[...].astype(F32)
        sg = jax.nn.sigmoid(gv)
        silu = gv * sg
        h_ref[...] = (silu * uv).astype(BF)
        d_up = (dh * silu).astype(BF)
        d_gate = (dh * uv * (sg * (1.0 + gv * (1.0 - sg)))).astype(BF)
        du_ref[...] = d_up
        dg_ref[...] = d_gate
        acc[...] += _dot(d_gate, wgt_ref[...]) + _dot(d_up, wut_ref[...])

        @pl.when(f == nf - 1)
        def _():
            xh, r = _rms(x_ref[...])
            dn = acc[...]
            dx_ref[...] = dy_ref[...] + _rms_bwd(dn * g_ref[...], xh, r)
            _accumulate(dgn_ref, i == 0, _colsum(dn * xh))

    tok = lambda i, f: (i, 0)
    chunk = lambda i, f: (i, f)
    return pl.pallas_call(
        body, name="ffn_bwd", grid=(T // tm, nf),
        in_specs=[pl.BlockSpec((tm, D), tok), pl.BlockSpec((tm, D), tok), pl.BlockSpec((1, D), lambda i, f: (0, 0)),
                  pl.BlockSpec((tm, tf), chunk), pl.BlockSpec((tm, tf), chunk),
                  pl.BlockSpec((D, tf), lambda i, f: (0, f)), pl.BlockSpec((tf, D), lambda i, f: (f, 0)),
                  pl.BlockSpec((tf, D), lambda i, f: (f, 0))],
        out_specs=[pl.BlockSpec((tm, D), tok), pl.BlockSpec((tm, D), tok), pl.BlockSpec((tm, D), tok),
                   pl.BlockSpec((tm, tf), chunk), pl.BlockSpec((tm, tf), chunk), pl.BlockSpec((tm, tf), chunk),
                   pl.BlockSpec((1, D), lambda i, f: (0, 0))],
        out_shape=[jax.ShapeDtypeStruct((T, D), F32), jax.ShapeDtypeStruct((T, D), BF),
                   jax.ShapeDtypeStruct((T, D), BF), jax.ShapeDtypeStruct((T, F), BF),
                   jax.ShapeDtypeStruct((T, F), BF), jax.ShapeDtypeStruct((T, F), BF),
                   jax.ShapeDtypeStruct((1, D), F32)],
        scratch_shapes=[pltpu.VMEM((tm, D), F32)],
        compiler_params=_params("arbitrary", "arbitrary"),
    )(x, dy, g, gg, uu, wd_t, wg_t, wu_t)


def mm_tn(a, b):
    T, M = a.shape
    N = b.shape[1]
    tk, bm, bn = _tile(T, TK_TN, 16), _tile(M, BM_TN, 128), _tile(N, BN_TN, 128)

    def body(a_ref, b_ref, o_ref):
        part = _dot_tn(a_ref[...].astype(BF), b_ref[...].astype(BF))
        _accumulate(o_ref, pl.program_id(2) == 0, part)

    return pl.pallas_call(
        body, name="mm_tn", grid=(M // bm, N // bn, T // tk),
        in_specs=[pl.BlockSpec((tk, bm), lambda i, j, k: (k, i)), pl.BlockSpec((tk, bn), lambda i, j, k: (k, j))],
        out_specs=pl.BlockSpec((bm, bn), lambda i, j, k: (i, j)),
        out_shape=jax.ShapeDtypeStruct((M, N), F32),
        compiler_params=_params("arbitrary", "arbitrary", "arbitrary"),
    )(a, b)


def mm_multi(pairs, res=None, out_dtype=F32):
    T = pairs[0][0].shape[0]
    N = pairs[0][1].shape[1]
    tm = _tile(T, TM_MM, 16)
    n = len(pairs)

    def body(*refs):
        o_ref = refs[-1]
        acc = refs[2 * n][...] if res is not None else None
        for k in range(n):
            part = _dot(refs[k][...].astype(BF), refs[n + k][...])
            acc = part if acc is None else acc + part
        o_ref[...] = acc.astype(out_dtype)

    ins = [a for a, _ in pairs] + [w for _, w in pairs]
    specs = [pl.BlockSpec((tm, a.shape[1]), lambda i: (i, 0)) for a, _ in pairs]
    specs += [pl.BlockSpec(w.shape, lambda i: (0, 0)) for _, w in pairs]
    if res is not None:
        ins.append(res)
        specs.append(pl.BlockSpec((tm, N), lambda i: (i, 0)))
    return pl.pallas_call(
        body, name="mm_multi", grid=(T // tm,), in_specs=specs,
        out_specs=pl.BlockSpec((tm, N), lambda i: (i, 0)),
        out_shape=jax.ShapeDtypeStruct((T, N), out_dtype),
        compiler_params=_params("arbitrary"),
    )(*ins)


def _causal_mask(t, q_major):
    r = lax.broadcasted_iota(jnp.int32, (t, t), 0)
    c = lax.broadcasted_iota(jnp.int32, (t, t), 1)
    return (c <= r) if q_major else (r <= c)


def attn_fwd(q, k, v):
    T = q.shape[0]
    t = _tile(T, TQ_ATTN, 128)
    nq = T // t

    def body(q_ref, k_ref, v_ref, o_ref, lse_ref):
        i = pl.program_id(1)
        qv = q_ref[...]

        def block(j):
            return pl.ds(pl.multiple_of(j * t, t), t)

        def update(s, j, m, l, acc):
            m2 = jnp.maximum(m, jnp.max(s, axis=-1, keepdims=True))
            p = jnp.exp(s - m2)
            scale = jnp.exp(m - m2)
            return (m2, scale * l + jnp.sum(p, axis=-1, keepdims=True),
                    scale * acc + _dot(p.astype(BF), v_ref[block(j), :]))

        def scores(j):
            return _dot_nt(qv, k_ref[block(j), :])

        def diagonal(carry):
            return update(jnp.where(_causal_mask(t, True), scores(i), -jnp.inf), i, *carry)

        def pair(j, carry, last_is_diagonal):
            s0, s1 = scores(j), scores(j + 1)
            if last_is_diagonal:
                s1 = jnp.where(_causal_mask(t, True), s1, -jnp.inf)
            return update(s1, j + 1, *update(s0, j, *carry))

        init = (jnp.full((t, 1), -1e30, F32), jnp.zeros((t, 1), F32), jnp.zeros((t, HEAD_SLOT), F32))
        carry = lax.fori_loop(0, i // 2, lambda jj, c: pair(2 * jj, c, False), init)
        m, l, acc = lax.cond(i % 2 == 1, lambda c: pair(i - 1, c, True), diagonal, carry)
        o_ref[...] = (acc / l).astype(BF)
        lse_ref[...] = m + jnp.log(l)

    return pl.pallas_call(
        body, name="attn_fwd", grid=(HEADS, nq),
        in_specs=[pl.BlockSpec((t, HEAD_SLOT), lambda h, i: (i, h)), pl.BlockSpec((T, HEAD_SLOT), lambda h, i: (0, h)),
                  pl.BlockSpec((T, HEAD_SLOT), lambda h, i: (0, h))],
        out_specs=[pl.BlockSpec((t, HEAD_SLOT), lambda h, i: (i, h)), pl.BlockSpec((None, t, 1), lambda h, i: (h, i, 0))],
        out_shape=[jax.ShapeDtypeStruct((T, HEADS * HEAD_SLOT), BF), jax.ShapeDtypeStruct((HEADS, T, 1), F32)],
        compiler_params=_params("arbitrary", "arbitrary"),
    )(q, k, v)


def attn_delta(o, do):
    T = o.shape[0]
    t = _tile(T, TQ_ATTN, 128)

    def body(o_ref, do_ref, delta_ref):
        delta_ref[...] = jnp.sum(do_ref[...].astype(F32) * o_ref[...].astype(F32), axis=-1, keepdims=True)

    blk = pl.BlockSpec((t, HEAD_SLOT), lambda h, i: (i, h))
    return pl.pallas_call(
        body, name="attn_delta", grid=(HEADS, T // t), in_specs=[blk, blk],
        out_specs=pl.BlockSpec((None, t, 1), lambda h, i: (h, i, 0)),
        out_shape=jax.ShapeDtypeStruct((HEADS, T, 1), F32),
        compiler_params=_params("arbitrary", "arbitrary"),
    )(o, do)


def attn_bwd(q, k, k_t, v, do, lse_rows, delta_rows):
    T = q.shape[0]
    t = _tile(T, TQ_ATTN, 128)
    nq = T // t

    def body(q_ref, k_ref, kt_ref, v_ref, do_ref, lse_ref, delta_ref, dk_ref, dv_ref, dqt_ref):
        j = pl.program_id(1)
        kv, vv, ktv = k_ref[...], v_ref[...], kt_ref[...]

        @pl.when(j == 0)
        def _():
            dqt_ref[...] = jnp.zeros_like(dqt_ref)

        def block(i):
            return pl.ds(pl.multiple_of(i * t, t), t)

        def scores(i, masked):
            st = _dot_nt(kv, q_ref[block(i), :])
            return jnp.where(_causal_mask(t, False), st, -jnp.inf) if masked else st

        def add(carry, st, i):
            dk, dv = carry
            qv, dov = q_ref[block(i), :], do_ref[block(i), :]
            pt = jnp.exp(st - lse_ref[pl.ds(i, 1), :])
            dst = (pt * (_dot_nt(vv, dov) - delta_ref[pl.ds(i, 1), :])).astype(BF)
            dqt_ref[i] += _dot(ktv, dst)
            return dk + _dot(dst, qv), dv + _dot(pt.astype(BF), dov)

        def pair(i, carry):
            s0, s1 = scores(i, False), scores(i + 1, False)
            return add(add(carry, s0, i), s1, i + 1)

        zero = jnp.zeros((t, HEAD_SLOT), F32)
        carry = add((zero, zero), scores(j, True), j)
        rest = nq - 1 - j
        carry = lax.fori_loop(0, rest // 2, lambda ii, c: pair(j + 1 + 2 * ii, c), carry)
        dk, dv = lax.cond(rest % 2 == 1, lambda c: add(c, scores(nq - 1, False), nq - 1), lambda c: c, carry)
        dk_ref[...] = dk
        dv_ref[...] = dv

    blk = pl.BlockSpec((t, HEAD_SLOT), lambda h, j: (j, h))
    full = pl.BlockSpec((T, HEAD_SLOT), lambda h, j: (0, h))
    rows = pl.BlockSpec((None, nq, t), lambda h, j: (h, 0, 0))
    return pl.pallas_call(
        body, name="attn_bwd", grid=(HEADS, nq),
        in_specs=[full, blk, pl.BlockSpec((HEAD_SLOT, t), lambda h, j: (h, j)), blk, full, rows, rows],
        out_specs=[blk, blk, pl.BlockSpec((None, nq, HEAD_SLOT, t), lambda h, j: (h, 0, 0, 0))],
        out_shape=[jax.ShapeDtypeStruct((T, HEADS * HEAD_SLOT), F32)] * 2
                  + [jax.ShapeDtypeStruct((HEADS, nq, HEAD_SLOT, t), F32)],
        compiler_params=_params("arbitrary", "arbitrary"),
    )(q, k, k_t, v, do, lse_rows, delta_rows)


def _prev_halo(tm):
    return lambda i: (jnp.maximum(i * (tm // HALO) - 1, 0), 0)


def _next_halo(tm, T):
    return lambda i: (jnp.minimum((i + 1) * (tm // HALO), T // HALO - 1), 0)


def _inv_count(row0, n, w):
    t = row0 + lax.broadcasted_iota(jnp.int32, (n, 1), 0)
    return 1.0 / jnp.minimum(t + 1, w).astype(F32)


def _pool_fwd(u_prev, u, row0):
    tm = u.shape[0]
    out = []
    for g, w in enumerate(POOL_WINDOWS):
        lanes = slice(g * POOL_GROUP, (g + 1) * POOL_GROUP)
        ue = jnp.concatenate([u_prev[:, lanes], u[:, lanes]], axis=0)
        s, step = ue, 1
        while step < w:
            s = s + pltpu.roll(s, step, 0)
            step *= 2
        out.append(s[HALO:, :] * _inv_count(row0, tm, w) - u[:, lanes])
    return out


def _pool_bwd(dp, dp_next, row0):
    tm = dp[0].shape[0]
    out = []
    for g, w in enumerate(POOL_WINDOWS):
        e = jnp.concatenate([dp[g] * _inv_count(row0, tm, w), dp_next[g] * (1.0 / w)], axis=0)
        n = tm + HALO
        s, step = e, 1
        while step < w:
            s = s + pltpu.roll(s, n - step, 0)
            step *= 2
        out.append(s[:tm, :] - dp[g])
    return out


def _mixa_front(x, xp, first, row0, g_ref, win_ref, qan_ref, wq_ref, kvan_ref, wkn_ref):
    xh, r = _rms(x)
    hn = (xh * g_ref[...]).astype(BF)
    z = _dot(hn, win_ref[...])
    xph, _ = _rms(xp)
    u_prev = _dot((xph * g_ref[...]).astype(BF), win_ref[:, :POOL_DIM]) * jnp.where(first, 0.0, 1.0)
    u = z[:, :POOL_DIM]
    pooled = _pool_fwd(u_prev, u, row0)
    c1, c2 = POOL_DIM + Q_RANK, POOL_DIM + Q_RANK + KV_RANK
    qh, rq = _rms(z[:, POOL_DIM:c1])
    nq = (qh * qan_ref[...]).astype(BF)
    kh, rk = _rms(z[:, c1:c2])
    nkv = (kh * kvan_ref[...]).astype(BF)
    qraw = _dot(nq, wq_ref[...])
    kraw = _dot(nkv, wkn_ref[...])
    krope = z[:, c2:c2 + HEAD_SLOT]
    return dict(xh=xh, r=r, hn=hn, pooled=pooled, qh=qh, rq=rq, nq=nq, kh=kh, rk=rk, nkv=nkv,
                qraw=qraw, kraw=kraw, krope=krope)


def mixa_pre_fwd(x, g, win, qan, wq, kvan, wkn, wv, qhn, khn, wpool, pscale, cos, sin):
    T, D = x.shape
    tm = _tile(T, TM_MIX_FWD, HALO)
    HS = HEADS * HEAD_SLOT

    def body(x_ref, xp_ref, g_ref, win_ref, qan_ref, wq_ref, kvan_ref, wkn_ref, wv_ref, qhn_ref, khn_ref,
             wpool_ref, pscale_ref, cos_ref, sin_ref, q_ref, k_ref, v_ref, po_ref):
        i = pl.program_id(0)
        a = _mixa_front(x_ref[...], xp_ref[...], i == 0, i * tm, g_ref, win_ref, qan_ref, wq_ref, kvan_ref, wkn_ref)
        for gi in range(len(POOL_WINDOWS)):
            lanes = slice(gi * POOL_GROUP, (gi + 1) * POOL_GROUP)
            po = _dot(a["pooled"][gi].astype(BF), wpool_ref[gi]) * pscale_ref[:, lanes]
            po_ref[:, lanes] = po.astype(BF)
        cosv, sinv = cos_ref[...], sin_ref[...]
        v_ref[...] = _dot(a["nkv"], wv_ref[...]).astype(BF)
        for h in range(HEADS):
            lanes = slice(h * HEAD_SLOT, (h + 1) * HEAD_SLOT)
            qn, _ = _rms(a["qraw"][:, lanes], QK_DIM)
            q_ref[:, lanes] = (_rope(qn * qhn_ref[...], cosv, sinv) * ATTN_SCALE).astype(BF)
            kn, _ = _rms(a["kraw"][:, lanes] + a["krope"], QK_DIM)
            k_ref[:, lanes] = _rope(kn * khn_ref[...], cosv, sinv).astype(BF)

    tok = lambda w: pl.BlockSpec((tm, w), lambda i: (i, 0))
    whole = lambda arr: pl.BlockSpec(arr.shape, lambda i: (0,) * arr.ndim)
    return pl.pallas_call(
        body, name="mixa_pre_fwd", grid=(T // tm,),
        in_specs=[tok(D), pl.BlockSpec((HALO, D), _prev_halo(tm))] + [whole(a) for a in
                  (g, win, qan, wq, kvan, wkn, wv, qhn, khn, wpool, pscale)] + [tok(HEAD_SLOT), tok(HEAD_SLOT)],
        out_specs=[tok(HS), tok(HS), tok(HS), tok(POOL_DIM)],
        out_shape=[jax.ShapeDtypeStruct((T, HS), BF)] * 3 + [jax.ShapeDtypeStruct((T, POOL_DIM), BF)],
        compiler_params=_params("arbitrary"),
    )(x, x, g, win, qan, wq, kvan, wkn, wv, qhn, khn, wpool, pscale, cos, sin)


def mixa_pre_bwd(x, dy, dq, dk, dv, dpo, g, win, win_t, qan, wq, wq_t, kvan, wkn, wkn_t, wv_t, qhn, khn,
                 wpool, wpool_t, pscale, cos, sin):
    T, D = x.shape
    tm = _tile(T, TM_MIX_BWD, HALO)
    HS = HEADS * HEAD_SLOT
    ZW = win.shape[1]
    nt = T // tm

    def body(x_ref, xp_ref, dy_ref, dq_ref, dk_ref, dv_ref, dpo_ref, dpon_ref, g_ref, win_ref, wint_ref, qan_ref,
             wq_ref, wqt_ref, kvan_ref, wkn_ref, wknt_ref, wvt_ref, qhn_ref, khn_ref, wpool_ref, wpoolt_ref,
             pscale_ref, cos_ref, sin_ref,
             dx_ref, hn_ref, dz_ref, nq_ref, dqraw_ref, nkv_ref, dkraw_ref, pooled_ref, dps_ref,
             dg_ref, dqan_ref, dkvan_ref, dqhn_ref, dkhn_ref, dpscale_ref):
        i = pl.program_id(0)
        first = i == 0
        a = _mixa_front(x_ref[...], xp_ref[...], first, i * tm, g_ref, win_ref, qan_ref, wq_ref, kvan_ref, wkn_ref)
        cosv, sinv = cos_ref[...], sin_ref[...]
        hn_ref[...] = a["hn"]
        nq_ref[...] = a["nq"]
        nkv_ref[...] = a["nkv"]

        has_next = jnp.where(i == nt - 1, 0.0, 1.0)
        dpool, dpool_next, dpscale = [], [], []
        for gi in range(len(POOL_WINDOWS)):
            lanes = slice(gi * POOL_GROUP, (gi + 1) * POOL_GROUP)
            pooled = a["pooled"][gi].astype(BF)
            pooled_ref[:, lanes] = pooled
            dpo_g = dpo_ref[:, lanes]
            dpscale.append(_colsum(dpo_g * _dot(pooled, wpool_ref[gi])))
            dps = (dpo_g * pscale_ref[:, lanes]).astype(BF)
            dps_ref[:, lanes] = dps
            dpool.append(_dot(dps, wpoolt_ref[gi]))
            dps_n = (dpon_ref[:, lanes] * pscale_ref[:, lanes] * has_next).astype(BF)
            dpool_next.append(_dot(dps_n, wpoolt_ref[gi]))
        du = jnp.concatenate(_pool_bwd(dpool, dpool_next, i * tm), axis=1)
        _accumulate(dpscale_ref, first, jnp.concatenate(dpscale, axis=1))

        dqhn = jnp.zeros((1, HEAD_SLOT), F32)
        dkhn = jnp.zeros((1, HEAD_SLOT), F32)
        dkrope = jnp.zeros((tm, HEAD_SLOT), F32)
        for h in range(HEADS):
            lanes = slice(h * HEAD_SLOT, (h + 1) * HEAD_SLOT)
            qhat, rq = _rms(a["qraw"][:, lanes], QK_DIM)
            dqn = _rope_bwd(dq_ref[:, lanes] * ATTN_SCALE, cosv, sinv)
            dqhn = dqhn + _colsum(dqn * qhat)
            dqraw_ref[:, lanes] = _rms_bwd(dqn * qhn_ref[...], qhat, rq, QK_DIM).astype(BF)
            khat, rk = _rms(a["kraw"][:, lanes] + a["krope"], QK_DIM)
            dkn = _rope_bwd(dk_ref[:, lanes], cosv, sinv)
            dkhn = dkhn + _colsum(dkn * khat)
            dkraw = _rms_bwd(dkn * khn_ref[...], khat, rk, QK_DIM)
            dkrope = dkrope + dkraw
            dkraw_ref[:, lanes] = dkraw.astype(BF)
        _accumulate(dqhn_ref, first, dqhn)
        _accumulate(dkhn_ref, first, dkhn)

        dnq = _dot(dqraw_ref[...], wqt_ref[...])
        _accumulate(dqan_ref, first, _colsum(dnq * a["qh"]))
        dql = _rms_bwd(dnq * qan_ref[...], a["qh"], a["rq"])
        dnkv = _dot(dkraw_ref[...], wknt_ref[...]) + _dot(dv_ref[...].astype(BF), wvt_ref[...])
        _accumulate(dkvan_ref, first, _colsum(dnkv * a["kh"]))
        dkvl = _rms_bwd(dnkv * kvan_ref[...], a["kh"], a["rk"])

        dz = jnp.concatenate([du, dql, dkvl, dkrope], axis=1).astype(BF)
        dz_ref[...] = dz
        dhn = _dot(dz, wint_ref[...])
        _accumulate(dg_ref, first, _colsum(dhn * a["xh"]))
        dx_ref[...] = dy_ref[...] + _rms_bwd(dhn * g_ref[...], a["xh"], a["r"])

    tok = lambda w: pl.BlockSpec((tm, w), lambda i: (i, 0))
    whole = lambda arr: pl.BlockSpec(arr.shape, lambda i: (0,) * arr.ndim)
    row = lambda w: pl.BlockSpec((1, w), lambda i: (0, 0))
    weights = (g, win, win_t, qan, wq, wq_t, kvan, wkn, wkn_t, wv_t, qhn, khn, wpool, wpool_t, pscale)
    return pl.pallas_call(
        body, name="mixa_pre_bwd", grid=(nt,),
        in_specs=[tok(D), pl.BlockSpec((HALO, D), _prev_halo(tm)), tok(D), tok(HS), tok(HS), tok(HS), tok(POOL_DIM),
                  pl.BlockSpec((HALO, POOL_DIM), _next_halo(tm, T))] + [whole(a) for a in weights]
                 + [tok(HEAD_SLOT), tok(HEAD_SLOT)],
        out_specs=[tok(D), tok(D), tok(ZW), tok(Q_RANK), tok(HS), tok(KV_RANK), tok(HS), tok(POOL_DIM), tok(POOL_DIM),
                   row(D), row(Q_RANK), row(KV_RANK), row(HEAD_SLOT), row(HEAD_SLOT), row(POOL_DIM)],
        out_shape=[jax.ShapeDtypeStruct((T, D), F32), jax.ShapeDtypeStruct((T, D), BF),
                   jax.ShapeDtypeStruct((T, ZW), BF), jax.ShapeDtypeStruct((T, Q_RANK), BF),
                   jax.ShapeDtypeStruct((T, HS), BF), jax.ShapeDtypeStruct((T, KV_RANK), BF),
                   jax.ShapeDtypeStruct((T, HS), BF), jax.ShapeDtypeStruct((T, POOL_DIM), BF),
                   jax.ShapeDtypeStruct((T, POOL_DIM), BF),
                   jax.ShapeDtypeStruct((1, D), F32), jax.ShapeDtypeStruct((1, Q_RANK), F32),
                   jax.ShapeDtypeStruct((1, KV_RANK), F32), jax.ShapeDtypeStruct((1, HEAD_SLOT), F32),
                   jax.ShapeDtypeStruct((1, HEAD_SLOT), F32), jax.ShapeDtypeStruct((1, POOL_DIM), F32)],
        compiler_params=_params("arbitrary"),
    )(x, x, dy, dq, dk, dv, dpo, dpo, *weights, cos, sin)


def _conv_taps(u_prev, u, cw_ref):
    ue = jnp.concatenate([u_prev, u], axis=0)
    u1 = pltpu.roll(ue, 1, 0)[HALO:, :]
    u2 = pltpu.roll(ue, 2, 0)[HALO:, :]
    return cw_ref[0:1, :] * u2 + cw_ref[1:2, :] * u1 + cw_ref[2:3, :] * u, u1, u2


def mixc_fwd(x, g, win, cw, wout):
    T, D = x.shape
    tm = _tile(T, TM_CONV_FWD, HALO)

    def body(x_ref, xp_ref, g_ref, win_ref, cw_ref, wout_ref, y_ref, z_ref):
        i = pl.program_id(0)
        xv = x_ref[...]
        xh, _ = _rms(xv)
        z = _dot((xh * g_ref[...]).astype(BF), win_ref[...])
        z_ref[...] = z.astype(BF)
        xph, _ = _rms(xp_ref[...])
        zp = _dot((xph * g_ref[...]).astype(BF), win_ref[:, D:])
        u_prev = zp[:, :D] * zp[:, D:] * jnp.where(i == 0, 0.0, 1.0)
        conv, _, _ = _conv_taps(u_prev, z[:, D:2 * D] * z[:, 2 * D:], cw_ref)
        y_ref[...] = xv + _dot((z[:, :D] * conv).astype(BF), wout_ref[...])

    tok = lambda w: pl.BlockSpec((tm, w), lambda i: (i, 0))
    whole = lambda arr: pl.BlockSpec(arr.shape, lambda i: (0,) * arr.ndim)
    return pl.pallas_call(
        body, name="mixc_fwd", grid=(T // tm,),
        in_specs=[tok(D), pl.BlockSpec((HALO, D), _prev_halo(tm)), whole(g), whole(win), whole(cw), whole(wout)],
        out_specs=[tok(D), tok(3 * D)],
        out_shape=[jax.ShapeDtypeStruct((T, D), F32), jax.ShapeDtypeStruct((T, 3 * D), BF)],
        compiler_params=_params("arbitrary"),
    )(x, x, g, win, cw, wout)


def mixc_bwd(x, dy, z, g, win_t, cw, wout_t):
    T, D = x.shape
    tm = _tile(T, TM_CONV_BWD, HALO)
    nt = T // tm

    def body(x_ref, dy_ref, dyn_ref, z_ref, zp_ref, zn_ref, g_ref, wint_ref, cw_ref, woutt_ref,
             dx_ref, hn_ref, dz_ref, v_ref, dcw_ref, dg_ref):
        i = pl.program_id(0)
        first = i == 0
        xh, r = _rms(x_ref[...])
        hn_ref[...] = (xh * g_ref[...]).astype(BF)
        zv = z_ref[...].astype(F32)
        gb, gc, hh = zv[:, :D], zv[:, D:2 * D], zv[:, 2 * D:]
        u = gc * hh
        zp = zp_ref[...].astype(F32)
        u_prev = zp[:, D:2 * D] * zp[:, 2 * D:] * jnp.where(first, 0.0, 1.0)
        conv, u1, u2 = _conv_taps(u_prev, u, cw_ref)
        v_ref[...] = (gb * conv).astype(BF)

        dv = _dot(dy_ref[...].astype(BF), woutt_ref[...])
        dconv = dv * gb
        dv_next = _dot(dyn_ref[...].astype(BF), woutt_ref[...])
        dconv_next = dv_next * zn_ref[:, :D].astype(F32) * jnp.where(i == nt - 1, 0.0, 1.0)
        de = jnp.concatenate([dconv, dconv_next], axis=0)
        n = tm + HALO
        du = (cw_ref[2:3, :] * dconv + cw_ref[1:2, :] * pltpu.roll(de, n - 1, 0)[:tm, :]
              + cw_ref[0:1, :] * pltpu.roll(de, n - 2, 0)[:tm, :])
        for tap, shifted in enumerate((u2, u1, u)):
            _accumulate(dcw_ref.at[tap:tap + 1, :], first, _colsum(dconv * shifted))
        dz = jnp.concatenate([dv * conv, du * hh, du * gc], axis=1).astype(BF)
        dz_ref[...] = dz
        dhn = _dot(dz, wint_ref[...])
        _accumulate(dg_ref, first, _colsum(dhn * xh))
        dx_ref[...] = dy_ref[...] + _rms_bwd(dhn * g_ref[...], xh, r)

    tok = lambda w: pl.BlockSpec((tm, w), lambda i: (i, 0))
    whole = lambda arr: pl.BlockSpec(arr.shape, lambda i: (0,) * arr.ndim)
    return pl.pallas_call(
        body, name="mixc_bwd", grid=(nt,),
        in_specs=[tok(D), tok(D), pl.BlockSpec((HALO, D), _next_halo(tm, T)), tok(3 * D),
                  pl.BlockSpec((HALO, 3 * D), _prev_halo(tm)), pl.BlockSpec((HALO, 3 * D), _next_halo(tm, T)),
                  whole(g), whole(win_t), whole(cw), whole(wout_t)],
        out_specs=[tok(D), tok(D), tok(3 * D), tok(D), pl.BlockSpec((3, D), lambda i: (0, 0)),
                   pl.BlockSpec((1, D), lambda i: (0, 0))],
        out_shape=[jax.ShapeDtypeStruct((T, D), F32), jax.ShapeDtypeStruct((T, D), BF),
                   jax.ShapeDtypeStruct((T, 3 * D), BF), jax.ShapeDtypeStruct((T, D), BF),
                   jax.ShapeDtypeStruct((3, D), F32), jax.ShapeDtypeStruct((1, D), F32)],
        compiler_params=_params("arbitrary"),
    )(x, dy, dy, z, z, z, g, win_t, cw, wout_t)


def loss_head(y, target):
    T, D = y.shape
    tm = _tile(T, TM_MM, 8)

    def body(y_ref, t_ref, sum_ref, dy_ref):
        err = y_ref[...] - t_ref[...]
        dy_ref[...] = err * (1.0 / D)
        part = jnp.sum(jnp.sum(err * err, axis=-1, keepdims=True) * (1.0 / D), axis=0, keepdims=True)
        _accumulate(sum_ref, pl.program_id(0) == 0, jnp.broadcast_to(part, sum_ref.shape))

    return pl.pallas_call(
        body, name="loss_head", grid=(T // tm,),
        in_specs=[pl.BlockSpec((tm, D), lambda i: (i, 0))] * 2,
        out_specs=[pl.BlockSpec((8, 128), lambda i: (0, 0)), pl.BlockSpec((tm, D), lambda i: (i, 0))],
        out_shape=[jax.ShapeDtypeStruct((8, 128), F32), jax.ShapeDtypeStruct((T, D), F32)],
        compiler_params=_params("arbitrary"),
    )(y, target)


def adamw(w, g, m, v):
    R, C = w.shape
    tr = _tile(R, TR_FLAT, 8)

    def body(w_ref, g_ref, m_ref, v_ref, d_ref, m2_ref, v2_ref):
        gv = g_ref[...]
        m2 = ADAM_B1 * m_ref[...] + (1.0 - ADAM_B1) * gv
        v2 = ADAM_B2 * v_ref[...] + (1.0 - ADAM_B2) * (gv * gv)
        m2_ref[...] = m2
        v2_ref[...] = v2
        m_hat = m2 / (1.0 - ADAM_B1 ** ADAM_STEP)
        v_hat = v2 / (1.0 - ADAM_B2 ** ADAM_STEP)
        d_ref[...] = -ADAM_LR * (m_hat / (jnp.sqrt(v_hat) + ADAM_EPS) + ADAM_WD * w_ref[...])

    spec = pl.BlockSpec((tr, C), lambda i: (i, 0))
    return pl.pallas_call(
        body, name="adamw", grid=(R // tr,), in_specs=[spec] * 4, out_specs=[spec] * 3,
        out_shape=[jax.ShapeDtypeStruct((R, C), F32)] * 3,
        compiler_params=_params("arbitrary"),
    )(w, g, m, v)


def sum_slots(a):
    S, R, C = a.shape
    tr = _tile(R, TR_FLAT // 2, 16)

    def body(a_ref, o_ref):
        acc = a_ref[0].astype(F32)
        for s in range(1, S):
            acc = acc + a_ref[s].astype(F32)
        o_ref[...] = acc

    return pl.pallas_call(
        body, name="sum_slots", grid=(R // tr,),
        in_specs=[pl.BlockSpec((S, tr, C), lambda i: (0, i, 0))],
        out_specs=pl.BlockSpec((tr, C), lambda i: (i, 0)),
        out_shape=jax.ShapeDtypeStruct((R, C), F32),
        compiler_params=_params("arbitrary"),
    )(a)


ANY = pl.BlockSpec(memory_space=pl.ANY)


def _place():
    return lax.axis_index("x"), lax.axis_index("y"), lax.axis_index("c")


class _LocalCopy:
    def __init__(self, src, dst, sem, rows):
        n = LOCAL_CHUNKS if rows % (16 * LOCAL_CHUNKS) == 0 else 1
        cr = rows // n
        self.parts = [pltpu.make_async_copy(src.at[pl.ds(q * cr, cr), :], dst.at[pl.ds(q * cr, cr), :], sem)
                      for q in range(n)]
        self.whole = pltpu.make_async_copy(src, dst, sem)

    def start(self):
        for part in self.parts:
            part.start()

    def wait(self):
        self.whole.wait()


def allgather_shards(w):
    R, C = w.shape
    half = R // 2
    n = GATHER_CHUNKS if half % (16 * GATHER_CHUNKS) == 0 else 1
    cr = half // n

    def body(w_ref, out_ref, send_sems, recv_sems, local_sem):
        x, y, c = _place()
        sibling = (x, y, 1 - c)
        chips = [(1 - x, y), (x, 1 - y), (1 - x, 1 - y)]

        def rows(px, py, pc, q):
            return out_ref.at[2 * px + py, pl.ds(pc * half + q * cr, cr), :]

        def copy(k, block, q, to, src=None):
            return pltpu.make_async_remote_copy(
                src_ref=rows(*block, q) if src is None else src, dst_ref=rows(*block, q),
                send_sem=send_sems.at[k * n + q], recv_sem=recv_sems.at[k * n + q], device_id=to, device_id_type=MESH)

        mine = _LocalCopy(w_ref, out_ref.at[2 * x + y], local_sem, R)
        mine.start()
        first = [copy(j, (x, y, c), q, (*chip, c), src=w_ref.at[pl.ds(c * half + q * cr, cr), :])
                 for q in range(n) for j, chip in enumerate(chips)]
        for cp in first:
            cp.start()
        passed = []
        for q in range(n):
            for j, chip in enumerate(chips):
                copy(j, (*chip, c), q, (x, y, c)).wait_recv()
                passed.append(copy(3 + j, (*chip, c), q, sibling))
                passed[-1].start()
        for q in range(n):
            for j, chip in enumerate(chips):
                copy(3 + j, (*chip, 1 - c), q, (x, y, c)).wait_recv()
        for cp in first + passed:
            cp.wait_send()
        mine.wait()

    return pl.pallas_call(
        body, name="allgather_shards", in_specs=[ANY], out_specs=ANY,
        out_shape=jax.ShapeDtypeStruct((4, R, C), w.dtype),
        scratch_shapes=[pltpu.SemaphoreType.DMA((6 * n,)), pltpu.SemaphoreType.DMA((6 * n,)), pltpu.SemaphoreType.DMA],
    )(w)


def exchange_partials(grads, small):
    _, R, C = grads.shape
    half = R // 2
    Rs = small.shape[0]

    def body(g_ref, s_ref, land_ref, sland_ref, send_sems, recv_sems, local_sems):
        x, y, c = _place()
        me = 4 * x + 2 * y + c
        peers = []
        for mask in range(1, 8):
            mx, my, mc = (mask >> 2) & 1, (mask >> 1) & 1, mask & 1
            peers.append(((1 - x) if mx else x, (1 - y) if my else y, (1 - c) if mc else c))

        def piece(px, py, pc):
            return g_ref.at[2 * px + py, pl.ds(pc * half, half), :]

        def big(k, sender, to):
            return pltpu.make_async_remote_copy(
                src_ref=piece(*to), dst_ref=land_ref.at[sender], send_sem=send_sems.at[k], recv_sem=recv_sems.at[k],
                device_id=to, device_id_type=MESH)

        def little(k, sender, to):
            return pltpu.make_async_remote_copy(
                src_ref=s_ref, dst_ref=sland_ref.at[sender], send_sem=send_sems.at[7 + k],
                recv_sem=recv_sems.at[7 + k], device_id=to, device_id_type=MESH)

        own_big = _LocalCopy(piece(x, y, c), land_ref.at[me], local_sems.at[0], half)
        own_small = pltpu.make_async_copy(s_ref, sland_ref.at[me], local_sems.at[1])
        own_big.start()
        own_small.start()
        sends = []
        for k, peer in enumerate(peers):
            sends += [little(k, me, peer), big(k, me, peer)]
        for cp in sends:
            cp.start()
        for k, (px, py, pc) in enumerate(peers):
            sender = 4 * px + 2 * py + pc
            little(k, sender, (x, y, c)).wait_recv()
            big(k, sender, (x, y, c)).wait_recv()
        for cp in sends:
            cp.wait_send()
        own_big.wait()
        own_small.wait()

    return pl.pallas_call(
        body, name="exchange_partials", in_specs=[ANY, ANY], out_specs=[ANY, ANY],
        out_shape=[jax.ShapeDtypeStruct((8, half, C), grads.dtype), jax.ShapeDtypeStruct((8, Rs, C), small.dtype)],
        scratch_shapes=[pltpu.SemaphoreType.DMA((14,)), pltpu.SemaphoreType.DMA((14,)), pltpu.SemaphoreType.DMA((2,))],
    )(grads, small)


def share_with_sibling(part):
    half, C = part.shape
    n = SIBLING_CHUNKS if half % (16 * SIBLING_CHUNKS) == 0 else 1
    cr = half // n

    def body(p_ref, out_ref, send_sem, recv_sem, local_sem):
        x, y, c = _place()

        def rows(pc):
            return out_ref.at[pl.ds(pc * half, half), :]

        own = _LocalCopy(p_ref, rows(c), local_sem, half)
        own.start()
        for q in range(n):
            pltpu.make_async_remote_copy(
                src_ref=p_ref.at[pl.ds(q * cr, cr), :], dst_ref=out_ref.at[pl.ds(c * half + q * cr, cr), :],
                send_sem=send_sem, recv_sem=recv_sem, device_id=(x, y, 1 - c), device_id_type=MESH).start()
        everything = pltpu.make_async_remote_copy(src_ref=p_ref, dst_ref=rows(1 - c), send_sem=send_sem,
                                                  recv_sem=recv_sem, device_id=(x, y, c), device_id_type=MESH)
        everything.wait_recv()
        everything.wait_send()
        own.wait()

    return pl.pallas_call(
        body, name="share_with_sibling", in_specs=[ANY], out_specs=ANY,
        out_shape=jax.ShapeDtypeStruct((2 * half, C), part.dtype),
        scratch_shapes=[pltpu.SemaphoreType.DMA, pltpu.SemaphoreType.DMA, pltpu.SemaphoreType.DMA],
    )(part)


FLAT_SEG = 16 * FLAT_COLS


def _seg_rows(n):
    return -(-n // FLAT_SEG) * 16


def _flat_rows(sizes):
    rows = sum(_seg_rows(n) for n in sizes)
    return -(-rows // FLAT_ROW_ALIGN) * FLAT_ROW_ALIGN


def pack_flat(arrays, lead=()):
    sizes = [int(np.prod(a.shape[len(lead):])) for a in arrays]
    total = _flat_rows(sizes)
    parts, used = [], 0
    for a, n in zip(arrays, sizes):
        rows = _seg_rows(n)
        flat = a.reshape(*lead, n)
        flat = jnp.pad(flat, [(0, 0)] * len(lead) + [(0, rows * FLAT_COLS - n)])
        parts.append(flat.reshape(*lead, rows, FLAT_COLS))
        used += rows
    if total > used:
        parts.append(jnp.zeros((*lead, total - used, FLAT_COLS), arrays[0].dtype))
    return jnp.concatenate(parts, axis=len(lead))


def unpack_flat(flat, shapes, lead=()):
    out, r0 = [], 0
    for shp in shapes:
        n = int(np.prod(shp))
        rows = _seg_rows(n)
        seg = flat[..., r0:r0 + rows, :].reshape(*lead, rows * FLAT_COLS)[..., :n]
        out.append(seg.reshape(*lead, *shp))
        r0 += rows
    return out


def _f32_bits_as(a, dtype):
    return lax.bitcast_convert_type(a, dtype).reshape(*a.shape[:-1], -1)


def _f32_from_bits(a):
    k = 4 // a.dtype.itemsize
    if k > 1:
        a = a.reshape(*a.shape[:-1], a.shape[-1] // k, k)
    return lax.bitcast_convert_type(a, F32)


def _join_shards(name, a):
    if name in COL_SHARDED:
        return jnp.transpose(a, (1, 2, 0, 3)).reshape(a.shape[1], a.shape[2], 4 * a.shape[3])
    return jnp.transpose(a, (1, 0, 2, 3)).reshape(a.shape[1], 4 * a.shape[2], a.shape[3])


def _split_shards(name, a):
    L, K, N = a.shape
    if name in COL_SHARDED:
        return jnp.transpose(a.reshape(L, K, 4, N // 4), (2, 0, 1, 3))
    return jnp.transpose(a.reshape(L, 4, K // 4, N), (1, 0, 2, 3))


def _pad_heads(a, width):
    a = a.reshape(*a.shape[:-1], HEADS, width)
    a = jnp.pad(a, [(0, 0)] * (a.ndim - 1) + [(0, HEAD_SLOT - width)])
    return a.reshape(*a.shape[:-2], HEADS * HEAD_SLOT)


def _unpad_heads(a, width):
    a = a.reshape(*a.shape[:-1], HEADS, HEAD_SLOT)[..., :width]
    return a.reshape(*a.shape[:-2], HEADS * width)


def _rope_tables(T):
    pos = jnp.arange(T, dtype=F32)
    inv_freq = ROPE_THETA ** (-jnp.arange(0, ROPE_DIM, 2, dtype=F32) / ROPE_DIM)
    ang = pos[:, None] * inv_freq[None, :]
    cos, sin = jnp.cos(ang), jnp.sin(ang)
    pad = HEAD_SLOT - QK_DIM
    cos_t = jnp.concatenate([jnp.ones((T, NOPE_DIM), F32), cos, cos, jnp.zeros((T, pad), F32)], axis=1)
    sin_t = jnp.concatenate([jnp.zeros((T, NOPE_DIM), F32), sin, sin, jnp.zeros((T, pad), F32)], axis=1)
    return cos_t, sin_t


def _even_weights(W, i):
    c3 = POOL_DIM + Q_RANK + KV_RANK
    w_in = W['a_w_in'][i]
    D = w_in.shape[0]
    rope_cols = jnp.concatenate([jnp.zeros((D, NOPE_DIM), BF), w_in[:, c3:], jnp.zeros((D, HEAD_SLOT - QK_DIM), BF)], axis=1)
    win = jnp.concatenate([w_in[:, :c3], rope_cols], axis=1)
    wq = _pad_heads(W['a_w_q_up'][i], QK_DIM)
    kv = W['a_w_kv_up'][i].reshape(KV_RANK, HEADS, NOPE_DIM + V_DIM)
    wkn = _pad_heads(kv[:, :, :NOPE_DIM].reshape(KV_RANK, HEADS * NOPE_DIM), NOPE_DIM)
    wv = _pad_heads(kv[:, :, NOPE_DIM:].reshape(KV_RANK, HEADS * V_DIM), V_DIM)
    w_out = W['a_w_out'][i]
    wo_pool = w_out[:POOL_DIM]
    wo_attn = _pad_heads(w_out[POOL_DIM:].T, V_DIM).T
    wpool = W['a_w_pool'][i]
    pad = lambda a: jnp.pad(a, (0, HEAD_SLOT - QK_DIM))[None, :]
    return dict(win=win, win_t=win.T, wq=wq, wq_t=wq.T, wkn=wkn, wkn_t=wkn.T, wv=wv, wv_t=wv.T,
                wo_pool=wo_pool, wo_pool_t=wo_pool.T, wo_attn=wo_attn, wo_attn_t=wo_attn.T,
                wpool=wpool, wpool_t=jnp.transpose(wpool, (0, 2, 1)),
                qan=W['a_q_a_norm'][i][None, :], kvan=W['a_kv_a_norm'][i][None, :],
                qhn=pad(W['a_q_head_norm'][i]), khn=pad(W['a_k_head_norm'][i]),
                pscale=W['a_pool_scale'][i][None, :], g=W['mix_norm'][2 * i][None, :])


def kernel(*args):
    p = dict(zip(INPUTS, args))
    x0 = p['x'][0]
    target = p['loss_target'][0]
    T, D = x0.shape

    shard_shapes = [p[n].shape for n in SHARDED]

    wire = [_f32_bits_as(p[n], BF) if n == 'c_conv_w' else p[n].astype(BF) for n in SHARDED]
    gathered = allgather_shards(pack_flat(wire))
    W = {}
    for n, a in zip(SHARDED, unpack_flat(gathered, [a.shape for a in wire], lead=(4,))):
        W[n] = _join_shards(n, _f32_from_bits(a) if n == 'c_conv_w' else a)
    for n in REPLICATED:
        W[n] = p[n]
    W['a_w_pool'] = p['a_w_pool'].astype(BF)
    cos, sin = _rope_tables(T)

    def ffn_weights(pre, l):
        wg, wu, wd = W[pre + '_w_gate'][l], W[pre + '_w_up'][l], W[pre + '_w_down'][l]
        return dict(g=W[pre + '_norm'][l][None, :], wg=wg, wu=wu, wd=wd, wg_t=wg.T, wu_t=wu.T, wd_t=wd.T)

    saved = []
    x = x0
    for l in range(DEPTH):
        s = dict(x0=x)
        f1 = ffn_weights('ffn1', l)
        x, s['g1'], s['u1'] = ffn_fwd(x, f1['g'], f1['wg'], f1['wu'], f1['wd'])
        s['x1'] = x
        if l % 2 == 0:
            e = _even_weights(W, l // 2)
            s['q'], s['k'], s['v'], s['po'] = mixa_pre_fwd(
                x, e['g'], e['win'], e['qan'], e['wq'], e['kvan'], e['wkn'], e['wv'], e['qhn'], e['khn'],
                e['wpool'], e['pscale'], cos, sin)
            s['o'], s['lse'] = attn_fwd(s['q'], s['k'], s['v'])
            x = mm_multi([(s['po'], e['wo_pool']), (s['o'], e['wo_attn'])], res=x)
        else:
            i = l // 2
            x, s['z'] = mixc_fwd(x, W['mix_norm'][l][None, :], W['c_w_in'][i], W['c_conv_w'][i].astype(F32),
                                 W['c_w_out'][i])
        s['x2'] = x
        f2 = ffn_weights('ffn2', l)
        x, s['g2'], s['u2'] = ffn_fwd(x, f2['g'], f2['wg'], f2['wu'], f2['wd'])
        saved.append(s)

    loss_sum, dy = loss_head(x, target)
    loss = lax.psum(0.5 * loss_sum[0, 0], AXES)

    G = {n: [None] * p[n].shape[0] for n in WEIGHTS}

    def ffn_back(pre, l, x_in, gg, uu, dy):
        f = ffn_weights(pre, l)
        dx, n, dyh, h, dgate, dup, dgn = ffn_bwd(x_in, dy, f['g'], gg, uu, f['wd_t'], f['wg_t'], f['wu_t'])
        G[pre + '_norm'][l] = dgn[0]
        G[pre + '_w_gate'][l] = mm_tn(n, dgate)
        G[pre + '_w_up'][l] = mm_tn(n, dup)
        G[pre + '_w_down'][l] = mm_tn(dyh, h).T
        return dx

    t_attn = _tile(T, TQ_ATTN, 128)
    for l in reversed(range(DEPTH)):
        s = saved[l]
        dy = ffn_back('ffn2', l, s['x2'], s['g2'], s['u2'], dy)
        i = l // 2
        if l % 2 == 0:
            e = _even_weights(W, i)
            G['a_w_out'][i] = jnp.concatenate(
                [mm_tn(s['po'], dy), _unpad_heads(mm_tn(s['o'], dy).T, V_DIM).T], axis=0)
            dpo = mm_multi([(dy, e['wo_pool_t'])])
            do = mm_multi([(dy, e['wo_attn_t'])], out_dtype=BF)
            as_rows = lambda a: a.reshape(HEADS, T // t_attn, t_attn)
            dk, dv, dq_t = attn_bwd(s['q'], s['k'], s['k'].T, s['v'], do, as_rows(s['lse']),
                                    as_rows(attn_delta(s['o'], do)))
            dq = jnp.transpose(dq_t, (1, 3, 0, 2)).reshape(T, HEADS * HEAD_SLOT)
            (dy, hn, dz, nq, dqraw, nkv, dkraw, pooled, dps, dg, dqan, dkvan, dqhn, dkhn, dpscale) = mixa_pre_bwd(
                s['x1'], dy, dq, dk, dv, dpo, e['g'], e['win'], e['win_t'], e['qan'], e['wq'], e['wq_t'], e['kvan'],
                e['wkn'], e['wkn_t'], e['wv_t'], e['qhn'], e['khn'], e['wpool'], e['wpool_t'], e['pscale'], cos, sin)
            c3 = POOL_DIM + Q_RANK + KV_RANK
            dwin = mm_tn(hn, dz)
            G['a_w_in'][i] = jnp.concatenate([dwin[:, :c3], dwin[:, c3 + NOPE_DIM:c3 + QK_DIM]], axis=1)
            G['a_w_q_up'][i] = _unpad_heads(mm_tn(nq, dqraw), QK_DIM)
            dwkn = _unpad_heads(mm_tn(nkv, dkraw), NOPE_DIM).reshape(KV_RANK, HEADS, NOPE_DIM)
            dwv = _unpad_heads(mm_tn(nkv, dv), V_DIM).reshape(KV_RANK, HEADS, V_DIM)
            G['a_w_kv_up'][i] = jnp.concatenate([dwkn, dwv], axis=2).reshape(KV_RANK, HEADS * (NOPE_DIM + V_DIM))
            dwp = mm_tn(pooled, dps)
            G['a_w_pool'][i] = jnp.stack([dwp[g * POOL_GROUP:(g + 1) * POOL_GROUP, g * POOL_GROUP:(g + 1) * POOL_GROUP]
                                          for g in range(len(POOL_WINDOWS))])
            G['mix_norm'][l] = dg[0]
            G['a_q_a_norm'][i] = dqan[0]
            G['a_kv_a_norm'][i] = dkvan[0]
            G['a_q_head_norm'][i] = dqhn[0, :QK_DIM]
            G['a_k_head_norm'][i] = dkhn[0, :QK_DIM]
            G['a_pool_scale'][i] = dpscale[0]
        else:
            w_in, w_out = W['c_w_in'][i], W['c_w_out'][i]
            dy_in = dy
            dy, hn, dz, gated, dcw, dg = mixc_bwd(s['x1'], dy, s['z'], W['mix_norm'][l][None, :], w_in.T,
                                                  W['c_conv_w'][i].astype(F32), w_out.T)
            G['c_w_in'][i] = mm_tn(hn, dz)
            G['c_w_out'][i] = mm_tn(gated, dy_in)
            G['c_conv_w'][i] = dcw
            G['mix_norm'][l] = dg[0]
        dy = ffn_back('ffn1', l, s['x0'], s['g1'], s['u1'], dy)
    grad_x = dy[None]

    G = {n: jnp.stack(v) for n, v in G.items()}
    partial_big = pack_flat([_split_shards(n, G[n]) for n in SHARDED], lead=(4,)).astype(BF)
    small_shapes = [p[n].shape for n in REPLICATED]
    partial_small = pack_flat([G[n] for n in REPLICATED])
    land, sland = exchange_partials(partial_big, partial_small)
    g_big = share_with_sibling(sum_slots(land))
    g_small = sum_slots(sland)

    outs = {}
    for names, shapes, g_flat in ((SHARDED, shard_shapes, g_big), (REPLICATED, small_shapes, g_small)):
        flat = lambda pre: pack_flat([p[pre + n] for n in names])
        delta, m2, v2 = adamw(flat(''), g_flat, flat('m_'), flat('v_'))
        for kind, arr in (('grad_', g_flat), ('delta_', delta), ('new_m_', m2), ('new_v_', v2)):
            for n, a in zip(names, unpack_flat(arr, shapes)):
                outs[kind + n] = a
    return (loss, grad_x, *[outs[k + n] for k in ('grad_', 'delta_', 'new_m_', 'new_v_') for n in WEIGHTS])
```

```python
import functools

import numpy as np
import jax
import jax.numpy as jnp
from jax import lax
from jax.experimental import pallas as pl
from jax.experimental.pallas import tpu as pltpu

BF, F32 = jnp.bfloat16, jnp.float32
MESH = pl.DeviceIdType.MESH
AXES = ("x", "y", "c")

NORM_EPS = 1e-6
DEPTH = 4
HEADS = 8
HEAD_SLOT = 128
QK_DIM, NOPE_DIM, ROPE_DIM, V_DIM = 96, 64, 32, 64
POOL_WINDOWS = (2, 4, 8, 16)
POOL_DIM, POOL_GROUP = 512, 128
Q_RANK, KV_RANK = 384, 256
ROPE_THETA = 10000.0
HALO = 16
ATTN_SCALE = QK_DIM ** -0.5

ADAM_LR, ADAM_B1, ADAM_B2, ADAM_EPS, ADAM_WD, ADAM_STEP = 0.001, 0.9, 0.999, 1e-08, 0.01, 10

TM_FFN_FWD, TM_FFN_BWD, TF_FFN = 512, 256, 256
TM_MIX_FWD, TM_MIX_BWD = 512, 256
TM_CONV_FWD, TM_CONV_BWD = 256, 256
TQ_ATTN = 512
TM_MM = 512
TK_TN, BM_TN, BN_TN = 2048, 1024, 1536
FLAT_COLS = 1024
FLAT_ROW_ALIGN = 1024
SIBLING_CHUNKS = 16
LOCAL_CHUNKS = 16
GATHER_CHUNKS = 8
TR_FLAT = 256
VMEM_LIMIT = 56 * 1024 * 1024

WEIGHTS = ['ffn1_norm', 'ffn1_w_gate', 'ffn1_w_up', 'ffn1_w_down', 'mix_norm', 'ffn2_norm', 'ffn2_w_gate',
           'ffn2_w_up', 'ffn2_w_down', 'a_w_in', 'a_q_a_norm', 'a_w_q_up', 'a_kv_a_norm', 'a_w_kv_up',
           'a_q_head_norm', 'a_k_head_norm', 'a_w_pool', 'a_pool_scale', 'a_w_out', 'c_w_in', 'c_conv_w',
           'c_w_out']
COL_SHARDED = ('ffn1_w_gate', 'ffn1_w_up', 'ffn2_w_gate', 'ffn2_w_up', 'a_w_in', 'a_w_q_up', 'a_w_kv_up',
               'c_w_in', 'c_conv_w')
ROW_SHARDED = ('ffn1_w_down', 'ffn2_w_down', 'a_w_out', 'c_w_out')
SHARDED = tuple(n for n in WEIGHTS if n in COL_SHARDED or n in ROW_SHARDED)
REPLICATED = tuple(n for n in WEIGHTS if n not in SHARDED)
INPUTS = ['x'] + WEIGHTS + ['loss_target'] + ['m_' + n for n in WEIGHTS] + ['v_' + n for n in WEIGHTS]


def _dot(a, b):
    return jnp.dot(a, b, preferred_element_type=F32)


def _dot_nt(a, b):
    return lax.dot_general(a, b, (((1,), (1,)), ((), ())), preferred_element_type=F32)


def _dot_tn(a, b):
    return lax.dot_general(a, b, (((0,), (0,)), ((), ())), preferred_element_type=F32)


def _params(*sem):
    return pltpu.CompilerParams(dimension_semantics=sem or None, vmem_limit_bytes=VMEM_LIMIT)


def _tile(n, cap, unit):
    if n <= cap:
        return n
    best = None
    for t in range(unit, cap + 1, unit):
        if n % t == 0:
            best = t
    assert best is not None, (n, cap, unit)
    return best


def _rms(x, width=None):
    ms = jnp.sum(x * x, axis=-1, keepdims=True) * (1.0 / (width or x.shape[-1]))
    r = lax.rsqrt(ms + NORM_EPS)
    return x * r, r


def _rms_bwd(a, xhat, r, width=None):
    return r * (a - xhat * (jnp.sum(a * xhat, axis=-1, keepdims=True) * (1.0 / (width or a.shape[-1]))))


def _colsum(a):
    return jnp.sum(a, axis=0, keepdims=True)


def _accumulate(ref, first, value):
    @pl.when(first)
    def _():
        ref[...] = value

    @pl.when(jnp.logical_not(first))
    def _():
        ref[...] += value


def _rot_half(v):
    lane = lax.broadcasted_iota(jnp.int32, v.shape, 1)
    rot = jnp.where(lane < NOPE_DIM + ROPE_DIM // 2, -pltpu.roll(v, HEAD_SLOT - ROPE_DIM // 2, 1),
                    pltpu.roll(v, ROPE_DIM // 2, 1))
    return jnp.where((lane >= NOPE_DIM) & (lane < QK_DIM), rot, 0.0)


def _rope(v, cos, sin):
    return v * cos + _rot_half(v) * sin


def _rope_bwd(d, cos, sin):
    return d * cos - _rot_half(d * sin)


def _resident(arr):
    return pl.BlockSpec(arr.shape, lambda i: (0,) * arr.ndim, pipeline_mode=pl.Buffered(1))


def ffn_fwd(x, g, wg, wu, wd):
    T, D = x.shape
    F = wg.shape[1]
    tm, tf = _tile(T, TM_FFN_FWD, 8), _tile(F, TF_FFN, 128)
    nf = F // tf

    def body(x_ref, g_ref, wg_ref, wu_ref, wd_ref, y_ref, gg_ref, uu_ref, h_sc):
        xv = x_ref[...]
        xh, _ = _rms(xv)
        n = (xh * g_ref[...]).astype(BF)

        def projections(c):
            cols = slice(c * tf, (c + 1) * tf)
            return _dot(n, wg_ref[:, cols]), _dot(n, wu_ref[:, cols])

        ahead = projections(0)
        for c in range(nf):
            gg, uu = ahead
            if c + 1 < nf:
                ahead = projections(c + 1)
            cols = slice(c * tf, (c + 1) * tf)
            gg_ref[:, cols] = gg.astype(BF)
            uu_ref[:, cols] = uu.astype(BF)
            h_sc[:, cols] = (gg * jax.nn.sigmoid(gg) * uu).astype(BF)
        y_ref[...] = xv + 0.5 * _dot(h_sc[...], wd_ref[...])

    tok = lambda w: pl.BlockSpec((tm, w), lambda i: (i, 0))
    return pl.pallas_call(
        body, name="ffn_fwd", grid=(T // tm,),
        in_specs=[tok(D), _resident(g), _resident(wg), _resident(wu), _resident(wd)],
        out_specs=[tok(D), tok(F), tok(F)],
        out_shape=[jax.ShapeDtypeStruct((T, D), F32), jax.ShapeDtypeStruct((T, F), BF),
                   jax.ShapeDtypeStruct((T, F), BF)],
        scratch_shapes=[pltpu.VMEM((tm, F), BF)],
        compiler_params=_params("arbitrary"),
    )(x, g, wg, wu, wd)


def ffn_bwd(x, dy, g, gg, uu, wd_t, wg_t, wu_t):
    T, D = x.shape
    F = gg.shape[1]
    tm, tf = _tile(T, TM_FFN_BWD, 8), _tile(F, TF_FFN, 128)
    nf = F // tf

    def body(x_ref, dy_ref, g_ref, gg_ref, uu_ref, wdt_ref, wgt_ref, wut_ref,
             dx_ref, n_ref, dyh_ref, h_ref, dg_ref, du_ref, dgn_ref):
        xh, r = _rms(x_ref[...])
        n_ref[...] = (xh * g_ref[...]).astype(BF)
        dyv = dy_ref[...]
        dyh = (0.5 * dyv).astype(BF)
        dyh_ref[...] = dyh

        def hidden_grad(c):
            return _dot(dyh, wdt_ref[:, c * tf:(c + 1) * tf])

        ahead = hidden_grad(0)
        for c in range(nf):
            dh = ahead
            if c + 1 < nf:
                ahead = hidden_grad(c + 1)
            cols = slice(c * tf, (c + 1) * tf)
            gv = gg_ref[:, cols].astype(F32)
            uv = uu_ref[:, cols].astype(F32)
            sg = jax.nn.sigmoid(gv)
            silu = gv * sg
            h_ref[:, cols] = (silu * uv).astype(BF)
            du_ref[:, cols] = (dh * silu).astype(BF)
            dg_ref[:, cols] = (dh * uv * (sg * (1.0 + gv * (1.0 - sg)))).astype(BF)
        dn = _dot(dg_ref[...], wgt_ref[...]) + _dot(du_ref[...], wut_ref[...])
        dx_ref[...] = dyv + _rms_bwd(dn * g_ref[...], xh, r)
        _accumulate(dgn_ref, pl.program_id(0) == 0, _colsum(dn * xh))

    tok = lambda w: pl.BlockSpec((tm, w), lambda i: (i, 0))
    return pl.pallas_call(
        body, name="ffn_bwd", grid=(T // tm,),
        in_specs=[tok(D), tok(D), _resident(g), tok(F), tok(F), _resident(wd_t), _resident(wg_t), _resident(wu_t)],
        out_specs=[tok(D), tok(D), tok(D), tok(F), tok(F), tok(F), pl.BlockSpec((1, D), lambda i: (0, 0))],
        out_shape=[jax.ShapeDtypeStruct((T, D), F32), jax.ShapeDtypeStruct((T, D), BF),
                   jax.ShapeDtypeStruct((T, D), BF), jax.ShapeDtypeStruct((T, F), BF),
                   jax.ShapeDtypeStruct((T, F), BF), jax.ShapeDtypeStruct((T, F), BF),
                   jax.ShapeDtypeStruct((1, D), F32)],
        compiler_params=_params("arbitrary"),
    )(x, dy, g, gg, uu, wd_t, wg_t, wu_t)


def mm_tn(a, b):
    T, M = a.shape
    N = b.shape[1]
    tk, bm, bn = _tile(T, TK_TN, 16), _tile(M, BM_TN, 128), _tile(N, BN_TN, 128)

    def body(a_ref, b_ref, o_ref):
        part = _dot_tn(a_ref[...].astype(BF), b_ref[...].astype(BF))
        _accumulate(o_ref, pl.program_id(2) == 0, part)

    return pl.pallas_call(
        body, name="mm_tn", grid=(M // bm, N // bn, T // tk),
        in_specs=[pl.BlockSpec((tk, bm), lambda i, j, k: (k, i)), pl.BlockSpec((tk, bn), lambda i, j, k: (k, j))],
        out_specs=pl.BlockSpec((bm, bn), lambda i, j, k: (i, j)),
        out_shape=jax.ShapeDtypeStruct((M, N), F32),
        compiler_params=_params("arbitrary", "arbitrary", "arbitrary"),
    )(a, b)


def mm_multi(pairs, res=None, out_dtype=F32):
    T = pairs[0][0].shape[0]
    N = pairs[0][1].shape[1]
    tm = _tile(T, TM_MM, 16)
    n = len(pairs)

    def body(*refs):
        o_ref = refs[-1]
        acc = refs[2 * n][...] if res is not None else None
        for k in range(n):
            part = _dot(refs[k][...].astype(BF), refs[n + k][...])
            acc = part if acc is None else acc + part
        o_ref[...] = acc.astype(out_dtype)

    ins = [a for a, _ in pairs] + [w for _, w in pairs]
    specs = [pl.BlockSpec((tm, a.shape[1]), lambda i: (i, 0)) for a, _ in pairs]
    specs += [pl.BlockSpec(w.shape, lambda i: (0, 0)) for _, w in pairs]
    if res is not None:
        ins.append(res)
        specs.append(pl.BlockSpec((tm, N), lambda i: (i, 0)))
    return pl.pallas_call(
        body, name="mm_multi", grid=(T // tm,), in_specs=specs,
        out_specs=pl.BlockSpec((tm, N), lambda i: (i, 0)),
        out_shape=jax.ShapeDtypeStruct((T, N), out_dtype),
        compiler_params=_params("arbitrary"),
    )(*ins)


def _causal_mask(t, q_major):
    r = lax.broadcasted_iota(jnp.int32, (t, t), 0)
    c = lax.broadcasted_iota(jnp.int32, (t, t), 1)
    return (c <= r) if q_major else (r <= c)


def _blocks_transposed(a, t):
    T = a.shape[0]
    return jnp.transpose(a.reshape(T // t, t, HEADS, HEAD_SLOT), (2, 0, 3, 1))


def _blocks_untransposed(a):
    H, n, d, t = a.shape
    return jnp.transpose(a, (1, 3, 0, 2)).reshape(n * t, H * d)


def attn_fwd(q, k, v_t):
    T = q.shape[0]
    t = _tile(T, TQ_ATTN, 128)
    nq = T // t

    def body(q_ref, k_ref, vt_ref, ot_ref, lse_ref):
        i = pl.program_id(1)
        qv = q_ref[...]

        def update(st, j, m, l, acc):
            m2 = jnp.maximum(m, jnp.max(st, axis=0, keepdims=True))
            pt = jnp.exp(st - m2)
            scale = jnp.exp(m - m2)
            return (m2, scale * l + jnp.sum(pt, axis=0, keepdims=True),
                    scale * acc + _dot(vt_ref[j], pt.astype(BF)))

        def scores(j, masked):
            st = _dot_nt(k_ref[pl.ds(pl.multiple_of(j * t, t), t), :], qv)
            return jnp.where(_causal_mask(t, False), st, -jnp.inf) if masked else st

        def pair(j, carry, last_is_diagonal):
            s0, s1 = scores(j, False), scores(j + 1, last_is_diagonal)
            return update(s1, j + 1, *update(s0, j, *carry))

        init = (jnp.full((1, t), -1e30, F32), jnp.zeros((1, t), F32), jnp.zeros((HEAD_SLOT, t), F32))
        carry = lax.fori_loop(0, i // 2, lambda jj, c: pair(2 * jj, c, False), init)
        m, l, acc = lax.cond(i % 2 == 1, lambda c: pair(i - 1, c, True),
                             lambda c: update(scores(i, True), i, *c), carry)
        ot_ref[...] = (acc / l).astype(BF)
        lse_ref[...] = m + jnp.log(l)

    return pl.pallas_call(
        body, name="attn_fwd", grid=(HEADS, nq),
        in_specs=[pl.BlockSpec((t, HEAD_SLOT), lambda h, i: (i, h)), pl.BlockSpec((T, HEAD_SLOT), lambda h, i: (0, h)),
                  pl.BlockSpec((None, nq, HEAD_SLOT, t), lambda h, i: (h, 0, 0, 0))],
        out_specs=[pl.BlockSpec((None, None, HEAD_SLOT, t), lambda h, i: (h, i, 0, 0)),
                   pl.BlockSpec((None, None, 1, t), lambda h, i: (h, i, 0, 0))],
        out_shape=[jax.ShapeDtypeStruct((HEADS, nq, HEAD_SLOT, t), BF), jax.ShapeDtypeStruct((HEADS, nq, 1, t), F32)],
        compiler_params=_params("arbitrary", "arbitrary"),
    )(q, k, v_t)


def attn_delta(o, do):
    T = o.shape[0]
    t = _tile(T, TQ_ATTN, 128)

    def body(o_ref, do_ref, delta_ref):
        delta_ref[...] = jnp.sum(do_ref[...].astype(F32) * o_ref[...].astype(F32), axis=-1, keepdims=True)

    blk = pl.BlockSpec((t, HEAD_SLOT), lambda h, i: (i, h))
    return pl.pallas_call(
        body, name="attn_delta", grid=(HEADS, T // t), in_specs=[blk, blk],
        out_specs=pl.BlockSpec((None, t, 1), lambda h, i: (h, i, 0)),
        out_shape=jax.ShapeDtypeStruct((HEADS, T, 1), F32),
        compiler_params=_params("arbitrary", "arbitrary"),
    )(o, do)


def attn_bwd(q, k, k_t, v, do, lse_rows, delta_rows):
    T = q.shape[0]
    t = _tile(T, TQ_ATTN, 128)
    nq = T // t

    def body(q_ref, k_ref, kt_ref, v_ref, do_ref, lse_ref, delta_ref, dk_ref, dv_ref, dqt_ref):
        j = pl.program_id(1)
        kv, vv, ktv = k_ref[...], v_ref[...], kt_ref[...]

        @pl.when(j == 0)
        def _():
            dqt_ref[...] = jnp.zeros_like(dqt_ref)

        def block(i):
            return pl.ds(pl.multiple_of(i * t, t), t)

        def scores(i, masked):
            st = _dot_nt(kv, q_ref[block(i), :])
            return jnp.where(_causal_mask(t, False), st, -jnp.inf) if masked else st

        def add(carry, st, i):
            dk, dv = carry
            qv, dov = q_ref[block(i), :], do_ref[block(i), :]
            pt = jnp.exp(st - lse_ref[pl.ds(i, 1), :])
            dst = (pt * (_dot_nt(vv, dov) - delta_ref[pl.ds(i, 1), :])).astype(BF)
            dqt_ref[i] += _dot(ktv, dst)
            return dk + _dot(dst, qv), dv + _dot(pt.astype(BF), dov)

        def pair(i, carry):
            s0, s1 = scores(i, False), scores(i + 1, False)
            return add(add(carry, s0, i), s1, i + 1)

        zero = jnp.zeros((t, HEAD_SLOT), F32)
        carry = add((zero, zero), scores(j, True), j)
        rest = nq - 1 - j
        carry = lax.fori_loop(0, rest // 2, lambda ii, c: pair(j + 1 + 2 * ii, c), carry)
        dk, dv = lax.cond(rest % 2 == 1, lambda c: add(c, scores(nq - 1, False), nq - 1), lambda c: c, carry)
        dk_ref[...] = dk
        dv_ref[...] = dv

    blk = pl.BlockSpec((t, HEAD_SLOT), lambda h, j: (j, h))
    full = pl.BlockSpec((T, HEAD_SLOT), lambda h, j: (0, h))
    rows = pl.BlockSpec((None, nq, t), lambda h, j: (h, 0, 0))
    return pl.pallas_call(
        body, name="attn_bwd", grid=(HEADS, nq),
        in_specs=[full, blk, pl.BlockSpec((HEAD_SLOT, t), lambda h, j: (h, j)), blk, full, rows, rows],
        out_specs=[blk, blk, pl.BlockSpec((None, nq, HEAD_SLOT, t), lambda h, j: (h, 0, 0, 0))],
        out_shape=[jax.ShapeDtypeStruct((T, HEADS * HEAD_SLOT), F32)] * 2
                  + [jax.ShapeDtypeStruct((HEADS, nq, HEAD_SLOT, t), F32)],
        compiler_params=_params("arbitrary", "arbitrary"),
    )(q, k, k_t, v, do, lse_rows, delta_rows)


def _prev_halo(tm):
    return lambda i: (jnp.maximum(i * (tm // HALO) - 1, 0), 0)


def _next_halo(tm, T):
    return lambda i: (jnp.minimum((i + 1) * (tm // HALO), T // HALO - 1), 0)


def _inv_count(row0, n, w):
    t = row0 + lax.broadcasted_iota(jnp.int32, (n, 1), 0)
    return 1.0 / jnp.minimum(t + 1, w).astype(F32)


def _pool_fwd(u_prev, u, row0):
    tm = u.shape[0]
    out = []
    for g, w in enumerate(POOL_WINDOWS):
        lanes = slice(g * POOL_GROUP, (g + 1) * POOL_GROUP)
        ue = jnp.concatenate([u_prev[:, lanes], u[:, lanes]], axis=0)
        s, step = ue, 1
        while step < w:
            s = s + pltpu.roll(s, step, 0)
            step *= 2
        out.append(s[HALO:, :] * _inv_count(row0, tm, w) - u[:, lanes])
    return out


def _pool_bwd(dp, dp_next, row0):
    tm = dp[0].shape[0]
    out = []
    for g, w in enumerate(POOL_WINDOWS):
        e = jnp.concatenate([dp[g] * _inv_count(row0, tm, w), dp_next[g] * (1.0 / w)], axis=0)
        n = tm + HALO
        s, step = e, 1
        while step < w:
            s = s + pltpu.roll(s, n - step, 0)
            step *= 2
        out.append(s[:tm, :] - dp[g])
    return out


def _mixa_front(x, xp, first, row0, g_ref, win_ref, qan_ref, wq_ref, kvan_ref, wkn_ref):
    xh, r = _rms(x)
    hn = (xh * g_ref[...]).astype(BF)
    z = _dot(hn, win_ref[...])
    xph, _ = _rms(xp)
    u_prev = _dot((xph * g_ref[...]).astype(BF), win_ref[:, :POOL_DIM]) * jnp.where(first, 0.0, 1.0)
    u = z[:, :POOL_DIM]
    pooled = _pool_fwd(u_prev, u, row0)
    c1, c2 = POOL_DIM + Q_RANK, POOL_DIM + Q_RANK + KV_RANK
    qh, rq = _rms(z[:, POOL_DIM:c1])
    nq = (qh * qan_ref[...]).astype(BF)
    kh, rk = _rms(z[:, c1:c2])
    nkv = (kh * kvan_ref[...]).astype(BF)
    qraw = _dot(nq, wq_ref[...])
    kraw = _dot(nkv, wkn_ref[...])
    krope = z[:, c2:c2 + HEAD_SLOT]
    return dict(xh=xh, r=r, hn=hn, pooled=pooled, qh=qh, rq=rq, nq=nq, kh=kh, rk=rk, nkv=nkv,
                qraw=qraw, kraw=kraw, krope=krope)


def mixa_pre_fwd(x, g, win, qan, wq, kvan, wkn, wv, qhn, khn, wpool, pscale, cos, sin):
    T, D = x.shape
    tm = _tile(T, TM_MIX_FWD, HALO)
    HS = HEADS * HEAD_SLOT

    def body(x_ref, xp_ref, g_ref, win_ref, qan_ref, wq_ref, kvan_ref, wkn_ref, wv_ref, qhn_ref, khn_ref,
             wpool_ref, pscale_ref, cos_ref, sin_ref, q_ref, k_ref, v_ref, po_ref):
        i = pl.program_id(0)
        a = _mixa_front(x_ref[...], xp_ref[...], i == 0, i * tm, g_ref, win_ref, qan_ref, wq_ref, kvan_ref, wkn_ref)
        for gi in range(len(POOL_WINDOWS)):
            lanes = slice(gi * POOL_GROUP, (gi + 1) * POOL_GROUP)
            po = _dot(a["pooled"][gi].astype(BF), wpool_ref[gi]) * pscale_ref[:, lanes]
            po_ref[:, lanes] = po.astype(BF)
        cosv, sinv = cos_ref[...], sin_ref[...]
        v_ref[...] = _dot(a["nkv"], wv_ref[...]).astype(BF)
        for h in range(HEADS):
            lanes = slice(h * HEAD_SLOT, (h + 1) * HEAD_SLOT)
            qn, _ = _rms(a["qraw"][:, lanes], QK_DIM)
            q_ref[:, lanes] = (_rope(qn * qhn_ref[...], cosv, sinv) * ATTN_SCALE).astype(BF)
            kn, _ = _rms(a["kraw"][:, lanes] + a["krope"], QK_DIM)
            k_ref[:, lanes] = _rope(kn * khn_ref[...], cosv, sinv).astype(BF)

    tok = lambda w: pl.BlockSpec((tm, w), lambda i: (i, 0))
    whole = lambda arr: pl.BlockSpec(arr.shape, lambda i: (0,) * arr.ndim)
    return pl.pallas_call(
        body, name="mixa_pre_fwd", grid=(T // tm,),
        in_specs=[tok(D), pl.BlockSpec((HALO, D), _prev_halo(tm))] + [whole(a) for a in
                  (g, win, qan, wq, kvan, wkn, wv, qhn, khn, wpool, pscale)] + [tok(HEAD_SLOT), tok(HEAD_SLOT)],
        out_specs=[tok(HS), tok(HS), tok(HS), tok(POOL_DIM)],
        out_shape=[jax.ShapeDtypeStruct((T, HS), BF)] * 3 + [jax.ShapeDtypeStruct((T, POOL_DIM), BF)],
        compiler_params=_params("arbitrary"),
    )(x, x, g, win, qan, wq, kvan, wkn, wv, qhn, khn, wpool, pscale, cos, sin)


def mixa_pre_bwd(x, dy, dq, dk, dv, dpo, g, win, win_t, qan, wq, wq_t, kvan, wkn, wkn_t, wv_t, qhn, khn,
                 wpool, wpool_t, pscale, cos, sin):
    T, D = x.shape
    tm = _tile(T, TM_MIX_BWD, HALO)
    HS = HEADS * HEAD_SLOT
    ZW = win.shape[1]
    nt = T // tm

    def body(x_ref, xp_ref, dy_ref, dq_ref, dk_ref, dv_ref, dpo_ref, dpon_ref, g_ref, win_ref, wint_ref, qan_ref,
             wq_ref, wqt_ref, kvan_ref, wkn_ref, wknt_ref, wvt_ref, qhn_ref, khn_ref, wpool_ref, wpoolt_ref,
             pscale_ref, cos_ref, sin_ref,
             dx_ref, hn_ref, dz_ref, nq_ref, dqraw_ref, nkv_ref, dkraw_ref, pooled_ref, dps_ref,
             dg_ref, dqan_ref, dkvan_ref, dqhn_ref, dkhn_ref, dpscale_ref):
        i = pl.program_id(0)
        first = i == 0
        a = _mixa_front(x_ref[...], xp_ref[...], first, i * tm, g_ref, win_ref, qan_ref, wq_ref, kvan_ref, wkn_ref)
        cosv, sinv = cos_ref[...], sin_ref[...]
        hn_ref[...] = a["hn"]
        nq_ref[...] = a["nq"]
        nkv_ref[...] = a["nkv"]

        has_next = jnp.where(i == nt - 1, 0.0, 1.0)
        dpool, dpool_next, dpscale = [], [], []
        for gi in range(len(POOL_WINDOWS)):
            lanes = slice(gi * POOL_GROUP, (gi + 1) * POOL_GROUP)
            pooled = a["pooled"][gi].astype(BF)
            pooled_ref[:, lanes] = pooled
            dpo_g = dpo_ref[:, lanes]
            dpscale.append(_colsum(dpo_g * _dot(pooled, wpool_ref[gi])))
            dps = (dpo_g * pscale_ref[:, lanes]).astype(BF)
            dps_ref[:, lanes] = dps
            dpool.append(_dot(dps, wpoolt_ref[gi]))
            dps_n = (dpon_ref[:, lanes] * pscale_ref[:, lanes] * has_next).astype(BF)
            dpool_next.append(_dot(dps_n, wpoolt_ref[gi]))
        du = jnp.concatenate(_pool_bwd(dpool, dpool_next, i * tm), axis=1)
        _accumulate(dpscale_ref, first, jnp.concatenate(dpscale, axis=1))

        dqhn = jnp.zeros((1, HEAD_SLOT), F32)
        dkhn = jnp.zeros((1, HEAD_SLOT), F32)
        dkrope = jnp.zeros((tm, HEAD_SLOT), F32)
        for h in range(HEADS):
            lanes = slice(h * HEAD_SLOT, (h + 1) * HEAD_SLOT)
            qhat, rq = _rms(a["qraw"][:, lanes], QK_DIM)
            dqn = _rope_bwd(dq_ref[:, lanes] * ATTN_SCALE, cosv, sinv)
            dqhn = dqhn + _colsum(dqn * qhat)
            dqraw_ref[:, lanes] = _rms_bwd(dqn * qhn_ref[...], qhat, rq, QK_DIM).astype(BF)
            khat, rk = _rms(a["kraw"][:, lanes] + a["krope"], QK_DIM)
            dkn = _rope_bwd(dk_ref[:, lanes], cosv, sinv)
            dkhn = dkhn + _colsum(dkn * khat)
            dkraw = _rms_bwd(dkn * khn_ref[...], khat, rk, QK_DIM)
            dkrope = dkrope + dkraw
            dkraw_ref[:, lanes] = dkraw.astype(BF)
        _accumulate(dqhn_ref, first, dqhn)
        _accumulate(dkhn_ref, first, dkhn)

        dnq = _dot(dqraw_ref[...], wqt_ref[...])
        _accumulate(dqan_ref, first, _colsum(dnq * a["qh"]))
        dql = _rms_bwd(dnq * qan_ref[...], a["qh"], a["rq"])
        dnkv = _dot(dkraw_ref[...], wknt_ref[...]) + _dot(dv_ref[...].astype(BF), wvt_ref[...])
        _accumulate(dkvan_ref, first, _colsum(dnkv * a["kh"]))
        dkvl = _rms_bwd(dnkv * kvan_ref[...], a["kh"], a["rk"])

        dz = jnp.concatenate([du, dql, dkvl, dkrope], axis=1).astype(BF)
        dz_ref[...] = dz
        dhn = _dot(dz, wint_ref[...])
        _accumulate(dg_ref, first, _colsum(dhn * a["xh"]))
        dx_ref[...] = dy_ref[...] + _rms_bwd(dhn * g_ref[...], a["xh"], a["r"])

    tok = lambda w: pl.BlockSpec((tm, w), lambda i: (i, 0))
    whole = lambda arr: pl.BlockSpec(arr.shape, lambda i: (0,) * arr.ndim)
    row = lambda w: pl.BlockSpec((1, w), lambda i: (0, 0))
    weights = (g, win, win_t, qan, wq, wq_t, kvan, wkn, wkn_t, wv_t, qhn, khn, wpool, wpool_t, pscale)
    return pl.pallas_call(
        body, name="mixa_pre_bwd", grid=(nt,),
        in_specs=[tok(D), pl.BlockSpec((HALO, D), _prev_halo(tm)), tok(D), tok(HS), tok(HS), tok(HS), tok(POOL_DIM),
                  pl.BlockSpec((HALO, POOL_DIM), _next_halo(tm, T))] + [whole(a) for a in weights]
                 + [tok(HEAD_SLOT), tok(HEAD_SLOT)],
        out_specs=[tok(D), tok(D), tok(ZW), tok(Q_RANK), tok(HS), tok(KV_RANK), tok(HS), tok(POOL_DIM), tok(POOL_DIM),
                   row(D), row(Q_RANK), row(KV_RANK), row(HEAD_SLOT), row(HEAD_SLOT), row(POOL_DIM)],
        out_shape=[jax.ShapeDtypeStruct((T, D), F32), jax.ShapeDtypeStruct((T, D), BF),
                   jax.ShapeDtypeStruct((T, ZW), BF), jax.ShapeDtypeStruct((T, Q_RANK), BF),
                   jax.ShapeDtypeStruct((T, HS), BF), jax.ShapeDtypeStruct((T, KV_RANK), BF),
                   jax.ShapeDtypeStruct((T, HS), BF), jax.ShapeDtypeStruct((T, POOL_DIM), BF),
                   jax.ShapeDtypeStruct((T, POOL_DIM), BF),
                   jax.ShapeDtypeStruct((1, D), F32), jax.ShapeDtypeStruct((1, Q_RANK), F32),
                   jax.ShapeDtypeStruct((1, KV_RANK), F32), jax.ShapeDtypeStruct((1, HEAD_SLOT), F32),
                   jax.ShapeDtypeStruct((1, HEAD_SLOT), F32), jax.ShapeDtypeStruct((1, POOL_DIM), F32)],
        compiler_params=_params("arbitrary"),
    )(x, x, dy, dq, dk, dv, dpo, dpo, *weights, cos, sin)


def _conv_taps(u_prev, u, cw_ref):
    ue = jnp.concatenate([u_prev, u], axis=0)
    u1 = pltpu.roll(ue, 1, 0)[HALO:, :]
    u2 = pltpu.roll(ue, 2, 0)[HALO:, :]
    return cw_ref[0:1, :] * u2 + cw_ref[1:2, :] * u1 + cw_ref[2:3, :] * u, u1, u2


def mixc_fwd(x, g, win, cw, wout):
    T, D = x.shape
    tm = _tile(T, TM_CONV_FWD, HALO)

    def body(x_ref, xp_ref, g_ref, win_ref, cw_ref, wout_ref, y_ref, z_ref):
        i = pl.program_id(0)
        xv = x_ref[...]
        xh, _ = _rms(xv)
        z = _dot((xh * g_ref[...]).astype(BF), win_ref[...])
        z_ref[...] = z.astype(BF)
        xph, _ = _rms(xp_ref[...])
        zp = _dot((xph * g_ref[...]).astype(BF), win_ref[:, D:])
        u_prev = zp[:, :D] * zp[:, D:] * jnp.where(i == 0, 0.0, 1.0)
        conv, _, _ = _conv_taps(u_prev, z[:, D:2 * D] * z[:, 2 * D:], cw_ref)
        y_ref[...] = xv + _dot((z[:, :D] * conv).astype(BF), wout_ref[...])

    tok = lambda w: pl.BlockSpec((tm, w), lambda i: (i, 0))
    whole = lambda arr: pl.BlockSpec(arr.shape, lambda i: (0,) * arr.ndim)
    return pl.pallas_call(
        body, name="mixc_fwd", grid=(T // tm,),
        in_specs=[tok(D), pl.BlockSpec((HALO, D), _prev_halo(tm)), whole(g), whole(win), whole(cw), whole(wout)],
        out_specs=[tok(D), tok(3 * D)],
        out_shape=[jax.ShapeDtypeStruct((T, D), F32), jax.ShapeDtypeStruct((T, 3 * D), BF)],
        compiler_params=_params("arbitrary"),
    )(x, x, g, win, cw, wout)


def mixc_bwd(x, dy, z, g, win_t, cw, wout_t):
    T, D = x.shape
    tm = _tile(T, TM_CONV_BWD, HALO)
    nt = T // tm

    def body(x_ref, dy_ref, dyn_ref, z_ref, zp_ref, zn_ref, g_ref, wint_ref, cw_ref, woutt_ref,
             dx_ref, hn_ref, dz_ref, v_ref, dcw_ref, dg_ref):
        i = pl.program_id(0)
        first = i == 0
        xh, r = _rms(x_ref[...])
        hn_ref[...] = (xh * g_ref[...]).astype(BF)
        zv = z_ref[...].astype(F32)
        gb, gc, hh = zv[:, :D], zv[:, D:2 * D], zv[:, 2 * D:]
        u = gc * hh
        zp = zp_ref[...].astype(F32)
        u_prev = zp[:, D:2 * D] * zp[:, 2 * D:] * jnp.where(first, 0.0, 1.0)
        conv, u1, u2 = _conv_taps(u_prev, u, cw_ref)
        v_ref[...] = (gb * conv).astype(BF)

        dv = _dot(dy_ref[...].astype(BF), woutt_ref[...])
        dconv = dv * gb
        dv_next = _dot(dyn_ref[...].astype(BF), woutt_ref[...])
        dconv_next = dv_next * zn_ref[:, :D].astype(F32) * jnp.where(i == nt - 1, 0.0, 1.0)
        de = jnp.concatenate([dconv, dconv_next], axis=0)
        n = tm + HALO
        du = (cw_ref[2:3, :] * dconv + cw_ref[1:2, :] * pltpu.roll(de, n - 1, 0)[:tm, :]
              + cw_ref[0:1, :] * pltpu.roll(de, n - 2, 0)[:tm, :])
        for tap, shifted in enumerate((u2, u1, u)):
            _accumulate(dcw_ref.at[tap:tap + 1, :], first, _colsum(dconv * shifted))
        dz = jnp.concatenate([dv * conv, du * hh, du * gc], axis=1).astype(BF)
        dz_ref[...] = dz
        dhn = _dot(dz, wint_ref[...])
        _accumulate(dg_ref, first, _colsum(dhn * xh))
        dx_ref[...] = dy_ref[...] + _rms_bwd(dhn * g_ref[...], xh, r)

    tok = lambda w: pl.BlockSpec((tm, w), lambda i: (i, 0))
    whole = lambda arr: pl.BlockSpec(arr.shape, lambda i: (0,) * arr.ndim)
    return pl.pallas_call(
        body, name="mixc_bwd", grid=(nt,),
        in_specs=[tok(D), tok(D), pl.BlockSpec((HALO, D), _next_halo(tm, T)), tok(3 * D),
                  pl.BlockSpec((HALO, 3 * D), _prev_halo(tm)), pl.BlockSpec((HALO, 3 * D), _next_halo(tm, T)),
                  whole(g), whole(win_t), whole(cw), whole(wout_t)],
        out_specs=[tok(D), tok(D), tok(3 * D), tok(D), pl.BlockSpec((3, D), lambda i: (0, 0)),
                   pl.BlockSpec((1, D), lambda i: (0, 0))],
        out_shape=[jax.ShapeDtypeStruct((T, D), F32), jax.ShapeDtypeStruct((T, D), BF),
                   jax.ShapeDtypeStruct((T, 3 * D), BF), jax.ShapeDtypeStruct((T, D), BF),
                   jax.ShapeDtypeStruct((3, D), F32), jax.ShapeDtypeStruct((1, D), F32)],
        compiler_params=_params("arbitrary"),
    )(x, dy, dy, z, z, z, g, win_t, cw, wout_t)


def loss_head(y, target):
    T, D = y.shape
    tm = _tile(T, TM_MM, 8)

    def body(y_ref, t_ref, sum_ref, dy_ref):
        err = y_ref[...] - t_ref[...]
        dy_ref[...] = err * (1.0 / D)
        part = jnp.sum(jnp.sum(err * err, axis=-1, keepdims=True) * (1.0 / D), axis=0, keepdims=True)
        _accumulate(sum_ref, pl.program_id(0) == 0, jnp.broadcast_to(part, sum_ref.shape))

    return pl.pallas_call(
        body, name="loss_head", grid=(T // tm,),
        in_specs=[pl.BlockSpec((tm, D), lambda i: (i, 0))] * 2,
        out_specs=[pl.BlockSpec((8, 128), lambda i: (0, 0)), pl.BlockSpec((tm, D), lambda i: (i, 0))],
        out_shape=[jax.ShapeDtypeStruct((8, 128), F32), jax.ShapeDtypeStruct((T, D), F32)],
        compiler_params=_params("arbitrary"),
    )(y, target)


def adamw(w, g, m, v):
    R, C = w.shape
    tr = _tile(R, TR_FLAT, 8)

    def body(w_ref, g_ref, m_ref, v_ref, d_ref, m2_ref, v2_ref):
        gv = g_ref[...]
        m2 = ADAM_B1 * m_ref[...] + (1.0 - ADAM_B1) * gv
        v2 = ADAM_B2 * v_ref[...] + (1.0 - ADAM_B2) * (gv * gv)
        m2_ref[...] = m2
        v2_ref[...] = v2
        m_hat = m2 / (1.0 - ADAM_B1 ** ADAM_STEP)
        v_hat = v2 / (1.0 - ADAM_B2 ** ADAM_STEP)
        d_ref[...] = -ADAM_LR * (m_hat / (jnp.sqrt(v_hat) + ADAM_EPS) + ADAM_WD * w_ref[...])

    spec = pl.BlockSpec((tr, C), lambda i: (i, 0))
    return pl.pallas_call(
        body, name="adamw", grid=(R // tr,), in_specs=[spec] * 4, out_specs=[spec] * 3,
        out_shape=[jax.ShapeDtypeStruct((R, C), F32)] * 3,
        compiler_params=_params("arbitrary"),
    )(w, g, m, v)


def sum_slots(a):
    S, R, C = a.shape
    tr = _tile(R, TR_FLAT // 2, 16)

    def body(a_ref, o_ref):
        acc = a_ref[0].astype(F32)
        for s in range(1, S):
            acc = acc + a_ref[s].astype(F32)
        o_ref[...] = acc

    return pl.pallas_call(
        body, name="sum_slots", grid=(R // tr,),
        in_specs=[pl.BlockSpec((S, tr, C), lambda i: (0, i, 0))],
        out_specs=pl.BlockSpec((tr, C), lambda i: (i, 0)),
        out_shape=jax.ShapeDtypeStruct((R, C), F32),
        compiler_params=_params("arbitrary"),
    )(a)


ANY = pl.BlockSpec(memory_space=pl.ANY)


def _place():
    return lax.axis_index("x"), lax.axis_index("y"), lax.axis_index("c")


class _LocalCopy:
    def __init__(self, src, dst, sem, rows):
        n = LOCAL_CHUNKS if rows % (16 * LOCAL_CHUNKS) == 0 else 1
        cr = rows // n
        self.parts = [pltpu.make_async_copy(src.at[pl.ds(q * cr, cr), :], dst.at[pl.ds(q * cr, cr), :], sem)
                      for q in range(n)]
        self.whole = pltpu.make_async_copy(src, dst, sem)

    def start(self):
        for part in self.parts:
            part.start()

    def wait(self):
        self.whole.wait()


def allgather_shards(w):
    R, C = w.shape
    half = R // 2
    n = GATHER_CHUNKS if half % (16 * GATHER_CHUNKS) == 0 else 1
    cr = half // n

    def body(w_ref, out_ref, send_sems, recv_sems, local_sem):
        x, y, c = _place()
        sibling = (x, y, 1 - c)
        chips = [(1 - x, y), (x, 1 - y), (1 - x, 1 - y)]

        def rows(px, py, pc, q):
            return out_ref.at[2 * px + py, pl.ds(pc * half + q * cr, cr), :]

        def copy(k, block, q, to, src=None):
            return pltpu.make_async_remote_copy(
                src_ref=rows(*block, q) if src is None else src, dst_ref=rows(*block, q),
                send_sem=send_sems.at[k * n + q], recv_sem=recv_sems.at[k * n + q], device_id=to, device_id_type=MESH)

        mine = _LocalCopy(w_ref, out_ref.at[2 * x + y], local_sem, R)
        mine.start()
        first = [copy(j, (x, y, c), q, (*chip, c), src=w_ref.at[pl.ds(c * half + q * cr, cr), :])
                 for q in range(n) for j, chip in enumerate(chips)]
        for cp in first:
            cp.start()
        passed = []
        for q in range(n):
            for j, chip in enumerate(chips):
                copy(j, (*chip, c), q, (x, y, c)).wait_recv()
                passed.append(copy(3 + j, (*chip, c), q, sibling))
                passed[-1].start()
        for q in range(n):
            for j, chip in enumerate(chips):
                copy(3 + j, (*chip, 1 - c), q, (x, y, c)).wait_recv()
        for cp in first + passed:
            cp.wait_send()
        mine.wait()

    return pl.pallas_call(
        body, name="allgather_shards", in_specs=[ANY], out_specs=ANY,
        out_shape=jax.ShapeDtypeStruct((4, R, C), w.dtype),
        scratch_shapes=[pltpu.SemaphoreType.DMA((6 * n,)), pltpu.SemaphoreType.DMA((6 * n,)), pltpu.SemaphoreType.DMA],
    )(w)


def exchange_partials(grads, small):
    _, R, C = grads.shape
    half = R // 2
    Rs = small.shape[0]

    def body(g_ref, s_ref, land_ref, sland_ref, send_sems, recv_sems, local_sems):
        x, y, c = _place()
        me = 4 * x + 2 * y + c
        peers = []
        for mask in range(1, 8):
            mx, my, mc = (mask >> 2) & 1, (mask >> 1) & 1, mask & 1
            peers.append(((1 - x) if mx else x, (1 - y) if my else y, (1 - c) if mc else c))

        def piece(px, py, pc):
            return g_ref.at[2 * px + py, pl.ds(pc * half, half), :]

        def big(k, sender, to):
            return pltpu.make_async_remote_copy(
                src_ref=piece(*to), dst_ref=land_ref.at[sender], send_sem=send_sems.at[k], recv_sem=recv_sems.at[k],
                device_id=to, device_id_type=MESH)

        def little(k, sender, to):
            return pltpu.make_async_remote_copy(
                src_ref=s_ref, dst_ref=sland_ref.at[sender], send_sem=send_sems.at[7 + k],
                recv_sem=recv_sems.at[7 + k], device_id=to, device_id_type=MESH)

        own_big = _LocalCopy(piece(x, y, c), land_ref.at[me], local_sems.at[0], half)
        own_small = pltpu.make_async_copy(s_ref, sland_ref.at[me], local_sems.at[1])
        own_big.start()
        own_small.start()
        sends = []
        for k, peer in enumerate(peers):
            sends += [little(k, me, peer), big(k, me, peer)]
        for cp in sends:
            cp.start()
        for k, (px, py, pc) in enumerate(peers):
            sender = 4 * px + 2 * py + pc
            little(k, sender, (x, y, c)).wait_recv()
            big(k, sender, (x, y, c)).wait_recv()
        for cp in sends:
            cp.wait_send()
        own_big.wait()
        own_small.wait()

    return pl.pallas_call(
        body, name="exchange_partials", in_specs=[ANY, ANY], out_specs=[ANY, ANY],
        out_shape=[jax.ShapeDtypeStruct((8, half, C), grads.dtype), jax.ShapeDtypeStruct((8, Rs, C), small.dtype)],
        scratch_shapes=[pltpu.SemaphoreType.DMA((14,)), pltpu.SemaphoreType.DMA((14,)), pltpu.SemaphoreType.DMA((2,))],
    )(grads, small)


def share_with_sibling(part):
    half, C = part.shape
    n = SIBLING_CHUNKS if half % (16 * SIBLING_CHUNKS) == 0 else 1
    cr = half // n

    def body(p_ref, out_ref, send_sem, recv_sem, local_sem):
        x, y, c = _place()

        def rows(pc):
            return out_ref.at[pl.ds(pc * half, half), :]

        own = _LocalCopy(p_ref, rows(c), local_sem, half)
        own.start()
        for q in range(n):
            pltpu.make_async_remote_copy(
                src_ref=p_ref.at[pl.ds(q * cr, cr), :], dst_ref=out_ref.at[pl.ds(c * half + q * cr, cr), :],
                send_sem=send_sem, recv_sem=recv_sem, device_id=(x, y, 1 - c), device_id_type=MESH).start()
        everything = pltpu.make_async_remote_copy(src_ref=p_ref, dst_ref=rows(1 - c), send_sem=send_sem,
                                                  recv_sem=recv_sem, device_id=(x, y, c), device_id_type=MESH)
        everything.wait_recv()
        everything.wait_send()
        own.wait()

    return pl.pallas_call(
        body, name="share_with_sibling", in_specs=[ANY], out_specs=ANY,
        out_shape=jax.ShapeDtypeStruct((2 * half, C), part.dtype),
        scratch_shapes=[pltpu.SemaphoreType.DMA, pltpu.SemaphoreType.DMA, pltpu.SemaphoreType.DMA],
    )(part)


FLAT_SEG = 16 * FLAT_COLS


def _seg_rows(n):
    return -(-n // FLAT_SEG) * 16


def _flat_rows(sizes):
    rows = sum(_seg_rows(n) for n in sizes)
    return -(-rows // FLAT_ROW_ALIGN) * FLAT_ROW_ALIGN


def pack_flat(arrays, lead=()):
    sizes = [int(np.prod(a.shape[len(lead):])) for a in arrays]
    total = _flat_rows(sizes)
    parts, used = [], 0
    for a, n in zip(arrays, sizes):
        rows = _seg_rows(n)
        flat = a.reshape(*lead, n)
        flat = jnp.pad(flat, [(0, 0)] * len(lead) + [(0, rows * FLAT_COLS - n)])
        parts.append(flat.reshape(*lead, rows, FLAT_COLS))
        used += rows
    if total > used:
        parts.append(jnp.zeros((*lead, total - used, FLAT_COLS), arrays[0].dtype))
    return jnp.concatenate(parts, axis=len(lead))


def unpack_flat(flat, shapes, lead=()):
    out, r0 = [], 0
    for shp in shapes:
        n = int(np.prod(shp))
        rows = _seg_rows(n)
        seg = flat[..., r0:r0 + rows, :].reshape(*lead, rows * FLAT_COLS)[..., :n]
        out.append(seg.reshape(*lead, *shp))
        r0 += rows
    return out


def _f32_bits_as(a, dtype):
    return lax.bitcast_convert_type(a, dtype).reshape(*a.shape[:-1], -1)


def _f32_from_bits(a):
    k = 4 // a.dtype.itemsize
    if k > 1:
        a = a.reshape(*a.shape[:-1], a.shape[-1] // k, k)
    return lax.bitcast_convert_type(a, F32)


def _join_shards(name, a):
    if name in COL_SHARDED:
        return jnp.transpose(a, (1, 2, 0, 3)).reshape(a.shape[1], a.shape[2], 4 * a.shape[3])
    return jnp.transpose(a, (1, 0, 2, 3)).reshape(a.shape[1], 4 * a.shape[2], a.shape[3])


def _split_shards(name, a):
    L, K, N = a.shape
    if name in COL_SHARDED:
        return jnp.transpose(a.reshape(L, K, 4, N // 4), (2, 0, 1, 3))
    return jnp.transpose(a.reshape(L, 4, K // 4, N), (1, 0, 2, 3))


def _pad_heads(a, width):
    a = a.reshape(*a.shape[:-1], HEADS, width)
    a = jnp.pad(a, [(0, 0)] * (a.ndim - 1) + [(0, HEAD_SLOT - width)])
    return a.reshape(*a.shape[:-2], HEADS * HEAD_SLOT)


def _unpad_heads(a, width):
    a = a.reshape(*a.shape[:-1], HEADS, HEAD_SLOT)[..., :width]
    return a.reshape(*a.shape[:-2], HEADS * width)


def _rope_tables(T):
    pos = jnp.arange(T, dtype=F32)
    inv_freq = ROPE_THETA ** (-jnp.arange(0, ROPE_DIM, 2, dtype=F32) / ROPE_DIM)
    ang = pos[:, None] * inv_freq[None, :]
    cos, sin = jnp.cos(ang), jnp.sin(ang)
    pad = HEAD_SLOT - QK_DIM
    cos_t = jnp.concatenate([jnp.ones((T, NOPE_DIM), F32), cos, cos, jnp.zeros((T, pad), F32)], axis=1)
    sin_t = jnp.concatenate([jnp.zeros((T, NOPE_DIM), F32), sin, sin, jnp.zeros((T, pad), F32)], axis=1)
    return cos_t, sin_t


def _even_weights(W, i):
    c3 = POOL_DIM + Q_RANK + KV_RANK
    w_in = W['a_w_in'][i]
    D = w_in.shape[0]
    rope_cols = jnp.concatenate([jnp.zeros((D, NOPE_DIM), BF), w_in[:, c3:], jnp.zeros((D, HEAD_SLOT - QK_DIM), BF)], axis=1)
    win = jnp.concatenate([w_in[:, :c3], rope_cols], axis=1)
    wq = _pad_heads(W['a_w_q_up'][i], QK_DIM)
    kv = W['a_w_kv_up'][i].reshape(KV_RANK, HEADS, NOPE_DIM + V_DIM)
    wkn = _pad_heads(kv[:, :, :NOPE_DIM].reshape(KV_RANK, HEADS * NOPE_DIM), NOPE_DIM)
    wv = _pad_heads(kv[:, :, NOPE_DIM:].reshape(KV_RANK, HEADS * V_DIM), V_DIM)
    w_out = W['a_w_out'][i]
    wo_pool = w_out[:POOL_DIM]
    wo_attn = _pad_heads(w_out[POOL_DIM:].T, V_DIM).T
    wpool = W['a_w_pool'][i]
    pad = lambda a: jnp.pad(a, (0, HEAD_SLOT - QK_DIM))[None, :]
    return dict(win=win, win_t=win.T, wq=wq, wq_t=wq.T, wkn=wkn, wkn_t=wkn.T, wv=wv, wv_t=wv.T,
                wo_pool=wo_pool, wo_pool_t=wo_pool.T, wo_attn=wo_attn, wo_attn_t=wo_attn.T,
                wpool=wpool, wpool_t=jnp.transpose(wpool, (0, 2, 1)),
                qan=W['a_q_a_norm'][i][None, :], kvan=W['a_kv_a_norm'][i][None, :],
                qhn=pad(W['a_q_head_norm'][i]), khn=pad(W['a_k_head_norm'][i]),
                pscale=W['a_pool_scale'][i][None, :], g=W['mix_norm'][2 * i][None, :])


def kernel(*args):
    p = dict(zip(INPUTS, args))
    x0 = p['x'][0]
    target = p['loss_target'][0]
    T, D = x0.shape

    shard_shapes = [p[n].shape for n in SHARDED]

    wire = [_f32_bits_as(p[n], BF) if n == 'c_conv_w' else p[n].astype(BF) for n in SHARDED]
    gathered = allgather_shards(pack_flat(wire))
    W = {}
    for n, a in zip(SHARDED, unpack_flat(gathered, [a.shape for a in wire], lead=(4,))):
        W[n] = _join_shards(n, _f32_from_bits(a) if n == 'c_conv_w' else a)
    for n in REPLICATED:
        W[n] = p[n]
    W['a_w_pool'] = p['a_w_pool'].astype(BF)
    cos, sin = _rope_tables(T)

    def ffn_weights(pre, l):
        wg, wu, wd = W[pre + '_w_gate'][l], W[pre + '_w_up'][l], W[pre + '_w_down'][l]
        return dict(g=W[pre + '_norm'][l][None, :], wg=wg, wu=wu, wd=wd, wg_t=wg.T, wu_t=wu.T, wd_t=wd.T)

    saved = []
    x = x0
    for l in range(DEPTH):
        s = dict(x0=x)
        f1 = ffn_weights('ffn1', l)
        x, s['g1'], s['u1'] = ffn_fwd(x, f1['g'], f1['wg'], f1['wu'], f1['wd'])
        s['x1'] = x
        if l % 2 == 0:
            e = _even_weights(W, l // 2)
            s['q'], s['k'], s['v'], s['po'] = mixa_pre_fwd(
                x, e['g'], e['win'], e['qan'], e['wq'], e['kvan'], e['wkn'], e['wv'], e['qhn'], e['khn'],
                e['wpool'], e['pscale'], cos, sin)
            o_t, s['lse'] = attn_fwd(s['q'], s['k'], _blocks_transposed(s['v'], _tile(T, TQ_ATTN, 128)))
            s['o'] = _blocks_untransposed(o_t)
            x = mm_multi([(s['po'], e['wo_pool']), (s['o'], e['wo_attn'])], res=x)
        else:
            i = l // 2
            x, s['z'] = mixc_fwd(x, W['mix_norm'][l][None, :], W['c_w_in'][i], W['c_conv_w'][i].astype(F32),
                                 W['c_w_out'][i])
        s['x2'] = x
        f2 = ffn_weights('ffn2', l)
        x, s['g2'], s['u2'] = ffn_fwd(x, f2['g'], f2['wg'], f2['wu'], f2['wd'])
        saved.append(s)

    loss_sum, dy = loss_head(x, target)
    loss = lax.psum(0.5 * loss_sum[0, 0], AXES)

    G = {n: [None] * p[n].shape[0] for n in WEIGHTS}

    def ffn_back(pre, l, x_in, gg, uu, dy):
        f = ffn_weights(pre, l)
        dx, n, dyh, h, dgate, dup, dgn = ffn_bwd(x_in, dy, f['g'], gg, uu, f['wd_t'], f['wg_t'], f['wu_t'])
        G[pre + '_norm'][l] = dgn[0]
        G[pre + '_w_gate'][l] = mm_tn(n, dgate)
        G[pre + '_w_up'][l] = mm_tn(n, dup)
        G[pre + '_w_down'][l] = mm_tn(dyh, h).T
        return dx

    t_attn = _tile(T, TQ_ATTN, 128)
    for l in reversed(range(DEPTH)):
        s = saved[l]
        dy = ffn_back('ffn2', l, s['x2'], s['g2'], s['u2'], dy)
        i = l // 2
        if l % 2 == 0:
            e = _even_weights(W, i)
            G['a_w_out'][i] = jnp.concatenate(
                [mm_tn(s['po'], dy), _unpad_heads(mm_tn(s['o'], dy).T, V_DIM).T], axis=0)
            dpo = mm_multi([(dy, e['wo_pool_t'])])
            do = mm_multi([(dy, e['wo_attn_t'])], out_dtype=BF)
            as_rows = lambda a: a.reshape(HEADS, T // t_attn, t_attn)
            dk, dv, dq_t = attn_bwd(s['q'], s['k'], s['k'].T, s['v'], do, as_rows(s['lse']),
                                    as_rows(attn_delta(s['o'], do)))
            dq = _blocks_untransposed(dq_t)
            (dy, hn, dz, nq, dqraw, nkv, dkraw, pooled, dps, dg, dqan, dkvan, dqhn, dkhn, dpscale) = mixa_pre_bwd(
                s['x1'], dy, dq, dk, dv, dpo, e['g'], e['win'], e['win_t'], e['qan'], e['wq'], e['wq_t'], e['kvan'],
                e['wkn'], e['wkn_t'], e['wv_t'], e['qhn'], e['khn'], e['wpool'], e['wpool_t'], e['pscale'], cos, sin)
            c3 = POOL_DIM + Q_RANK + KV_RANK
            dwin = mm_tn(hn, dz)
            G['a_w_in'][i] = jnp.concatenate([dwin[:, :c3], dwin[:, c3 + NOPE_DIM:c3 + QK_DIM]], axis=1)
            G['a_w_q_up'][i] = _unpad_heads(mm_tn(nq, dqraw), QK_DIM)
            dwkn = _unpad_heads(mm_tn(nkv, dkraw), NOPE_DIM).reshape(KV_RANK, HEADS, NOPE_DIM)
            dwv = _unpad_heads(mm_tn(nkv, dv), V_DIM).reshape(KV_RANK, HEADS, V_DIM)
            G['a_w_kv_up'][i] = jnp.concatenate([dwkn, dwv], axis=2).reshape(KV_RANK, HEADS * (NOPE_DIM + V_DIM))
            dwp = mm_tn(pooled, dps)
            G['a_w_pool'][i] = jnp.stack([dwp[g * POOL_GROUP:(g + 1) * POOL_GROUP, g * POOL_GROUP:(g + 1) * POOL_GROUP]
                                          for g in range(len(POOL_WINDOWS))])
            G['mix_norm'][l] = dg[0]
            G['a_q_a_norm'][i] = dqan[0]
            G['a_kv_a_norm'][i] = dkvan[0]
            G['a_q_head_norm'][i] = dqhn[0, :QK_DIM]
            G['a_k_head_norm'][i] = dkhn[0, :QK_DIM]
            G['a_pool_scale'][i] = dpscale[0]
        else:
            w_in, w_out = W['c_w_in'][i], W['c_w_out'][i]
            dy_in = dy
            dy, hn, dz, gated, dcw, dg = mixc_bwd(s['x1'], dy, s['z'], W['mix_norm'][l][None, :], w_in.T,
                                                  W['c_conv_w'][i].astype(F32), w_out.T)
            G['c_w_in'][i] = mm_tn(hn, dz)
            G['c_w_out'][i] = mm_tn(gated, dy_in)
            G['c_conv_w'][i] = dcw
            G['mix_norm'][l] = dg[0]
        dy = ffn_back('ffn1', l, s['x0'], s['g1'], s['u1'], dy)
    grad_x = dy[None]

    G = {n: jnp.stack(v) for n, v in G.items()}
    partial_big = pack_flat([_split_shards(n, G[n]) for n in SHARDED], lead=(4,)).astype(BF)
    small_shapes = [p[n].shape for n in REPLICATED]
    partial_small = pack_flat([G[n] for n in REPLICATED])
    land, sland = exchange_partials(partial_big, partial_small)
    g_big = share_with_sibling(sum_slots(land))
    g_small = sum_slots(sland)

    outs = {}
    for names, shapes, g_flat in ((SHARDED, shard_shapes, g_big), (REPLICATED, small_shapes, g_small)):
        flat = lambda pre: pack_flat([p[pre + n] for n in names])
        delta, m2, v2 = adamw(flat(''), g_flat, flat('m_'), flat('v_'))
        for kind, arr in (('grad_', g_flat), ('delta_', delta), ('new_m_', m2), ('new_v_', v2)):
            for n, a in zip(names, unpack_flat(arr, shapes)):
                outs[kind + n] = a
    return (loss, grad_x, *[outs[k + n] for k in ('grad_', 'delta_', 'new_m_', 'new_v_') for n in WEIGHTS])
```

```python
import functools

import numpy as np
import jax
import jax.numpy as jnp
from jax import lax
from jax.experimental import pallas as pl
from jax.experimental.pallas import tpu as pltpu

BF, F32 = jnp.bfloat16, jnp.float32
MESH = pl.DeviceIdType.MESH
AXES = ("x", "y", "c")

NORM_EPS = 1e-6
DEPTH = 4
HEADS = 8
HEAD_SLOT = 128
QK_DIM, NOPE_DIM, ROPE_DIM, V_DIM = 96, 64, 32, 64
POOL_WINDOWS = (2, 4, 8, 16)
POOL_DIM, POOL_GROUP = 512, 128
Q_RANK, KV_RANK = 384, 256
ROPE_THETA = 10000.0
HALO = 16
ATTN_SCALE = QK_DIM ** -0.5
LOG2_E = 1.4426950408889634

ADAM_LR, ADAM_B1, ADAM_B2, ADAM_EPS, ADAM_WD, ADAM_STEP = 0.001, 0.9, 0.999, 1e-08, 0.01, 10

TM_FFN_FWD, TM_FFN_BWD, TF_FFN = 512, 256, 256
TM_MIX_FWD, TM_MIX_BWD = 512, 256
TM_CONV_FWD, TM_CONV_BWD = 256, 256
TQ_ATTN = 512
TM_MM = 512
TK_TN, BM_TN, BN_TN = 2048, 1024, 1536
FLAT_COLS = 1024
FLAT_ROW_ALIGN = 1024
SIBLING_CHUNKS = 16
LOCAL_CHUNKS = 16
GATHER_CHUNKS = 8
TR_FLAT = 256
VMEM_LIMIT = 56 * 1024 * 1024

WEIGHTS = ['ffn1_norm', 'ffn1_w_gate', 'ffn1_w_up', 'ffn1_w_down', 'mix_norm', 'ffn2_norm', 'ffn2_w_gate',
           'ffn2_w_up', 'ffn2_w_down', 'a_w_in', 'a_q_a_norm', 'a_w_q_up', 'a_kv_a_norm', 'a_w_kv_up',
           'a_q_head_norm', 'a_k_head_norm', 'a_w_pool', 'a_pool_scale', 'a_w_out', 'c_w_in', 'c_conv_w',
           'c_w_out']
COL_SHARDED = ('ffn1_w_gate', 'ffn1_w_up', 'ffn2_w_gate', 'ffn2_w_up', 'a_w_in', 'a_w_q_up', 'a_w_kv_up',
               'c_w_in', 'c_conv_w')
ROW_SHARDED = ('ffn1_w_down', 'ffn2_w_down', 'a_w_out', 'c_w_out')
SHARDED = tuple(n for n in WEIGHTS if n in COL_SHARDED or n in ROW_SHARDED)
REPLICATED = tuple(n for n in WEIGHTS if n not in SHARDED)
INPUTS = ['x'] + WEIGHTS + ['loss_target'] + ['m_' + n for n in WEIGHTS] + ['v_' + n for n in WEIGHTS]


def _dot(a, b):
    return jnp.dot(a, b, preferred_element_type=F32)


def _dot_nt(a, b):
    return lax.dot_general(a, b, (((1,), (1,)), ((), ())), preferred_element_type=F32)


def _dot_tn(a, b):
    return lax.dot_general(a, b, (((0,), (0,)), ((), ())), preferred_element_type=F32)


def _params(*sem):
    return pltpu.CompilerParams(dimension_semantics=sem or None, vmem_limit_bytes=VMEM_LIMIT)


def _tile(n, cap, unit):
    if n <= cap:
        return n
    best = None
    for t in range(unit, cap + 1, unit):
        if n % t == 0:
            best = t
    assert best is not None, (n, cap, unit)
    return best


def _rms(x, width=None):
    ms = jnp.sum(x * x, axis=-1, keepdims=True) * (1.0 / (width or x.shape[-1]))
    r = lax.rsqrt(ms + NORM_EPS)
    return x * r, r


def _rms_bwd(a, xhat, r, width=None):
    return r * (a - xhat * (jnp.sum(a * xhat, axis=-1, keepdims=True) * (1.0 / (width or a.shape[-1]))))


def _colsum(a):
    return jnp.sum(a, axis=0, keepdims=True)


def _accumulate(ref, first, value):
    @pl.when(first)
    def _():
        ref[...] = value

    @pl.when(jnp.logical_not(first))
    def _():
        ref[...] += value


def _rot_half(v):
    lane = lax.broadcasted_iota(jnp.int32, v.shape, 1)
    rot = jnp.where(lane < NOPE_DIM + ROPE_DIM // 2, -pltpu.roll(v, HEAD_SLOT - ROPE_DIM // 2, 1),
                    pltpu.roll(v, ROPE_DIM // 2, 1))
    return jnp.where((lane >= NOPE_DIM) & (lane < QK_DIM), rot, 0.0)


def _rope(v, cos, sin):
    return v * cos + _rot_half(v) * sin


def _rope_bwd(d, cos, sin):
    return d * cos - _rot_half(d * sin)


def _resident(arr):
    return pl.BlockSpec(arr.shape, lambda i: (0,) * arr.ndim, pipeline_mode=pl.Buffered(1))


def ffn_fwd(x, g, wg, wu, wd):
    T, D = x.shape
    F = wg.shape[1]
    tm, tf = _tile(T, TM_FFN_FWD, 8), _tile(F, TF_FFN, 128)
    nf = F // tf

    def body(x_ref, g_ref, wg_ref, wu_ref, wd_ref, y_ref, gg_ref, uu_ref, h_sc):
        xv = x_ref[...]
        xh, _ = _rms(xv)
        n = (xh * g_ref[...]).astype(BF)

        def projections(c):
            cols = slice(c * tf, (c + 1) * tf)
            return _dot(n, wg_ref[:, cols]), _dot(n, wu_ref[:, cols])

        ahead = projections(0)
        for c in range(nf):
            gg, uu = ahead
            if c + 1 < nf:
                ahead = projections(c + 1)
            cols = slice(c * tf, (c + 1) * tf)
            gg_ref[:, cols] = gg.astype(BF)
            uu_ref[:, cols] = uu.astype(BF)
            h_sc[:, cols] = (gg * jax.nn.sigmoid(gg) * uu).astype(BF)
        y_ref[...] = xv + 0.5 * _dot(h_sc[...], wd_ref[...])

    tok = lambda w: pl.BlockSpec((tm, w), lambda i: (i, 0))
    return pl.pallas_call(
        body, name="ffn_fwd", grid=(T // tm,),
        in_specs=[tok(D), _resident(g), _resident(wg), _resident(wu), _resident(wd)],
        out_specs=[tok(D), tok(F), tok(F)],
        out_shape=[jax.ShapeDtypeStruct((T, D), F32), jax.ShapeDtypeStruct((T, F), BF),
                   jax.ShapeDtypeStruct((T, F), BF)],
        scratch_shapes=[pltpu.VMEM((tm, F), BF)],
        compiler_params=_params("arbitrary"),
    )(x, g, wg, wu, wd)


def ffn_bwd(x, dy, g, gg, uu, wd_t, wg_t, wu_t):
    T, D = x.shape
    F = gg.shape[1]
    tm, tf = _tile(T, TM_FFN_BWD, 8), _tile(F, TF_FFN, 128)
    nf = F // tf

    def body(x_ref, dy_ref, g_ref, gg_ref, uu_ref, wdt_ref, wgt_ref, wut_ref,
             dx_ref, n_ref, dyh_ref, h_ref, dg_ref, du_ref, dgn_ref):
        xh, r = _rms(x_ref[...])
        n_ref[...] = (xh * g_ref[...]).astype(BF)
        dyv = dy_ref[...]
        dyh = (0.5 * dyv).astype(BF)
        dyh_ref[...] = dyh

        def hidden_grad(c):
            return _dot(dyh, wdt_ref[:, c * tf:(c + 1) * tf])

        ahead = hidden_grad(0)
        for c in range(nf):
            dh = ahead
            if c + 1 < nf:
                ahead = hidden_grad(c + 1)
            cols = slice(c * tf, (c + 1) * tf)
            gv = gg_ref[:, cols].astype(F32)
            uv = uu_ref[:, cols].astype(F32)
            sg = jax.nn.sigmoid(gv)
            silu = gv * sg
            h_ref[:, cols] = (silu * uv).astype(BF)
            du_ref[:, cols] = (dh * silu).astype(BF)
            dg_ref[:, cols] = (dh * uv * (sg * (1.0 + gv * (1.0 - sg)))).astype(BF)
        dn = _dot(dg_ref[...], wgt_ref[...]) + _dot(du_ref[...], wut_ref[...])
        dx_ref[...] = dyv + _rms_bwd(dn * g_ref[...], xh, r)
        _accumulate(dgn_ref, pl.program_id(0) == 0, _colsum(dn * xh))

    tok = lambda w: pl.BlockSpec((tm, w), lambda i: (i, 0))
    return pl.pallas_call(
        body, name="ffn_bwd", grid=(T // tm,),
        in_specs=[tok(D), tok(D), _resident(g), tok(F), tok(F), _resident(wd_t), _resident(wg_t), _resident(wu_t)],
        out_specs=[tok(D), tok(D), tok(D), tok(F), tok(F), tok(F), pl.BlockSpec((1, D), lambda i: (0, 0))],
        out_shape=[jax.ShapeDtypeStruct((T, D), F32), jax.ShapeDtypeStruct((T, D), BF),
                   jax.ShapeDtypeStruct((T, D), BF), jax.ShapeDtypeStruct((T, F), BF),
                   jax.ShapeDtypeStruct((T, F), BF), jax.ShapeDtypeStruct((T, F), BF),
                   jax.ShapeDtypeStruct((1, D), F32)],
        compiler_params=_params("arbitrary"),
    )(x, dy, g, gg, uu, wd_t, wg_t, wu_t)


def mm_tn(a, b):
    T, M = a.shape
    N = b.shape[1]
    tk, bm, bn = _tile(T, TK_TN, 16), _tile(M, BM_TN, 128), _tile(N, BN_TN, 128)

    def body(a_ref, b_ref, o_ref):
        part = _dot_tn(a_ref[...].astype(BF), b_ref[...].astype(BF))
        _accumulate(o_ref, pl.program_id(2) == 0, part)

    return pl.pallas_call(
        body, name="mm_tn", grid=(M // bm, N // bn, T // tk),
        in_specs=[pl.BlockSpec((tk, bm), lambda i, j, k: (k, i)), pl.BlockSpec((tk, bn), lambda i, j, k: (k, j))],
        out_specs=pl.BlockSpec((bm, bn), lambda i, j, k: (i, j)),
        out_shape=jax.ShapeDtypeStruct((M, N), F32),
        compiler_params=_params("arbitrary", "arbitrary", "arbitrary"),
    )(a, b)


def mm_multi(pairs, res=None, out_dtype=F32):
    T = pairs[0][0].shape[0]
    N = pairs[0][1].shape[1]
    tm = _tile(T, TM_MM, 16)
    n = len(pairs)

    def body(*refs):
        o_ref = refs[-1]
        acc = refs[2 * n][...] if res is not None else None
        for k in range(n):
            part = _dot(refs[k][...].astype(BF), refs[n + k][...])
            acc = part if acc is None else acc + part
        o_ref[...] = acc.astype(out_dtype)

    ins = [a for a, _ in pairs] + [w for _, w in pairs]
    specs = [pl.BlockSpec((tm, a.shape[1]), lambda i: (i, 0)) for a, _ in pairs]
    specs += [pl.BlockSpec(w.shape, lambda i: (0, 0)) for _, w in pairs]
    if res is not None:
        ins.append(res)
        specs.append(pl.BlockSpec((tm, N), lambda i: (i, 0)))
    return pl.pallas_call(
        body, name="mm_multi", grid=(T // tm,), in_specs=specs,
        out_specs=pl.BlockSpec((tm, N), lambda i: (i, 0)),
        out_shape=jax.ShapeDtypeStruct((T, N), out_dtype),
        compiler_params=_params("arbitrary"),
    )(*ins)


def _causal_mask(t, q_major):
    r = lax.broadcasted_iota(jnp.int32, (t, t), 0)
    c = lax.broadcasted_iota(jnp.int32, (t, t), 1)
    return (c <= r) if q_major else (r <= c)


def _blocks_transposed(a, t):
    T = a.shape[0]
    return jnp.transpose(a.reshape(T // t, t, HEADS, HEAD_SLOT), (2, 0, 3, 1))


def _blocks_untransposed(a):
    H, n, d, t = a.shape
    return jnp.transpose(a, (1, 3, 0, 2)).reshape(n * t, H * d)


def attn_fwd(q, k, v_t):
    T = q.shape[0]
    t = _tile(T, TQ_ATTN, 128)
    nq = T // t

    def body(q_ref, k_ref, vt_ref, ot_ref, lse_ref):
        i = pl.program_id(1)
        qv = q_ref[...]

        def update(st, j, m, l, acc):
            m2 = jnp.maximum(m, jnp.max(st, axis=0, keepdims=True))
            pt = jnp.exp2(st - m2)
            scale = jnp.exp2(m - m2)
            return (m2, scale * l + jnp.sum(pt, axis=0, keepdims=True),
                    scale * acc + _dot(vt_ref[j], pt.astype(BF)))

        def scores(j, masked):
            st = _dot_nt(k_ref[pl.ds(pl.multiple_of(j * t, t), t), :], qv)
            return jnp.where(_causal_mask(t, False), st, -jnp.inf) if masked else st

        def pair(j, carry, last_is_diagonal):
            s0, s1 = scores(j, False), scores(j + 1, last_is_diagonal)
            return update(s1, j + 1, *update(s0, j, *carry))

        init = (jnp.full((1, t), -1e30, F32), jnp.zeros((1, t), F32), jnp.zeros((HEAD_SLOT, t), F32))
        carry = lax.fori_loop(0, i // 2, lambda jj, c: pair(2 * jj, c, False), init)
        m, l, acc = lax.cond(i % 2 == 1, lambda c: pair(i - 1, c, True),
                             lambda c: update(scores(i, True), i, *c), carry)
        ot_ref[...] = (acc / l).astype(BF)
        lse_ref[...] = m + jnp.log2(l)

    return pl.pallas_call(
        body, name="attn_fwd", grid=(HEADS, nq),
        in_specs=[pl.BlockSpec((t, HEAD_SLOT), lambda h, i: (i, h)), pl.BlockSpec((T, HEAD_SLOT), lambda h, i: (0, h)),
                  pl.BlockSpec((None, nq, HEAD_SLOT, t), lambda h, i: (h, 0, 0, 0))],
        out_specs=[pl.BlockSpec((None, None, HEAD_SLOT, t), lambda h, i: (h, i, 0, 0)),
                   pl.BlockSpec((None, None, 1, t), lambda h, i: (h, i, 0, 0))],
        out_shape=[jax.ShapeDtypeStruct((HEADS, nq, HEAD_SLOT, t), BF), jax.ShapeDtypeStruct((HEADS, nq, 1, t), F32)],
        compiler_params=_params("arbitrary", "arbitrary"),
    )(q, k, v_t)


def attn_delta(o, do):
    T = o.shape[0]
    t = _tile(T, TQ_ATTN, 128)

    def body(o_ref, do_ref, delta_ref):
        prod = do_ref[...].astype(F32) * o_ref[...].astype(F32)
        delta_ref[...] = jnp.sum(prod.T, axis=0, keepdims=True)

    blk = pl.BlockSpec((t, HEAD_SLOT), lambda h, i: (i, h))
    return pl.pallas_call(
        body, name="attn_delta", grid=(HEADS, T // t), in_specs=[blk, blk],
        out_specs=pl.BlockSpec((None, None, 1, t), lambda h, i: (h, i, 0, 0)),
        out_shape=jax.ShapeDtypeStruct((HEADS, T // t, 1, t), F32),
        compiler_params=_params("arbitrary", "arbitrary"),
    )(o, do)


def attn_bwd(q, k, k_t, v, do, lse_rows, delta_rows):
    T = q.shape[0]
    t = _tile(T, TQ_ATTN, 128)
    nq = T // t

    def body(q_ref, k_ref, kt_ref, v_ref, do_ref, lse_ref, delta_ref, dk_ref, dv_ref, dqt_ref):
        j = pl.program_id(1)
        kv, vv, ktv = k_ref[...], v_ref[...], kt_ref[...]

        @pl.when(j == 0)
        def _():
            dqt_ref[...] = jnp.zeros_like(dqt_ref)

        def block(i):
            return pl.ds(pl.multiple_of(i * t, t), t)

        def scores(i, masked):
            st = _dot_nt(kv, q_ref[block(i), :])
            return jnp.where(_causal_mask(t, False), st, -jnp.inf) if masked else st

        def add(carry, st, i):
            dk, dv = carry
            qv, dov = q_ref[block(i), :], do_ref[block(i), :]
            pt = jnp.exp2(st - lse_ref[pl.ds(i, 1), :])
            dst = (pt * (_dot_nt(vv, dov) - delta_ref[pl.ds(i, 1), :])).astype(BF)
            dqt_ref[i] += _dot(ktv, dst)
            return dk + _dot(dst, qv), dv + _dot(pt.astype(BF), dov)

        def pair(i, carry):
            s0, s1 = scores(i, False), scores(i + 1, False)
            return add(add(carry, s0, i), s1, i + 1)

        zero = jnp.zeros((t, HEAD_SLOT), F32)
        carry = add((zero, zero), scores(j, True), j)
        rest = nq - 1 - j
        carry = lax.fori_loop(0, rest // 2, lambda ii, c: pair(j + 1 + 2 * ii, c), carry)
        dk, dv = lax.cond(rest % 2 == 1, lambda c: add(c, scores(nq - 1, False), nq - 1), lambda c: c, carry)
        dk_ref[...] = dk * (1.0 / LOG2_E)
        dv_ref[...] = dv

    blk = pl.BlockSpec((t, HEAD_SLOT), lambda h, j: (j, h))
    full = pl.BlockSpec((T, HEAD_SLOT), lambda h, j: (0, h))
    rows = pl.BlockSpec((None, nq, t), lambda h, j: (h, 0, 0))
    return pl.pallas_call(
        body, name="attn_bwd", grid=(HEADS, nq),
        in_specs=[full, blk, pl.BlockSpec((HEAD_SLOT, t), lambda h, j: (h, j)), blk, full, rows, rows],
        out_specs=[blk, blk, pl.BlockSpec((None, nq, HEAD_SLOT, t), lambda h, j: (h, 0, 0, 0))],
        out_shape=[jax.ShapeDtypeStruct((T, HEADS * HEAD_SLOT), F32)] * 2
                  + [jax.ShapeDtypeStruct((HEADS, nq, HEAD_SLOT, t), F32)],
        compiler_params=_params("arbitrary", "arbitrary"),
    )(q, k, k_t, v, do, lse_rows, delta_rows)


def _prev_halo(tm):
    return lambda i: (jnp.maximum(i * (tm // HALO) - 1, 0), 0)


def _next_halo(tm, T):
    return lambda i: (jnp.minimum((i + 1) * (tm // HALO), T // HALO - 1), 0)


def _inv_count(row0, n, w):
    t = row0 + lax.broadcasted_iota(jnp.int32, (n, 1), 0)
    return 1.0 / jnp.minimum(t + 1, w).astype(F32)


def _pool_fwd(u_prev, u, row0):
    tm = u.shape[0]
    out = []
    for g, w in enumerate(POOL_WINDOWS):
        lanes = slice(g * POOL_GROUP, (g + 1) * POOL_GROUP)
        ue = jnp.concatenate([u_prev[:, lanes], u[:, lanes]], axis=0)
        s, step = ue, 1
        while step < w:
            s = s + pltpu.roll(s, step, 0)
            step *= 2
        out.append(s[HALO:, :] * _inv_count(row0, tm, w) - u[:, lanes])
    return out


def _pool_bwd(dp, dp_next, row0):
    tm = dp[0].shape[0]
    out = []
    for g, w in enumerate(POOL_WINDOWS):
        e = jnp.concatenate([dp[g] * _inv_count(row0, tm, w), dp_next[g] * (1.0 / w)], axis=0)
        n = tm + HALO
        s, step = e, 1
        while step < w:
            s = s + pltpu.roll(s, n - step, 0)
            step *= 2
        out.append(s[:tm, :] - dp[g])
    return out


def _mixa_front(x, xp, first, row0, g_ref, win_ref, qan_ref, wq_ref, kvan_ref, wkn_ref):
    xh, r = _rms(x)
    hn = (xh * g_ref[...]).astype(BF)
    z = _dot(hn, win_ref[...])
    xph, _ = _rms(xp)
    u_prev = _dot((xph * g_ref[...]).astype(BF), win_ref[:, :POOL_DIM]) * jnp.where(first, 0.0, 1.0)
    u = z[:, :POOL_DIM]
    pooled = _pool_fwd(u_prev, u, row0)
    c1, c2 = POOL_DIM + Q_RANK, POOL_DIM + Q_RANK + KV_RANK
    qh, rq = _rms(z[:, POOL_DIM:c1])
    nq = (qh * qan_ref[...]).astype(BF)
    kh, rk = _rms(z[:, c1:c2])
    nkv = (kh * kvan_ref[...]).astype(BF)
    qraw = _dot(nq, wq_ref[...])
    kraw = _dot(nkv, wkn_ref[...])
    krope = z[:, c2:c2 + HEAD_SLOT]
    return dict(xh=xh, r=r, hn=hn, pooled=pooled, qh=qh, rq=rq, nq=nq, kh=kh, rk=rk, nkv=nkv,
                qraw=qraw, kraw=kraw, krope=krope)


def mixa_pre_fwd(x, g, win, qan, wq, kvan, wkn, wv, qhn, khn, wpool, pscale, cos, sin):
    T, D = x.shape
    tm = _tile(T, TM_MIX_FWD, HALO)
    HS = HEADS * HEAD_SLOT

    def body(x_ref, xp_ref, g_ref, win_ref, qan_ref, wq_ref, kvan_ref, wkn_ref, wv_ref, qhn_ref, khn_ref,
             wpool_ref, pscale_ref, cos_ref, sin_ref, q_ref, k_ref, v_ref, po_ref):
        i = pl.program_id(0)
        a = _mixa_front(x_ref[...], xp_ref[...], i == 0, i * tm, g_ref, win_ref, qan_ref, wq_ref, kvan_ref, wkn_ref)
        for gi in range(len(POOL_WINDOWS)):
            lanes = slice(gi * POOL_GROUP, (gi + 1) * POOL_GROUP)
            po = _dot(a["pooled"][gi].astype(BF), wpool_ref[gi]) * pscale_ref[:, lanes]
            po_ref[:, lanes] = po.astype(BF)
        cosv, sinv = cos_ref[...], sin_ref[...]
        v_ref[...] = _dot(a["nkv"], wv_ref[...]).astype(BF)
        for h in range(HEADS):
            lanes = slice(h * HEAD_SLOT, (h + 1) * HEAD_SLOT)
            qn, _ = _rms(a["qraw"][:, lanes], QK_DIM)
            q_ref[:, lanes] = (_rope(qn * qhn_ref[...], cosv, sinv) * (ATTN_SCALE * LOG2_E)).astype(BF)
            kn, _ = _rms(a["kraw"][:, lanes] + a["krope"], QK_DIM)
            k_ref[:, lanes] = _rope(kn * khn_ref[...], cosv, sinv).astype(BF)

    tok = lambda w: pl.BlockSpec((tm, w), lambda i: (i, 0))
    whole = lambda arr: pl.BlockSpec(arr.shape, lambda i: (0,) * arr.ndim)
    return pl.pallas_call(
        body, name="mixa_pre_fwd", grid=(T // tm,),
        in_specs=[tok(D), pl.BlockSpec((HALO, D), _prev_halo(tm))] + [whole(a) for a in
                  (g, win, qan, wq, kvan, wkn, wv, qhn, khn, wpool, pscale)] + [tok(HEAD_SLOT), tok(HEAD_SLOT)],
        out_specs=[tok(HS), tok(HS), tok(HS), tok(POOL_DIM)],
        out_shape=[jax.ShapeDtypeStruct((T, HS), BF)] * 3 + [jax.ShapeDtypeStruct((T, POOL_DIM), BF)],
        compiler_params=_params("arbitrary"),
    )(x, x, g, win, qan, wq, kvan, wkn, wv, qhn, khn, wpool, pscale, cos, sin)


def mixa_pre_bwd(x, dy, dq, dk, dv, dpo, g, win, win_t, qan, wq, wq_t, kvan, wkn, wkn_t, wv_t, qhn, khn,
                 wpool, wpool_t, pscale, cos, sin):
    T, D = x.shape
    tm = _tile(T, TM_MIX_BWD, HALO)
    HS = HEADS * HEAD_SLOT
    ZW = win.shape[1]
    nt = T // tm

    def body(x_ref, xp_ref, dy_ref, dq_ref, dk_ref, dv_ref, dpo_ref, dpon_ref, g_ref, win_ref, wint_ref, qan_ref,
             wq_ref, wqt_ref, kvan_ref, wkn_ref, wknt_ref, wvt_ref, qhn_ref, khn_ref, wpool_ref, wpoolt_ref,
             pscale_ref, cos_ref, sin_ref,
             dx_ref, hn_ref, dz_ref, nq_ref, dqraw_ref, nkv_ref, dkraw_ref, pooled_ref, dps_ref,
             dg_ref, dqan_ref, dkvan_ref, dqhn_ref, dkhn_ref, dpscale_ref):
        i = pl.program_id(0)
        first = i == 0
        a = _mixa_front(x_ref[...], xp_ref[...], first, i * tm, g_ref, win_ref, qan_ref, wq_ref, kvan_ref, wkn_ref)
        cosv, sinv = cos_ref[...], sin_ref[...]
        hn_ref[...] = a["hn"]
        nq_ref[...] = a["nq"]
        nkv_ref[...] = a["nkv"]

        has_next = jnp.where(i == nt - 1, 0.0, 1.0)
        dpool, dpool_next, dpscale = [], [], []
        for gi in range(len(POOL_WINDOWS)):
            lanes = slice(gi * POOL_GROUP, (gi + 1) * POOL_GROUP)
            pooled = a["pooled"][gi].astype(BF)
            pooled_ref[:, lanes] = pooled
            dpo_g = dpo_ref[:, lanes]
            dpscale.append(_colsum(dpo_g * _dot(pooled, wpool_ref[gi])))
            dps = (dpo_g * pscale_ref[:, lanes]).astype(BF)
            dps_ref[:, lanes] = dps
            dpool.append(_dot(dps, wpoolt_ref[gi]))
            dps_n = (dpon_ref[:, lanes] * pscale_ref[:, lanes] * has_next).astype(BF)
            dpool_next.append(_dot(dps_n, wpoolt_ref[gi]))
        du = jnp.concatenate(_pool_bwd(dpool, dpool_next, i * tm), axis=1)
        _accumulate(dpscale_ref, first, jnp.concatenate(dpscale, axis=1))

        dqhn = jnp.zeros((1, HEAD_SLOT), F32)
        dkhn = jnp.zeros((1, HEAD_SLOT), F32)
        dkrope = jnp.zeros((tm, HEAD_SLOT), F32)
        for h in range(HEADS):
            lanes = slice(h * HEAD_SLOT, (h + 1) * HEAD_SLOT)
            qhat, rq = _rms(a["qraw"][:, lanes], QK_DIM)
            dqn = _rope_bwd(dq_ref[:, lanes] * ATTN_SCALE, cosv, sinv)
            dqhn = dqhn + _colsum(dqn * qhat)
            dqraw_ref[:, lanes] = _rms_bwd(dqn * qhn_ref[...], qhat, rq, QK_DIM).astype(BF)
            khat, rk = _rms(a["kraw"][:, lanes] + a["krope"], QK_DIM)
            dkn = _rope_bwd(dk_ref[:, lanes], cosv, sinv)
            dkhn = dkhn + _colsum(dkn * khat)
            dkraw = _rms_bwd(dkn * khn_ref[...], khat, rk, QK_DIM)
            dkrope = dkrope + dkraw
            dkraw_ref[:, lanes] = dkraw.astype(BF)
        _accumulate(dqhn_ref, first, dqhn)
        _accumulate(dkhn_ref, first, dkhn)

        dnq = _dot(dqraw_ref[...], wqt_ref[...])
        _accumulate(dqan_ref, first, _colsum(dnq * a["qh"]))
        dql = _rms_bwd(dnq * qan_ref[...], a["qh"], a["rq"])
        dnkv = _dot(dkraw_ref[...], wknt_ref[...]) + _dot(dv_ref[...].astype(BF), wvt_ref[...])
        _accumulate(dkvan_ref, first, _colsum(dnkv * a["kh"]))
        dkvl = _rms_bwd(dnkv * kvan_ref[...], a["kh"], a["rk"])

        dz = jnp.concatenate([du, dql, dkvl, dkrope], axis=1).astype(BF)
        dz_ref[...] = dz
        dhn = _dot(dz, wint_ref[...])
        _accumulate(dg_ref, first, _colsum(dhn * a["xh"]))
        dx_ref[...] = dy_ref[...] + _rms_bwd(dhn * g_ref[...], a["xh"], a["r"])

    tok = lambda w: pl.BlockSpec((tm, w), lambda i: (i, 0))
    whole = lambda arr: pl.BlockSpec(arr.shape, lambda i: (0,) * arr.ndim)
    row = lambda w: pl.BlockSpec((1, w), lambda i: (0, 0))
    weights = (g, win, win_t, qan, wq, wq_t, kvan, wkn, wkn_t, wv_t, qhn, khn, wpool, wpool_t, pscale)
    return pl.pallas_call(
        body, name="mixa_pre_bwd", grid=(nt,),
        in_specs=[tok(D), pl.BlockSpec((HALO, D), _prev_halo(tm)), tok(D), tok(HS), tok(HS), tok(HS), tok(POOL_DIM),
                  pl.BlockSpec((HALO, POOL_DIM), _next_halo(tm, T))] + [whole(a) for a in weights]
                 + [tok(HEAD_SLOT), tok(HEAD_SLOT)],
        out_specs=[tok(D), tok(D), tok(ZW), tok(Q_RANK), tok(HS), tok(KV_RANK), tok(HS), tok(POOL_DIM), tok(POOL_DIM),
                   row(D), row(Q_RANK), row(KV_RANK), row(HEAD_SLOT), row(HEAD_SLOT), row(POOL_DIM)],
        out_shape=[jax.ShapeDtypeStruct((T, D), F32), jax.ShapeDtypeStruct((T, D), BF),
                   jax.ShapeDtypeStruct((T, ZW), BF), jax.ShapeDtypeStruct((T, Q_RANK), BF),
                   jax.ShapeDtypeStruct((T, HS), BF), jax.ShapeDtypeStruct((T, KV_RANK), BF),
                   jax.ShapeDtypeStruct((T, HS), BF), jax.ShapeDtypeStruct((T, POOL_DIM), BF),
                   jax.ShapeDtypeStruct((T, POOL_DIM), BF),
                   jax.ShapeDtypeStruct((1, D), F32), jax.ShapeDtypeStruct((1, Q_RANK), F32),
                   jax.ShapeDtypeStruct((1, KV_RANK), F32), jax.ShapeDtypeStruct((1, HEAD_SLOT), F32),
                   jax.ShapeDtypeStruct((1, HEAD_SLOT), F32), jax.ShapeDtypeStruct((1, POOL_DIM), F32)],
        compiler_params=_params("arbitrary"),
    )(x, x, dy, dq, dk, dv, dpo, dpo, *weights, cos, sin)


def _conv_taps(u_prev, u, cw_ref):
    ue = jnp.concatenate([u_prev, u], axis=0)
    u1 = pltpu.roll(ue, 1, 0)[HALO:, :]
    u2 = pltpu.roll(ue, 2, 0)[HALO:, :]
    return cw_ref[0:1, :] * u2 + cw_ref[1:2, :] * u1 + cw_ref[2:3, :] * u, u1, u2


def mixc_fwd(x, g, win, cw, wout):
    T, D = x.shape
    tm = _tile(T, TM_CONV_FWD, HALO)

    def body(x_ref, xp_ref, g_ref, win_ref, cw_ref, wout_ref, y_ref, z_ref):
        i = pl.program_id(0)
        xv = x_ref[...]
        xh, _ = _rms(xv)
        z = _dot((xh * g_ref[...]).astype(BF), win_ref[...])
        z_ref[...] = z.astype(BF)
        xph, _ = _rms(xp_ref[...])
        zp = _dot((xph * g_ref[...]).astype(BF), win_ref[:, D:])
        u_prev = zp[:, :D] * zp[:, D:] * jnp.where(i == 0, 0.0, 1.0)
        conv, _, _ = _conv_taps(u_prev, z[:, D:2 * D] * z[:, 2 * D:], cw_ref)
        y_ref[...] = xv + _dot((z[:, :D] * conv).astype(BF), wout_ref[...])

    tok = lambda w: pl.BlockSpec((tm, w), lambda i: (i, 0))
    whole = lambda arr: pl.BlockSpec(arr.shape, lambda i: (0,) * arr.ndim)
    return pl.pallas_call(
        body, name="mixc_fwd", grid=(T // tm,),
        in_specs=[tok(D), pl.BlockSpec((HALO, D), _prev_halo(tm)), whole(g), whole(win), whole(cw), whole(wout)],
        out_specs=[tok(D), tok(3 * D)],
        out_shape=[jax.ShapeDtypeStruct((T, D), F32), jax.ShapeDtypeStruct((T, 3 * D), BF)],
        compiler_params=_params("arbitrary"),
    )(x, x, g, win, cw, wout)


def mixc_bwd(x, dy, z, g, win_t, cw, wout_t):
    T, D = x.shape
    tm = _tile(T, TM_CONV_BWD, HALO)
    nt = T // tm

    def body(x_ref, dy_ref, dyn_ref, z_ref, zp_ref, zn_ref, g_ref, wint_ref, cw_ref, woutt_ref,
             dx_ref, hn_ref, dz_ref, v_ref, dcw_ref, dg_ref):
        i = pl.program_id(0)
        first = i == 0
        xh, r = _rms(x_ref[...])
        hn_ref[...] = (xh * g_ref[...]).astype(BF)
        zv = z_ref[...].astype(F32)
        gb, gc, hh = zv[:, :D], zv[:, D:2 * D], zv[:, 2 * D:]
        u = gc * hh
        zp = zp_ref[...].astype(F32)
        u_prev = zp[:, D:2 * D] * zp[:, 2 * D:] * jnp.where(first, 0.0, 1.0)
        conv, u1, u2 = _conv_taps(u_prev, u, cw_ref)
        v_ref[...] = (gb * conv).astype(BF)

        dv = _dot(dy_ref[...].astype(BF), woutt_ref[...])
        dconv = dv * gb
        dv_next = _dot(dyn_ref[...].astype(BF), woutt_ref[...])
        dconv_next = dv_next * zn_ref[:, :D].astype(F32) * jnp.where(i == nt - 1, 0.0, 1.0)
        de = jnp.concatenate([dconv, dconv_next], axis=0)
        n = tm + HALO
        du = (cw_ref[2:3, :] * dconv + cw_ref[1:2, :] * pltpu.roll(de, n - 1, 0)[:tm, :]
              + cw_ref[0:1, :] * pltpu.roll(de, n - 2, 0)[:tm, :])
        for tap, shifted in enumerate((u2, u1, u)):
            _accumulate(dcw_ref.at[tap:tap + 1, :], first, _colsum(dconv * shifted))
        dz = jnp.concatenate([dv * conv, du * hh, du * gc], axis=1).astype(BF)
        dz_ref[...] = dz
        dhn = _dot(dz, wint_ref[...])
        _accumulate(dg_ref, first, _colsum(dhn * xh))
        dx_ref[...] = dy_ref[...] + _rms_bwd(dhn * g_ref[...], xh, r)

    tok = lambda w: pl.BlockSpec((tm, w), lambda i: (i, 0))
    whole = lambda arr: pl.BlockSpec(arr.shape, lambda i: (0,) * arr.ndim)
    return pl.pallas_call(
        body, name="mixc_bwd", grid=(nt,),
        in_specs=[tok(D), tok(D), pl.BlockSpec((HALO, D), _next_halo(tm, T)), tok(3 * D),
                  pl.BlockSpec((HALO, 3 * D), _prev_halo(tm)), pl.BlockSpec((HALO, 3 * D), _next_halo(tm, T)),
                  whole(g), whole(win_t), whole(cw), whole(wout_t)],
        out_specs=[tok(D), tok(D), tok(3 * D), tok(D), pl.BlockSpec((3, D), lambda i: (0, 0)),
                   pl.BlockSpec((1, D), lambda i: (0, 0))],
        out_shape=[jax.ShapeDtypeStruct((T, D), F32), jax.ShapeDtypeStruct((T, D), BF),
                   jax.ShapeDtypeStruct((T, 3 * D), BF), jax.ShapeDtypeStruct((T, D), BF),
                   jax.ShapeDtypeStruct((3, D), F32), jax.ShapeDtypeStruct((1, D), F32)],
        compiler_params=_params("arbitrary"),
    )(x, dy, dy, z, z, z, g, win_t, cw, wout_t)


def loss_head(y, target):
    T, D = y.shape
    tm = _tile(T, TM_MM, 8)

    def body(y_ref, t_ref, sum_ref, dy_ref):
        err = y_ref[...] - t_ref[...]
        dy_ref[...] = err * (1.0 / D)
        part = jnp.sum(jnp.sum(err * err, axis=-1, keepdims=True) * (1.0 / D), axis=0, keepdims=True)
        _accumulate(sum_ref, pl.program_id(0) == 0, jnp.broadcast_to(part, sum_ref.shape))

    return pl.pallas_call(
        body, name="loss_head", grid=(T // tm,),
        in_specs=[pl.BlockSpec((tm, D), lambda i: (i, 0))] * 2,
        out_specs=[pl.BlockSpec((8, 128), lambda i: (0, 0)), pl.BlockSpec((tm, D), lambda i: (i, 0))],
        out_shape=[jax.ShapeDtypeStruct((8, 128), F32), jax.ShapeDtypeStruct((T, D), F32)],
        compiler_params=_params("arbitrary"),
    )(y, target)


def adamw(w, g, m, v):
    R, C = w.shape
    tr = _tile(R, TR_FLAT, 16)

    def body(w_ref, g_ref, m_ref, v_ref, g32_ref, d_ref, m2_ref, v2_ref):
        gv = g_ref[...].astype(F32)
        g32_ref[...] = gv
        m2 = ADAM_B1 * m_ref[...] + (1.0 - ADAM_B1) * gv
        v2 = ADAM_B2 * v_ref[...] + (1.0 - ADAM_B2) * (gv * gv)
        m2_ref[...] = m2
        v2_ref[...] = v2
        m_hat = m2 / (1.0 - ADAM_B1 ** ADAM_STEP)
        v_hat = v2 / (1.0 - ADAM_B2 ** ADAM_STEP)
        d_ref[...] = -ADAM_LR * (m_hat / (jnp.sqrt(v_hat) + ADAM_EPS) + ADAM_WD * w_ref[...])

    spec = pl.BlockSpec((tr, C), lambda i: (i, 0))
    return pl.pallas_call(
        body, name="adamw", grid=(R // tr,), in_specs=[spec] * 4, out_specs=[spec] * 4,
        out_shape=[jax.ShapeDtypeStruct((R, C), F32)] * 4,
        compiler_params=_params("arbitrary"),
    )(w, g, m, v)


def sum_slots(a, out_dtype):
    S, R, C = a.shape
    tr = _tile(R, TR_FLAT // 2, 16)

    def body(a_ref, o_ref):
        acc = a_ref[0].astype(F32)
        for s in range(1, S):
            acc = acc + a_ref[s].astype(F32)
        o_ref[...] = acc.astype(out_dtype)

    return pl.pallas_call(
        body, name="sum_slots", grid=(R // tr,),
        in_specs=[pl.BlockSpec((S, tr, C), lambda i: (0, i, 0))],
        out_specs=pl.BlockSpec((tr, C), lambda i: (i, 0)),
        out_shape=jax.ShapeDtypeStruct((R, C), out_dtype),
        compiler_params=_params("arbitrary"),
    )(a)


ANY = pl.BlockSpec(memory_space=pl.ANY)


def _place():
    return lax.axis_index("x"), lax.axis_index("y"), lax.axis_index("c")


class _LocalCopy:
    def __init__(self, src, dst, sem, rows):
        n = LOCAL_CHUNKS if rows % (16 * LOCAL_CHUNKS) == 0 else 1
        cr = rows // n
        self.parts = [pltpu.make_async_copy(src.at[pl.ds(q * cr, cr), :], dst.at[pl.ds(q * cr, cr), :], sem)
                      for q in range(n)]
        self.whole = pltpu.make_async_copy(src, dst, sem)

    def start(self):
        for part in self.parts:
            part.start()

    def wait(self):
        self.whole.wait()


def allgather_shards(w):
    R, C = w.shape
    half = R // 2
    n = GATHER_CHUNKS if half % (16 * GATHER_CHUNKS) == 0 else 1
    cr = half // n

    def body(w_ref, out_ref, send_sems, recv_sems, local_sem):
        x, y, c = _place()
        sibling = (x, y, 1 - c)
        chips = [(1 - x, y), (x, 1 - y), (1 - x, 1 - y)]

        def rows(px, py, pc, q):
            return out_ref.at[2 * px + py, pl.ds(pc * half + q * cr, cr), :]

        def copy(k, block, q, to, src=None):
            return pltpu.make_async_remote_copy(
                src_ref=rows(*block, q) if src is None else src, dst_ref=rows(*block, q),
                send_sem=send_sems.at[k * n + q], recv_sem=recv_sems.at[k * n + q], device_id=to, device_id_type=MESH)

        mine = _LocalCopy(w_ref, out_ref.at[2 * x + y], local_sem, R)
        mine.start()
        first = [copy(j, (x, y, c), q, (*chip, c), src=w_ref.at[pl.ds(c * half + q * cr, cr), :])
                 for q in range(n) for j, chip in enumerate(chips)]
        for cp in first:
            cp.start()
        passed = []
        for q in range(n):
            for j, chip in enumerate(chips):
                copy(j, (*chip, c), q, (x, y, c)).wait_recv()
                passed.append(copy(3 + j, (*chip, c), q, sibling))
                passed[-1].start()
        for q in range(n):
            for j, chip in enumerate(chips):
                copy(3 + j, (*chip, 1 - c), q, (x, y, c)).wait_recv()
        for cp in first + passed:
            cp.wait_send()
        mine.wait()

    return pl.pallas_call(
        body, name="allgather_shards", in_specs=[ANY], out_specs=ANY,
        out_shape=jax.ShapeDtypeStruct((4, R, C), w.dtype),
        scratch_shapes=[pltpu.SemaphoreType.DMA((6 * n,)), pltpu.SemaphoreType.DMA((6 * n,)), pltpu.SemaphoreType.DMA],
    )(w)


def exchange_partials(grads, small):
    _, R, C = grads.shape
    half = R // 2
    Rs = small.shape[0]

    def body(g_ref, s_ref, land_ref, sland_ref, send_sems, recv_sems, local_sems):
        x, y, c = _place()
        me = 4 * x + 2 * y + c
        peers = []
        for mask in range(1, 8):
            mx, my, mc = (mask >> 2) & 1, (mask >> 1) & 1, mask & 1
            peers.append(((1 - x) if mx else x, (1 - y) if my else y, (1 - c) if mc else c))

        def piece(px, py, pc):
            return g_ref.at[2 * px + py, pl.ds(pc * half, half), :]

        def big(k, sender, to):
            return pltpu.make_async_remote_copy(
                src_ref=piece(*to), dst_ref=land_ref.at[sender], send_sem=send_sems.at[k], recv_sem=recv_sems.at[k],
                device_id=to, device_id_type=MESH)

        def little(k, sender, to):
            return pltpu.make_async_remote_copy(
                src_ref=s_ref, dst_ref=sland_ref.at[sender], send_sem=send_sems.at[7 + k],
                recv_sem=recv_sems.at[7 + k], device_id=to, device_id_type=MESH)

        own_big = _LocalCopy(piece(x, y, c), land_ref.at[me], local_sems.at[0], half)
        own_small = pltpu.make_async_copy(s_ref, sland_ref.at[me], local_sems.at[1])
        own_big.start()
        own_small.start()
        sends = []
        for k, peer in enumerate(peers):
            sends += [little(k, me, peer), big(k, me, peer)]
        for cp in sends:
            cp.start()
        for k, (px, py, pc) in enumerate(peers):
            sender = 4 * px + 2 * py + pc
            little(k, sender, (x, y, c)).wait_recv()
            big(k, sender, (x, y, c)).wait_recv()
        for cp in sends:
            cp.wait_send()
        own_big.wait()
        own_small.wait()

    return pl.pallas_call(
        body, name="exchange_partials", in_specs=[ANY, ANY], out_specs=[ANY, ANY],
        out_shape=[jax.ShapeDtypeStruct((8, half, C), grads.dtype), jax.ShapeDtypeStruct((8, Rs, C), small.dtype)],
        scratch_shapes=[pltpu.SemaphoreType.DMA((14,)), pltpu.SemaphoreType.DMA((14,)), pltpu.SemaphoreType.DMA((2,))],
    )(grads, small)


def share_with_sibling(part):
    half, C = part.shape
    n = SIBLING_CHUNKS if half % (16 * SIBLING_CHUNKS) == 0 else 1
    cr = half // n

    def body(p_ref, out_ref, send_sem, recv_sem, local_sem):
        x, y, c = _place()

        def rows(pc):
            return out_ref.at[pl.ds(pc * half, half), :]

        own = _LocalCopy(p_ref, rows(c), local_sem, half)
        own.start()
        for q in range(n):
            pltpu.make_async_remote_copy(
                src_ref=p_ref.at[pl.ds(q * cr, cr), :], dst_ref=out_ref.at[pl.ds(c * half + q * cr, cr), :],
                send_sem=send_sem, recv_sem=recv_sem, device_id=(x, y, 1 - c), device_id_type=MESH).start()
        everything = pltpu.make_async_remote_copy(src_ref=p_ref, dst_ref=rows(1 - c), send_sem=send_sem,
                                                  recv_sem=recv_sem, device_id=(x, y, c), device_id_type=MESH)
        everything.wait_recv()
        everything.wait_send()
        own.wait()

    return pl.pallas_call(
        body, name="share_with_sibling", in_specs=[ANY], out_specs=ANY,
        out_shape=jax.ShapeDtypeStruct((2 * half, C), part.dtype),
        scratch_shapes=[pltpu.SemaphoreType.DMA, pltpu.SemaphoreType.DMA, pltpu.SemaphoreType.DMA],
    )(part)


FLAT_SEG = 16 * FLAT_COLS


def _seg_rows(n):
    return -(-n // FLAT_SEG) * 16


def _flat_rows(sizes):
    rows = sum(_seg_rows(n) for n in sizes)
    return -(-rows // FLAT_ROW_ALIGN) * FLAT_ROW_ALIGN


def pack_flat(arrays, lead=()):
    sizes = [int(np.prod(a.shape[len(lead):])) for a in arrays]
    total = _flat_rows(sizes)
    parts, used = [], 0
    for a, n in zip(arrays, sizes):
        rows = _seg_rows(n)
        flat = a.reshape(*lead, n)
        flat = jnp.pad(flat, [(0, 0)] * len(lead) + [(0, rows * FLAT_COLS - n)])
        parts.append(flat.reshape(*lead, rows, FLAT_COLS))
        used += rows
    if total > used:
        parts.append(jnp.zeros((*lead, total - used, FLAT_COLS), arrays[0].dtype))
    return jnp.concatenate(parts, axis=len(lead))


def unpack_flat(flat, shapes, lead=()):
    out, r0 = [], 0
    for shp in shapes:
        n = int(np.prod(shp))
        rows = _seg_rows(n)
        seg = flat[..., r0:r0 + rows, :].reshape(*lead, rows * FLAT_COLS)[..., :n]
        out.append(seg.reshape(*lead, *shp))
        r0 += rows
    return out


def _f32_bits_as(a, dtype):
    return lax.bitcast_convert_type(a, dtype).reshape(*a.shape[:-1], -1)


def _f32_from_bits(a):
    k = 4 // a.dtype.itemsize
    if k > 1:
        a = a.reshape(*a.shape[:-1], a.shape[-1] // k, k)
    return lax.bitcast_convert_type(a, F32)


def _join_shards(name, a):
    if name in COL_SHARDED:
        return jnp.transpose(a, (1, 2, 0, 3)).reshape(a.shape[1], a.shape[2], 4 * a.shape[3])
    return jnp.transpose(a, (1, 0, 2, 3)).reshape(a.shape[1], 4 * a.shape[2], a.shape[3])


def _split_shards(name, a):
    L, K, N = a.shape
    if name in COL_SHARDED:
        return jnp.transpose(a.reshape(L, K, 4, N // 4), (2, 0, 1, 3))
    return jnp.transpose(a.reshape(L, 4, K // 4, N), (1, 0, 2, 3))


def _pad_heads(a, width):
    a = a.reshape(*a.shape[:-1], HEADS, width)
    a = jnp.pad(a, [(0, 0)] * (a.ndim - 1) + [(0, HEAD_SLOT - width)])
    return a.reshape(*a.shape[:-2], HEADS * HEAD_SLOT)


def _unpad_heads(a, width):
    a = a.reshape(*a.shape[:-1], HEADS, HEAD_SLOT)[..., :width]
    return a.reshape(*a.shape[:-2], HEADS * width)


def _rope_tables(T):
    pos = jnp.arange(T, dtype=F32)
    inv_freq = ROPE_THETA ** (-jnp.arange(0, ROPE_DIM, 2, dtype=F32) / ROPE_DIM)
    ang = pos[:, None] * inv_freq[None, :]
    cos, sin = jnp.cos(ang), jnp.sin(ang)
    pad = HEAD_SLOT - QK_DIM
    cos_t = jnp.concatenate([jnp.ones((T, NOPE_DIM), F32), cos, cos, jnp.zeros((T, pad), F32)], axis=1)
    sin_t = jnp.concatenate([jnp.zeros((T, NOPE_DIM), F32), sin, sin, jnp.zeros((T, pad), F32)], axis=1)
    return cos_t, sin_t


def _even_weights(W, i):
    c3 = POOL_DIM + Q_RANK + KV_RANK
    w_in = W['a_w_in'][i]
    D = w_in.shape[0]
    rope_cols = jnp.concatenate([jnp.zeros((D, NOPE_DIM), BF), w_in[:, c3:], jnp.zeros((D, HEAD_SLOT - QK_DIM), BF)], axis=1)
    win = jnp.concatenate([w_in[:, :c3], rope_cols], axis=1)
    wq = _pad_heads(W['a_w_q_up'][i], QK_DIM)
    kv = W['a_w_kv_up'][i].reshape(KV_RANK, HEADS, NOPE_DIM + V_DIM)
    wkn = _pad_heads(kv[:, :, :NOPE_DIM].reshape(KV_RANK, HEADS * NOPE_DIM), NOPE_DIM)
    wv = _pad_heads(kv[:, :, NOPE_DIM:].reshape(KV_RANK, HEADS * V_DIM), V_DIM)
    w_out = W['a_w_out'][i]
    wo_pool = w_out[:POOL_DIM]
    wo_attn = _pad_heads(w_out[POOL_DIM:].T, V_DIM).T
    wpool = W['a_w_pool'][i]
    pad = lambda a: jnp.pad(a, (0, HEAD_SLOT - QK_DIM))[None, :]
    return dict(win=win, win_t=win.T, wq=wq, wq_t=wq.T, wkn=wkn, wkn_t=wkn.T, wv=wv, wv_t=wv.T,
                wo_pool=wo_pool, wo_pool_t=wo_pool.T, wo_attn=wo_attn, wo_attn_t=wo_attn.T,
                wpool=wpool, wpool_t=jnp.transpose(wpool, (0, 2, 1)),
                qan=W['a_q_a_norm'][i][None, :], kvan=W['a_kv_a_norm'][i][None, :],
                qhn=pad(W['a_q_head_norm'][i]), khn=pad(W['a_k_head_norm'][i]),
                pscale=W['a_pool_scale'][i][None, :], g=W['mix_norm'][2 * i][None, :])


def kernel(*args):
    p = dict(zip(INPUTS, args))
    x0 = p['x'][0]
    target = p['loss_target'][0]
    T, D = x0.shape

    shard_shapes = [p[n].shape for n in SHARDED]

    wire = [_f32_bits_as(p[n], BF) if n == 'c_conv_w' else p[n].astype(BF) for n in SHARDED]
    gathered = allgather_shards(pack_flat(wire))
    W = {}
    for n, a in zip(SHARDED, unpack_flat(gathered, [a.shape for a in wire], lead=(4,))):
        W[n] = _join_shards(n, _f32_from_bits(a) if n == 'c_conv_w' else a)
    for n in REPLICATED:
        W[n] = p[n]
    W['a_w_pool'] = p['a_w_pool'].astype(BF)
    cos, sin = _rope_tables(T)

    def ffn_weights(pre, l):
        wg, wu, wd = W[pre + '_w_gate'][l], W[pre + '_w_up'][l], W[pre + '_w_down'][l]
        return dict(g=W[pre + '_norm'][l][None, :], wg=wg, wu=wu, wd=wd, wg_t=wg.T, wu_t=wu.T, wd_t=wd.T)

    saved = []
    x = x0
    for l in range(DEPTH):
        s = dict(x0=x)
        f1 = ffn_weights('ffn1', l)
        x, s['g1'], s['u1'] = ffn_fwd(x, f1['g'], f1['wg'], f1['wu'], f1['wd'])
        s['x1'] = x
        if l % 2 == 0:
            e = _even_weights(W, l // 2)
            s['q'], s['k'], s['v'], s['po'] = mixa_pre_fwd(
                x, e['g'], e['win'], e['qan'], e['wq'], e['kvan'], e['wkn'], e['wv'], e['qhn'], e['khn'],
                e['wpool'], e['pscale'], cos, sin)
            o_t, s['lse'] = attn_fwd(s['q'], s['k'], _blocks_transposed(s['v'], _tile(T, TQ_ATTN, 128)))
            s['o'] = _blocks_untransposed(o_t)
            x = mm_multi([(s['po'], e['wo_pool']), (s['o'], e['wo_attn'])], res=x)
        else:
            i = l // 2
            x, s['z'] = mixc_fwd(x, W['mix_norm'][l][None, :], W['c_w_in'][i], W['c_conv_w'][i].astype(F32),
                                 W['c_w_out'][i])
        s['x2'] = x
        f2 = ffn_weights('ffn2', l)
        x, s['g2'], s['u2'] = ffn_fwd(x, f2['g'], f2['wg'], f2['wu'], f2['wd'])
        saved.append(s)

    loss_sum, dy = loss_head(x, target)
    loss = lax.psum(0.5 * loss_sum[0, 0], AXES)

    G = {n: [None] * p[n].shape[0] for n in WEIGHTS}

    def ffn_back(pre, l, x_in, gg, uu, dy):
        f = ffn_weights(pre, l)
        dx, n, dyh, h, dgate, dup, dgn = ffn_bwd(x_in, dy, f['g'], gg, uu, f['wd_t'], f['wg_t'], f['wu_t'])
        G[pre + '_norm'][l] = dgn[0]
        G[pre + '_w_gate'][l] = mm_tn(n, dgate)
        G[pre + '_w_up'][l] = mm_tn(n, dup)
        G[pre + '_w_down'][l] = mm_tn(dyh, h).T
        return dx

    t_attn = _tile(T, TQ_ATTN, 128)
    for l in reversed(range(DEPTH)):
        s = saved[l]
        dy = ffn_back('ffn2', l, s['x2'], s['g2'], s['u2'], dy)
        i = l // 2
        if l % 2 == 0:
            e = _even_weights(W, i)
            G['a_w_out'][i] = jnp.concatenate(
                [mm_tn(s['po'], dy), _unpad_heads(mm_tn(s['o'], dy).T, V_DIM).T], axis=0)
            dpo = mm_multi([(dy, e['wo_pool_t'])])
            do = mm_multi([(dy, e['wo_attn_t'])], out_dtype=BF)
            as_rows = lambda a: a.reshape(HEADS, T // t_attn, t_attn)
            dk, dv, dq_t = attn_bwd(s['q'], s['k'], s['k'].T, s['v'], do, as_rows(s['lse']),
                                    as_rows(attn_delta(s['o'], do)))
            dq = _blocks_untransposed(dq_t)
            (dy, hn, dz, nq, dqraw, nkv, dkraw, pooled, dps, dg, dqan, dkvan, dqhn, dkhn, dpscale) = mixa_pre_bwd(
                s['x1'], dy, dq, dk, dv, dpo, e['g'], e['win'], e['win_t'], e['qan'], e['wq'], e['wq_t'], e['kvan'],
                e['wkn'], e['wkn_t'], e['wv_t'], e['qhn'], e['khn'], e['wpool'], e['wpool_t'], e['pscale'], cos, sin)
            c3 = POOL_DIM + Q_RANK + KV_RANK
            dwin = mm_tn(hn, dz)
            G['a_w_in'][i] = jnp.concatenate([dwin[:, :c3], dwin[:, c3 + NOPE_DIM:c3 + QK_DIM]], axis=1)
            G['a_w_q_up'][i] = _unpad_heads(mm_tn(nq, dqraw), QK_DIM)
            dwkn = _unpad_heads(mm_tn(nkv, dkraw), NOPE_DIM).reshape(KV_RANK, HEADS, NOPE_DIM)
            dwv = _unpad_heads(mm_tn(nkv, dv), V_DIM).reshape(KV_RANK, HEADS, V_DIM)
            G['a_w_kv_up'][i] = jnp.concatenate([dwkn, dwv], axis=2).reshape(KV_RANK, HEADS * (NOPE_DIM + V_DIM))
            dwp = mm_tn(pooled, dps)
            G['a_w_pool'][i] = jnp.stack([dwp[g * POOL_GROUP:(g + 1) * POOL_GROUP, g * POOL_GROUP:(g + 1) * POOL_GROUP]
                                          for g in range(len(POOL_WINDOWS))])
            G['mix_norm'][l] = dg[0]
            G['a_q_a_norm'][i] = dqan[0]
            G['a_kv_a_norm'][i] = dkvan[0]
            G['a_q_head_norm'][i] = dqhn[0, :QK_DIM]
            G['a_k_head_norm'][i] = dkhn[0, :QK_DIM]
            G['a_pool_scale'][i] = dpscale[0]
        else:
            w_in, w_out = W['c_w_in'][i], W['c_w_out'][i]
            dy_in = dy
            dy, hn, dz, gated, dcw, dg = mixc_bwd(s['x1'], dy, s['z'], W['mix_norm'][l][None, :], w_in.T,
                                                  W['c_conv_w'][i].astype(F32), w_out.T)
            G['c_w_in'][i] = mm_tn(hn, dz)
            G['c_w_out'][i] = mm_tn(gated, dy_in)
            G['c_conv_w'][i] = dcw
            G['mix_norm'][l] = dg[0]
        dy = ffn_back('ffn1', l, s['x0'], s['g1'], s['u1'], dy)
    grad_x = dy[None]

    G = {n: jnp.stack(v) for n, v in G.items()}
    partial_big = pack_flat([_split_shards(n, G[n]) for n in SHARDED], lead=(4,)).astype(BF)
    small_shapes = [p[n].shape for n in REPLICATED]
    partial_small = pack_flat([G[n] for n in REPLICATED])
    land, sland = exchange_partials(partial_big, partial_small)
    g_big = share_with_sibling(sum_slots(land, BF))
    g_small = sum_slots(sland, F32)

    outs = {}
    for names, shapes, g_flat in ((SHARDED, shard_shapes, g_big), (REPLICATED, small_shapes, g_small)):
        flat = lambda pre: pack_flat([p[pre + n] for n in names])
        g32, delta, m2, v2 = adamw(flat(''), g_flat, flat('m_'), flat('v_'))
        for kind, arr in (('grad_', g32), ('delta_', delta), ('new_m_', m2), ('new_v_', v2)):
            for n, a in zip(names, unpack_flat(arr, shapes)):
                outs[kind + n] = a
    return (loss, grad_x, *[outs[k + n] for k in ('grad_', 'delta_', 'new_m_', 'new_v_') for n in WEIGHTS])
```

```python
import functools

import numpy as np
import jax
import jax.numpy as jnp
from jax import lax
from jax.experimental import pallas as pl
from jax.experimental.pallas import tpu as pltpu

BF, F32 = jnp.bfloat16, jnp.float32
MESH = pl.DeviceIdType.MESH
AXES = ("x", "y", "c")

NORM_EPS = 1e-6
DEPTH = 4
HEADS = 8
HEAD_SLOT = 128
QK_DIM, NOPE_DIM, ROPE_DIM, V_DIM = 96, 64, 32, 64
POOL_WINDOWS = (2, 4, 8, 16)
POOL_DIM, POOL_GROUP = 512, 128
Q_RANK, KV_RANK = 384, 256
ROPE_THETA = 10000.0
HALO = 16
ATTN_SCALE = QK_DIM ** -0.5
LOG2_E = 1.4426950408889634

ADAM_LR, ADAM_B1, ADAM_B2, ADAM_EPS, ADAM_WD, ADAM_STEP = 0.001, 0.9, 0.999, 1e-08, 0.01, 10

TM_FFN_FWD, TM_FFN_BWD, TF_FFN = 512, 256, 256
TM_MIX_FWD, TM_MIX_BWD = 512, 256
TM_CONV_FWD, TM_CONV_BWD = 256, 256
TQ_ATTN = 512
TM_MM = 512
TK_TN, BM_TN, BN_TN = 2048, 1024, 1536
FLAT_COLS = 1024
FLAT_ROW_ALIGN = 1024
SIBLING_CHUNKS = 16
LOCAL_CHUNKS = 16
GATHER_CHUNKS = 8
TR_FLAT = 256
VMEM_LIMIT = 56 * 1024 * 1024

WEIGHTS = ['ffn1_norm', 'ffn1_w_gate', 'ffn1_w_up', 'ffn1_w_down', 'mix_norm', 'ffn2_norm', 'ffn2_w_gate',
           'ffn2_w_up', 'ffn2_w_down', 'a_w_in', 'a_q_a_norm', 'a_w_q_up', 'a_kv_a_norm', 'a_w_kv_up',
           'a_q_head_norm', 'a_k_head_norm', 'a_w_pool', 'a_pool_scale', 'a_w_out', 'c_w_in', 'c_conv_w',
           'c_w_out']
COL_SHARDED = ('ffn1_w_gate', 'ffn1_w_up', 'ffn2_w_gate', 'ffn2_w_up', 'a_w_in', 'a_w_q_up', 'a_w_kv_up',
               'c_w_in', 'c_conv_w')
ROW_SHARDED = ('ffn1_w_down', 'ffn2_w_down', 'a_w_out', 'c_w_out')
SHARDED = tuple(n for n in WEIGHTS if n in COL_SHARDED or n in ROW_SHARDED)
REPLICATED = tuple(n for n in WEIGHTS if n not in SHARDED)
INPUTS = ['x'] + WEIGHTS + ['loss_target'] + ['m_' + n for n in WEIGHTS] + ['v_' + n for n in WEIGHTS]


def _dot(a, b):
    return jnp.dot(a, b, preferred_element_type=F32)


def _dot_nt(a, b):
    return lax.dot_general(a, b, (((1,), (1,)), ((), ())), preferred_element_type=F32)


def _dot_tn(a, b):
    return lax.dot_general(a, b, (((0,), (0,)), ((), ())), preferred_element_type=F32)


def _params(*sem):
    return pltpu.CompilerParams(dimension_semantics=sem or None, vmem_limit_bytes=VMEM_LIMIT)


def _tile(n, cap, unit):
    if n <= cap:
        return n
    best = None
    for t in range(unit, cap + 1, unit):
        if n % t == 0:
            best = t
    assert best is not None, (n, cap, unit)
    return best


def _rms(x, width=None):
    ms = jnp.sum(x * x, axis=-1, keepdims=True) * (1.0 / (width or x.shape[-1]))
    r = lax.rsqrt(ms + NORM_EPS)
    return x * r, r


def _rms_bwd(a, xhat, r, width=None):
    return r * (a - xhat * (jnp.sum(a * xhat, axis=-1, keepdims=True) * (1.0 / (width or a.shape[-1]))))


def _colsum(a):
    return jnp.sum(a, axis=0, keepdims=True)


def _accumulate(ref, first, value):
    @pl.when(first)
    def _():
        ref[...] = value

    @pl.when(jnp.logical_not(first))
    def _():
        ref[...] += value


def _rot_half(v):
    lane = lax.broadcasted_iota(jnp.int32, v.shape, 1)
    rot = jnp.where(lane < NOPE_DIM + ROPE_DIM // 2, -pltpu.roll(v, HEAD_SLOT - ROPE_DIM // 2, 1),
                    pltpu.roll(v, ROPE_DIM // 2, 1))
    return jnp.where((lane >= NOPE_DIM) & (lane < QK_DIM), rot, 0.0)


def _rope(v, cos, sin):
    return v * cos + _rot_half(v) * sin


def _rope_bwd(d, cos, sin):
    return d * cos - _rot_half(d * sin)


def _resident(arr):
    return pl.BlockSpec(arr.shape, lambda i: (0,) * arr.ndim, pipeline_mode=pl.Buffered(1))


def ffn_fwd(x, g, wg, wu, wd):
    T, D = x.shape
    F = wg.shape[1]
    tm, tf = _tile(T, TM_FFN_FWD, 8), _tile(F, TF_FFN, 128)
    nf = F // tf

    def body(x_ref, g_ref, wg_ref, wu_ref, wd_ref, y_ref, gg_ref, uu_ref, h_sc):
        xv = x_ref[...]
        xh, _ = _rms(xv)
        n = (xh * g_ref[...]).astype(BF)

        def projections(c):
            cols = slice(c * tf, (c + 1) * tf)
            return _dot(n, wg_ref[:, cols]), _dot(n, wu_ref[:, cols])

        ahead = projections(0)
        for c in range(nf):
            gg, uu = ahead
            if c + 1 < nf:
                ahead = projections(c + 1)
            cols = slice(c * tf, (c + 1) * tf)
            gg_ref[:, cols] = gg.astype(BF)
            uu_ref[:, cols] = uu.astype(BF)
            h_sc[:, cols] = (gg * jax.nn.sigmoid(gg) * uu).astype(BF)
        y_ref[...] = xv + 0.5 * _dot(h_sc[...], wd_ref[...])

    tok = lambda w: pl.BlockSpec((tm, w), lambda i: (i, 0))
    return pl.pallas_call(
        body, name="ffn_fwd", grid=(T // tm,),
        in_specs=[tok(D), _resident(g), _resident(wg), _resident(wu), _resident(wd)],
        out_specs=[tok(D), tok(F), tok(F)],
        out_shape=[jax.ShapeDtypeStruct((T, D), F32), jax.ShapeDtypeStruct((T, F), BF),
                   jax.ShapeDtypeStruct((T, F), BF)],
        scratch_shapes=[pltpu.VMEM((tm, F), BF)],
        compiler_params=_params("arbitrary"),
    )(x, g, wg, wu, wd)


def ffn_bwd(x, dy, g, gg, uu, wd_t, wg_t, wu_t):
    T, D = x.shape
    F = gg.shape[1]
    tm, tf = _tile(T, TM_FFN_BWD, 8), _tile(F, TF_FFN, 128)
    nf = F // tf

    def body(x_ref, dy_ref, g_ref, gg_ref, uu_ref, wdt_ref, wgt_ref, wut_ref,
             dx_ref, n_ref, dyh_ref, h_ref, dg_ref, du_ref, dgn_ref):
        xh, r = _rms(x_ref[...])
        n_ref[...] = (xh * g_ref[...]).astype(BF)
        dyv = dy_ref[...]
        dyh = (0.5 * dyv).astype(BF)
        dyh_ref[...] = dyh

        def hidden_grad(c):
            return _dot(dyh, wdt_ref[:, c * tf:(c + 1) * tf])

        ahead = hidden_grad(0)
        for c in range(nf):
            dh = ahead
            if c + 1 < nf:
                ahead = hidden_grad(c + 1)
            cols = slice(c * tf, (c + 1) * tf)
            gv = gg_ref[:, cols].astype(F32)
            uv = uu_ref[:, cols].astype(F32)
            sg = jax.nn.sigmoid(gv)
            silu = gv * sg
            h_ref[:, cols] = (silu * uv).astype(BF)
            du_ref[:, cols] = (dh * silu).astype(BF)
            dg_ref[:, cols] = (dh * uv * (sg * (1.0 + gv * (1.0 - sg)))).astype(BF)
        dn = _dot(dg_ref[...], wgt_ref[...]) + _dot(du_ref[...], wut_ref[...])
        dx_ref[...] = dyv + _rms_bwd(dn * g_ref[...], xh, r)
        _accumulate(dgn_ref, pl.program_id(0) == 0, _colsum(dn * xh))

    tok = lambda w: pl.BlockSpec((tm, w), lambda i: (i, 0))
    return pl.pallas_call(
        body, name="ffn_bwd", grid=(T // tm,),
        in_specs=[tok(D), tok(D), _resident(g), tok(F), tok(F), _resident(wd_t), _resident(wg_t), _resident(wu_t)],
        out_specs=[tok(D), tok(D), tok(D), tok(F), tok(F), tok(F), pl.BlockSpec((1, D), lambda i: (0, 0))],
        out_shape=[jax.ShapeDtypeStruct((T, D), F32), jax.ShapeDtypeStruct((T, D), BF),
                   jax.ShapeDtypeStruct((T, D), BF), jax.ShapeDtypeStruct((T, F), BF),
                   jax.ShapeDtypeStruct((T, F), BF), jax.ShapeDtypeStruct((T, F), BF),
                   jax.ShapeDtypeStruct((1, D), F32)],
        compiler_params=_params("arbitrary"),
    )(x, dy, g, gg, uu, wd_t, wg_t, wu_t)


def mm_tn(a, b):
    T, M = a.shape
    N = b.shape[1]
    tk, bm, bn = _tile(T, TK_TN, 16), _tile(M, BM_TN, 128), _tile(N, BN_TN, 128)

    def body(a_ref, b_ref, o_ref):
        part = _dot_tn(a_ref[...].astype(BF), b_ref[...].astype(BF))
        _accumulate(o_ref, pl.program_id(2) == 0, part)

    return pl.pallas_call(
        body, name="mm_tn", grid=(M // bm, N // bn, T // tk),
        in_specs=[pl.BlockSpec((tk, bm), lambda i, j, k: (k, i)), pl.BlockSpec((tk, bn), lambda i, j, k: (k, j))],
        out_specs=pl.BlockSpec((bm, bn), lambda i, j, k: (i, j)),
        out_shape=jax.ShapeDtypeStruct((M, N), F32),
        compiler_params=_params("arbitrary", "arbitrary", "arbitrary"),
    )(a, b)


def mm_multi(pairs, res=None, out_dtype=F32):
    T = pairs[0][0].shape[0]
    N = pairs[0][1].shape[1]
    tm = _tile(T, TM_MM, 16)
    n = len(pairs)

    def body(*refs):
        o_ref = refs[-1]
        acc = refs[2 * n][...] if res is not None else None
        for k in range(n):
            part = _dot(refs[k][...].astype(BF), refs[n + k][...])
            acc = part if acc is None else acc + part
        o_ref[...] = acc.astype(out_dtype)

    ins = [a for a, _ in pairs] + [w for _, w in pairs]
    specs = [pl.BlockSpec((tm, a.shape[1]), lambda i: (i, 0)) for a, _ in pairs]
    specs += [pl.BlockSpec(w.shape, lambda i: (0, 0)) for _, w in pairs]
    if res is not None:
        ins.append(res)
        specs.append(pl.BlockSpec((tm, N), lambda i: (i, 0)))
    return pl.pallas_call(
        body, name="mm_multi", grid=(T // tm,), in_specs=specs,
        out_specs=pl.BlockSpec((tm, N), lambda i: (i, 0)),
        out_shape=jax.ShapeDtypeStruct((T, N), out_dtype),
        compiler_params=_params("arbitrary"),
    )(*ins)


def _causal_mask(t, q_major):
    r = lax.broadcasted_iota(jnp.int32, (t, t), 0)
    c = lax.broadcasted_iota(jnp.int32, (t, t), 1)
    return (c <= r) if q_major else (r <= c)


def _blocks_transposed(a, t):
    T = a.shape[0]
    return jnp.transpose(a.reshape(T // t, t, HEADS, HEAD_SLOT), (2, 0, 3, 1))


def _blocks_untransposed(a):
    H, n, d, t = a.shape
    return jnp.transpose(a, (1, 3, 0, 2)).reshape(n * t, H * d)


def attn_fwd(q, k, v_t):
    T = q.shape[0]
    t = _tile(T, TQ_ATTN, 128)
    nq = T // t

    def body(q_ref, k_ref, vt_ref, ot_ref, lse_ref):
        i = pl.program_id(1)
        qv = q_ref[...]

        def update(st, j, m, l, acc):
            m2 = jnp.maximum(m, jnp.max(st, axis=0, keepdims=True))
            pt = jnp.exp2(st - m2)
            scale = jnp.exp2(m - m2)
            return (m2, scale * l + jnp.sum(pt, axis=0, keepdims=True),
                    scale * acc + _dot(vt_ref[j], pt.astype(BF)))

        def scores(j, masked):
            st = _dot_nt(k_ref[pl.ds(pl.multiple_of(j * t, t), t), :], qv)
            return jnp.where(_causal_mask(t, False), st, -jnp.inf) if masked else st

        def pair(j, carry, last_is_diagonal):
            s0, s1 = scores(j, False), scores(j + 1, last_is_diagonal)
            return update(s1, j + 1, *update(s0, j, *carry))

        init = (jnp.full((1, t), -1e30, F32), jnp.zeros((1, t), F32), jnp.zeros((HEAD_SLOT, t), F32))
        carry = lax.fori_loop(0, i // 2, lambda jj, c: pair(2 * jj, c, False), init)
        m, l, acc = lax.cond(i % 2 == 1, lambda c: pair(i - 1, c, True),
                             lambda c: update(scores(i, True), i, *c), carry)
        ot_ref[...] = (acc / l).astype(BF)
        lse_ref[...] = m + jnp.log2(l)

    return pl.pallas_call(
        body, name="attn_fwd", grid=(HEADS, nq),
        in_specs=[pl.BlockSpec((t, HEAD_SLOT), lambda h, i: (i, h)), pl.BlockSpec((T, HEAD_SLOT), lambda h, i: (0, h)),
                  pl.BlockSpec((None, nq, HEAD_SLOT, t), lambda h, i: (h, 0, 0, 0))],
        out_specs=[pl.BlockSpec((None, None, HEAD_SLOT, t), lambda h, i: (h, i, 0, 0)),
                   pl.BlockSpec((None, None, 1, t), lambda h, i: (h, i, 0, 0))],
        out_shape=[jax.ShapeDtypeStruct((HEADS, nq, HEAD_SLOT, t), BF), jax.ShapeDtypeStruct((HEADS, nq, 1, t), F32)],
        compiler_params=_params("arbitrary", "arbitrary"),
    )(q, k, v_t)


def attn_delta(o, do):
    T = o.shape[0]
    t = _tile(T, TQ_ATTN, 128)

    def body(o_ref, do_ref, delta_ref):
        prod = do_ref[...].astype(F32) * o_ref[...].astype(F32)
        delta_ref[...] = jnp.sum(prod.T, axis=0, keepdims=True)

    blk = pl.BlockSpec((t, HEAD_SLOT), lambda h, i: (i, h))
    return pl.pallas_call(
        body, name="attn_delta", grid=(HEADS, T // t), in_specs=[blk, blk],
        out_specs=pl.BlockSpec((None, None, 1, t), lambda h, i: (h, i, 0, 0)),
        out_shape=jax.ShapeDtypeStruct((HEADS, T // t, 1, t), F32),
        compiler_params=_params("arbitrary", "arbitrary"),
    )(o, do)


def attn_bwd(q, k, k_t, v, do, lse_rows, delta_rows, exchange=None):
    T = q.shape[0]
    t = _tile(T, TQ_ATTN, 128)
    nq = T // t

    def body(q_ref, k_ref, kt_ref, v_ref, do_ref, lse_ref, delta_ref, *rest):
        if exchange is None:
            dk_ref, dv_ref, dqt_ref = rest
        else:
            g_ref, dk_ref, dv_ref, dqt_ref, land_ref, send_sems, recv_sems, local_sem = rest
            first = (pl.program_id(0) == 0) & (pl.program_id(1) == 0)
            last = (pl.program_id(0) == HEADS - 1) & (pl.program_id(1) == nq - 1)
            pl.when(first)(lambda: _HalvesExchange(g_ref, land_ref, send_sems, recv_sems, local_sem).start())
        j = pl.program_id(1)
        kv, vv, ktv = k_ref[...], v_ref[...], kt_ref[...]

        @pl.when(j == 0)
        def _():
            dqt_ref[...] = jnp.zeros_like(dqt_ref)

        def block(i):
            return pl.ds(pl.multiple_of(i * t, t), t)

        def scores(i, masked):
            st = _dot_nt(kv, q_ref[block(i), :])
            return jnp.where(_causal_mask(t, False), st, -jnp.inf) if masked else st

        def add(carry, st, i):
            dk, dv = carry
            qv, dov = q_ref[block(i), :], do_ref[block(i), :]
            pt = jnp.exp2(st - lse_ref[pl.ds(i, 1), :])
            dst = (pt * (_dot_nt(vv, dov) - delta_ref[pl.ds(i, 1), :])).astype(BF)
            dqt_ref[i] += _dot(ktv, dst)
            return dk + _dot(dst, qv), dv + _dot(pt.astype(BF), dov)

        def pair(i, carry):
            s0, s1 = scores(i, False), scores(i + 1, False)
            return add(add(carry, s0, i), s1, i + 1)

        zero = jnp.zeros((t, HEAD_SLOT), F32)
        carry = add((zero, zero), scores(j, True), j)
        rest = nq - 1 - j
        carry = lax.fori_loop(0, rest // 2, lambda ii, c: pair(j + 1 + 2 * ii, c), carry)
        dk, dv = lax.cond(rest % 2 == 1, lambda c: add(c, scores(nq - 1, False), nq - 1), lambda c: c, carry)
        dk_ref[...] = dk * (1.0 / LOG2_E)
        dv_ref[...] = dv
        if exchange is not None:
            pl.when(last)(lambda: _HalvesExchange(g_ref, land_ref, send_sems, recv_sems, local_sem).wait())

    blk = pl.BlockSpec((t, HEAD_SLOT), lambda h, j: (j, h))
    full = pl.BlockSpec((T, HEAD_SLOT), lambda h, j: (0, h))
    rows = pl.BlockSpec((None, nq, t), lambda h, j: (h, 0, 0))
    operands = [q, k, k_t, v, do, lse_rows, delta_rows]
    in_specs = [full, blk, pl.BlockSpec((HEAD_SLOT, t), lambda h, j: (h, j)), blk, full, rows, rows]
    out_specs = [blk, blk, pl.BlockSpec((None, nq, HEAD_SLOT, t), lambda h, j: (h, 0, 0, 0))]
    out_shape = [jax.ShapeDtypeStruct((T, HEADS * HEAD_SLOT), F32)] * 2 + [
        jax.ShapeDtypeStruct((HEADS, nq, HEAD_SLOT, t), F32)]
    scratch = []
    if exchange is not None:
        operands.append(exchange)
        in_specs.append(pl.BlockSpec(memory_space=pl.ANY))
        out_specs.append(pl.BlockSpec(memory_space=pl.ANY))
        out_shape.append(jax.ShapeDtypeStruct((8, exchange.shape[1] // 2, exchange.shape[2]), exchange.dtype))
        scratch = [pltpu.SemaphoreType.DMA((7,)), pltpu.SemaphoreType.DMA((7,)), pltpu.SemaphoreType.DMA]
    return pl.pallas_call(
        body, name="attn_bwd" if exchange is None else "attn_bwd_exchange", grid=(HEADS, nq),
        in_specs=in_specs, out_specs=out_specs, out_shape=out_shape, scratch_shapes=scratch,
        compiler_params=_params("arbitrary", "arbitrary"),
    )(*operands)


def _prev_halo(tm):
    return lambda i: (jnp.maximum(i * (tm // HALO) - 1, 0), 0)


def _next_halo(tm, T):
    return lambda i: (jnp.minimum((i + 1) * (tm // HALO), T // HALO - 1), 0)


def _inv_count(row0, n, w):
    t = row0 + lax.broadcasted_iota(jnp.int32, (n, 1), 0)
    return 1.0 / jnp.minimum(t + 1, w).astype(F32)


def _pool_fwd(u_prev, u, row0):
    tm = u.shape[0]
    out = []
    for g, w in enumerate(POOL_WINDOWS):
        lanes = slice(g * POOL_GROUP, (g + 1) * POOL_GROUP)
        ue = jnp.concatenate([u_prev[:, lanes], u[:, lanes]], axis=0)
        s, step = ue, 1
        while step < w:
            s = s + pltpu.roll(s, step, 0)
            step *= 2
        out.append(s[HALO:, :] * _inv_count(row0, tm, w) - u[:, lanes])
    return out


def _pool_bwd(dp, dp_next, row0):
    tm = dp[0].shape[0]
    out = []
    for g, w in enumerate(POOL_WINDOWS):
        e = jnp.concatenate([dp[g] * _inv_count(row0, tm, w), dp_next[g] * (1.0 / w)], axis=0)
        n = tm + HALO
        s, step = e, 1
        while step < w:
            s = s + pltpu.roll(s, n - step, 0)
            step *= 2
        out.append(s[:tm, :] - dp[g])
    return out


def _mixa_front(x, xp, first, row0, g_ref, win_ref, qan_ref, wq_ref, kvan_ref, wkn_ref):
    xh, r = _rms(x)
    hn = (xh * g_ref[...]).astype(BF)
    z = _dot(hn, win_ref[...])
    xph, _ = _rms(xp)
    u_prev = _dot((xph * g_ref[...]).astype(BF), win_ref[:, :POOL_DIM]) * jnp.where(first, 0.0, 1.0)
    u = z[:, :POOL_DIM]
    pooled = _pool_fwd(u_prev, u, row0)
    c1, c2 = POOL_DIM + Q_RANK, POOL_DIM + Q_RANK + KV_RANK
    qh, rq = _rms(z[:, POOL_DIM:c1])
    nq = (qh * qan_ref[...]).astype(BF)
    kh, rk = _rms(z[:, c1:c2])
    nkv = (kh * kvan_ref[...]).astype(BF)
    qraw = _dot(nq, wq_ref[...])
    kraw = _dot(nkv, wkn_ref[...])
    krope = z[:, c2:c2 + HEAD_SLOT]
    return dict(xh=xh, r=r, hn=hn, pooled=pooled, qh=qh, rq=rq, nq=nq, kh=kh, rk=rk, nkv=nkv,
                qraw=qraw, kraw=kraw, krope=krope)


def mixa_pre_fwd(x, g, win, qan, wq, kvan, wkn, wv, qhn, khn, wpool, pscale, cos, sin):
    T, D = x.shape
    tm = _tile(T, TM_MIX_FWD, HALO)
    HS = HEADS * HEAD_SLOT

    def body(x_ref, xp_ref, g_ref, win_ref, qan_ref, wq_ref, kvan_ref, wkn_ref, wv_ref, qhn_ref, khn_ref,
             wpool_ref, pscale_ref, cos_ref, sin_ref, q_ref, k_ref, v_ref, po_ref):
        i = pl.program_id(0)
        a = _mixa_front(x_ref[...], xp_ref[...], i == 0, i * tm, g_ref, win_ref, qan_ref, wq_ref, kvan_ref, wkn_ref)
        for gi in range(len(POOL_WINDOWS)):
            lanes = slice(gi * POOL_GROUP, (gi + 1) * POOL_GROUP)
            po = _dot(a["pooled"][gi].astype(BF), wpool_ref[gi]) * pscale_ref[:, lanes]
            po_ref[:, lanes] = po.astype(BF)
        cosv, sinv = cos_ref[...], sin_ref[...]
        v_ref[...] = _dot(a["nkv"], wv_ref[...]).astype(BF)
        for h in range(HEADS):
            lanes = slice(h * HEAD_SLOT, (h + 1) * HEAD_SLOT)
            qn, _ = _rms(a["qraw"][:, lanes], QK_DIM)
            q_ref[:, lanes] = (_rope(qn * qhn_ref[...], cosv, sinv) * (ATTN_SCALE * LOG2_E)).astype(BF)
            kn, _ = _rms(a["kraw"][:, lanes] + a["krope"], QK_DIM)
            k_ref[:, lanes] = _rope(kn * khn_ref[...], cosv, sinv).astype(BF)

    tok = lambda w: pl.BlockSpec((tm, w), lambda i: (i, 0))
    whole = lambda arr: pl.BlockSpec(arr.shape, lambda i: (0,) * arr.ndim)
    return pl.pallas_call(
        body, name="mixa_pre_fwd", grid=(T // tm,),
        in_specs=[tok(D), pl.BlockSpec((HALO, D), _prev_halo(tm))] + [whole(a) for a in
                  (g, win, qan, wq, kvan, wkn, wv, qhn, khn, wpool, pscale)] + [tok(HEAD_SLOT), tok(HEAD_SLOT)],
        out_specs=[tok(HS), tok(HS), tok(HS), tok(POOL_DIM)],
        out_shape=[jax.ShapeDtypeStruct((T, HS), BF)] * 3 + [jax.ShapeDtypeStruct((T, POOL_DIM), BF)],
        compiler_params=_params("arbitrary"),
    )(x, x, g, win, qan, wq, kvan, wkn, wv, qhn, khn, wpool, pscale, cos, sin)


def mixa_pre_bwd(x, dy, dq, dk, dv, dpo, g, win, win_t, qan, wq, wq_t, kvan, wkn, wkn_t, wv_t, qhn, khn,
                 wpool, wpool_t, pscale, cos, sin):
    T, D = x.shape
    tm = _tile(T, TM_MIX_BWD, HALO)
    HS = HEADS * HEAD_SLOT
    ZW = win.shape[1]
    nt = T // tm

    def body(x_ref, xp_ref, dy_ref, dq_ref, dk_ref, dv_ref, dpo_ref, dpon_ref, g_ref, win_ref, wint_ref, qan_ref,
             wq_ref, wqt_ref, kvan_ref, wkn_ref, wknt_ref, wvt_ref, qhn_ref, khn_ref, wpool_ref, wpoolt_ref,
             pscale_ref, cos_ref, sin_ref,
             dx_ref, hn_ref, dz_ref, nq_ref, dqraw_ref, nkv_ref, dkraw_ref, pooled_ref, dps_ref,
             dg_ref, dqan_ref, dkvan_ref, dqhn_ref, dkhn_ref, dpscale_ref):
        i = pl.program_id(0)
        first = i == 0
        a = _mixa_front(x_ref[...], xp_ref[...], first, i * tm, g_ref, win_ref, qan_ref, wq_ref, kvan_ref, wkn_ref)
        cosv, sinv = cos_ref[...], sin_ref[...]
        hn_ref[...] = a["hn"]
        nq_ref[...] = a["nq"]
        nkv_ref[...] = a["nkv"]

        has_next = jnp.where(i == nt - 1, 0.0, 1.0)
        dpool, dpool_next, dpscale = [], [], []
        for gi in range(len(POOL_WINDOWS)):
            lanes = slice(gi * POOL_GROUP, (gi + 1) * POOL_GROUP)
            pooled = a["pooled"][gi].astype(BF)
            pooled_ref[:, lanes] = pooled
            dpo_g = dpo_ref[:, lanes]
            dpscale.append(_colsum(dpo_g * _dot(pooled, wpool_ref[gi])))
            dps = (dpo_g * pscale_ref[:, lanes]).astype(BF)
            dps_ref[:, lanes] = dps
            dpool.append(_dot(dps, wpoolt_ref[gi]))
            dps_n = (dpon_ref[:, lanes] * pscale_ref[:, lanes] * has_next).astype(BF)
            dpool_next.append(_dot(dps_n, wpoolt_ref[gi]))
        du = jnp.concatenate(_pool_bwd(dpool, dpool_next, i * tm), axis=1)
        _accumulate(dpscale_ref, first, jnp.concatenate(dpscale, axis=1))

        dqhn = jnp.zeros((1, HEAD_SLOT), F32)
        dkhn = jnp.zeros((1, HEAD_SLOT), F32)
        dkrope = jnp.zeros((tm, HEAD_SLOT), F32)
        for h in range(HEADS):
            lanes = slice(h * HEAD_SLOT, (h + 1) * HEAD_SLOT)
            qhat, rq = _rms(a["qraw"][:, lanes], QK_DIM)
            dqn = _rope_bwd(dq_ref[:, lanes] * ATTN_SCALE, cosv, sinv)
            dqhn = dqhn + _colsum(dqn * qhat)
            dqraw_ref[:, lanes] = _rms_bwd(dqn * qhn_ref[...], qhat, rq, QK_DIM).astype(BF)
            khat, rk = _rms(a["kraw"][:, lanes] + a["krope"], QK_DIM)
            dkn = _rope_bwd(dk_ref[:, lanes], cosv, sinv)
            dkhn = dkhn + _colsum(dkn * khat)
            dkraw = _rms_bwd(dkn * khn_ref[...], khat, rk, QK_DIM)
            dkrope = dkrope + dkraw
            dkraw_ref[:, lanes] = dkraw.astype(BF)
        _accumulate(dqhn_ref, first, dqhn)
        _accumulate(dkhn_ref, first, dkhn)

        dnq = _dot(dqraw_ref[...], wqt_ref[...])
        _accumulate(dqan_ref, first, _colsum(dnq * a["qh"]))
        dql = _rms_bwd(dnq * qan_ref[...], a["qh"], a["rq"])
        dnkv = _dot(dkraw_ref[...], wknt_ref[...]) + _dot(dv_ref[...].astype(BF), wvt_ref[...])
        _accumulate(dkvan_ref, first, _colsum(dnkv * a["kh"]))
        dkvl = _rms_bwd(dnkv * kvan_ref[...], a["kh"], a["rk"])

        dz = jnp.concatenate([du, dql, dkvl, dkrope], axis=1).astype(BF)
        dz_ref[...] = dz
        dhn = _dot(dz, wint_ref[...])
        _accumulate(dg_ref, first, _colsum(dhn * a["xh"]))
        dx_ref[...] = dy_ref[...] + _rms_bwd(dhn * g_ref[...], a["xh"], a["r"])

    tok = lambda w: pl.BlockSpec((tm, w), lambda i: (i, 0))
    whole = lambda arr: pl.BlockSpec(arr.shape, lambda i: (0,) * arr.ndim)
    row = lambda w: pl.BlockSpec((1, w), lambda i: (0, 0))
    weights = (g, win, win_t, qan, wq, wq_t, kvan, wkn, wkn_t, wv_t, qhn, khn, wpool, wpool_t, pscale)
    return pl.pallas_call(
        body, name="mixa_pre_bwd", grid=(nt,),
        in_specs=[tok(D), pl.BlockSpec((HALO, D), _prev_halo(tm)), tok(D), tok(HS), tok(HS), tok(HS), tok(POOL_DIM),
                  pl.BlockSpec((HALO, POOL_DIM), _next_halo(tm, T))] + [whole(a) for a in weights]
                 + [tok(HEAD_SLOT), tok(HEAD_SLOT)],
        out_specs=[tok(D), tok(D), tok(ZW), tok(Q_RANK), tok(HS), tok(KV_RANK), tok(HS), tok(POOL_DIM), tok(POOL_DIM),
                   row(D), row(Q_RANK), row(KV_RANK), row(HEAD_SLOT), row(HEAD_SLOT), row(POOL_DIM)],
        out_shape=[jax.ShapeDtypeStruct((T, D), F32), jax.ShapeDtypeStruct((T, D), BF),
                   jax.ShapeDtypeStruct((T, ZW), BF), jax.ShapeDtypeStruct((T, Q_RANK), BF),
                   jax.ShapeDtypeStruct((T, HS), BF), jax.ShapeDtypeStruct((T, KV_RANK), BF),
                   jax.ShapeDtypeStruct((T, HS), BF), jax.ShapeDtypeStruct((T, POOL_DIM), BF),
                   jax.ShapeDtypeStruct((T, POOL_DIM), BF),
                   jax.ShapeDtypeStruct((1, D), F32), jax.ShapeDtypeStruct((1, Q_RANK), F32),
                   jax.ShapeDtypeStruct((1, KV_RANK), F32), jax.ShapeDtypeStruct((1, HEAD_SLOT), F32),
                   jax.ShapeDtypeStruct((1, HEAD_SLOT), F32), jax.ShapeDtypeStruct((1, POOL_DIM), F32)],
        compiler_params=_params("arbitrary"),
    )(x, x, dy, dq, dk, dv, dpo, dpo, *weights, cos, sin)


def _conv_taps(u_prev, u, cw_ref):
    ue = jnp.concatenate([u_prev, u], axis=0)
    u1 = pltpu.roll(ue, 1, 0)[HALO:, :]
    u2 = pltpu.roll(ue, 2, 0)[HALO:, :]
    return cw_ref[0:1, :] * u2 + cw_ref[1:2, :] * u1 + cw_ref[2:3, :] * u, u1, u2


def mixc_fwd(x, g, win, cw, wout):
    T, D = x.shape
    tm = _tile(T, TM_CONV_FWD, HALO)

    def body(x_ref, xp_ref, g_ref, win_ref, cw_ref, wout_ref, y_ref, z_ref):
        i = pl.program_id(0)
        xv = x_ref[...]
        xh, _ = _rms(xv)
        z = _dot((xh * g_ref[...]).astype(BF), win_ref[...])
        z_ref[...] = z.astype(BF)
        xph, _ = _rms(xp_ref[...])
        zp = _dot((xph * g_ref[...]).astype(BF), win_ref[:, D:])
        u_prev = zp[:, :D] * zp[:, D:] * jnp.where(i == 0, 0.0, 1.0)
        conv, _, _ = _conv_taps(u_prev, z[:, D:2 * D] * z[:, 2 * D:], cw_ref)
        y_ref[...] = xv + _dot((z[:, :D] * conv).astype(BF), wout_ref[...])

    tok = lambda w: pl.BlockSpec((tm, w), lambda i: (i, 0))
    whole = lambda arr: pl.BlockSpec(arr.shape, lambda i: (0,) * arr.ndim)
    return pl.pallas_call(
        body, name="mixc_fwd", grid=(T // tm,),
        in_specs=[tok(D), pl.BlockSpec((HALO, D), _prev_halo(tm)), whole(g), whole(win), whole(cw), whole(wout)],
        out_specs=[tok(D), tok(3 * D)],
        out_shape=[jax.ShapeDtypeStruct((T, D), F32), jax.ShapeDtypeStruct((T, 3 * D), BF)],
        compiler_params=_params("arbitrary"),
    )(x, x, g, win, cw, wout)


def mixc_bwd(x, dy, z, g, win_t, cw, wout_t):
    T, D = x.shape
    tm = _tile(T, TM_CONV_BWD, HALO)
    nt = T // tm

    def body(x_ref, dy_ref, dyn_ref, z_ref, zp_ref, zn_ref, g_ref, wint_ref, cw_ref, woutt_ref,
             dx_ref, hn_ref, dz_ref, v_ref, dcw_ref, dg_ref):
        i = pl.program_id(0)
        first = i == 0
        xh, r = _rms(x_ref[...])
        hn_ref[...] = (xh * g_ref[...]).astype(BF)
        zv = z_ref[...].astype(F32)
        gb, gc, hh = zv[:, :D], zv[:, D:2 * D], zv[:, 2 * D:]
        u = gc * hh
        zp = zp_ref[...].astype(F32)
        u_prev = zp[:, D:2 * D] * zp[:, 2 * D:] * jnp.where(first, 0.0, 1.0)
        conv, u1, u2 = _conv_taps(u_prev, u, cw_ref)
        v_ref[...] = (gb * conv).astype(BF)

        dv = _dot(dy_ref[...].astype(BF), woutt_ref[...])
        dconv = dv * gb
        dv_next = _dot(dyn_ref[...].astype(BF), woutt_ref[...])
        dconv_next = dv_next * zn_ref[:, :D].astype(F32) * jnp.where(i == nt - 1, 0.0, 1.0)
        de = jnp.concatenate([dconv, dconv_next], axis=0)
        n = tm + HALO
        du = (cw_ref[2:3, :] * dconv + cw_ref[1:2, :] * pltpu.roll(de, n - 1, 0)[:tm, :]
              + cw_ref[0:1, :] * pltpu.roll(de, n - 2, 0)[:tm, :])
        for tap, shifted in enumerate((u2, u1, u)):
            _accumulate(dcw_ref.at[tap:tap + 1, :], first, _colsum(dconv * shifted))
        dz = jnp.concatenate([dv * conv, du * hh, du * gc], axis=1).astype(BF)
        dz_ref[...] = dz
        dhn = _dot(dz, wint_ref[...])
        _accumulate(dg_ref, first, _colsum(dhn * xh))
        dx_ref[...] = dy_ref[...] + _rms_bwd(dhn * g_ref[...], xh, r)

    tok = lambda w: pl.BlockSpec((tm, w), lambda i: (i, 0))
    whole = lambda arr: pl.BlockSpec(arr.shape, lambda i: (0,) * arr.ndim)
    return pl.pallas_call(
        body, name="mixc_bwd", grid=(nt,),
        in_specs=[tok(D), tok(D), pl.BlockSpec((HALO, D), _next_halo(tm, T)), tok(3 * D),
                  pl.BlockSpec((HALO, 3 * D), _prev_halo(tm)), pl.BlockSpec((HALO, 3 * D), _next_halo(tm, T)),
                  whole(g), whole(win_t), whole(cw), whole(wout_t)],
        out_specs=[tok(D), tok(D), tok(3 * D), tok(D), pl.BlockSpec((3, D), lambda i: (0, 0)),
                   pl.BlockSpec((1, D), lambda i: (0, 0))],
        out_shape=[jax.ShapeDtypeStruct((T, D), F32), jax.ShapeDtypeStruct((T, D), BF),
                   jax.ShapeDtypeStruct((T, 3 * D), BF), jax.ShapeDtypeStruct((T, D), BF),
                   jax.ShapeDtypeStruct((3, D), F32), jax.ShapeDtypeStruct((1, D), F32)],
        compiler_params=_params("arbitrary"),
    )(x, dy, dy, z, z, z, g, win_t, cw, wout_t)


def loss_head(y, target):
    T, D = y.shape
    tm = _tile(T, TM_MM, 8)

    def body(y_ref, t_ref, sum_ref, dy_ref):
        err = y_ref[...] - t_ref[...]
        dy_ref[...] = err * (1.0 / D)
        part = jnp.sum(jnp.sum(err * err, axis=-1, keepdims=True) * (1.0 / D), axis=0, keepdims=True)
        _accumulate(sum_ref, pl.program_id(0) == 0, jnp.broadcast_to(part, sum_ref.shape))

    return pl.pallas_call(
        body, name="loss_head", grid=(T // tm,),
        in_specs=[pl.BlockSpec((tm, D), lambda i: (i, 0))] * 2,
        out_specs=[pl.BlockSpec((8, 128), lambda i: (0, 0)), pl.BlockSpec((tm, D), lambda i: (i, 0))],
        out_shape=[jax.ShapeDtypeStruct((8, 128), F32), jax.ShapeDtypeStruct((T, D), F32)],
        compiler_params=_params("arbitrary"),
    )(y, target)


def adamw(w, g, m, v):
    R, C = w.shape
    tr = _tile(R, TR_FLAT, 16)

    def body(w_ref, g_ref, m_ref, v_ref, g32_ref, d_ref, m2_ref, v2_ref):
        gv = g_ref[...].astype(F32)
        g32_ref[...] = gv
        m2 = ADAM_B1 * m_ref[...] + (1.0 - ADAM_B1) * gv
        v2 = ADAM_B2 * v_ref[...] + (1.0 - ADAM_B2) * (gv * gv)
        m2_ref[...] = m2
        v2_ref[...] = v2
        m_hat = m2 / (1.0 - ADAM_B1 ** ADAM_STEP)
        v_hat = v2 / (1.0 - ADAM_B2 ** ADAM_STEP)
        d_ref[...] = -ADAM_LR * (m_hat / (jnp.sqrt(v_hat) + ADAM_EPS) + ADAM_WD * w_ref[...])

    spec = pl.BlockSpec((tr, C), lambda i: (i, 0))
    return pl.pallas_call(
        body, name="adamw", grid=(R // tr,), in_specs=[spec] * 4, out_specs=[spec] * 4,
        out_shape=[jax.ShapeDtypeStruct((R, C), F32)] * 4,
        compiler_params=_params("arbitrary"),
    )(w, g, m, v)


def sum_slots(a, out_dtype):
    S, R, C = a.shape
    tr = _tile(R, TR_FLAT // 2, 16)

    def body(a_ref, o_ref):
        acc = a_ref[0].astype(F32)
        for s in range(1, S):
            acc = acc + a_ref[s].astype(F32)
        o_ref[...] = acc.astype(out_dtype)

    return pl.pallas_call(
        body, name="sum_slots", grid=(R // tr,),
        in_specs=[pl.BlockSpec((S, tr, C), lambda i: (0, i, 0))],
        out_specs=pl.BlockSpec((tr, C), lambda i: (i, 0)),
        out_shape=jax.ShapeDtypeStruct((R, C), out_dtype),
        compiler_params=_params("arbitrary"),
    )(a)


ANY = pl.BlockSpec(memory_space=pl.ANY)


def _place():
    return lax.axis_index("x"), lax.axis_index("y"), lax.axis_index("c")


class _LocalCopy:
    def __init__(self, src, dst, sem, rows):
        n = LOCAL_CHUNKS if rows % (16 * LOCAL_CHUNKS) == 0 else 1
        cr = rows // n
        self.parts = [pltpu.make_async_copy(src.at[pl.ds(q * cr, cr), :], dst.at[pl.ds(q * cr, cr), :], sem)
                      for q in range(n)]
        self.whole = pltpu.make_async_copy(src, dst, sem)

    def start(self):
        for part in self.parts:
            part.start()

    def wait(self):
        self.whole.wait()


class _HalvesExchange:
    def __init__(self, g_ref, land_ref, send_sems, recv_sems, local_sem):
        half = g_ref.shape[1] // 2
        x, y, c = _place()
        me = 4 * x + 2 * y + c
        self.peers = []
        for mask in range(1, 8):
            mx, my, mc = (mask >> 2) & 1, (mask >> 1) & 1, mask & 1
            self.peers.append(((1 - x) if mx else x, (1 - y) if my else y, (1 - c) if mc else c))

        def piece(px, py, pc):
            return g_ref.at[2 * px + py, pl.ds(pc * half, half), :]

        def copy(k, sender, to):
            return pltpu.make_async_remote_copy(
                src_ref=piece(*to), dst_ref=land_ref.at[sender], send_sem=send_sems.at[k], recv_sem=recv_sems.at[k],
                device_id=to, device_id_type=MESH)

        self.own = _LocalCopy(piece(x, y, c), land_ref.at[me], local_sem, half)
        self.sends = [copy(k, me, peer) for k, peer in enumerate(self.peers)]
        self.arrivals = [copy(k, 4 * px + 2 * py + pc, (x, y, c)) for k, (px, py, pc) in enumerate(self.peers)]

    def start(self):
        self.own.start()
        for cp in self.sends:
            cp.start()

    def wait(self):
        for cp in self.arrivals:
            cp.wait_recv()
        for cp in self.sends:
            cp.wait_send()
        self.own.wait()


def allgather_shards(w):
    R, C = w.shape
    half = R // 2
    n = GATHER_CHUNKS if half % (16 * GATHER_CHUNKS) == 0 else 1
    cr = half // n

    def body(w_ref, out_ref, send_sems, recv_sems, local_sem):
        x, y, c = _place()
        sibling = (x, y, 1 - c)
        chips = [(1 - x, y), (x, 1 - y), (1 - x, 1 - y)]

        def rows(px, py, pc, q):
            return out_ref.at[2 * px + py, pl.ds(pc * half + q * cr, cr), :]

        def copy(k, block, q, to, src=None):
            return pltpu.make_async_remote_copy(
                src_ref=rows(*block, q) if src is None else src, dst_ref=rows(*block, q),
                send_sem=send_sems.at[k * n + q], recv_sem=recv_sems.at[k * n + q], device_id=to, device_id_type=MESH)

        mine = _LocalCopy(w_ref, out_ref.at[2 * x + y], local_sem, R)
        mine.start()
        first = [copy(j, (x, y, c), q, (*chip, c), src=w_ref.at[pl.ds(c * half + q * cr, cr), :])
                 for q in range(n) for j, chip in enumerate(chips)]
        for cp in first:
            cp.start()
        passed = []
        for q in range(n):
            for j, chip in enumerate(chips):
                copy(j, (*chip, c), q, (x, y, c)).wait_recv()
                passed.append(copy(3 + j, (*chip, c), q, sibling))
                passed[-1].start()
        for q in range(n):
            for j, chip in enumerate(chips):
                copy(3 + j, (*chip, 1 - c), q, (x, y, c)).wait_recv()
        for cp in first + passed:
            cp.wait_send()
        mine.wait()

    return pl.pallas_call(
        body, name="allgather_shards", in_specs=[ANY], out_specs=ANY,
        out_shape=jax.ShapeDtypeStruct((4, R, C), w.dtype),
        scratch_shapes=[pltpu.SemaphoreType.DMA((6 * n,)), pltpu.SemaphoreType.DMA((6 * n,)), pltpu.SemaphoreType.DMA],
    )(w)


def exchange_partials(grads, small):
    _, R, C = grads.shape
    half = R // 2
    Rs = small.shape[0]

    def body(g_ref, s_ref, land_ref, sland_ref, send_sems, recv_sems, local_sems):
        x, y, c = _place()
        me = 4 * x + 2 * y + c
        big = _HalvesExchange(g_ref, land_ref, send_sems, recv_sems, local_sems.at[0])

        def little(k, sender, to):
            return pltpu.make_async_remote_copy(
                src_ref=s_ref, dst_ref=sland_ref.at[sender], send_sem=send_sems.at[7 + k],
                recv_sem=recv_sems.at[7 + k], device_id=to, device_id_type=MESH)

        own_small = pltpu.make_async_copy(s_ref, sland_ref.at[me], local_sems.at[1])
        own_small.start()
        sends = [little(k, me, peer) for k, peer in enumerate(big.peers)]
        for cp in sends:
            cp.start()
        big.start()
        for k, (px, py, pc) in enumerate(big.peers):
            little(k, 4 * px + 2 * py + pc, (x, y, c)).wait_recv()
        big.wait()
        for cp in sends:
            cp.wait_send()
        own_small.wait()

    return pl.pallas_call(
        body, name="exchange_partials", in_specs=[ANY, ANY], out_specs=[ANY, ANY],
        out_shape=[jax.ShapeDtypeStruct((8, half, C), grads.dtype), jax.ShapeDtypeStruct((8, Rs, C), small.dtype)],
        scratch_shapes=[pltpu.SemaphoreType.DMA((14,)), pltpu.SemaphoreType.DMA((14,)), pltpu.SemaphoreType.DMA((2,))],
    )(grads, small)


def share_with_sibling(parts):
    C = parts[0].shape[1]
    halves = [p.shape[0] for p in parts]
    offsets = [2 * sum(halves[:i]) for i in range(len(parts))]

    def body(*refs):
        p_refs, out_ref = refs[:len(parts)], refs[len(parts)]
        send_sems, recv_sems, local_sems = refs[len(parts) + 1:]
        x, y, c = _place()
        pending = []
        for i, (p_ref, half, off) in enumerate(zip(p_refs, halves, offsets)):
            def rows(pc, half=half, off=off):
                return out_ref.at[pl.ds(off + pc * half, half), :]

            own = _LocalCopy(p_ref, rows(c), local_sems.at[i], half)
            own.start()
            n = SIBLING_CHUNKS if half % (16 * SIBLING_CHUNKS) == 0 else 1
            cr = half // n
            for q in range(n):
                pltpu.make_async_remote_copy(
                    src_ref=p_ref.at[pl.ds(q * cr, cr), :], dst_ref=out_ref.at[pl.ds(off + c * half + q * cr, cr), :],
                    send_sem=send_sems.at[i], recv_sem=recv_sems.at[i], device_id=(x, y, 1 - c),
                    device_id_type=MESH).start()
            pending.append((own, pltpu.make_async_remote_copy(
                src_ref=p_ref, dst_ref=rows(1 - c), send_sem=send_sems.at[i], recv_sem=recv_sems.at[i],
                device_id=(x, y, c), device_id_type=MESH)))
        for own, everything in pending:
            everything.wait_recv()
            everything.wait_send()
            own.wait()

    k = len(parts)
    return pl.pallas_call(
        body, name="share_with_sibling", in_specs=[ANY] * k, out_specs=ANY,
        out_shape=jax.ShapeDtypeStruct((2 * sum(halves), C), parts[0].dtype),
        scratch_shapes=[pltpu.SemaphoreType.DMA((k,)), pltpu.SemaphoreType.DMA((k,)), pltpu.SemaphoreType.DMA((k,))],
    )(*parts)


FLAT_SEG = 16 * FLAT_COLS


def _seg_rows(n):
    return -(-n // FLAT_SEG) * 16


def _flat_rows(sizes):
    rows = sum(_seg_rows(n) for n in sizes)
    return -(-rows // FLAT_ROW_ALIGN) * FLAT_ROW_ALIGN


def pack_flat(groups, lead=()):
    parts = []
    for arrays in groups:
        sizes = [int(np.prod(a.shape[len(lead):])) for a in arrays]
        used = 0
        for a, n in zip(arrays, sizes):
            rows = _seg_rows(n)
            flat = jnp.pad(a.reshape(*lead, n), [(0, 0)] * len(lead) + [(0, rows * FLAT_COLS - n)])
            parts.append(flat.reshape(*lead, rows, FLAT_COLS))
            used += rows
        if _flat_rows(sizes) > used:
            parts.append(jnp.zeros((*lead, _flat_rows(sizes) - used, FLAT_COLS), arrays[0].dtype))
    return jnp.concatenate(parts, axis=len(lead))


def unpack_flat(flat, group_shapes, lead=()):
    out, r0 = [], 0
    for shapes in group_shapes:
        arrays, start = [], r0
        for shp in shapes:
            n = int(np.prod(shp))
            rows = _seg_rows(n)
            seg = flat[..., r0:r0 + rows, :].reshape(*lead, rows * FLAT_COLS)[..., :n]
            arrays.append(seg.reshape(*lead, *shp))
            r0 += rows
        r0 = start + _flat_rows([int(np.prod(shp)) for shp in shapes])
        out.append(arrays)
    return out


def _f32_bits_as(a, dtype):
    return lax.bitcast_convert_type(a, dtype).reshape(*a.shape[:-1], -1)


def _f32_from_bits(a):
    k = 4 // a.dtype.itemsize
    if k > 1:
        a = a.reshape(*a.shape[:-1], a.shape[-1] // k, k)
    return lax.bitcast_convert_type(a, F32)


def _join_shards(name, a):
    if name in COL_SHARDED:
        return jnp.transpose(a, (1, 2, 0, 3)).reshape(a.shape[1], a.shape[2], 4 * a.shape[3])
    return jnp.transpose(a, (1, 0, 2, 3)).reshape(a.shape[1], 4 * a.shape[2], a.shape[3])


def _split_shards(name, a):
    L, K, N = a.shape
    if name in COL_SHARDED:
        return jnp.transpose(a.reshape(L, K, 4, N // 4), (2, 0, 1, 3))
    return jnp.transpose(a.reshape(L, 4, K // 4, N), (1, 0, 2, 3))


def _first_layers(name):
    return 0 if name.startswith('c_') else 1


def _pad_heads(a, width):
    a = a.reshape(*a.shape[:-1], HEADS, width)
    a = jnp.pad(a, [(0, 0)] * (a.ndim - 1) + [(0, HEAD_SLOT - width)])
    return a.reshape(*a.shape[:-2], HEADS * HEAD_SLOT)


def _unpad_heads(a, width):
    a = a.reshape(*a.shape[:-1], HEADS, HEAD_SLOT)[..., :width]
    return a.reshape(*a.shape[:-2], HEADS * width)


def _rope_tables(T):
    pos = jnp.arange(T, dtype=F32)
    inv_freq = ROPE_THETA ** (-jnp.arange(0, ROPE_DIM, 2, dtype=F32) / ROPE_DIM)
    ang = pos[:, None] * inv_freq[None, :]
    cos, sin = jnp.cos(ang), jnp.sin(ang)
    pad = HEAD_SLOT - QK_DIM
    cos_t = jnp.concatenate([jnp.ones((T, NOPE_DIM), F32), cos, cos, jnp.zeros((T, pad), F32)], axis=1)
    sin_t = jnp.concatenate([jnp.zeros((T, NOPE_DIM), F32), sin, sin, jnp.zeros((T, pad), F32)], axis=1)
    return cos_t, sin_t


def _even_weights(W, i):
    c3 = POOL_DIM + Q_RANK + KV_RANK
    w_in = W['a_w_in'][i]
    D = w_in.shape[0]
    rope_cols = jnp.concatenate([jnp.zeros((D, NOPE_DIM), BF), w_in[:, c3:], jnp.zeros((D, HEAD_SLOT - QK_DIM), BF)], axis=1)
    win = jnp.concatenate([w_in[:, :c3], rope_cols], axis=1)
    wq = _pad_heads(W['a_w_q_up'][i], QK_DIM)
    kv = W['a_w_kv_up'][i].reshape(KV_RANK, HEADS, NOPE_DIM + V_DIM)
    wkn = _pad_heads(kv[:, :, :NOPE_DIM].reshape(KV_RANK, HEADS * NOPE_DIM), NOPE_DIM)
    wv = _pad_heads(kv[:, :, NOPE_DIM:].reshape(KV_RANK, HEADS * V_DIM), V_DIM)
    w_out = W['a_w_out'][i]
    wo_pool = w_out[:POOL_DIM]
    wo_attn = _pad_heads(w_out[POOL_DIM:].T, V_DIM).T
    wpool = W['a_w_pool'][i]
    pad = lambda a: jnp.pad(a, (0, HEAD_SLOT - QK_DIM))[None, :]
    return dict(win=win, win_t=win.T, wq=wq, wq_t=wq.T, wkn=wkn, wkn_t=wkn.T, wv=wv, wv_t=wv.T,
                wo_pool=wo_pool, wo_pool_t=wo_pool.T, wo_attn=wo_attn, wo_attn_t=wo_attn.T,
                wpool=wpool, wpool_t=jnp.transpose(wpool, (0, 2, 1)),
                qan=W['a_q_a_norm'][i][None, :], kvan=W['a_kv_a_norm'][i][None, :],
                qhn=pad(W['a_q_head_norm'][i]), khn=pad(W['a_k_head_norm'][i]),
                pscale=W['a_pool_scale'][i][None, :], g=W['mix_norm'][2 * i][None, :])


def kernel(*args):
    p = dict(zip(INPUTS, args))
    x0 = p['x'][0]
    target = p['loss_target'][0]
    T, D = x0.shape

    shard_shapes = [p[n].shape for n in SHARDED]

    wire = [_f32_bits_as(p[n], BF) if n == 'c_conv_w' else p[n].astype(BF) for n in SHARDED]
    gathered = allgather_shards(pack_flat([wire]))
    W = {}
    for n, a in zip(SHARDED, unpack_flat(gathered, [[a.shape for a in wire]], lead=(4,))[0]):
        W[n] = _join_shards(n, _f32_from_bits(a) if n == 'c_conv_w' else a)
    for n in REPLICATED:
        W[n] = p[n]
    W['a_w_pool'] = p['a_w_pool'].astype(BF)
    cos, sin = _rope_tables(T)

    def ffn_weights(pre, l):
        wg, wu, wd = W[pre + '_w_gate'][l], W[pre + '_w_up'][l], W[pre + '_w_down'][l]
        return dict(g=W[pre + '_norm'][l][None, :], wg=wg, wu=wu, wd=wd, wg_t=wg.T, wu_t=wu.T, wd_t=wd.T)

    saved = []
    x = x0
    for l in range(DEPTH):
        s = dict(x0=x)
        f1 = ffn_weights('ffn1', l)
        x, s['g1'], s['u1'] = ffn_fwd(x, f1['g'], f1['wg'], f1['wu'], f1['wd'])
        s['x1'] = x
        if l % 2 == 0:
            e = _even_weights(W, l // 2)
            s['q'], s['k'], s['v'], s['po'] = mixa_pre_fwd(
                x, e['g'], e['win'], e['qan'], e['wq'], e['kvan'], e['wkn'], e['wv'], e['qhn'], e['khn'],
                e['wpool'], e['pscale'], cos, sin)
            o_t, s['lse'] = attn_fwd(s['q'], s['k'], _blocks_transposed(s['v'], _tile(T, TQ_ATTN, 128)))
            s['o'] = _blocks_untransposed(o_t)
            x = mm_multi([(s['po'], e['wo_pool']), (s['o'], e['wo_attn'])], res=x)
        else:
            i = l // 2
            x, s['z'] = mixc_fwd(x, W['mix_norm'][l][None, :], W['c_w_in'][i], W['c_conv_w'][i].astype(F32),
                                 W['c_w_out'][i])
        s['x2'] = x
        f2 = ffn_weights('ffn2', l)
        x, s['g2'], s['u2'] = ffn_fwd(x, f2['g'], f2['wg'], f2['wu'], f2['wd'])
        saved.append(s)

    loss_sum, dy = loss_head(x, target)
    loss = lax.psum(0.5 * loss_sum[0, 0], AXES)

    G = {n: [None] * p[n].shape[0] for n in WEIGHTS}

    def ffn_back(pre, l, x_in, gg, uu, dy):
        f = ffn_weights(pre, l)
        dx, n, dyh, h, dgate, dup, dgn = ffn_bwd(x_in, dy, f['g'], gg, uu, f['wd_t'], f['wg_t'], f['wu_t'])
        G[pre + '_norm'][l] = dgn[0]
        G[pre + '_w_gate'][l] = mm_tn(n, dgate)
        G[pre + '_w_up'][l] = mm_tn(n, dup)
        G[pre + '_w_down'][l] = mm_tn(dyh, h).T
        return dx

    def partial_shards(late):
        arrays = [_split_shards(n, jnp.stack(G[n][_first_layers(n):] if late else G[n][:_first_layers(n)]))
                  for n in SHARDED if (p[n].shape[0] > _first_layers(n) if late else _first_layers(n) > 0)]
        return pack_flat([arrays], lead=(4,)).astype(BF)

    t_attn = _tile(T, TQ_ATTN, 128)
    for l in reversed(range(DEPTH)):
        s = saved[l]
        dy = ffn_back('ffn2', l, s['x2'], s['g2'], s['u2'], dy)
        i = l // 2
        if l % 2 == 0:
            e = _even_weights(W, i)
            G['a_w_out'][i] = jnp.concatenate(
                [mm_tn(s['po'], dy), _unpad_heads(mm_tn(s['o'], dy).T, V_DIM).T], axis=0)
            dpo = mm_multi([(dy, e['wo_pool_t'])])
            do = mm_multi([(dy, e['wo_attn_t'])], out_dtype=BF)
            as_rows = lambda a: a.reshape(HEADS, T // t_attn, t_attn)
            attn_args = (s['q'], s['k'], s['k'].T, s['v'], do, as_rows(s['lse']), as_rows(attn_delta(s['o'], do)))
            if l == 0:
                dk, dv, dq_t, land_late = attn_bwd(*attn_args, exchange=partial_shards(late=True))
            else:
                dk, dv, dq_t = attn_bwd(*attn_args)
            dq = _blocks_untransposed(dq_t)
            (dy, hn, dz, nq, dqraw, nkv, dkraw, pooled, dps, dg, dqan, dkvan, dqhn, dkhn, dpscale) = mixa_pre_bwd(
                s['x1'], dy, dq, dk, dv, dpo, e['g'], e['win'], e['win_t'], e['qan'], e['wq'], e['wq_t'], e['kvan'],
                e['wkn'], e['wkn_t'], e['wv_t'], e['qhn'], e['khn'], e['wpool'], e['wpool_t'], e['pscale'], cos, sin)
            c3 = POOL_DIM + Q_RANK + KV_RANK
            dwin = mm_tn(hn, dz)
            G['a_w_in'][i] = jnp.concatenate([dwin[:, :c3], dwin[:, c3 + NOPE_DIM:c3 + QK_DIM]], axis=1)
            G['a_w_q_up'][i] = _unpad_heads(mm_tn(nq, dqraw), QK_DIM)
            dwkn = _unpad_heads(mm_tn(nkv, dkraw), NOPE_DIM).reshape(KV_RANK, HEADS, NOPE_DIM)
            dwv = _unpad_heads(mm_tn(nkv, dv), V_DIM).reshape(KV_RANK, HEADS, V_DIM)
            G['a_w_kv_up'][i] = jnp.concatenate([dwkn, dwv], axis=2).reshape(KV_RANK, HEADS * (NOPE_DIM + V_DIM))
            dwp = mm_tn(pooled, dps)
            G['a_w_pool'][i] = jnp.stack([dwp[g * POOL_GROUP:(g + 1) * POOL_GROUP, g * POOL_GROUP:(g + 1) * POOL_GROUP]
                                          for g in range(len(POOL_WINDOWS))])
            G['mix_norm'][l] = dg[0]
            G['a_q_a_norm'][i] = dqan[0]
            G['a_kv_a_norm'][i] = dkvan[0]
            G['a_q_head_norm'][i] = dqhn[0, :QK_DIM]
            G['a_k_head_norm'][i] = dkhn[0, :QK_DIM]
            G['a_pool_scale'][i] = dpscale[0]
        else:
            w_in, w_out = W['c_w_in'][i], W['c_w_out'][i]
            dy_in = dy
            dy, hn, dz, gated, dcw, dg = mixc_bwd(s['x1'], dy, s['z'], W['mix_norm'][l][None, :], w_in.T,
                                                  W['c_conv_w'][i].astype(F32), w_out.T)
            G['c_w_in'][i] = mm_tn(hn, dz)
            G['c_w_out'][i] = mm_tn(gated, dy_in)
            G['c_conv_w'][i] = dcw
            G['mix_norm'][l] = dg[0]
        dy = ffn_back('ffn1', l, s['x0'], s['g1'], s['u1'], dy)
    grad_x = dy[None]

    partial_small = pack_flat([[jnp.stack(G[n]) for n in REPLICATED]])
    land_first, sland = exchange_partials(partial_shards(late=False), partial_small)
    g_big = share_with_sibling([sum_slots(land_first, BF), sum_slots(land_late, BF)])
    g_small = sum_slots(sland, F32)

    def layer_groups(pre):
        first = [p[pre + n][:_first_layers(n)] for n in SHARDED if _first_layers(n) > 0]
        late = [p[pre + n][_first_layers(n):] for n in SHARDED if p[n].shape[0] > _first_layers(n)]
        return [first, late]

    outs = {}
    g32, delta, m2, v2 = adamw(pack_flat(layer_groups('')), g_big, pack_flat(layer_groups('m_')),
                               pack_flat(layer_groups('v_')))
    group_shapes = [[a.shape for a in group] for group in layer_groups('')]
    for kind, arr in (('grad_', g32), ('delta_', delta), ('new_m_', m2), ('new_v_', v2)):
        first, late = (iter(group) for group in unpack_flat(arr, group_shapes))
        for n in SHARDED:
            pieces = ([next(first)] if _first_layers(n) > 0 else []) + (
                [next(late)] if p[n].shape[0] > _first_layers(n) else [])
            outs[kind + n] = pieces[0] if len(pieces) == 1 else jnp.concatenate(pieces, axis=0)
    small_groups = [[p[n].shape for n in REPLICATED]]
    flat = lambda pre: pack_flat([[p[pre + n] for n in REPLICATED]])
    g32, delta, m2, v2 = adamw(flat(''), g_small, flat('m_'), flat('v_'))
    for kind, arr in (('grad_', g32), ('delta_', delta), ('new_m_', m2), ('new_v_', v2)):
        for n, a in zip(REPLICATED, unpack_flat(arr, small_groups)[0]):
            outs[kind + n] = a
    return (loss, grad_x, *[outs[k + n] for k in ('grad_', 'delta_', 'new_m_', 'new_v_') for n in WEIGHTS])
```

```python
import functools

import numpy as np
import jax
import jax.numpy as jnp
from jax import lax
from jax.experimental import pallas as pl
from jax.experimental.pallas import tpu as pltpu

BF, F32 = jnp.bfloat16, jnp.float32
MESH = pl.DeviceIdType.MESH
AXES = ("x", "y", "c")

NORM_EPS = 1e-6
DEPTH = 4
HEADS = 8
HEAD_SLOT = 128
QK_DIM, NOPE_DIM, ROPE_DIM, V_DIM = 96, 64, 32, 64
POOL_WINDOWS = (2, 4, 8, 16)
POOL_DIM, POOL_GROUP = 512, 128
Q_RANK, KV_RANK = 384, 256
ROPE_THETA = 10000.0
HALO = 16
ATTN_SCALE = QK_DIM ** -0.5
LOG2_E = 1.4426950408889634

ADAM_LR, ADAM_B1, ADAM_B2, ADAM_EPS, ADAM_WD, ADAM_STEP = 0.001, 0.9, 0.999, 1e-08, 0.01, 10

TM_FFN_FWD, TM_FFN_BWD, TF_FFN = 512, 256, 256
TM_MIX_FWD, TM_MIX_BWD = 512, 256
TM_CONV_FWD, TM_CONV_BWD = 256, 256
TQ_ATTN = 512
TM_MM = 512
TK_TN, BM_TN, BN_TN = 2048, 1024, 1536
FLAT_COLS = 1024
FLAT_ROW_ALIGN = 1024
SIBLING_CHUNKS = 16
LOCAL_CHUNKS = 16
GATHER_CHUNKS = 8
TR_FLAT = 256
VMEM_LIMIT = 56 * 1024 * 1024

WEIGHTS = ['ffn1_norm', 'ffn1_w_gate', 'ffn1_w_up', 'ffn1_w_down', 'mix_norm', 'ffn2_norm', 'ffn2_w_gate',
           'ffn2_w_up', 'ffn2_w_down', 'a_w_in', 'a_q_a_norm', 'a_w_q_up', 'a_kv_a_norm', 'a_w_kv_up',
           'a_q_head_norm', 'a_k_head_norm', 'a_w_pool', 'a_pool_scale', 'a_w_out', 'c_w_in', 'c_conv_w',
           'c_w_out']
COL_SHARDED = ('ffn1_w_gate', 'ffn1_w_up', 'ffn2_w_gate', 'ffn2_w_up', 'a_w_in', 'a_w_q_up', 'a_w_kv_up',
               'c_w_in', 'c_conv_w')
ROW_SHARDED = ('ffn1_w_down', 'ffn2_w_down', 'a_w_out', 'c_w_out')
SHARDED = tuple(n for n in WEIGHTS if n in COL_SHARDED or n in ROW_SHARDED)
REPLICATED = tuple(n for n in WEIGHTS if n not in SHARDED)
INPUTS = ['x'] + WEIGHTS + ['loss_target'] + ['m_' + n for n in WEIGHTS] + ['v_' + n for n in WEIGHTS]


def _dot(a, b):
    return jnp.dot(a, b, preferred_element_type=F32)


def _dot_nt(a, b):
    return lax.dot_general(a, b, (((1,), (1,)), ((), ())), preferred_element_type=F32)


def _dot_tn(a, b):
    return lax.dot_general(a, b, (((0,), (0,)), ((), ())), preferred_element_type=F32)


def _params(*sem):
    return pltpu.CompilerParams(dimension_semantics=sem or None, vmem_limit_bytes=VMEM_LIMIT)


def _tile(n, cap, unit):
    if n <= cap:
        return n
    best = None
    for t in range(unit, cap + 1, unit):
        if n % t == 0:
            best = t
    assert best is not None, (n, cap, unit)
    return best


def _rms(x, width=None):
    ms = jnp.sum(x * x, axis=-1, keepdims=True) * (1.0 / (width or x.shape[-1]))
    r = lax.rsqrt(ms + NORM_EPS)
    return x * r, r


def _rms_bwd(a, xhat, r, width=None):
    return r * (a - xhat * (jnp.sum(a * xhat, axis=-1, keepdims=True) * (1.0 / (width or a.shape[-1]))))


def _colsum(a):
    return jnp.sum(a, axis=0, keepdims=True)


def _accumulate(ref, first, value):
    @pl.when(first)
    def _():
        ref[...] = value

    @pl.when(jnp.logical_not(first))
    def _():
        ref[...] += value


def _rot_half(v):
    lane = lax.broadcasted_iota(jnp.int32, v.shape, 1)
    rot = jnp.where(lane < NOPE_DIM + ROPE_DIM // 2, -pltpu.roll(v, HEAD_SLOT - ROPE_DIM // 2, 1),
                    pltpu.roll(v, ROPE_DIM // 2, 1))
    return jnp.where((lane >= NOPE_DIM) & (lane < QK_DIM), rot, 0.0)


def _rope(v, cos, sin):
    return v * cos + _rot_half(v) * sin


def _rope_bwd(d, cos, sin):
    return d * cos - _rot_half(d * sin)


def _resident(arr):
    return pl.BlockSpec(arr.shape, lambda i: (0,) * arr.ndim, pipeline_mode=pl.Buffered(1))


def ffn_fwd(x, g, wg, wu, wd):
    T, D = x.shape
    F = wg.shape[1]
    tm, tf = _tile(T, TM_FFN_FWD, 8), _tile(F, TF_FFN, 128)
    nf = F // tf

    def body(x_ref, g_ref, wg_ref, wu_ref, wd_ref, y_ref, gg_ref, uu_ref, h_sc):
        xv = x_ref[...]
        xh, _ = _rms(xv)
        n = (xh * g_ref[...]).astype(BF)

        def projections(c):
            cols = slice(c * tf, (c + 1) * tf)
            return _dot(n, wg_ref[:, cols]), _dot(n, wu_ref[:, cols])

        ahead = projections(0)
        for c in range(nf):
            gg, uu = ahead
            if c + 1 < nf:
                ahead = projections(c + 1)
            cols = slice(c * tf, (c + 1) * tf)
            gg_ref[:, cols] = gg.astype(BF)
            uu_ref[:, cols] = uu.astype(BF)
            h_sc[:, cols] = (gg * jax.nn.sigmoid(gg) * uu).astype(BF)
        y_ref[...] = xv + 0.5 * _dot(h_sc[...], wd_ref[...])

    tok = lambda w: pl.BlockSpec((tm, w), lambda i: (i, 0))
    return pl.pallas_call(
        body, name="ffn_fwd", grid=(T // tm,),
        in_specs=[tok(D), _resident(g), _resident(wg), _resident(wu), _resident(wd)],
        out_specs=[tok(D), tok(F), tok(F)],
        out_shape=[jax.ShapeDtypeStruct((T, D), F32), jax.ShapeDtypeStruct((T, F), BF),
                   jax.ShapeDtypeStruct((T, F), BF)],
        scratch_shapes=[pltpu.VMEM((tm, F), BF)],
        compiler_params=_params("arbitrary"),
    )(x, g, wg, wu, wd)


def ffn_bwd(x, dy, g, gg, uu, wd_t, wg_t, wu_t):
    T, D = x.shape
    F = gg.shape[1]
    tm, tf = _tile(T, TM_FFN_BWD, 8), _tile(F, TF_FFN, 128)
    nf = F // tf

    def body(x_ref, dy_ref, g_ref, gg_ref, uu_ref, wdt_ref, wgt_ref, wut_ref,
             dx_ref, n_ref, dyh_ref, h_ref, dg_ref, du_ref, dgn_ref):
        xh, r = _rms(x_ref[...])
        n_ref[...] = (xh * g_ref[...]).astype(BF)
        dyv = dy_ref[...]
        dyh = (0.5 * dyv).astype(BF)
        dyh_ref[...] = dyh

        def hidden_grad(c):
            return _dot(dyh, wdt_ref[:, c * tf:(c + 1) * tf])

        ahead = hidden_grad(0)
        for c in range(nf):
            dh = ahead
            if c + 1 < nf:
                ahead = hidden_grad(c + 1)
            cols = slice(c * tf, (c + 1) * tf)
            gv = gg_ref[:, cols].astype(F32)
            uv = uu_ref[:, cols].astype(F32)
            sg = jax.nn.sigmoid(gv)
            silu = gv * sg
            h_ref[:, cols] = (silu * uv).astype(BF)
            du_ref[:, cols] = (dh * silu).astype(BF)
            dg_ref[:, cols] = (dh * uv * (sg * (1.0 + gv * (1.0 - sg)))).astype(BF)
        dn = _dot(dg_ref[...], wgt_ref[...]) + _dot(du_ref[...], wut_ref[...])
        dx_ref[...] = dyv + _rms_bwd(dn * g_ref[...], xh, r)
        _accumulate(dgn_ref, pl.program_id(0) == 0, _colsum(dn * xh))

    tok = lambda w: pl.BlockSpec((tm, w), lambda i: (i, 0))
    return pl.pallas_call(
        body, name="ffn_bwd", grid=(T // tm,),
        in_specs=[tok(D), tok(D), _resident(g), tok(F), tok(F), _resident(wd_t), _resident(wg_t), _resident(wu_t)],
        out_specs=[tok(D), tok(D), tok(D), tok(F), tok(F), tok(F), pl.BlockSpec((1, D), lambda i: (0, 0))],
        out_shape=[jax.ShapeDtypeStruct((T, D), F32), jax.ShapeDtypeStruct((T, D), BF),
                   jax.ShapeDtypeStruct((T, D), BF), jax.ShapeDtypeStruct((T, F), BF),
                   jax.ShapeDtypeStruct((T, F), BF), jax.ShapeDtypeStruct((T, F), BF),
                   jax.ShapeDtypeStruct((1, D), F32)],
        compiler_params=_params("arbitrary"),
    )(x, dy, g, gg, uu, wd_t, wg_t, wu_t)


def mm_tn(a, b):
    T, M = a.shape
    N = b.shape[1]
    tk, bm, bn = _tile(T, TK_TN, 16), _tile(M, BM_TN, 128), _tile(N, BN_TN, 128)

    def body(a_ref, b_ref, o_ref):
        part = _dot_tn(a_ref[...].astype(BF), b_ref[...].astype(BF))
        _accumulate(o_ref, pl.program_id(2) == 0, part)

    return pl.pallas_call(
        body, name="mm_tn", grid=(M // bm, N // bn, T // tk),
        in_specs=[pl.BlockSpec((tk, bm), lambda i, j, k: (k, i)), pl.BlockSpec((tk, bn), lambda i, j, k: (k, j))],
        out_specs=pl.BlockSpec((bm, bn), lambda i, j, k: (i, j)),
        out_shape=jax.ShapeDtypeStruct((M, N), F32),
        compiler_params=_params("arbitrary", "arbitrary", "arbitrary"),
    )(a, b)


def mm_multi(pairs, res=None, out_dtype=F32):
    T = pairs[0][0].shape[0]
    N = pairs[0][1].shape[1]
    tm = _tile(T, TM_MM, 16)
    n = len(pairs)

    def body(*refs):
        o_ref = refs[-1]
        acc = refs[2 * n][...] if res is not None else None
        for k in range(n):
            part = _dot(refs[k][...].astype(BF), refs[n + k][...])
            acc = part if acc is None else acc + part
        o_ref[...] = acc.astype(out_dtype)

    ins = [a for a, _ in pairs] + [w for _, w in pairs]
    specs = [pl.BlockSpec((tm, a.shape[1]), lambda i: (i, 0)) for a, _ in pairs]
    specs += [pl.BlockSpec(w.shape, lambda i: (0, 0)) for _, w in pairs]
    if res is not None:
        ins.append(res)
        specs.append(pl.BlockSpec((tm, N), lambda i: (i, 0)))
    return pl.pallas_call(
        body, name="mm_multi", grid=(T // tm,), in_specs=specs,
        out_specs=pl.BlockSpec((tm, N), lambda i: (i, 0)),
        out_shape=jax.ShapeDtypeStruct((T, N), out_dtype),
        compiler_params=_params("arbitrary"),
    )(*ins)


def _causal_mask(t, q_major):
    r = lax.broadcasted_iota(jnp.int32, (t, t), 0)
    c = lax.broadcasted_iota(jnp.int32, (t, t), 1)
    return (c <= r) if q_major else (r <= c)


def _blocks_transposed(a, t):
    T = a.shape[0]
    return jnp.transpose(a.reshape(T // t, t, HEADS, HEAD_SLOT), (2, 0, 3, 1))


def _blocks_untransposed(a):
    H, n, d, t = a.shape
    return jnp.transpose(a, (1, 3, 0, 2)).reshape(n * t, H * d)


def attn_fwd(q, k, v_t, gather=None):
    T = q.shape[0]
    t = _tile(T, TQ_ATTN, 128)
    nq = T // t

    def body(q_ref, k_ref, vt_ref, *rest):
        if gather is None:
            ot_ref, lse_ref = rest
        else:
            w_ref, ot_ref, lse_ref, all_ref, send_sems, recv_sems, local_sem = rest
            step = pl.program_id(0) * nq + pl.program_id(1)
            moving = lambda: _ShardGather(w_ref, all_ref, send_sems, recv_sems, local_sem)
            pl.when(step == 0)(lambda: moving().start())
            pl.when(step == (HEADS // 2) * nq)(lambda: moving().forward())
        i = pl.program_id(1)
        qv = q_ref[...]

        def update(st, j, m, l, acc):
            m2 = jnp.maximum(m, jnp.max(st, axis=0, keepdims=True))
            pt = jnp.exp2(st - m2)
            scale = jnp.exp2(m - m2)
            return (m2, scale * l + jnp.sum(pt, axis=0, keepdims=True),
                    scale * acc + _dot(vt_ref[j], pt.astype(BF)))

        def scores(j, masked):
            st = _dot_nt(k_ref[pl.ds(pl.multiple_of(j * t, t), t), :], qv)
            return jnp.where(_causal_mask(t, False), st, -jnp.inf) if masked else st

        def pair(j, carry, last_is_diagonal):
            s0, s1 = scores(j, False), scores(j + 1, last_is_diagonal)
            return update(s1, j + 1, *update(s0, j, *carry))

        init = (jnp.full((1, t), -1e30, F32), jnp.zeros((1, t), F32), jnp.zeros((HEAD_SLOT, t), F32))
        carry = lax.fori_loop(0, i // 2, lambda jj, c: pair(2 * jj, c, False), init)
        m, l, acc = lax.cond(i % 2 == 1, lambda c: pair(i - 1, c, True),
                             lambda c: update(scores(i, True), i, *c), carry)
        ot_ref[...] = (acc / l).astype(BF)
        lse_ref[...] = m + jnp.log2(l)
        if gather is not None:
            pl.when(step == HEADS * nq - 1)(lambda: moving().finish())

    operands = [q, k, v_t]
    in_specs = [pl.BlockSpec((t, HEAD_SLOT), lambda h, i: (i, h)), pl.BlockSpec((T, HEAD_SLOT), lambda h, i: (0, h)),
                pl.BlockSpec((None, nq, HEAD_SLOT, t), lambda h, i: (h, 0, 0, 0))]
    out_specs = [pl.BlockSpec((None, None, HEAD_SLOT, t), lambda h, i: (h, i, 0, 0)),
                 pl.BlockSpec((None, None, 1, t), lambda h, i: (h, i, 0, 0))]
    out_shape = [jax.ShapeDtypeStruct((HEADS, nq, HEAD_SLOT, t), BF), jax.ShapeDtypeStruct((HEADS, nq, 1, t), F32)]
    scratch = []
    if gather is not None:
        operands.append(gather)
        in_specs.append(pl.BlockSpec(memory_space=pl.ANY))
        out_specs.append(pl.BlockSpec(memory_space=pl.ANY))
        out_shape.append(jax.ShapeDtypeStruct((4, *gather.shape), gather.dtype))
        scratch = [pltpu.SemaphoreType.DMA((6,)), pltpu.SemaphoreType.DMA((6,)), pltpu.SemaphoreType.DMA]
    return pl.pallas_call(
        body, name="attn_fwd" if gather is None else "attn_fwd_gather", grid=(HEADS, nq),
        in_specs=in_specs, out_specs=out_specs, out_shape=out_shape, scratch_shapes=scratch,
        compiler_params=_params("arbitrary", "arbitrary"),
    )(*operands)


def attn_delta(o, do):
    T = o.shape[0]
    t = _tile(T, TQ_ATTN, 128)

    def body(o_ref, do_ref, delta_ref):
        prod = do_ref[...].astype(F32) * o_ref[...].astype(F32)
        delta_ref[...] = jnp.sum(prod.T, axis=0, keepdims=True)

    blk = pl.BlockSpec((t, HEAD_SLOT), lambda h, i: (i, h))
    return pl.pallas_call(
        body, name="attn_delta", grid=(HEADS, T // t), in_specs=[blk, blk],
        out_specs=pl.BlockSpec((None, None, 1, t), lambda h, i: (h, i, 0, 0)),
        out_shape=jax.ShapeDtypeStruct((HEADS, T // t, 1, t), F32),
        compiler_params=_params("arbitrary", "arbitrary"),
    )(o, do)


def attn_bwd(q, k, k_t, v, do, lse_rows, delta_rows, exchange=None):
    T = q.shape[0]
    t = _tile(T, TQ_ATTN, 128)
    nq = T // t

    def body(q_ref, k_ref, kt_ref, v_ref, do_ref, lse_ref, delta_ref, *rest):
        if exchange is None:
            dk_ref, dv_ref, dqt_ref = rest
        else:
            g_ref, dk_ref, dv_ref, dqt_ref, land_ref, send_sems, recv_sems, local_sem = rest
            first = (pl.program_id(0) == 0) & (pl.program_id(1) == 0)
            last = (pl.program_id(0) == HEADS - 1) & (pl.program_id(1) == nq - 1)
            pl.when(first)(lambda: _HalvesExchange(g_ref, land_ref, send_sems, recv_sems, local_sem).start())
        j = pl.program_id(1)
        kv, vv, ktv = k_ref[...], v_ref[...], kt_ref[...]

        @pl.when(j == 0)
        def _():
            dqt_ref[...] = jnp.zeros_like(dqt_ref)

        def block(i):
            return pl.ds(pl.multiple_of(i * t, t), t)

        def scores(i, masked):
            st = _dot_nt(kv, q_ref[block(i), :])
            return jnp.where(_causal_mask(t, False), st, -jnp.inf) if masked else st

        def add(carry, st, i):
            dk, dv = carry
            qv, dov = q_ref[block(i), :], do_ref[block(i), :]
            pt = jnp.exp2(st - lse_ref[pl.ds(i, 1), :])
            dst = (pt * (_dot_nt(vv, dov) - delta_ref[pl.ds(i, 1), :])).astype(BF)
            dqt_ref[i] += _dot(ktv, dst)
            return dk + _dot(dst, qv), dv + _dot(pt.astype(BF), dov)

        def pair(i, carry):
            s0, s1 = scores(i, False), scores(i + 1, False)
            return add(add(carry, s0, i), s1, i + 1)

        zero = jnp.zeros((t, HEAD_SLOT), F32)
        carry = add((zero, zero), scores(j, True), j)
        rest = nq - 1 - j
        carry = lax.fori_loop(0, rest // 2, lambda ii, c: pair(j + 1 + 2 * ii, c), carry)
        dk, dv = lax.cond(rest % 2 == 1, lambda c: add(c, scores(nq - 1, False), nq - 1), lambda c: c, carry)
        dk_ref[...] = dk * (1.0 / LOG2_E)
        dv_ref[...] = dv
        if exchange is not None:
            pl.when(last)(lambda: _HalvesExchange(g_ref, land_ref, send_sems, recv_sems, local_sem).wait())

    blk = pl.BlockSpec((t, HEAD_SLOT), lambda h, j: (j, h))
    full = pl.BlockSpec((T, HEAD_SLOT), lambda h, j: (0, h))
    rows = pl.BlockSpec((None, nq, t), lambda h, j: (h, 0, 0))
    operands = [q, k, k_t, v, do, lse_rows, delta_rows]
    in_specs = [full, blk, pl.BlockSpec((HEAD_SLOT, t), lambda h, j: (h, j)), blk, full, rows, rows]
    out_specs = [blk, blk, pl.BlockSpec((None, nq, HEAD_SLOT, t), lambda h, j: (h, 0, 0, 0))]
    out_shape = [jax.ShapeDtypeStruct((T, HEADS * HEAD_SLOT), F32)] * 2 + [
        jax.ShapeDtypeStruct((HEADS, nq, HEAD_SLOT, t), F32)]
    scratch = []
    if exchange is not None:
        operands.append(exchange)
        in_specs.append(pl.BlockSpec(memory_space=pl.ANY))
        out_specs.append(pl.BlockSpec(memory_space=pl.ANY))
        out_shape.append(jax.ShapeDtypeStruct((8, exchange.shape[1] // 2, exchange.shape[2]), exchange.dtype))
        scratch = [pltpu.SemaphoreType.DMA((7,)), pltpu.SemaphoreType.DMA((7,)), pltpu.SemaphoreType.DMA]
    return pl.pallas_call(
        body, name="attn_bwd" if exchange is None else "attn_bwd_exchange", grid=(HEADS, nq),
        in_specs=in_specs, out_specs=out_specs, out_shape=out_shape, scratch_shapes=scratch,
        compiler_params=_params("arbitrary", "arbitrary"),
    )(*operands)


def _prev_halo(tm):
    return lambda i: (jnp.maximum(i * (tm // HALO) - 1, 0), 0)


def _next_halo(tm, T):
    return lambda i: (jnp.minimum((i + 1) * (tm // HALO), T // HALO - 1), 0)


def _inv_count(row0, n, w):
    t = row0 + lax.broadcasted_iota(jnp.int32, (n, 1), 0)
    return 1.0 / jnp.minimum(t + 1, w).astype(F32)


def _pool_fwd(u_prev, u, row0):
    tm = u.shape[0]
    out = []
    for g, w in enumerate(POOL_WINDOWS):
        lanes = slice(g * POOL_GROUP, (g + 1) * POOL_GROUP)
        ue = jnp.concatenate([u_prev[:, lanes], u[:, lanes]], axis=0)
        s, step = ue, 1
        while step < w:
            s = s + pltpu.roll(s, step, 0)
            step *= 2
        out.append(s[HALO:, :] * _inv_count(row0, tm, w) - u[:, lanes])
    return out


def _pool_bwd(dp, dp_next, row0):
    tm = dp[0].shape[0]
    out = []
    for g, w in enumerate(POOL_WINDOWS):
        e = jnp.concatenate([dp[g] * _inv_count(row0, tm, w), dp_next[g] * (1.0 / w)], axis=0)
        n = tm + HALO
        s, step = e, 1
        while step < w:
            s = s + pltpu.roll(s, n - step, 0)
            step *= 2
        out.append(s[:tm, :] - dp[g])
    return out


def _mixa_front(x, xp, first, row0, g_ref, win_ref, qan_ref, wq_ref, kvan_ref, wkn_ref):
    xh, r = _rms(x)
    hn = (xh * g_ref[...]).astype(BF)
    z = _dot(hn, win_ref[...])
    xph, _ = _rms(xp)
    u_prev = _dot((xph * g_ref[...]).astype(BF), win_ref[:, :POOL_DIM]) * jnp.where(first, 0.0, 1.0)
    u = z[:, :POOL_DIM]
    pooled = _pool_fwd(u_prev, u, row0)
    c1, c2 = POOL_DIM + Q_RANK, POOL_DIM + Q_RANK + KV_RANK
    qh, rq = _rms(z[:, POOL_DIM:c1])
    nq = (qh * qan_ref[...]).astype(BF)
    kh, rk = _rms(z[:, c1:c2])
    nkv = (kh * kvan_ref[...]).astype(BF)
    qraw = _dot(nq, wq_ref[...])
    kraw = _dot(nkv, wkn_ref[...])
    krope = z[:, c2:c2 + HEAD_SLOT]
    return dict(xh=xh, r=r, hn=hn, pooled=pooled, qh=qh, rq=rq, nq=nq, kh=kh, rk=rk, nkv=nkv,
                qraw=qraw, kraw=kraw, krope=krope)


def mixa_pre_fwd(x, g, win, qan, wq, kvan, wkn, wv, qhn, khn, wpool, pscale, cos, sin):
    T, D = x.shape
    tm = _tile(T, TM_MIX_FWD, HALO)
    HS = HEADS * HEAD_SLOT

    def body(x_ref, xp_ref, g_ref, win_ref, qan_ref, wq_ref, kvan_ref, wkn_ref, wv_ref, qhn_ref, khn_ref,
             wpool_ref, pscale_ref, cos_ref, sin_ref, q_ref, k_ref, v_ref, po_ref):
        i = pl.program_id(0)
        a = _mixa_front(x_ref[...], xp_ref[...], i == 0, i * tm, g_ref, win_ref, qan_ref, wq_ref, kvan_ref, wkn_ref)
        for gi in range(len(POOL_WINDOWS)):
            lanes = slice(gi * POOL_GROUP, (gi + 1) * POOL_GROUP)
            po = _dot(a["pooled"][gi].astype(BF), wpool_ref[gi]) * pscale_ref[:, lanes]
            po_ref[:, lanes] = po.astype(BF)
        cosv, sinv = cos_ref[...], sin_ref[...]
        v_ref[...] = _dot(a["nkv"], wv_ref[...]).astype(BF)
        for h in range(HEADS):
            lanes = slice(h * HEAD_SLOT, (h + 1) * HEAD_SLOT)
            qn, _ = _rms(a["qraw"][:, lanes], QK_DIM)
            q_ref[:, lanes] = (_rope(qn * qhn_ref[...], cosv, sinv) * (ATTN_SCALE * LOG2_E)).astype(BF)
            kn, _ = _rms(a["kraw"][:, lanes] + a["krope"], QK_DIM)
            k_ref[:, lanes] = _rope(kn * khn_ref[...], cosv, sinv).astype(BF)

    tok = lambda w: pl.BlockSpec((tm, w), lambda i: (i, 0))
    whole = lambda arr: pl.BlockSpec(arr.shape, lambda i: (0,) * arr.ndim)
    return pl.pallas_call(
        body, name="mixa_pre_fwd", grid=(T // tm,),
        in_specs=[tok(D), pl.BlockSpec((HALO, D), _prev_halo(tm))] + [whole(a) for a in
                  (g, win, qan, wq, kvan, wkn, wv, qhn, khn, wpool, pscale)] + [tok(HEAD_SLOT), tok(HEAD_SLOT)],
        out_specs=[tok(HS), tok(HS), tok(HS), tok(POOL_DIM)],
        out_shape=[jax.ShapeDtypeStruct((T, HS), BF)] * 3 + [jax.ShapeDtypeStruct((T, POOL_DIM), BF)],
        compiler_params=_params("arbitrary"),
    )(x, x, g, win, qan, wq, kvan, wkn, wv, qhn, khn, wpool, pscale, cos, sin)


def mixa_pre_bwd(x, dy, dq, dk, dv, dpo, g, win, win_t, qan, wq, wq_t, kvan, wkn, wkn_t, wv_t, qhn, khn,
                 wpool, wpool_t, pscale, cos, sin):
    T, D = x.shape
    tm = _tile(T, TM_MIX_BWD, HALO)
    HS = HEADS * HEAD_SLOT
    ZW = win.shape[1]
    nt = T // tm

    def body(x_ref, xp_ref, dy_ref, dq_ref, dk_ref, dv_ref, dpo_ref, dpon_ref, g_ref, win_ref, wint_ref, qan_ref,
             wq_ref, wqt_ref, kvan_ref, wkn_ref, wknt_ref, wvt_ref, qhn_ref, khn_ref, wpool_ref, wpoolt_ref,
             pscale_ref, cos_ref, sin_ref,
             dx_ref, hn_ref, dz_ref, nq_ref, dqraw_ref, nkv_ref, dkraw_ref, pooled_ref, dps_ref,
             dg_ref, dqan_ref, dkvan_ref, dqhn_ref, dkhn_ref, dpscale_ref):
        i = pl.program_id(0)
        first = i == 0
        a = _mixa_front(x_ref[...], xp_ref[...], first, i * tm, g_ref, win_ref, qan_ref, wq_ref, kvan_ref, wkn_ref)
        cosv, sinv = cos_ref[...], sin_ref[...]
        hn_ref[...] = a["hn"]
        nq_ref[...] = a["nq"]
        nkv_ref[...] = a["nkv"]

        has_next = jnp.where(i == nt - 1, 0.0, 1.0)
        dpool, dpool_next, dpscale = [], [], []
        for gi in range(len(POOL_WINDOWS)):
            lanes = slice(gi * POOL_GROUP, (gi + 1) * POOL_GROUP)
            pooled = a["pooled"][gi].astype(BF)
            pooled_ref[:, lanes] = pooled
            dpo_g = dpo_ref[:, lanes]
            dpscale.append(_colsum(dpo_g * _dot(pooled, wpool_ref[gi])))
            dps = (dpo_g * pscale_ref[:, lanes]).astype(BF)
            dps_ref[:, lanes] = dps
            dpool.append(_dot(dps, wpoolt_ref[gi]))
            dps_n = (dpon_ref[:, lanes] * pscale_ref[:, lanes] * has_next).astype(BF)
            dpool_next.append(_dot(dps_n, wpoolt_ref[gi]))
        du = jnp.concatenate(_pool_bwd(dpool, dpool_next, i * tm), axis=1)
        _accumulate(dpscale_ref, first, jnp.concatenate(dpscale, axis=1))

        dqhn = jnp.zeros((1, HEAD_SLOT), F32)
        dkhn = jnp.zeros((1, HEAD_SLOT), F32)
        dkrope = jnp.zeros((tm, HEAD_SLOT), F32)
        for h in range(HEADS):
            lanes = slice(h * HEAD_SLOT, (h + 1) * HEAD_SLOT)
            qhat, rq = _rms(a["qraw"][:, lanes], QK_DIM)
            dqn = _rope_bwd(dq_ref[:, lanes] * ATTN_SCALE, cosv, sinv)
            dqhn = dqhn + _colsum(dqn * qhat)
            dqraw_ref[:, lanes] = _rms_bwd(dqn * qhn_ref[...], qhat, rq, QK_DIM).astype(BF)
            khat, rk = _rms(a["kraw"][:, lanes] + a["krope"], QK_DIM)
            dkn = _rope_bwd(dk_ref[:, lanes], cosv, sinv)
            dkhn = dkhn + _colsum(dkn * khat)
            dkraw = _rms_bwd(dkn * khn_ref[...], khat, rk, QK_DIM)
            dkrope = dkrope + dkraw
            dkraw_ref[:, lanes] = dkraw.astype(BF)
        _accumulate(dqhn_ref, first, dqhn)
        _accumulate(dkhn_ref, first, dkhn)

        dnq = _dot(dqraw_ref[...], wqt_ref[...])
        _accumulate(dqan_ref, first, _colsum(dnq * a["qh"]))
        dql = _rms_bwd(dnq * qan_ref[...], a["qh"], a["rq"])
        dnkv = _dot(dkraw_ref[...], wknt_ref[...]) + _dot(dv_ref[...].astype(BF), wvt_ref[...])
        _accumulate(dkvan_ref, first, _colsum(dnkv * a["kh"]))
        dkvl = _rms_bwd(dnkv * kvan_ref[...], a["kh"], a["rk"])

        dz = jnp.concatenate([du, dql, dkvl, dkrope], axis=1).astype(BF)
        dz_ref[...] = dz
        dhn = _dot(dz, wint_ref[...])
        _accumulate(dg_ref, first, _colsum(dhn * a["xh"]))
        dx_ref[...] = dy_ref[...] + _rms_bwd(dhn * g_ref[...], a["xh"], a["r"])

    tok = lambda w: pl.BlockSpec((tm, w), lambda i: (i, 0))
    whole = lambda arr: pl.BlockSpec(arr.shape, lambda i: (0,) * arr.ndim)
    row = lambda w: pl.BlockSpec((1, w), lambda i: (0, 0))
    weights = (g, win, win_t, qan, wq, wq_t, kvan, wkn, wkn_t, wv_t, qhn, khn, wpool, wpool_t, pscale)
    return pl.pallas_call(
        body, name="mixa_pre_bwd", grid=(nt,),
        in_specs=[tok(D), pl.BlockSpec((HALO, D), _prev_halo(tm)), tok(D), tok(HS), tok(HS), tok(HS), tok(POOL_DIM),
                  pl.BlockSpec((HALO, POOL_DIM), _next_halo(tm, T))] + [whole(a) for a in weights]
                 + [tok(HEAD_SLOT), tok(HEAD_SLOT)],
        out_specs=[tok(D), tok(D), tok(ZW), tok(Q_RANK), tok(HS), tok(KV_RANK), tok(HS), tok(POOL_DIM), tok(POOL_DIM),
                   row(D), row(Q_RANK), row(KV_RANK), row(HEAD_SLOT), row(HEAD_SLOT), row(POOL_DIM)],
        out_shape=[jax.ShapeDtypeStruct((T, D), F32), jax.ShapeDtypeStruct((T, D), BF),
                   jax.ShapeDtypeStruct((T, ZW), BF), jax.ShapeDtypeStruct((T, Q_RANK), BF),
                   jax.ShapeDtypeStruct((T, HS), BF), jax.ShapeDtypeStruct((T, KV_RANK), BF),
                   jax.ShapeDtypeStruct((T, HS), BF), jax.ShapeDtypeStruct((T, POOL_DIM), BF),
                   jax.ShapeDtypeStruct((T, POOL_DIM), BF),
                   jax.ShapeDtypeStruct((1, D), F32), jax.ShapeDtypeStruct((1, Q_RANK), F32),
                   jax.ShapeDtypeStruct((1, KV_RANK), F32), jax.ShapeDtypeStruct((1, HEAD_SLOT), F32),
                   jax.ShapeDtypeStruct((1, HEAD_SLOT), F32), jax.ShapeDtypeStruct((1, POOL_DIM), F32)],
        compiler_params=_params("arbitrary"),
    )(x, x, dy, dq, dk, dv, dpo, dpo, *weights, cos, sin)


def _conv_taps(u_prev, u, cw_ref):
    ue = jnp.concatenate([u_prev, u], axis=0)
    u1 = pltpu.roll(ue, 1, 0)[HALO:, :]
    u2 = pltpu.roll(ue, 2, 0)[HALO:, :]
    return cw_ref[0:1, :] * u2 + cw_ref[1:2, :] * u1 + cw_ref[2:3, :] * u, u1, u2


def mixc_fwd(x, g, win, cw, wout):
    T, D = x.shape
    tm = _tile(T, TM_CONV_FWD, HALO)

    def body(x_ref, xp_ref, g_ref, win_ref, cw_ref, wout_ref, y_ref, z_ref):
        i = pl.program_id(0)
        xv = x_ref[...]
        xh, _ = _rms(xv)
        z = _dot((xh * g_ref[...]).astype(BF), win_ref[...])
        z_ref[...] = z.astype(BF)
        xph, _ = _rms(xp_ref[...])
        zp = _dot((xph * g_ref[...]).astype(BF), win_ref[:, D:])
        u_prev = zp[:, :D] * zp[:, D:] * jnp.where(i == 0, 0.0, 1.0)
        conv, _, _ = _conv_taps(u_prev, z[:, D:2 * D] * z[:, 2 * D:], cw_ref)
        y_ref[...] = xv + _dot((z[:, :D] * conv).astype(BF), wout_ref[...])

    tok = lambda w: pl.BlockSpec((tm, w), lambda i: (i, 0))
    whole = lambda arr: pl.BlockSpec(arr.shape, lambda i: (0,) * arr.ndim)
    return pl.pallas_call(
        body, name="mixc_fwd", grid=(T // tm,),
        in_specs=[tok(D), pl.BlockSpec((HALO, D), _prev_halo(tm)), whole(g), whole(win), whole(cw), whole(wout)],
        out_specs=[tok(D), tok(3 * D)],
        out_shape=[jax.ShapeDtypeStruct((T, D), F32), jax.ShapeDtypeStruct((T, 3 * D), BF)],
        compiler_params=_params("arbitrary"),
    )(x, x, g, win, cw, wout)


def mixc_bwd(x, dy, z, g, win_t, cw, wout_t):
    T, D = x.shape
    tm = _tile(T, TM_CONV_BWD, HALO)
    nt = T // tm

    def body(x_ref, dy_ref, dyn_ref, z_ref, zp_ref, zn_ref, g_ref, wint_ref, cw_ref, woutt_ref,
             dx_ref, hn_ref, dz_ref, v_ref, dcw_ref, dg_ref):
        i = pl.program_id(0)
        first = i == 0
        xh, r = _rms(x_ref[...])
        hn_ref[...] = (xh * g_ref[...]).astype(BF)
        zv = z_ref[...].astype(F32)
        gb, gc, hh = zv[:, :D], zv[:, D:2 * D], zv[:, 2 * D:]
        u = gc * hh
        zp = zp_ref[...].astype(F32)
        u_prev = zp[:, D:2 * D] * zp[:, 2 * D:] * jnp.where(first, 0.0, 1.0)
        conv, u1, u2 = _conv_taps(u_prev, u, cw_ref)
        v_ref[...] = (gb * conv).astype(BF)

        dv = _dot(dy_ref[...].astype(BF), woutt_ref[...])
        dconv = dv * gb
        dv_next = _dot(dyn_ref[...].astype(BF), woutt_ref[...])
        dconv_next = dv_next * zn_ref[:, :D].astype(F32) * jnp.where(i == nt - 1, 0.0, 1.0)
        de = jnp.concatenate([dconv, dconv_next], axis=0)
        n = tm + HALO
        du = (cw_ref[2:3, :] * dconv + cw_ref[1:2, :] * pltpu.roll(de, n - 1, 0)[:tm, :]
              + cw_ref[0:1, :] * pltpu.roll(de, n - 2, 0)[:tm, :])
        for tap, shifted in enumerate((u2, u1, u)):
            _accumulate(dcw_ref.at[tap:tap + 1, :], first, _colsum(dconv * shifted))
        dz = jnp.concatenate([dv * conv, du * hh, du * gc], axis=1).astype(BF)
        dz_ref[...] = dz
        dhn = _dot(dz, wint_ref[...])
        _accumulate(dg_ref, first, _colsum(dhn * xh))
        dx_ref[...] = dy_ref[...] + _rms_bwd(dhn * g_ref[...], xh, r)

    tok = lambda w: pl.BlockSpec((tm, w), lambda i: (i, 0))
    whole = lambda arr: pl.BlockSpec(arr.shape, lambda i: (0,) * arr.ndim)
    return pl.pallas_call(
        body, name="mixc_bwd", grid=(nt,),
        in_specs=[tok(D), tok(D), pl.BlockSpec((HALO, D), _next_halo(tm, T)), tok(3 * D),
                  pl.BlockSpec((HALO, 3 * D), _prev_halo(tm)), pl.BlockSpec((HALO, 3 * D), _next_halo(tm, T)),
                  whole(g), whole(win_t), whole(cw), whole(wout_t)],
        out_specs=[tok(D), tok(D), tok(3 * D), tok(D), pl.BlockSpec((3, D), lambda i: (0, 0)),
                   pl.BlockSpec((1, D), lambda i: (0, 0))],
        out_shape=[jax.ShapeDtypeStruct((T, D), F32), jax.ShapeDtypeStruct((T, D), BF),
                   jax.ShapeDtypeStruct((T, 3 * D), BF), jax.ShapeDtypeStruct((T, D), BF),
                   jax.ShapeDtypeStruct((3, D), F32), jax.ShapeDtypeStruct((1, D), F32)],
        compiler_params=_params("arbitrary"),
    )(x, dy, dy, z, z, z, g, win_t, cw, wout_t)


def loss_head(y, target):
    T, D = y.shape
    tm = _tile(T, TM_MM, 8)

    def body(y_ref, t_ref, sum_ref, dy_ref):
        err = y_ref[...] - t_ref[...]
        dy_ref[...] = err * (1.0 / D)
        part = jnp.sum(jnp.sum(err * err, axis=-1, keepdims=True) * (1.0 / D), axis=0, keepdims=True)
        _accumulate(sum_ref, pl.program_id(0) == 0, jnp.broadcast_to(part, sum_ref.shape))

    return pl.pallas_call(
        body, name="loss_head", grid=(T // tm,),
        in_specs=[pl.BlockSpec((tm, D), lambda i: (i, 0))] * 2,
        out_specs=[pl.BlockSpec((8, 128), lambda i: (0, 0)), pl.BlockSpec((tm, D), lambda i: (i, 0))],
        out_shape=[jax.ShapeDtypeStruct((8, 128), F32), jax.ShapeDtypeStruct((T, D), F32)],
        compiler_params=_params("arbitrary"),
    )(y, target)


def adamw(w, g, m, v):
    R, C = w.shape
    tr = _tile(R, TR_FLAT, 16)

    def body(w_ref, g_ref, m_ref, v_ref, g32_ref, d_ref, m2_ref, v2_ref):
        gv = g_ref[...].astype(F32)
        g32_ref[...] = gv
        m2 = ADAM_B1 * m_ref[...] + (1.0 - ADAM_B1) * gv
        v2 = ADAM_B2 * v_ref[...] + (1.0 - ADAM_B2) * (gv * gv)
        m2_ref[...] = m2
        v2_ref[...] = v2
        m_hat = m2 / (1.0 - ADAM_B1 ** ADAM_STEP)
        v_hat = v2 / (1.0 - ADAM_B2 ** ADAM_STEP)
        d_ref[...] = -ADAM_LR * (m_hat / (jnp.sqrt(v_hat) + ADAM_EPS) + ADAM_WD * w_ref[...])

    spec = pl.BlockSpec((tr, C), lambda i: (i, 0))
    return pl.pallas_call(
        body, name="adamw", grid=(R // tr,), in_specs=[spec] * 4, out_specs=[spec] * 4,
        out_shape=[jax.ShapeDtypeStruct((R, C), F32)] * 4,
        compiler_params=_params("arbitrary"),
    )(w, g, m, v)


def sum_slots(a, out_dtype):
    S, R, C = a.shape
    tr = _tile(R, TR_FLAT // 2, 16)

    def body(a_ref, o_ref):
        acc = a_ref[0].astype(F32)
        for s in range(1, S):
            acc = acc + a_ref[s].astype(F32)
        o_ref[...] = acc.astype(out_dtype)

    return pl.pallas_call(
        body, name="sum_slots", grid=(R // tr,),
        in_specs=[pl.BlockSpec((S, tr, C), lambda i: (0, i, 0))],
        out_specs=pl.BlockSpec((tr, C), lambda i: (i, 0)),
        out_shape=jax.ShapeDtypeStruct((R, C), out_dtype),
        compiler_params=_params("arbitrary"),
    )(a)


ANY = pl.BlockSpec(memory_space=pl.ANY)


def _place():
    return lax.axis_index("x"), lax.axis_index("y"), lax.axis_index("c")


class _LocalCopy:
    def __init__(self, src, dst, sem, rows):
        n = LOCAL_CHUNKS if rows % (16 * LOCAL_CHUNKS) == 0 else 1
        cr = rows // n
        self.parts = [pltpu.make_async_copy(src.at[pl.ds(q * cr, cr), :], dst.at[pl.ds(q * cr, cr), :], sem)
                      for q in range(n)]
        self.whole = pltpu.make_async_copy(src, dst, sem)

    def start(self):
        for part in self.parts:
            part.start()

    def wait(self):
        self.whole.wait()


class _HalvesExchange:
    def __init__(self, g_ref, land_ref, send_sems, recv_sems, local_sem):
        half = g_ref.shape[1] // 2
        x, y, c = _place()
        me = 4 * x + 2 * y + c
        self.peers = []
        for mask in range(1, 8):
            mx, my, mc = (mask >> 2) & 1, (mask >> 1) & 1, mask & 1
            self.peers.append(((1 - x) if mx else x, (1 - y) if my else y, (1 - c) if mc else c))

        def piece(px, py, pc):
            return g_ref.at[2 * px + py, pl.ds(pc * half, half), :]

        def copy(k, sender, to):
            return pltpu.make_async_remote_copy(
                src_ref=piece(*to), dst_ref=land_ref.at[sender], send_sem=send_sems.at[k], recv_sem=recv_sems.at[k],
                device_id=to, device_id_type=MESH)

        self.own = _LocalCopy(piece(x, y, c), land_ref.at[me], local_sem, half)
        self.sends = [copy(k, me, peer) for k, peer in enumerate(self.peers)]
        self.arrivals = [copy(k, 4 * px + 2 * py + pc, (x, y, c)) for k, (px, py, pc) in enumerate(self.peers)]

    def start(self):
        self.own.start()
        for cp in self.sends:
            cp.start()

    def wait(self):
        for cp in self.arrivals:
            cp.wait_recv()
        for cp in self.sends:
            cp.wait_send()
        self.own.wait()


class _ShardGather:
    def __init__(self, w_ref, out_ref, send_sems, recv_sems, local_sem):
        R = w_ref.shape[0]
        half = R // 2
        x, y, c = _place()
        me, sibling = (x, y, c), (x, y, 1 - c)
        chips = [(1 - x, y), (x, 1 - y), (1 - x, 1 - y)]

        def rows(px, py, pc):
            return out_ref.at[2 * px + py, pl.ds(pc * half, half), :]

        def copy(k, block, to, src=None):
            return pltpu.make_async_remote_copy(
                src_ref=rows(*block) if src is None else src, dst_ref=rows(*block),
                send_sem=send_sems.at[k], recv_sem=recv_sems.at[k], device_id=to, device_id_type=MESH)

        self.mine = _LocalCopy(w_ref, out_ref.at[2 * x + y], local_sem, R)
        self.first = [copy(j, me, (*chip, c), src=w_ref.at[pl.ds(c * half, half), :]) for j, chip in enumerate(chips)]
        self.landed = [copy(j, (*chip, c), me) for j, chip in enumerate(chips)]
        self.passed = [copy(3 + j, (*chip, c), sibling) for j, chip in enumerate(chips)]
        self.from_sibling = [copy(3 + j, (*chip, 1 - c), me) for j, chip in enumerate(chips)]

    def start(self):
        self.mine.start()
        for cp in self.first:
            cp.start()

    def forward(self):
        for landed, onward in zip(self.landed, self.passed):
            landed.wait_recv()
            onward.start()

    def finish(self):
        for cp in self.from_sibling:
            cp.wait_recv()
        for cp in self.first + self.passed:
            cp.wait_send()
        self.mine.wait()


def allgather_shards(w):
    R, C = w.shape
    half = R // 2
    n = GATHER_CHUNKS if half % (16 * GATHER_CHUNKS) == 0 else 1
    cr = half // n

    def body(w_ref, out_ref, send_sems, recv_sems, local_sem):
        x, y, c = _place()
        sibling = (x, y, 1 - c)
        chips = [(1 - x, y), (x, 1 - y), (1 - x, 1 - y)]

        def rows(px, py, pc, q):
            return out_ref.at[2 * px + py, pl.ds(pc * half + q * cr, cr), :]

        def copy(k, block, q, to, src=None):
            return pltpu.make_async_remote_copy(
                src_ref=rows(*block, q) if src is None else src, dst_ref=rows(*block, q),
                send_sem=send_sems.at[k * n + q], recv_sem=recv_sems.at[k * n + q], device_id=to, device_id_type=MESH)

        mine = _LocalCopy(w_ref, out_ref.at[2 * x + y], local_sem, R)
        mine.start()
        first = [copy(j, (x, y, c), q, (*chip, c), src=w_ref.at[pl.ds(c * half + q * cr, cr), :])
                 for q in range(n) for j, chip in enumerate(chips)]
        for cp in first:
            cp.start()
        passed = []
        for q in range(n):
            for j, chip in enumerate(chips):
                copy(j, (*chip, c), q, (x, y, c)).wait_recv()
                passed.append(copy(3 + j, (*chip, c), q, sibling))
                passed[-1].start()
        for q in range(n):
            for j, chip in enumerate(chips):
                copy(3 + j, (*chip, 1 - c), q, (x, y, c)).wait_recv()
        for cp in first + passed:
            cp.wait_send()
        mine.wait()

    return pl.pallas_call(
        body, name="allgather_shards", in_specs=[ANY], out_specs=ANY,
        out_shape=jax.ShapeDtypeStruct((4, R, C), w.dtype),
        scratch_shapes=[pltpu.SemaphoreType.DMA((6 * n,)), pltpu.SemaphoreType.DMA((6 * n,)), pltpu.SemaphoreType.DMA],
    )(w)


def exchange_partials(grads, small):
    _, R, C = grads.shape
    half = R // 2
    Rs = small.shape[0]

    def body(g_ref, s_ref, land_ref, sland_ref, send_sems, recv_sems, local_sems):
        x, y, c = _place()
        me = 4 * x + 2 * y + c
        big = _HalvesExchange(g_ref, land_ref, send_sems, recv_sems, local_sems.at[0])

        def little(k, sender, to):
            return pltpu.make_async_remote_copy(
                src_ref=s_ref, dst_ref=sland_ref.at[sender], send_sem=send_sems.at[7 + k],
                recv_sem=recv_sems.at[7 + k], device_id=to, device_id_type=MESH)

        own_small = pltpu.make_async_copy(s_ref, sland_ref.at[me], local_sems.at[1])
        own_small.start()
        sends = [little(k, me, peer) for k, peer in enumerate(big.peers)]
        for cp in sends:
            cp.start()
        big.start()
        for k, (px, py, pc) in enumerate(big.peers):
            little(k, 4 * px + 2 * py + pc, (x, y, c)).wait_recv()
        big.wait()
        for cp in sends:
            cp.wait_send()
        own_small.wait()

    return pl.pallas_call(
        body, name="exchange_partials", in_specs=[ANY, ANY], out_specs=[ANY, ANY],
        out_shape=[jax.ShapeDtypeStruct((8, half, C), grads.dtype), jax.ShapeDtypeStruct((8, Rs, C), small.dtype)],
        scratch_shapes=[pltpu.SemaphoreType.DMA((14,)), pltpu.SemaphoreType.DMA((14,)), pltpu.SemaphoreType.DMA((2,))],
    )(grads, small)


def share_with_sibling(parts):
    C = parts[0].shape[1]
    halves = [p.shape[0] for p in parts]
    offsets = [2 * sum(halves[:i]) for i in range(len(parts))]

    def body(*refs):
        p_refs, out_ref = refs[:len(parts)], refs[len(parts)]
        send_sems, recv_sems, local_sems = refs[len(parts) + 1:]
        x, y, c = _place()
        pending = []
        for i, (p_ref, half, off) in enumerate(zip(p_refs, halves, offsets)):
            def rows(pc, half=half, off=off):
                return out_ref.at[pl.ds(off + pc * half, half), :]

            own = _LocalCopy(p_ref, rows(c), local_sems.at[i], half)
            own.start()
            n = SIBLING_CHUNKS if half % (16 * SIBLING_CHUNKS) == 0 else 1
            cr = half // n
            for q in range(n):
                pltpu.make_async_remote_copy(
                    src_ref=p_ref.at[pl.ds(q * cr, cr), :], dst_ref=out_ref.at[pl.ds(off + c * half + q * cr, cr), :],
                    send_sem=send_sems.at[i], recv_sem=recv_sems.at[i], device_id=(x, y, 1 - c),
                    device_id_type=MESH).start()
            pending.append((own, pltpu.make_async_remote_copy(
                src_ref=p_ref, dst_ref=rows(1 - c), send_sem=send_sems.at[i], recv_sem=recv_sems.at[i],
                device_id=(x, y, c), device_id_type=MESH)))
        for own, everything in pending:
            everything.wait_recv()
            everything.wait_send()
            own.wait()

    k = len(parts)
    return pl.pallas_call(
        body, name="share_with_sibling", in_specs=[ANY] * k, out_specs=ANY,
        out_shape=jax.ShapeDtypeStruct((2 * sum(halves), C), parts[0].dtype),
        scratch_shapes=[pltpu.SemaphoreType.DMA((k,)), pltpu.SemaphoreType.DMA((k,)), pltpu.SemaphoreType.DMA((k,))],
    )(*parts)


FLAT_SEG = 16 * FLAT_COLS


def _seg_rows(n):
    return -(-n // FLAT_SEG) * 16


def _flat_rows(sizes):
    rows = sum(_seg_rows(n) for n in sizes)
    return -(-rows // FLAT_ROW_ALIGN) * FLAT_ROW_ALIGN


def pack_flat(groups, lead=()):
    parts = []
    for arrays in groups:
        sizes = [int(np.prod(a.shape[len(lead):])) for a in arrays]
        used = 0
        for a, n in zip(arrays, sizes):
            rows = _seg_rows(n)
            flat = jnp.pad(a.reshape(*lead, n), [(0, 0)] * len(lead) + [(0, rows * FLAT_COLS - n)])
            parts.append(flat.reshape(*lead, rows, FLAT_COLS))
            used += rows
        if _flat_rows(sizes) > used:
            parts.append(jnp.zeros((*lead, _flat_rows(sizes) - used, FLAT_COLS), arrays[0].dtype))
    return jnp.concatenate(parts, axis=len(lead))


def unpack_flat(flat, group_shapes, lead=()):
    out, r0 = [], 0
    for shapes in group_shapes:
        arrays, start = [], r0
        for shp in shapes:
            n = int(np.prod(shp))
            rows = _seg_rows(n)
            seg = flat[..., r0:r0 + rows, :].reshape(*lead, rows * FLAT_COLS)[..., :n]
            arrays.append(seg.reshape(*lead, *shp))
            r0 += rows
        r0 = start + _flat_rows([int(np.prod(shp)) for shp in shapes])
        out.append(arrays)
    return out


def _f32_bits_as(a, dtype):
    return lax.bitcast_convert_type(a, dtype).reshape(*a.shape[:-1], -1)


def _f32_from_bits(a):
    k = 4 // a.dtype.itemsize
    if k > 1:
        a = a.reshape(*a.shape[:-1], a.shape[-1] // k, k)
    return lax.bitcast_convert_type(a, F32)


def _join_shards(name, a):
    if name in COL_SHARDED:
        return jnp.transpose(a, (1, 2, 0, 3)).reshape(a.shape[1], a.shape[2], 4 * a.shape[3])
    return jnp.transpose(a, (1, 0, 2, 3)).reshape(a.shape[1], 4 * a.shape[2], a.shape[3])


def _split_shards(name, a):
    L, K, N = a.shape
    if name in COL_SHARDED:
        return jnp.transpose(a.reshape(L, K, 4, N // 4), (2, 0, 1, 3))
    return jnp.transpose(a.reshape(L, 4, K // 4, N), (1, 0, 2, 3))


def _first_layers(name):
    return 0 if name.startswith('c_') else 1


def _pad_heads(a, width):
    a = a.reshape(*a.shape[:-1], HEADS, width)
    a = jnp.pad(a, [(0, 0)] * (a.ndim - 1) + [(0, HEAD_SLOT - width)])
    return a.reshape(*a.shape[:-2], HEADS * HEAD_SLOT)


def _unpad_heads(a, width):
    a = a.reshape(*a.shape[:-1], HEADS, HEAD_SLOT)[..., :width]
    return a.reshape(*a.shape[:-2], HEADS * width)


def _rope_tables(T):
    pos = jnp.arange(T, dtype=F32)
    inv_freq = ROPE_THETA ** (-jnp.arange(0, ROPE_DIM, 2, dtype=F32) / ROPE_DIM)
    ang = pos[:, None] * inv_freq[None, :]
    cos, sin = jnp.cos(ang), jnp.sin(ang)
    pad = HEAD_SLOT - QK_DIM
    cos_t = jnp.concatenate([jnp.ones((T, NOPE_DIM), F32), cos, cos, jnp.zeros((T, pad), F32)], axis=1)
    sin_t = jnp.concatenate([jnp.zeros((T, NOPE_DIM), F32), sin, sin, jnp.zeros((T, pad), F32)], axis=1)
    return cos_t, sin_t


def _even_weights(W, i):
    c3 = POOL_DIM + Q_RANK + KV_RANK
    w_in = W['a_w_in'][i]
    D = w_in.shape[0]
    rope_cols = jnp.concatenate([jnp.zeros((D, NOPE_DIM), BF), w_in[:, c3:], jnp.zeros((D, HEAD_SLOT - QK_DIM), BF)], axis=1)
    win = jnp.concatenate([w_in[:, :c3], rope_cols], axis=1)
    wq = _pad_heads(W['a_w_q_up'][i], QK_DIM)
    kv = W['a_w_kv_up'][i].reshape(KV_RANK, HEADS, NOPE_DIM + V_DIM)
    wkn = _pad_heads(kv[:, :, :NOPE_DIM].reshape(KV_RANK, HEADS * NOPE_DIM), NOPE_DIM)
    wv = _pad_heads(kv[:, :, NOPE_DIM:].reshape(KV_RANK, HEADS * V_DIM), V_DIM)
    w_out = W['a_w_out'][i]
    wo_pool = w_out[:POOL_DIM]
    wo_attn = _pad_heads(w_out[POOL_DIM:].T, V_DIM).T
    wpool = W['a_w_pool'][i]
    pad = lambda a: jnp.pad(a, (0, HEAD_SLOT - QK_DIM))[None, :]
    return dict(win=win, win_t=win.T, wq=wq, wq_t=wq.T, wkn=wkn, wkn_t=wkn.T, wv=wv, wv_t=wv.T,
                wo_pool=wo_pool, wo_pool_t=wo_pool.T, wo_attn=wo_attn, wo_attn_t=wo_attn.T,
                wpool=wpool, wpool_t=jnp.transpose(wpool, (0, 2, 1)),
                qan=W['a_q_a_norm'][i][None, :], kvan=W['a_kv_a_norm'][i][None, :],
                qhn=pad(W['a_q_head_norm'][i]), khn=pad(W['a_k_head_norm'][i]),
                pscale=W['a_pool_scale'][i][None, :], g=W['mix_norm'][2 * i][None, :])


def kernel(*args):
    p = dict(zip(INPUTS, args))
    x0 = p['x'][0]
    target = p['loss_target'][0]
    T, D = x0.shape

    def wire(n, a):
        return _f32_bits_as(a, BF) if n == 'c_conv_w' else a.astype(BF)

    first = [(n, wire(n, p[n][:_first_layers(n)])) for n in SHARDED if _first_layers(n) > 0]
    late = [(n, wire(n, p[n][_first_layers(n):])) for n in SHARDED if p[n].shape[0] > _first_layers(n)]
    W = {n: [] for n in SHARDED}

    def receive(gathered, group):
        shapes = [[a.shape for _, a in group]]
        for (n, _), got in zip(group, unpack_flat(gathered, shapes, lead=(4,))[0]):
            whole = _join_shards(n, _f32_from_bits(got) if n == 'c_conv_w' else got)
            W[n] += [whole[i] for i in range(whole.shape[0])]

    receive(allgather_shards(pack_flat([[a for _, a in first]])), first)
    late_shards = pack_flat([[a for _, a in late]])
    for n in REPLICATED:
        W[n] = p[n]
    W['a_w_pool'] = p['a_w_pool'].astype(BF)
    cos, sin = _rope_tables(T)

    def ffn_weights(pre, l):
        wg, wu, wd = W[pre + '_w_gate'][l], W[pre + '_w_up'][l], W[pre + '_w_down'][l]
        return dict(g=W[pre + '_norm'][l][None, :], wg=wg, wu=wu, wd=wd, wg_t=wg.T, wu_t=wu.T, wd_t=wd.T)

    saved = []
    x = x0
    for l in range(DEPTH):
        s = dict(x0=x)
        f1 = ffn_weights('ffn1', l)
        x, s['g1'], s['u1'] = ffn_fwd(x, f1['g'], f1['wg'], f1['wu'], f1['wd'])
        s['x1'] = x
        if l % 2 == 0:
            e = _even_weights(W, l // 2)
            s['q'], s['k'], s['v'], s['po'] = mixa_pre_fwd(
                x, e['g'], e['win'], e['qan'], e['wq'], e['kvan'], e['wkn'], e['wv'], e['qhn'], e['khn'],
                e['wpool'], e['pscale'], cos, sin)
            v_t = _blocks_transposed(s['v'], _tile(T, TQ_ATTN, 128))
            if l == 0:
                o_t, s['lse'], gathered_late = attn_fwd(s['q'], s['k'], v_t, gather=late_shards)
                receive(gathered_late, late)
            else:
                o_t, s['lse'] = attn_fwd(s['q'], s['k'], v_t)
            s['o'] = _blocks_untransposed(o_t)
            x = mm_multi([(s['po'], e['wo_pool']), (s['o'], e['wo_attn'])], res=x)
        else:
            i = l // 2
            x, s['z'] = mixc_fwd(x, W['mix_norm'][l][None, :], W['c_w_in'][i], W['c_conv_w'][i].astype(F32),
                                 W['c_w_out'][i])
        s['x2'] = x
        f2 = ffn_weights('ffn2', l)
        x, s['g2'], s['u2'] = ffn_fwd(x, f2['g'], f2['wg'], f2['wu'], f2['wd'])
        saved.append(s)

    loss_sum, dy = loss_head(x, target)
    loss = lax.psum(0.5 * loss_sum[0, 0], AXES)

    G = {n: [None] * p[n].shape[0] for n in WEIGHTS}

    def ffn_back(pre, l, x_in, gg, uu, dy):
        f = ffn_weights(pre, l)
        dx, n, dyh, h, dgate, dup, dgn = ffn_bwd(x_in, dy, f['g'], gg, uu, f['wd_t'], f['wg_t'], f['wu_t'])
        G[pre + '_norm'][l] = dgn[0]
        G[pre + '_w_gate'][l] = mm_tn(n, dgate)
        G[pre + '_w_up'][l] = mm_tn(n, dup)
        G[pre + '_w_down'][l] = mm_tn(dyh, h).T
        return dx

    def partial_shards(late):
        arrays = [_split_shards(n, jnp.stack(G[n][_first_layers(n):] if late else G[n][:_first_layers(n)]))
                  for n in SHARDED if (p[n].shape[0] > _first_layers(n) if late else _first_layers(n) > 0)]
        return pack_flat([arrays], lead=(4,)).astype(BF)

    t_attn = _tile(T, TQ_ATTN, 128)
    for l in reversed(range(DEPTH)):
        s = saved[l]
        dy = ffn_back('ffn2', l, s['x2'], s['g2'], s['u2'], dy)
        i = l // 2
        if l % 2 == 0:
            e = _even_weights(W, i)
            G['a_w_out'][i] = jnp.concatenate(
                [mm_tn(s['po'], dy), _unpad_heads(mm_tn(s['o'], dy).T, V_DIM).T], axis=0)
            dpo = mm_multi([(dy, e['wo_pool_t'])])
            do = mm_multi([(dy, e['wo_attn_t'])], out_dtype=BF)
            as_rows = lambda a: a.reshape(HEADS, T // t_attn, t_attn)
            attn_args = (s['q'], s['k'], s['k'].T, s['v'], do, as_rows(s['lse']), as_rows(attn_delta(s['o'], do)))
            if l == 0:
                dk, dv, dq_t, land_late = attn_bwd(*attn_args, exchange=partial_shards(late=True))
            else:
                dk, dv, dq_t = attn_bwd(*attn_args)
            dq = _blocks_untransposed(dq_t)
            (dy, hn, dz, nq, dqraw, nkv, dkraw, pooled, dps, dg, dqan, dkvan, dqhn, dkhn, dpscale) = mixa_pre_bwd(
                s['x1'], dy, dq, dk, dv, dpo, e['g'], e['win'], e['win_t'], e['qan'], e['wq'], e['wq_t'], e['kvan'],
                e['wkn'], e['wkn_t'], e['wv_t'], e['qhn'], e['khn'], e['wpool'], e['wpool_t'], e['pscale'], cos, sin)
            c3 = POOL_DIM + Q_RANK + KV_RANK
            dwin = mm_tn(hn, dz)
            G['a_w_in'][i] = jnp.concatenate([dwin[:, :c3], dwin[:, c3 + NOPE_DIM:c3 + QK_DIM]], axis=1)
            G['a_w_q_up'][i] = _unpad_heads(mm_tn(nq, dqraw), QK_DIM)
            dwkn = _unpad_heads(mm_tn(nkv, dkraw), NOPE_DIM).reshape(KV_RANK, HEADS, NOPE_DIM)
            dwv = _unpad_heads(mm_tn(nkv, dv), V_DIM).reshape(KV_RANK, HEADS, V_DIM)
            G['a_w_kv_up'][i] = jnp.concatenate([dwkn, dwv], axis=2).reshape(KV_RANK, HEADS * (NOPE_DIM + V_DIM))
            dwp = mm_tn(pooled, dps)
            G['a_w_pool'][i] = jnp.stack([dwp[g * POOL_GROUP:(g + 1) * POOL_GROUP, g * POOL_GROUP:(g + 1) * POOL_GROUP]
                                          for g in range(len(POOL_WINDOWS))])
            G['mix_norm'][l] = dg[0]
            G['a_q_a_norm'][i] = dqan[0]
            G['a_kv_a_norm'][i] = dkvan[0]
            G['a_q_head_norm'][i] = dqhn[0, :QK_DIM]
            G['a_k_head_norm'][i] = dkhn[0, :QK_DIM]
            G['a_pool_scale'][i] = dpscale[0]
        else:
            w_in, w_out = W['c_w_in'][i], W['c_w_out'][i]
            dy_in = dy
            dy, hn, dz, gated, dcw, dg = mixc_bwd(s['x1'], dy, s['z'], W['mix_norm'][l][None, :], w_in.T,
                                                  W['c_conv_w'][i].astype(F32), w_out.T)
            G['c_w_in'][i] = mm_tn(hn, dz)
            G['c_w_out'][i] = mm_tn(gated, dy_in)
            G['c_conv_w'][i] = dcw
            G['mix_norm'][l] = dg[0]
        dy = ffn_back('ffn1', l, s['x0'], s['g1'], s['u1'], dy)
    grad_x = dy[None]

    partial_small = pack_flat([[jnp.stack(G[n]) for n in REPLICATED]])
    land_first, sland = exchange_partials(partial_shards(late=False), partial_small)
    g_big = share_with_sibling([sum_slots(land_first, BF), sum_slots(land_late, BF)])
    g_small = sum_slots(sland, F32)

    def layer_groups(pre):
        first = [p[pre + n][:_first_layers(n)] for n in SHARDED if _first_layers(n) > 0]
        late = [p[pre + n][_first_layers(n):] for n in SHARDED if p[n].shape[0] > _first_layers(n)]
        return [first, late]

    outs = {}
    g32, delta, m2, v2 = adamw(pack_flat(layer_groups('')), g_big, pack_flat(layer_groups('m_')),
                               pack_flat(layer_groups('v_')))
    group_shapes = [[a.shape for a in group] for group in layer_groups('')]
    for kind, arr in (('grad_', g32), ('delta_', delta), ('new_m_', m2), ('new_v_', v2)):
        first, late = (iter(group) for group in unpack_flat(arr, group_shapes))
        for n in SHARDED:
            pieces = ([next(first)] if _first_layers(n) > 0 else []) + (
                [next(late)] if p[n].shape[0] > _first_layers(n) else [])
            outs[kind + n] = pieces[0] if len(pieces) == 1 else jnp.concatenate(pieces, axis=0)
    small_groups = [[p[n].shape for n in REPLICATED]]
    flat = lambda pre: pack_flat([[p[pre + n] for n in REPLICATED]])
    g32, delta, m2, v2 = adamw(flat(''), g_small, flat('m_'), flat('v_'))
    for kind, arr in (('grad_', g32), ('delta_', delta), ('new_m_', m2), ('new_v_', v2)):
        for n, a in zip(REPLICATED, unpack_flat(arr, small_groups)[0]):
            outs[kind + n] = a
    return (loss, grad_x, *[outs[k + n] for k in ('grad_', 'delta_', 'new_m_', 'new_v_') for n in WEIGHTS])
```

```python
import functools

import numpy as np
import jax
import jax.numpy as jnp
from jax import lax
from jax.experimental import pallas as pl
from jax.experimental.pallas import tpu as pltpu

BF, F32 = jnp.bfloat16, jnp.float32
MESH = pl.DeviceIdType.MESH
AXES = ("x", "y", "c")

NORM_EPS = 1e-6
DEPTH = 4
HEADS = 8
HEAD_SLOT = 128
QK_DIM, NOPE_DIM, ROPE_DIM, V_DIM = 96, 64, 32, 64
POOL_WINDOWS = (2, 4, 8, 16)
POOL_DIM, POOL_GROUP = 512, 128
Q_RANK, KV_RANK = 384, 256
ROPE_THETA = 10000.0
HALO = 16
ATTN_SCALE = QK_DIM ** -0.5
LOG2_E = 1.4426950408889634

ADAM_LR, ADAM_B1, ADAM_B2, ADAM_EPS, ADAM_WD, ADAM_STEP = 0.001, 0.9, 0.999, 1e-08, 0.01, 10

TM_FFN_FWD, TM_FFN_BWD, TF_FFN = 512, 256, 256
TM_MIX_FWD, TM_MIX_BWD = 512, 256
TM_CONV_FWD, TM_CONV_BWD = 256, 256
TQ_ATTN = 512
TM_MM = 512
TK_TN, BM_TN, BN_TN = 2048, 1024, 1536
FLAT_COLS = 1024
FLAT_ROW_ALIGN = 1024
SIBLING_CHUNKS = 16
LOCAL_CHUNKS = 16
GATHER_CHUNKS = 8
TR_FLAT = 256
VMEM_LIMIT = 56 * 1024 * 1024

WEIGHTS = ['ffn1_norm', 'ffn1_w_gate', 'ffn1_w_up', 'ffn1_w_down', 'mix_norm', 'ffn2_norm', 'ffn2_w_gate',
           'ffn2_w_up', 'ffn2_w_down', 'a_w_in', 'a_q_a_norm', 'a_w_q_up', 'a_kv_a_norm', 'a_w_kv_up',
           'a_q_head_norm', 'a_k_head_norm', 'a_w_pool', 'a_pool_scale', 'a_w_out', 'c_w_in', 'c_conv_w',
           'c_w_out']
COL_SHARDED = ('ffn1_w_gate', 'ffn1_w_up', 'ffn2_w_gate', 'ffn2_w_up', 'a_w_in', 'a_w_q_up', 'a_w_kv_up',
               'c_w_in', 'c_conv_w')
ROW_SHARDED = ('ffn1_w_down', 'ffn2_w_down', 'a_w_out', 'c_w_out')
SHARDED = tuple(n for n in WEIGHTS if n in COL_SHARDED or n in ROW_SHARDED)
REPLICATED = tuple(n for n in WEIGHTS if n not in SHARDED)
INPUTS = ['x'] + WEIGHTS + ['loss_target'] + ['m_' + n for n in WEIGHTS] + ['v_' + n for n in WEIGHTS]


def _dot(a, b):
    return jnp.dot(a, b, preferred_element_type=F32)


def _dot_nt(a, b):
    return lax.dot_general(a, b, (((1,), (1,)), ((), ())), preferred_element_type=F32)


def _dot_tn(a, b):
    return lax.dot_general(a, b, (((0,), (0,)), ((), ())), preferred_element_type=F32)


def _params(*sem):
    return pltpu.CompilerParams(dimension_semantics=sem or None, vmem_limit_bytes=VMEM_LIMIT)


def _tile(n, cap, unit):
    if n <= cap:
        return n
    best = None
    for t in range(unit, cap + 1, unit):
        if n % t == 0:
            best = t
    assert best is not None, (n, cap, unit)
    return best


def _rms(x, width=None):
    ms = jnp.sum(x * x, axis=-1, keepdims=True) * (1.0 / (width or x.shape[-1]))
    r = lax.rsqrt(ms + NORM_EPS)
    return x * r, r


def _rms_bwd(a, xhat, r, width=None):
    return r * (a - xhat * (jnp.sum(a * xhat, axis=-1, keepdims=True) * (1.0 / (width or a.shape[-1]))))


def _colsum(a):
    return jnp.sum(a, axis=0, keepdims=True)


def _accumulate(ref, first, value):
    @pl.when(first)
    def _():
        ref[...] = value

    @pl.when(jnp.logical_not(first))
    def _():
        ref[...] += value


def _rot_half(v):
    lane = lax.broadcasted_iota(jnp.int32, v.shape, 1)
    rot = jnp.where(lane < NOPE_DIM + ROPE_DIM // 2, -pltpu.roll(v, HEAD_SLOT - ROPE_DIM // 2, 1),
                    pltpu.roll(v, ROPE_DIM // 2, 1))
    return jnp.where((lane >= NOPE_DIM) & (lane < QK_DIM), rot, 0.0)


def _rope(v, cos, sin):
    return v * cos + _rot_half(v) * sin


def _rope_bwd(d, cos, sin):
    return d * cos - _rot_half(d * sin)


def _resident(arr):
    return pl.BlockSpec(arr.shape, lambda i: (0,) * arr.ndim, pipeline_mode=pl.Buffered(1))


def ffn_fwd(x, g, wg, wu, wd):
    T, D = x.shape
    F = wg.shape[1]
    tm, tf = _tile(T, TM_FFN_FWD, 8), _tile(F, TF_FFN, 128)
    nf = F // tf

    def body(x_ref, g_ref, wg_ref, wu_ref, wd_ref, y_ref, gg_ref, uu_ref, h_sc):
        xv = x_ref[...]
        xh, _ = _rms(xv)
        n = (xh * g_ref[...]).astype(BF)

        def projections(c):
            cols = slice(c * tf, (c + 1) * tf)
            return _dot(n, wg_ref[:, cols]), _dot(n, wu_ref[:, cols])

        ahead = projections(0)
        for c in range(nf):
            gg, uu = ahead
            if c + 1 < nf:
                ahead = projections(c + 1)
            cols = slice(c * tf, (c + 1) * tf)
            gg_ref[:, cols] = gg.astype(BF)
            uu_ref[:, cols] = uu.astype(BF)
            h_sc[:, cols] = (gg * jax.nn.sigmoid(gg) * uu).astype(BF)
        y_ref[...] = xv + 0.5 * _dot(h_sc[...], wd_ref[...])

    tok = lambda w: pl.BlockSpec((tm, w), lambda i: (i, 0))
    return pl.pallas_call(
        body, name="ffn_fwd", grid=(T // tm,),
        in_specs=[tok(D), _resident(g), _resident(wg), _resident(wu), _resident(wd)],
        out_specs=[tok(D), tok(F), tok(F)],
        out_shape=[jax.ShapeDtypeStruct((T, D), F32), jax.ShapeDtypeStruct((T, F), BF),
                   jax.ShapeDtypeStruct((T, F), BF)],
        scratch_shapes=[pltpu.VMEM((tm, F), BF)],
        compiler_params=_params("arbitrary"),
    )(x, g, wg, wu, wd)


def ffn_bwd(x, dy, g, gg, uu, wd_t, wg_t, wu_t):
    T, D = x.shape
    F = gg.shape[1]
    tm, tf = _tile(T, TM_FFN_BWD, 8), _tile(F, TF_FFN, 128)
    nf = F // tf

    def body(x_ref, dy_ref, g_ref, gg_ref, uu_ref, wdt_ref, wgt_ref, wut_ref,
             dx_ref, n_ref, dyh_ref, h_ref, dg_ref, du_ref, dgn_ref):
        xh, r = _rms(x_ref[...])
        n_ref[...] = (xh * g_ref[...]).astype(BF)
        dyv = dy_ref[...]
        dyh = (0.5 * dyv).astype(BF)
        dyh_ref[...] = dyh

        def hidden_grad(c):
            return _dot(dyh, wdt_ref[:, c * tf:(c + 1) * tf])

        ahead = hidden_grad(0)
        for c in range(nf):
            dh = ahead
            if c + 1 < nf:
                ahead = hidden_grad(c + 1)
            cols = slice(c * tf, (c + 1) * tf)
            gv = gg_ref[:, cols].astype(F32)
            uv = uu_ref[:, cols].astype(F32)
            sg = jax.nn.sigmoid(gv)
            silu = gv * sg
            h_ref[:, cols] = (silu * uv).astype(BF)
            du_ref[:, cols] = (dh * silu).astype(BF)
            dg_ref[:, cols] = (dh * uv * (sg * (1.0 + gv * (1.0 - sg)))).astype(BF)
        dn = _dot(dg_ref[...], wgt_ref[...]) + _dot(du_ref[...], wut_ref[...])
        dx_ref[...] = dyv + _rms_bwd(dn * g_ref[...], xh, r)
        _accumulate(dgn_ref, pl.program_id(0) == 0, _colsum(dn * xh))

    tok = lambda w: pl.BlockSpec((tm, w), lambda i: (i, 0))
    return pl.pallas_call(
        body, name="ffn_bwd", grid=(T // tm,),
        in_specs=[tok(D), tok(D), _resident(g), tok(F), tok(F), _resident(wd_t), _resident(wg_t), _resident(wu_t)],
        out_specs=[tok(D), tok(D), tok(D), tok(F), tok(F), tok(F), pl.BlockSpec((1, D), lambda i: (0, 0))],
        out_shape=[jax.ShapeDtypeStruct((T, D), F32), jax.ShapeDtypeStruct((T, D), BF),
                   jax.ShapeDtypeStruct((T, D), BF), jax.ShapeDtypeStruct((T, F), BF),
                   jax.ShapeDtypeStruct((T, F), BF), jax.ShapeDtypeStruct((T, F), BF),
                   jax.ShapeDtypeStruct((1, D), F32)],
        compiler_params=_params("arbitrary"),
    )(x, dy, g, gg, uu, wd_t, wg_t, wu_t)


def mm_tn(a, b):
    T, M = a.shape
    N = b.shape[1]
    tk, bm, bn = _tile(T, TK_TN, 16), _tile(M, BM_TN, 128), _tile(N, BN_TN, 128)

    def body(a_ref, b_ref, o_ref):
        part = _dot_tn(a_ref[...].astype(BF), b_ref[...].astype(BF))
        _accumulate(o_ref, pl.program_id(2) == 0, part)

    return pl.pallas_call(
        body, name="mm_tn", grid=(M // bm, N // bn, T // tk),
        in_specs=[pl.BlockSpec((tk, bm), lambda i, j, k: (k, i)), pl.BlockSpec((tk, bn), lambda i, j, k: (k, j))],
        out_specs=pl.BlockSpec((bm, bn), lambda i, j, k: (i, j)),
        out_shape=jax.ShapeDtypeStruct((M, N), F32),
        compiler_params=_params("arbitrary", "arbitrary", "arbitrary"),
    )(a, b)


def mm_multi(pairs, res=None, out_dtype=F32):
    T = pairs[0][0].shape[0]
    N = pairs[0][1].shape[1]
    tm = _tile(T, TM_MM, 16)
    n = len(pairs)

    def body(*refs):
        o_ref = refs[-1]
        acc = refs[2 * n][...] if res is not None else None
        for k in range(n):
            part = _dot(refs[k][...].astype(BF), refs[n + k][...])
            acc = part if acc is None else acc + part
        o_ref[...] = acc.astype(out_dtype)

    ins = [a for a, _ in pairs] + [w for _, w in pairs]
    specs = [pl.BlockSpec((tm, a.shape[1]), lambda i: (i, 0)) for a, _ in pairs]
    specs += [pl.BlockSpec(w.shape, lambda i: (0, 0)) for _, w in pairs]
    if res is not None:
        ins.append(res)
        specs.append(pl.BlockSpec((tm, N), lambda i: (i, 0)))
    return pl.pallas_call(
        body, name="mm_multi", grid=(T // tm,), in_specs=specs,
        out_specs=pl.BlockSpec((tm, N), lambda i: (i, 0)),
        out_shape=jax.ShapeDtypeStruct((T, N), out_dtype),
        compiler_params=_params("arbitrary"),
    )(*ins)


def _causal_mask(t, q_major):
    r = lax.broadcasted_iota(jnp.int32, (t, t), 0)
    c = lax.broadcasted_iota(jnp.int32, (t, t), 1)
    return (c <= r) if q_major else (r <= c)


def _blocks_transposed(a, t):
    T = a.shape[0]
    return jnp.transpose(a.reshape(T // t, t, HEADS, HEAD_SLOT), (2, 0, 3, 1))


def _blocks_untransposed(a):
    H, n, d, t = a.shape
    return jnp.transpose(a, (1, 3, 0, 2)).reshape(n * t, H * d)


def attn_fwd(q, k, v_t, gather=None):
    T = q.shape[0]
    t = _tile(T, TQ_ATTN, 128)
    nq = T // t

    def body(q_ref, k_ref, vt_ref, *rest):
        if gather is None:
            ot_ref, lse_ref = rest
        else:
            w_ref, ot_ref, lse_ref, all_ref, send_sems, recv_sems, local_sem = rest
            step = pl.program_id(0) * nq + pl.program_id(1)
            moving = lambda: _ShardGather(w_ref, all_ref, send_sems, recv_sems, local_sem)
            pl.when(step == 0)(lambda: moving().start())
            pl.when(step == (HEADS // 2) * nq)(lambda: moving().forward())
        i = pl.program_id(1)
        qv = q_ref[...]

        def update(st, j, m, l, acc):
            m2 = jnp.maximum(m, jnp.max(st, axis=0, keepdims=True))
            pt = jnp.exp2(st - m2)
            scale = jnp.exp2(m - m2)
            return (m2, scale * l + jnp.sum(pt, axis=0, keepdims=True),
                    scale * acc + _dot(vt_ref[j], pt.astype(BF)))

        def scores(j, masked):
            st = _dot_nt(k_ref[pl.ds(pl.multiple_of(j * t, t), t), :], qv)
            return jnp.where(_causal_mask(t, False), st, -jnp.inf) if masked else st

        def pair(j, carry, last_is_diagonal):
            s0, s1 = scores(j, False), scores(j + 1, last_is_diagonal)
            return update(s1, j + 1, *update(s0, j, *carry))

        init = (jnp.full((1, t), -1e30, F32), jnp.zeros((1, t), F32), jnp.zeros((HEAD_SLOT, t), F32))
        carry = lax.fori_loop(0, i // 2, lambda jj, c: pair(2 * jj, c, False), init)
        m, l, acc = lax.cond(i % 2 == 1, lambda c: pair(i - 1, c, True),
                             lambda c: update(scores(i, True), i, *c), carry)
        ot_ref[...] = (acc / l).astype(BF)
        lse_ref[...] = m + jnp.log2(l)
        if gather is not None:
            pl.when(step == HEADS * nq - 1)(lambda: moving().finish())

    operands = [q, k, v_t]
    in_specs = [pl.BlockSpec((t, HEAD_SLOT), lambda h, i: (i, h)), pl.BlockSpec((T, HEAD_SLOT), lambda h, i: (0, h)),
                pl.BlockSpec((None, nq, HEAD_SLOT, t), lambda h, i: (h, 0, 0, 0))]
    out_specs = [pl.BlockSpec((None, None, HEAD_SLOT, t), lambda h, i: (h, i, 0, 0)),
                 pl.BlockSpec((None, None, 1, t), lambda h, i: (h, i, 0, 0))]
    out_shape = [jax.ShapeDtypeStruct((HEADS, nq, HEAD_SLOT, t), BF), jax.ShapeDtypeStruct((HEADS, nq, 1, t), F32)]
    scratch = []
    if gather is not None:
        operands.append(gather)
        in_specs.append(pl.BlockSpec(memory_space=pl.ANY))
        out_specs.append(pl.BlockSpec(memory_space=pl.ANY))
        out_shape.append(jax.ShapeDtypeStruct((4, *gather.shape), gather.dtype))
        scratch = [pltpu.SemaphoreType.DMA((6,)), pltpu.SemaphoreType.DMA((6,)), pltpu.SemaphoreType.DMA]
    return pl.pallas_call(
        body, name="attn_fwd" if gather is None else "attn_fwd_gather", grid=(HEADS, nq),
        in_specs=in_specs, out_specs=out_specs, out_shape=out_shape, scratch_shapes=scratch,
        compiler_params=_params("arbitrary", "arbitrary"),
    )(*operands)


def attn_delta(o, do):
    T = o.shape[0]
    t = _tile(T, TQ_ATTN, 128)

    def body(o_ref, do_ref, delta_ref):
        prod = do_ref[...].astype(F32) * o_ref[...].astype(F32)
        delta_ref[...] = jnp.sum(prod.T, axis=0, keepdims=True)

    blk = pl.BlockSpec((t, HEAD_SLOT), lambda h, i: (i, h))
    return pl.pallas_call(
        body, name="attn_delta", grid=(HEADS, T // t), in_specs=[blk, blk],
        out_specs=pl.BlockSpec((None, None, 1, t), lambda h, i: (h, i, 0, 0)),
        out_shape=jax.ShapeDtypeStruct((HEADS, T // t, 1, t), F32),
        compiler_params=_params("arbitrary", "arbitrary"),
    )(o, do)


def attn_bwd(q, k, k_t, v, do, lse_rows, delta_rows, exchange=None):
    T = q.shape[0]
    t = _tile(T, TQ_ATTN, 128)
    nq = T // t

    def body(q_ref, k_ref, kt_ref, v_ref, do_ref, lse_ref, delta_ref, *rest):
        if exchange is None:
            dk_ref, dv_ref, dqt_ref = rest
        else:
            g_ref, dk_ref, dv_ref, dqt_ref, land_ref, send_sems, recv_sems, local_sem = rest
            first = (pl.program_id(0) == 0) & (pl.program_id(1) == 0)
            last = (pl.program_id(0) == HEADS - 1) & (pl.program_id(1) == nq - 1)
            pl.when(first)(lambda: _HalvesExchange(g_ref, land_ref, send_sems, recv_sems, local_sem).start())
        j = pl.program_id(1)
        kv, vv, ktv = k_ref[...], v_ref[...], kt_ref[...]

        @pl.when(j == 0)
        def _():
            dqt_ref[...] = jnp.zeros_like(dqt_ref)

        def block(i):
            return pl.ds(pl.multiple_of(i * t, t), t)

        def scores(i, masked):
            st = _dot_nt(kv, q_ref[block(i), :])
            return jnp.where(_causal_mask(t, False), st, -jnp.inf) if masked else st

        def add(carry, st, i):
            dk, dv = carry
            qv, dov = q_ref[block(i), :], do_ref[block(i), :]
            pt = jnp.exp2(st - lse_ref[pl.ds(i, 1), :])
            dst = (pt * (_dot_nt(vv, dov) - delta_ref[pl.ds(i, 1), :])).astype(BF)
            dqt_ref[i] += _dot(ktv, dst)
            return dk + _dot(dst, qv), dv + _dot(pt.astype(BF), dov)

        def pair(i, carry):
            s0, s1 = scores(i, False), scores(i + 1, False)
            return add(add(carry, s0, i), s1, i + 1)

        zero = jnp.zeros((t, HEAD_SLOT), F32)
        carry = add((zero, zero), scores(j, True), j)
        rest = nq - 1 - j
        carry = lax.fori_loop(0, rest // 2, lambda ii, c: pair(j + 1 + 2 * ii, c), carry)
        dk, dv = lax.cond(rest % 2 == 1, lambda c: add(c, scores(nq - 1, False), nq - 1), lambda c: c, carry)
        dk_ref[...] = dk * (1.0 / LOG2_E)
        dv_ref[...] = dv
        if exchange is not None:
            pl.when(last)(lambda: _HalvesExchange(g_ref, land_ref, send_sems, recv_sems, local_sem).wait())

    blk = pl.BlockSpec((t, HEAD_SLOT), lambda h, j: (j, h))
    full = pl.BlockSpec((T, HEAD_SLOT), lambda h, j: (0, h))
    rows = pl.BlockSpec((None, nq, t), lambda h, j: (h, 0, 0))
    operands = [q, k, k_t, v, do, lse_rows, delta_rows]
    in_specs = [full, blk, pl.BlockSpec((HEAD_SLOT, t), lambda h, j: (h, j)), blk, full, rows, rows]
    out_specs = [blk, blk, pl.BlockSpec((None, nq, HEAD_SLOT, t), lambda h, j: (h, 0, 0, 0))]
    out_shape = [jax.ShapeDtypeStruct((T, HEADS * HEAD_SLOT), F32)] * 2 + [
        jax.ShapeDtypeStruct((HEADS, nq, HEAD_SLOT, t), F32)]
    scratch = []
    if exchange is not None:
        operands.append(exchange)
        in_specs.append(pl.BlockSpec(memory_space=pl.ANY))
        out_specs.append(pl.BlockSpec(memory_space=pl.ANY))
        out_shape.append(jax.ShapeDtypeStruct((8, exchange.shape[1] // 2, exchange.shape[2]), exchange.dtype))
        scratch = [pltpu.SemaphoreType.DMA((7,)), pltpu.SemaphoreType.DMA((7,)), pltpu.SemaphoreType.DMA]
    return pl.pallas_call(
        body, name="attn_bwd" if exchange is None else "attn_bwd_exchange", grid=(HEADS, nq),
        in_specs=in_specs, out_specs=out_specs, out_shape=out_shape, scratch_shapes=scratch,
        compiler_params=_params("arbitrary", "arbitrary"),
    )(*operands)


def _prev_halo(tm):
    return lambda i: (jnp.maximum(i * (tm // HALO) - 1, 0), 0)


def _next_halo(tm, T):
    return lambda i: (jnp.minimum((i + 1) * (tm // HALO), T // HALO - 1), 0)


def _inv_count(row0, n, w):
    t = row0 + lax.broadcasted_iota(jnp.int32, (n, 1), 0)
    return 1.0 / jnp.minimum(t + 1, w).astype(F32)


def _pool_fwd(u_prev, u, row0):
    tm = u.shape[0]
    out = []
    for g, w in enumerate(POOL_WINDOWS):
        lanes = slice(g * POOL_GROUP, (g + 1) * POOL_GROUP)
        ue = jnp.concatenate([u_prev[:, lanes], u[:, lanes]], axis=0)
        s, step = ue, 1
        while step < w:
            s = s + pltpu.roll(s, step, 0)
            step *= 2
        out.append(s[HALO:, :] * _inv_count(row0, tm, w) - u[:, lanes])
    return out


def _pool_bwd(dp, dp_next, row0):
    tm = dp[0].shape[0]
    out = []
    for g, w in enumerate(POOL_WINDOWS):
        e = jnp.concatenate([dp[g] * _inv_count(row0, tm, w), dp_next[g] * (1.0 / w)], axis=0)
        n = tm + HALO
        s, step = e, 1
        while step < w:
            s = s + pltpu.roll(s, n - step, 0)
            step *= 2
        out.append(s[:tm, :] - dp[g])
    return out


def _mixa_front(x, xp, first, row0, g_ref, win_ref, qan_ref, wq_ref, kvan_ref, wkn_ref):
    xh, r = _rms(x)
    hn = (xh * g_ref[...]).astype(BF)
    z = _dot(hn, win_ref[...])
    xph, _ = _rms(xp)
    u_prev = _dot((xph * g_ref[...]).astype(BF), win_ref[:, :POOL_DIM]) * jnp.where(first, 0.0, 1.0)
    u = z[:, :POOL_DIM]
    pooled = _pool_fwd(u_prev, u, row0)
    c1, c2 = POOL_DIM + Q_RANK, POOL_DIM + Q_RANK + KV_RANK
    qh, rq = _rms(z[:, POOL_DIM:c1])
    nq = (qh * qan_ref[...]).astype(BF)
    kh, rk = _rms(z[:, c1:c2])
    nkv = (kh * kvan_ref[...]).astype(BF)
    qraw = _dot(nq, wq_ref[...])
    kraw = _dot(nkv, wkn_ref[...])
    krope = z[:, c2:c2 + HEAD_SLOT]
    return dict(xh=xh, r=r, hn=hn, pooled=pooled, qh=qh, rq=rq, nq=nq, kh=kh, rk=rk, nkv=nkv,
                qraw=qraw, kraw=kraw, krope=krope)


def mixa_pre_fwd(x, g, win, qan, wq, kvan, wkn, wv, qhn, khn, wpool, pscale, cos, sin):
    T, D = x.shape
    tm = _tile(T, TM_MIX_FWD, HALO)
    HS = HEADS * HEAD_SLOT

    def body(x_ref, xp_ref, g_ref, win_ref, qan_ref, wq_ref, kvan_ref, wkn_ref, wv_ref, qhn_ref, khn_ref,
             wpool_ref, pscale_ref, cos_ref, sin_ref, q_ref, k_ref, v_ref, po_ref):
        i = pl.program_id(0)
        a = _mixa_front(x_ref[...], xp_ref[...], i == 0, i * tm, g_ref, win_ref, qan_ref, wq_ref, kvan_ref, wkn_ref)
        for gi in range(len(POOL_WINDOWS)):
            lanes = slice(gi * POOL_GROUP, (gi + 1) * POOL_GROUP)
            po = _dot(a["pooled"][gi].astype(BF), wpool_ref[gi]) * pscale_ref[:, lanes]
            po_ref[:, lanes] = po.astype(BF)
        cosv, sinv = cos_ref[...], sin_ref[...]
        v_ref[...] = _dot(a["nkv"], wv_ref[...]).astype(BF)
        for h in range(HEADS):
            lanes = slice(h * HEAD_SLOT, (h + 1) * HEAD_SLOT)
            qn, _ = _rms(a["qraw"][:, lanes], QK_DIM)
            q_ref[:, lanes] = (_rope(qn * qhn_ref[...], cosv, sinv) * (ATTN_SCALE * LOG2_E)).astype(BF)
            kn, _ = _rms(a["kraw"][:, lanes] + a["krope"], QK_DIM)
            k_ref[:, lanes] = _rope(kn * khn_ref[...], cosv, sinv).astype(BF)

    tok = lambda w: pl.BlockSpec((tm, w), lambda i: (i, 0))
    whole = lambda arr: pl.BlockSpec(arr.shape, lambda i: (0,) * arr.ndim)
    return pl.pallas_call(
        body, name="mixa_pre_fwd", grid=(T // tm,),
        in_specs=[tok(D), pl.BlockSpec((HALO, D), _prev_halo(tm))] + [whole(a) for a in
                  (g, win, qan, wq, kvan, wkn, wv, qhn, khn, wpool, pscale)] + [tok(HEAD_SLOT), tok(HEAD_SLOT)],
        out_specs=[tok(HS), tok(HS), tok(HS), tok(POOL_DIM)],
        out_shape=[jax.ShapeDtypeStruct((T, HS), BF)] * 3 + [jax.ShapeDtypeStruct((T, POOL_DIM), BF)],
        compiler_params=_params("arbitrary"),
    )(x, x, g, win, qan, wq, kvan, wkn, wv, qhn, khn, wpool, pscale, cos, sin)


def mixa_pre_bwd(x, dy, dq, dk, dv, dpo, g, win, win_t, qan, wq, wq_t, kvan, wkn, wkn_t, wv_t, qhn, khn,
                 wpool, wpool_t, pscale, cos, sin, share=None):
    T, D = x.shape
    tm = _tile(T, TM_MIX_BWD, HALO)
    HS = HEADS * HEAD_SLOT
    ZW = win.shape[1]
    nt = T // tm

    def body(*refs):
        (x_ref, xp_ref, dy_ref, dq_ref, dk_ref, dv_ref, dpo_ref, dpon_ref, g_ref, win_ref, wint_ref, qan_ref,
         wq_ref, wqt_ref, kvan_ref, wkn_ref, wknt_ref, wvt_ref, qhn_ref, khn_ref, wpool_ref, wpoolt_ref,
         pscale_ref, cos_ref, sin_ref) = refs[:25]
        refs = refs[25:]
        if share is not None:
            p_ref, refs = refs[0], refs[1:]
            moving = lambda: _SiblingShare(p_ref, *refs[15:])
        (dx_ref, hn_ref, dz_ref, nq_ref, dqraw_ref, nkv_ref, dkraw_ref, pooled_ref, dps_ref,
         dg_ref, dqan_ref, dkvan_ref, dqhn_ref, dkhn_ref, dpscale_ref) = refs[:15]
        i = pl.program_id(0)
        first = i == 0
        if share is not None:
            pl.when(first)(lambda: moving().start())
        a = _mixa_front(x_ref[...], xp_ref[...], first, i * tm, g_ref, win_ref, qan_ref, wq_ref, kvan_ref, wkn_ref)
        cosv, sinv = cos_ref[...], sin_ref[...]
        hn_ref[...] = a["hn"]
        nq_ref[...] = a["nq"]
        nkv_ref[...] = a["nkv"]

        has_next = jnp.where(i == nt - 1, 0.0, 1.0)
        dpool, dpool_next, dpscale = [], [], []
        for gi in range(len(POOL_WINDOWS)):
            lanes = slice(gi * POOL_GROUP, (gi + 1) * POOL_GROUP)
            pooled = a["pooled"][gi].astype(BF)
            pooled_ref[:, lanes] = pooled
            dpo_g = dpo_ref[:, lanes]
            dpscale.append(_colsum(dpo_g * _dot(pooled, wpool_ref[gi])))
            dps = (dpo_g * pscale_ref[:, lanes]).astype(BF)
            dps_ref[:, lanes] = dps
            dpool.append(_dot(dps, wpoolt_ref[gi]))
            dps_n = (dpon_ref[:, lanes] * pscale_ref[:, lanes] * has_next).astype(BF)
            dpool_next.append(_dot(dps_n, wpoolt_ref[gi]))
        du = jnp.concatenate(_pool_bwd(dpool, dpool_next, i * tm), axis=1)
        _accumulate(dpscale_ref, first, jnp.concatenate(dpscale, axis=1))

        dqhn = jnp.zeros((1, HEAD_SLOT), F32)
        dkhn = jnp.zeros((1, HEAD_SLOT), F32)
        dkrope = jnp.zeros((tm, HEAD_SLOT), F32)
        for h in range(HEADS):
            lanes = slice(h * HEAD_SLOT, (h + 1) * HEAD_SLOT)
            qhat, rq = _rms(a["qraw"][:, lanes], QK_DIM)
            dqn = _rope_bwd(dq_ref[:, lanes] * ATTN_SCALE, cosv, sinv)
            dqhn = dqhn + _colsum(dqn * qhat)
            dqraw_ref[:, lanes] = _rms_bwd(dqn * qhn_ref[...], qhat, rq, QK_DIM).astype(BF)
            khat, rk = _rms(a["kraw"][:, lanes] + a["krope"], QK_DIM)
            dkn = _rope_bwd(dk_ref[:, lanes], cosv, sinv)
            dkhn = dkhn + _colsum(dkn * khat)
            dkraw = _rms_bwd(dkn * khn_ref[...], khat, rk, QK_DIM)
            dkrope = dkrope + dkraw
            dkraw_ref[:, lanes] = dkraw.astype(BF)
        _accumulate(dqhn_ref, first, dqhn)
        _accumulate(dkhn_ref, first, dkhn)

        dnq = _dot(dqraw_ref[...], wqt_ref[...])
        _accumulate(dqan_ref, first, _colsum(dnq * a["qh"]))
        dql = _rms_bwd(dnq * qan_ref[...], a["qh"], a["rq"])
        dnkv = _dot(dkraw_ref[...], wknt_ref[...]) + _dot(dv_ref[...].astype(BF), wvt_ref[...])
        _accumulate(dkvan_ref, first, _colsum(dnkv * a["kh"]))
        dkvl = _rms_bwd(dnkv * kvan_ref[...], a["kh"], a["rk"])

        dz = jnp.concatenate([du, dql, dkvl, dkrope], axis=1).astype(BF)
        dz_ref[...] = dz
        dhn = _dot(dz, wint_ref[...])
        _accumulate(dg_ref, first, _colsum(dhn * a["xh"]))
        dx_ref[...] = dy_ref[...] + _rms_bwd(dhn * g_ref[...], a["xh"], a["r"])
        if share is not None:
            pl.when(i == nt - 1)(lambda: moving().wait())

    tok = lambda w: pl.BlockSpec((tm, w), lambda i: (i, 0))
    whole = lambda arr: pl.BlockSpec(arr.shape, lambda i: (0,) * arr.ndim)
    row = lambda w: pl.BlockSpec((1, w), lambda i: (0, 0))
    weights = (g, win, win_t, qan, wq, wq_t, kvan, wkn, wkn_t, wv_t, qhn, khn, wpool, wpool_t, pscale)
    operands = [x, x, dy, dq, dk, dv, dpo, dpo, *weights, cos, sin]
    in_specs = ([tok(D), pl.BlockSpec((HALO, D), _prev_halo(tm)), tok(D), tok(HS), tok(HS), tok(HS), tok(POOL_DIM),
                 pl.BlockSpec((HALO, POOL_DIM), _next_halo(tm, T))] + [whole(a) for a in weights]
                + [tok(HEAD_SLOT), tok(HEAD_SLOT)])
    out_specs = [tok(D), tok(D), tok(ZW), tok(Q_RANK), tok(HS), tok(KV_RANK), tok(HS), tok(POOL_DIM), tok(POOL_DIM),
                 row(D), row(Q_RANK), row(KV_RANK), row(HEAD_SLOT), row(HEAD_SLOT), row(POOL_DIM)]
    out_shape = [jax.ShapeDtypeStruct((T, D), F32), jax.ShapeDtypeStruct((T, D), BF),
                 jax.ShapeDtypeStruct((T, ZW), BF), jax.ShapeDtypeStruct((T, Q_RANK), BF),
                 jax.ShapeDtypeStruct((T, HS), BF), jax.ShapeDtypeStruct((T, KV_RANK), BF),
                 jax.ShapeDtypeStruct((T, HS), BF), jax.ShapeDtypeStruct((T, POOL_DIM), BF),
                 jax.ShapeDtypeStruct((T, POOL_DIM), BF),
                 jax.ShapeDtypeStruct((1, D), F32), jax.ShapeDtypeStruct((1, Q_RANK), F32),
                 jax.ShapeDtypeStruct((1, KV_RANK), F32), jax.ShapeDtypeStruct((1, HEAD_SLOT), F32),
                 jax.ShapeDtypeStruct((1, HEAD_SLOT), F32), jax.ShapeDtypeStruct((1, POOL_DIM), F32)]
    scratch = []
    if share is not None:
        operands.append(share)
        in_specs.append(pl.BlockSpec(memory_space=pl.ANY))
        out_specs.append(pl.BlockSpec(memory_space=pl.ANY))
        out_shape.append(jax.ShapeDtypeStruct((2 * share.shape[0], share.shape[1]), share.dtype))
        scratch = [pltpu.SemaphoreType.DMA, pltpu.SemaphoreType.DMA, pltpu.SemaphoreType.DMA]
    return pl.pallas_call(
        body, name="mixa_pre_bwd" if share is None else "mixa_pre_bwd_share", grid=(nt,),
        in_specs=in_specs, out_specs=out_specs, out_shape=out_shape, scratch_shapes=scratch,
        compiler_params=_params("arbitrary"),
    )(*operands)


def _conv_taps(u_prev, u, cw_ref):
    ue = jnp.concatenate([u_prev, u], axis=0)
    u1 = pltpu.roll(ue, 1, 0)[HALO:, :]
    u2 = pltpu.roll(ue, 2, 0)[HALO:, :]
    return cw_ref[0:1, :] * u2 + cw_ref[1:2, :] * u1 + cw_ref[2:3, :] * u, u1, u2


def mixc_fwd(x, g, win, cw, wout):
    T, D = x.shape
    tm = _tile(T, TM_CONV_FWD, HALO)

    def body(x_ref, xp_ref, g_ref, win_ref, cw_ref, wout_ref, y_ref, z_ref):
        i = pl.program_id(0)
        xv = x_ref[...]
        xh, _ = _rms(xv)
        z = _dot((xh * g_ref[...]).astype(BF), win_ref[...])
        z_ref[...] = z.astype(BF)
        xph, _ = _rms(xp_ref[...])
        zp = _dot((xph * g_ref[...]).astype(BF), win_ref[:, D:])
        u_prev = zp[:, :D] * zp[:, D:] * jnp.where(i == 0, 0.0, 1.0)
        conv, _, _ = _conv_taps(u_prev, z[:, D:2 * D] * z[:, 2 * D:], cw_ref)
        y_ref[...] = xv + _dot((z[:, :D] * conv).astype(BF), wout_ref[...])

    tok = lambda w: pl.BlockSpec((tm, w), lambda i: (i, 0))
    whole = lambda arr: pl.BlockSpec(arr.shape, lambda i: (0,) * arr.ndim)
    return pl.pallas_call(
        body, name="mixc_fwd", grid=(T // tm,),
        in_specs=[tok(D), pl.BlockSpec((HALO, D), _prev_halo(tm)), whole(g), whole(win), whole(cw), whole(wout)],
        out_specs=[tok(D), tok(3 * D)],
        out_shape=[jax.ShapeDtypeStruct((T, D), F32), jax.ShapeDtypeStruct((T, 3 * D), BF)],
        compiler_params=_params("arbitrary"),
    )(x, x, g, win, cw, wout)


def mixc_bwd(x, dy, z, g, win_t, cw, wout_t):
    T, D = x.shape
    tm = _tile(T, TM_CONV_BWD, HALO)
    nt = T // tm

    def body(x_ref, dy_ref, dyn_ref, z_ref, zp_ref, zn_ref, g_ref, wint_ref, cw_ref, woutt_ref,
             dx_ref, hn_ref, dz_ref, v_ref, dcw_ref, dg_ref):
        i = pl.program_id(0)
        first = i == 0
        xh, r = _rms(x_ref[...])
        hn_ref[...] = (xh * g_ref[...]).astype(BF)
        zv = z_ref[...].astype(F32)
        gb, gc, hh = zv[:, :D], zv[:, D:2 * D], zv[:, 2 * D:]
        u = gc * hh
        zp = zp_ref[...].astype(F32)
        u_prev = zp[:, D:2 * D] * zp[:, 2 * D:] * jnp.where(first, 0.0, 1.0)
        conv, u1, u2 = _conv_taps(u_prev, u, cw_ref)
        v_ref[...] = (gb * conv).astype(BF)

        dv = _dot(dy_ref[...].astype(BF), woutt_ref[...])
        dconv = dv * gb
        dv_next = _dot(dyn_ref[...].astype(BF), woutt_ref[...])
        dconv_next = dv_next * zn_ref[:, :D].astype(F32) * jnp.where(i == nt - 1, 0.0, 1.0)
        de = jnp.concatenate([dconv, dconv_next], axis=0)
        n = tm + HALO
        du = (cw_ref[2:3, :] * dconv + cw_ref[1:2, :] * pltpu.roll(de, n - 1, 0)[:tm, :]
              + cw_ref[0:1, :] * pltpu.roll(de, n - 2, 0)[:tm, :])
        for tap, shifted in enumerate((u2, u1, u)):
            _accumulate(dcw_ref.at[tap:tap + 1, :], first, _colsum(dconv * shifted))
        dz = jnp.concatenate([dv * conv, du * hh, du * gc], axis=1).astype(BF)
        dz_ref[...] = dz
        dhn = _dot(dz, wint_ref[...])
        _accumulate(dg_ref, first, _colsum(dhn * xh))
        dx_ref[...] = dy_ref[...] + _rms_bwd(dhn * g_ref[...], xh, r)

    tok = lambda w: pl.BlockSpec((tm, w), lambda i: (i, 0))
    whole = lambda arr: pl.BlockSpec(arr.shape, lambda i: (0,) * arr.ndim)
    return pl.pallas_call(
        body, name="mixc_bwd", grid=(nt,),
        in_specs=[tok(D), tok(D), pl.BlockSpec((HALO, D), _next_halo(tm, T)), tok(3 * D),
                  pl.BlockSpec((HALO, 3 * D), _prev_halo(tm)), pl.BlockSpec((HALO, 3 * D), _next_halo(tm, T)),
                  whole(g), whole(win_t), whole(cw), whole(wout_t)],
        out_specs=[tok(D), tok(D), tok(3 * D), tok(D), pl.BlockSpec((3, D), lambda i: (0, 0)),
                   pl.BlockSpec((1, D), lambda i: (0, 0))],
        out_shape=[jax.ShapeDtypeStruct((T, D), F32), jax.ShapeDtypeStruct((T, D), BF),
                   jax.ShapeDtypeStruct((T, 3 * D), BF), jax.ShapeDtypeStruct((T, D), BF),
                   jax.ShapeDtypeStruct((3, D), F32), jax.ShapeDtypeStruct((1, D), F32)],
        compiler_params=_params("arbitrary"),
    )(x, dy, dy, z, z, z, g, win_t, cw, wout_t)


def loss_head(y, target):
    T, D = y.shape
    tm = _tile(T, TM_MM, 8)

    def body(y_ref, t_ref, sum_ref, dy_ref):
        err = y_ref[...] - t_ref[...]
        dy_ref[...] = err * (1.0 / D)
        part = jnp.sum(jnp.sum(err * err, axis=-1, keepdims=True) * (1.0 / D), axis=0, keepdims=True)
        _accumulate(sum_ref, pl.program_id(0) == 0, jnp.broadcast_to(part, sum_ref.shape))

    return pl.pallas_call(
        body, name="loss_head", grid=(T // tm,),
        in_specs=[pl.BlockSpec((tm, D), lambda i: (i, 0))] * 2,
        out_specs=[pl.BlockSpec((8, 128), lambda i: (0, 0)), pl.BlockSpec((tm, D), lambda i: (i, 0))],
        out_shape=[jax.ShapeDtypeStruct((8, 128), F32), jax.ShapeDtypeStruct((T, D), F32)],
        compiler_params=_params("arbitrary"),
    )(y, target)


def adamw(w, g, m, v):
    R, C = w.shape
    tr = _tile(R, TR_FLAT, 16)

    def body(w_ref, g_ref, m_ref, v_ref, g32_ref, d_ref, m2_ref, v2_ref):
        gv = g_ref[...].astype(F32)
        g32_ref[...] = gv
        m2 = ADAM_B1 * m_ref[...] + (1.0 - ADAM_B1) * gv
        v2 = ADAM_B2 * v_ref[...] + (1.0 - ADAM_B2) * (gv * gv)
        m2_ref[...] = m2
        v2_ref[...] = v2
        m_hat = m2 / (1.0 - ADAM_B1 ** ADAM_STEP)
        v_hat = v2 / (1.0 - ADAM_B2 ** ADAM_STEP)
        d_ref[...] = -ADAM_LR * (m_hat / (jnp.sqrt(v_hat) + ADAM_EPS) + ADAM_WD * w_ref[...])

    spec = pl.BlockSpec((tr, C), lambda i: (i, 0))
    return pl.pallas_call(
        body, name="adamw", grid=(R // tr,), in_specs=[spec] * 4, out_specs=[spec] * 4,
        out_shape=[jax.ShapeDtypeStruct((R, C), F32)] * 4,
        compiler_params=_params("arbitrary"),
    )(w, g, m, v)


def sum_slots(a, out_dtype):
    S, R, C = a.shape
    tr = _tile(R, TR_FLAT // 2, 16)

    def body(a_ref, o_ref):
        acc = a_ref[0].astype(F32)
        for s in range(1, S):
            acc = acc + a_ref[s].astype(F32)
        o_ref[...] = acc.astype(out_dtype)

    return pl.pallas_call(
        body, name="sum_slots", grid=(R // tr,),
        in_specs=[pl.BlockSpec((S, tr, C), lambda i: (0, i, 0))],
        out_specs=pl.BlockSpec((tr, C), lambda i: (i, 0)),
        out_shape=jax.ShapeDtypeStruct((R, C), out_dtype),
        compiler_params=_params("arbitrary"),
    )(a)


ANY = pl.BlockSpec(memory_space=pl.ANY)


def _place():
    return lax.axis_index("x"), lax.axis_index("y"), lax.axis_index("c")


class _LocalCopy:
    def __init__(self, src, dst, sem, rows):
        n = LOCAL_CHUNKS if rows % (16 * LOCAL_CHUNKS) == 0 else 1
        cr = rows // n
        self.parts = [pltpu.make_async_copy(src.at[pl.ds(q * cr, cr), :], dst.at[pl.ds(q * cr, cr), :], sem)
                      for q in range(n)]
        self.whole = pltpu.make_async_copy(src, dst, sem)

    def start(self):
        for part in self.parts:
            part.start()

    def wait(self):
        self.whole.wait()


class _HalvesExchange:
    def __init__(self, g_ref, land_ref, send_sems, recv_sems, local_sem):
        half = g_ref.shape[1] // 2
        x, y, c = _place()
        me = 4 * x + 2 * y + c
        self.peers = []
        for mask in range(1, 8):
            mx, my, mc = (mask >> 2) & 1, (mask >> 1) & 1, mask & 1
            self.peers.append(((1 - x) if mx else x, (1 - y) if my else y, (1 - c) if mc else c))

        def piece(px, py, pc):
            return g_ref.at[2 * px + py, pl.ds(pc * half, half), :]

        def copy(k, sender, to):
            return pltpu.make_async_remote_copy(
                src_ref=piece(*to), dst_ref=land_ref.at[sender], send_sem=send_sems.at[k], recv_sem=recv_sems.at[k],
                device_id=to, device_id_type=MESH)

        self.own = _LocalCopy(piece(x, y, c), land_ref.at[me], local_sem, half)
        self.sends = [copy(k, me, peer) for k, peer in enumerate(self.peers)]
        self.arrivals = [copy(k, 4 * px + 2 * py + pc, (x, y, c)) for k, (px, py, pc) in enumerate(self.peers)]

    def start(self):
        self.own.start()
        for cp in self.sends:
            cp.start()

    def wait(self):
        for cp in self.arrivals:
            cp.wait_recv()
        for cp in self.sends:
            cp.wait_send()
        self.own.wait()


class _ShardGather:
    def __init__(self, w_ref, out_ref, send_sems, recv_sems, local_sem):
        R = w_ref.shape[0]
        half = R // 2
        x, y, c = _place()
        me, sibling = (x, y, c), (x, y, 1 - c)
        chips = [(1 - x, y), (x, 1 - y), (1 - x, 1 - y)]

        def rows(px, py, pc):
            return out_ref.at[2 * px + py, pl.ds(pc * half, half), :]

        def copy(k, block, to, src=None):
            return pltpu.make_async_remote_copy(
                src_ref=rows(*block) if src is None else src, dst_ref=rows(*block),
                send_sem=send_sems.at[k], recv_sem=recv_sems.at[k], device_id=to, device_id_type=MESH)

        self.mine = _LocalCopy(w_ref, out_ref.at[2 * x + y], local_sem, R)
        self.first = [copy(j, me, (*chip, c), src=w_ref.at[pl.ds(c * half, half), :]) for j, chip in enumerate(chips)]
        self.landed = [copy(j, (*chip, c), me) for j, chip in enumerate(chips)]
        self.passed = [copy(3 + j, (*chip, c), sibling) for j, chip in enumerate(chips)]
        self.from_sibling = [copy(3 + j, (*chip, 1 - c), me) for j, chip in enumerate(chips)]

    def start(self):
        self.mine.start()
        for cp in self.first:
            cp.start()

    def forward(self):
        for landed, onward in zip(self.landed, self.passed):
            landed.wait_recv()
            onward.start()

    def finish(self):
        for cp in self.from_sibling:
            cp.wait_recv()
        for cp in self.first + self.passed:
            cp.wait_send()
        self.mine.wait()


def allgather_shards(w):
    R, C = w.shape
    half = R // 2
    n = GATHER_CHUNKS if half % (16 * GATHER_CHUNKS) == 0 else 1
    cr = half // n

    def body(w_ref, out_ref, send_sems, recv_sems, local_sem):
        x, y, c = _place()
        sibling = (x, y, 1 - c)
        chips = [(1 - x, y), (x, 1 - y), (1 - x, 1 - y)]

        def rows(px, py, pc, q):
            return out_ref.at[2 * px + py, pl.ds(pc * half + q * cr, cr), :]

        def copy(k, block, q, to, src=None):
            return pltpu.make_async_remote_copy(
                src_ref=rows(*block, q) if src is None else src, dst_ref=rows(*block, q),
                send_sem=send_sems.at[k * n + q], recv_sem=recv_sems.at[k * n + q], device_id=to, device_id_type=MESH)

        mine = _LocalCopy(w_ref, out_ref.at[2 * x + y], local_sem, R)
        mine.start()
        first = [copy(j, (x, y, c), q, (*chip, c), src=w_ref.at[pl.ds(c * half + q * cr, cr), :])
                 for q in range(n) for j, chip in enumerate(chips)]
        for cp in first:
            cp.start()
        passed = []
        for q in range(n):
            for j, chip in enumerate(chips):
                copy(j, (*chip, c), q, (x, y, c)).wait_recv()
                passed.append(copy(3 + j, (*chip, c), q, sibling))
                passed[-1].start()
        for q in range(n):
            for j, chip in enumerate(chips):
                copy(3 + j, (*chip, 1 - c), q, (x, y, c)).wait_recv()
        for cp in first + passed:
            cp.wait_send()
        mine.wait()

    return pl.pallas_call(
        body, name="allgather_shards", in_specs=[ANY], out_specs=ANY,
        out_shape=jax.ShapeDtypeStruct((4, R, C), w.dtype),
        scratch_shapes=[pltpu.SemaphoreType.DMA((6 * n,)), pltpu.SemaphoreType.DMA((6 * n,)), pltpu.SemaphoreType.DMA],
    )(w)


def exchange_partials(grads, small):
    _, R, C = grads.shape
    half = R // 2
    Rs = small.shape[0]

    def body(g_ref, s_ref, land_ref, sland_ref, send_sems, recv_sems, local_sems):
        x, y, c = _place()
        me = 4 * x + 2 * y + c
        big = _HalvesExchange(g_ref, land_ref, send_sems, recv_sems, local_sems.at[0])

        def little(k, sender, to):
            return pltpu.make_async_remote_copy(
                src_ref=s_ref, dst_ref=sland_ref.at[sender], send_sem=send_sems.at[7 + k],
                recv_sem=recv_sems.at[7 + k], device_id=to, device_id_type=MESH)

        own_small = pltpu.make_async_copy(s_ref, sland_ref.at[me], local_sems.at[1])
        own_small.start()
        sends = [little(k, me, peer) for k, peer in enumerate(big.peers)]
        for cp in sends:
            cp.start()
        big.start()
        for k, (px, py, pc) in enumerate(big.peers):
            little(k, 4 * px + 2 * py + pc, (x, y, c)).wait_recv()
        big.wait()
        for cp in sends:
            cp.wait_send()
        own_small.wait()

    return pl.pallas_call(
        body, name="exchange_partials", in_specs=[ANY, ANY], out_specs=[ANY, ANY],
        out_shape=[jax.ShapeDtypeStruct((8, half, C), grads.dtype), jax.ShapeDtypeStruct((8, Rs, C), small.dtype)],
        scratch_shapes=[pltpu.SemaphoreType.DMA((14,)), pltpu.SemaphoreType.DMA((14,)), pltpu.SemaphoreType.DMA((2,))],
    )(grads, small)


class _SiblingShare:
    def __init__(self, p_ref, out_ref, send_sem, recv_sem, local_sem):
        half = p_ref.shape[0]
        x, y, c = _place()
        n = SIBLING_CHUNKS if half % (16 * SIBLING_CHUNKS) == 0 else 1
        cr = half // n

        def rows(pc):
            return out_ref.at[pl.ds(pc * half, half), :]

        self.own = _LocalCopy(p_ref, rows(c), local_sem, half)
        self.sends = [pltpu.make_async_remote_copy(
            src_ref=p_ref.at[pl.ds(q * cr, cr), :], dst_ref=out_ref.at[pl.ds(c * half + q * cr, cr), :],
            send_sem=send_sem, recv_sem=recv_sem, device_id=(x, y, 1 - c), device_id_type=MESH) for q in range(n)]
        self.everything = pltpu.make_async_remote_copy(
            src_ref=p_ref, dst_ref=rows(1 - c), send_sem=send_sem, recv_sem=recv_sem, device_id=(x, y, c),
            device_id_type=MESH)

    def start(self):
        self.own.start()
        for cp in self.sends:
            cp.start()

    def wait(self):
        self.everything.wait_recv()
        self.everything.wait_send()
        self.own.wait()


def share_with_sibling(parts):
    C = parts[0].shape[1]
    halves = [p.shape[0] for p in parts]
    offsets = [2 * sum(halves[:i]) for i in range(len(parts))]

    def body(*refs):
        p_refs, out_ref = refs[:len(parts)], refs[len(parts)]
        send_sems, recv_sems, local_sems = refs[len(parts) + 1:]
        x, y, c = _place()
        pending = []
        for i, (p_ref, half, off) in enumerate(zip(p_refs, halves, offsets)):
            def rows(pc, half=half, off=off):
                return out_ref.at[pl.ds(off + pc * half, half), :]

            own = _LocalCopy(p_ref, rows(c), local_sems.at[i], half)
            own.start()
            n = SIBLING_CHUNKS if half % (16 * SIBLING_CHUNKS) == 0 else 1
            cr = half // n
            for q in range(n):
                pltpu.make_async_remote_copy(
                    src_ref=p_ref.at[pl.ds(q * cr, cr), :], dst_ref=out_ref.at[pl.ds(off + c * half + q * cr, cr), :],
                    send_sem=send_sems.at[i], recv_sem=recv_sems.at[i], device_id=(x, y, 1 - c),
                    device_id_type=MESH).start()
            pending.append((own, pltpu.make_async_remote_copy(
                src_ref=p_ref, dst_ref=rows(1 - c), send_sem=send_sems.at[i], recv_sem=recv_sems.at[i],
                device_id=(x, y, c), device_id_type=MESH)))
        for own, everything in pending:
            everything.wait_recv()
            everything.wait_send()
            own.wait()

    k = len(parts)
    return pl.pallas_call(
        body, name="share_with_sibling", in_specs=[ANY] * k, out_specs=ANY,
        out_shape=jax.ShapeDtypeStruct((2 * sum(halves), C), parts[0].dtype),
        scratch_shapes=[pltpu.SemaphoreType.DMA((k,)), pltpu.SemaphoreType.DMA((k,)), pltpu.SemaphoreType.DMA((k,))],
    )(*parts)


FLAT_SEG = 16 * FLAT_COLS


def _seg_rows(n):
    return -(-n // FLAT_SEG) * 16


def _flat_rows(sizes):
    rows = sum(_seg_rows(n) for n in sizes)
    return -(-rows // FLAT_ROW_ALIGN) * FLAT_ROW_ALIGN


def pack_flat(groups, lead=()):
    parts = []
    for arrays in groups:
        sizes = [int(np.prod(a.shape[len(lead):])) for a in arrays]
        used = 0
        for a, n in zip(arrays, sizes):
            rows = _seg_rows(n)
            flat = jnp.pad(a.reshape(*lead, n), [(0, 0)] * len(lead) + [(0, rows * FLAT_COLS - n)])
            parts.append(flat.reshape(*lead, rows, FLAT_COLS))
            used += rows
        if _flat_rows(sizes) > used:
            parts.append(jnp.zeros((*lead, _flat_rows(sizes) - used, FLAT_COLS), arrays[0].dtype))
    return jnp.concatenate(parts, axis=len(lead))


def unpack_flat(flat, group_shapes, lead=()):
    out, r0 = [], 0
    for shapes in group_shapes:
        arrays, start = [], r0
        for shp in shapes:
            n = int(np.prod(shp))
            rows = _seg_rows(n)
            seg = flat[..., r0:r0 + rows, :].reshape(*lead, rows * FLAT_COLS)[..., :n]
            arrays.append(seg.reshape(*lead, *shp))
            r0 += rows
        r0 = start + _flat_rows([int(np.prod(shp)) for shp in shapes])
        out.append(arrays)
    return out


def _f32_bits_as(a, dtype):
    return lax.bitcast_convert_type(a, dtype).reshape(*a.shape[:-1], -1)


def _f32_from_bits(a):
    k = 4 // a.dtype.itemsize
    if k > 1:
        a = a.reshape(*a.shape[:-1], a.shape[-1] // k, k)
    return lax.bitcast_convert_type(a, F32)


def _join_shards(name, a):
    if name in COL_SHARDED:
        return jnp.transpose(a, (1, 2, 0, 3)).reshape(a.shape[1], a.shape[2], 4 * a.shape[3])
    return jnp.transpose(a, (1, 0, 2, 3)).reshape(a.shape[1], 4 * a.shape[2], a.shape[3])


def _split_shards(name, a):
    L, K, N = a.shape
    if name in COL_SHARDED:
        return jnp.transpose(a.reshape(L, K, 4, N // 4), (2, 0, 1, 3))
    return jnp.transpose(a.reshape(L, 4, K // 4, N), (1, 0, 2, 3))


def _first_layers(name):
    return 0 if name.startswith('c_') else 1


def _last_done(name):
    return 0 if name.startswith('ffn2_') else _first_layers(name)


def _pad_heads(a, width):
    a = a.reshape(*a.shape[:-1], HEADS, width)
    a = jnp.pad(a, [(0, 0)] * (a.ndim - 1) + [(0, HEAD_SLOT - width)])
    return a.reshape(*a.shape[:-2], HEADS * HEAD_SLOT)


def _unpad_heads(a, width):
    a = a.reshape(*a.shape[:-1], HEADS, HEAD_SLOT)[..., :width]
    return a.reshape(*a.shape[:-2], HEADS * width)


def _rope_tables(T):
    pos = jnp.arange(T, dtype=F32)
    inv_freq = ROPE_THETA ** (-jnp.arange(0, ROPE_DIM, 2, dtype=F32) / ROPE_DIM)
    ang = pos[:, None] * inv_freq[None, :]
    cos, sin = jnp.cos(ang), jnp.sin(ang)
    pad = HEAD_SLOT - QK_DIM
    cos_t = jnp.concatenate([jnp.ones((T, NOPE_DIM), F32), cos, cos, jnp.zeros((T, pad), F32)], axis=1)
    sin_t = jnp.concatenate([jnp.zeros((T, NOPE_DIM), F32), sin, sin, jnp.zeros((T, pad), F32)], axis=1)
    return cos_t, sin_t


def _even_weights(W, i):
    c3 = POOL_DIM + Q_RANK + KV_RANK
    w_in = W['a_w_in'][i]
    D = w_in.shape[0]
    rope_cols = jnp.concatenate([jnp.zeros((D, NOPE_DIM), BF), w_in[:, c3:], jnp.zeros((D, HEAD_SLOT - QK_DIM), BF)], axis=1)
    win = jnp.concatenate([w_in[:, :c3], rope_cols], axis=1)
    wq = _pad_heads(W['a_w_q_up'][i], QK_DIM)
    kv = W['a_w_kv_up'][i].reshape(KV_RANK, HEADS, NOPE_DIM + V_DIM)
    wkn = _pad_heads(kv[:, :, :NOPE_DIM].reshape(KV_RANK, HEADS * NOPE_DIM), NOPE_DIM)
    wv = _pad_heads(kv[:, :, NOPE_DIM:].reshape(KV_RANK, HEADS * V_DIM), V_DIM)
    w_out = W['a_w_out'][i]
    wo_pool = w_out[:POOL_DIM]
    wo_attn = _pad_heads(w_out[POOL_DIM:].T, V_DIM).T
    wpool = W['a_w_pool'][i]
    pad = lambda a: jnp.pad(a, (0, HEAD_SLOT - QK_DIM))[None, :]
    return dict(win=win, win_t=win.T, wq=wq, wq_t=wq.T, wkn=wkn, wkn_t=wkn.T, wv=wv, wv_t=wv.T,
                wo_pool=wo_pool, wo_pool_t=wo_pool.T, wo_attn=wo_attn, wo_attn_t=wo_attn.T,
                wpool=wpool, wpool_t=jnp.transpose(wpool, (0, 2, 1)),
                qan=W['a_q_a_norm'][i][None, :], kvan=W['a_kv_a_norm'][i][None, :],
                qhn=pad(W['a_q_head_norm'][i]), khn=pad(W['a_k_head_norm'][i]),
                pscale=W['a_pool_scale'][i][None, :], g=W['mix_norm'][2 * i][None, :])


def kernel(*args):
    p = dict(zip(INPUTS, args))
    x0 = p['x'][0]
    target = p['loss_target'][0]
    T, D = x0.shape

    def wire(n, a):
        return _f32_bits_as(a, BF) if n == 'c_conv_w' else a.astype(BF)

    first = [(n, wire(n, p[n][:_first_layers(n)])) for n in SHARDED if _first_layers(n) > 0]
    late = [(n, wire(n, p[n][_first_layers(n):])) for n in SHARDED if p[n].shape[0] > _first_layers(n)]
    W = {n: [] for n in SHARDED}

    def receive(gathered, group):
        shapes = [[a.shape for _, a in group]]
        for (n, _), got in zip(group, unpack_flat(gathered, shapes, lead=(4,))[0]):
            whole = _join_shards(n, _f32_from_bits(got) if n == 'c_conv_w' else got)
            W[n] += [whole[i] for i in range(whole.shape[0])]

    receive(allgather_shards(pack_flat([[a for _, a in first]])), first)
    late_shards = pack_flat([[a for _, a in late]])
    for n in REPLICATED:
        W[n] = p[n]
    W['a_w_pool'] = p['a_w_pool'].astype(BF)
    cos, sin = _rope_tables(T)

    def ffn_weights(pre, l):
        wg, wu, wd = W[pre + '_w_gate'][l], W[pre + '_w_up'][l], W[pre + '_w_down'][l]
        return dict(g=W[pre + '_norm'][l][None, :], wg=wg, wu=wu, wd=wd, wg_t=wg.T, wu_t=wu.T, wd_t=wd.T)

    saved = []
    x = x0
    for l in range(DEPTH):
        s = dict(x0=x)
        f1 = ffn_weights('ffn1', l)
        x, s['g1'], s['u1'] = ffn_fwd(x, f1['g'], f1['wg'], f1['wu'], f1['wd'])
        s['x1'] = x
        if l % 2 == 0:
            e = _even_weights(W, l // 2)
            s['q'], s['k'], s['v'], s['po'] = mixa_pre_fwd(
                x, e['g'], e['win'], e['qan'], e['wq'], e['kvan'], e['wkn'], e['wv'], e['qhn'], e['khn'],
                e['wpool'], e['pscale'], cos, sin)
            v_t = _blocks_transposed(s['v'], _tile(T, TQ_ATTN, 128))
            if l == 0:
                o_t, s['lse'], gathered_late = attn_fwd(s['q'], s['k'], v_t, gather=late_shards)
                receive(gathered_late, late)
            else:
                o_t, s['lse'] = attn_fwd(s['q'], s['k'], v_t)
            s['o'] = _blocks_untransposed(o_t)
            x = mm_multi([(s['po'], e['wo_pool']), (s['o'], e['wo_attn'])], res=x)
        else:
            i = l // 2
            x, s['z'] = mixc_fwd(x, W['mix_norm'][l][None, :], W['c_w_in'][i], W['c_conv_w'][i].astype(F32),
                                 W['c_w_out'][i])
        s['x2'] = x
        f2 = ffn_weights('ffn2', l)
        x, s['g2'], s['u2'] = ffn_fwd(x, f2['g'], f2['wg'], f2['wu'], f2['wd'])
        saved.append(s)

    loss_sum, dy = loss_head(x, target)
    loss = lax.psum(0.5 * loss_sum[0, 0], AXES)

    G = {n: [None] * p[n].shape[0] for n in WEIGHTS}

    def ffn_back(pre, l, x_in, gg, uu, dy):
        f = ffn_weights(pre, l)
        dx, n, dyh, h, dgate, dup, dgn = ffn_bwd(x_in, dy, f['g'], gg, uu, f['wd_t'], f['wg_t'], f['wu_t'])
        G[pre + '_norm'][l] = dgn[0]
        G[pre + '_w_gate'][l] = mm_tn(n, dgate)
        G[pre + '_w_up'][l] = mm_tn(n, dup)
        G[pre + '_w_down'][l] = mm_tn(dyh, h).T
        return dx

    def partial_shards(late):
        arrays = [_split_shards(n, jnp.stack(G[n][_last_done(n):] if late else G[n][:_last_done(n)]))
                  for n in SHARDED if (p[n].shape[0] > _last_done(n) if late else _last_done(n) > 0)]
        return pack_flat([arrays], lead=(4,)).astype(BF)

    t_attn = _tile(T, TQ_ATTN, 128)
    for l in reversed(range(DEPTH)):
        s = saved[l]
        dy = ffn_back('ffn2', l, s['x2'], s['g2'], s['u2'], dy)
        i = l // 2
        if l % 2 == 0:
            e = _even_weights(W, i)
            G['a_w_out'][i] = jnp.concatenate(
                [mm_tn(s['po'], dy), _unpad_heads(mm_tn(s['o'], dy).T, V_DIM).T], axis=0)
            dpo = mm_multi([(dy, e['wo_pool_t'])])
            do = mm_multi([(dy, e['wo_attn_t'])], out_dtype=BF)
            as_rows = lambda a: a.reshape(HEADS, T // t_attn, t_attn)
            attn_args = (s['q'], s['k'], s['k'].T, s['v'], do, as_rows(s['lse']), as_rows(attn_delta(s['o'], do)))
            mix_args = (e['g'], e['win'], e['win_t'], e['qan'], e['wq'], e['wq_t'], e['kvan'], e['wkn'], e['wkn_t'],
                        e['wv_t'], e['qhn'], e['khn'], e['wpool'], e['wpool_t'], e['pscale'], cos, sin)
            if l == 0:
                dk, dv, dq_t, land_late = attn_bwd(*attn_args, exchange=partial_shards(late=True))
                (dy, hn, dz, nq, dqraw, nkv, dkraw, pooled, dps, dg, dqan, dkvan, dqhn, dkhn, dpscale,
                 g_late) = mixa_pre_bwd(s['x1'], dy, _blocks_untransposed(dq_t), dk, dv, dpo, *mix_args,
                                        share=sum_slots(land_late, BF))
            else:
                dk, dv, dq_t = attn_bwd(*attn_args)
                (dy, hn, dz, nq, dqraw, nkv, dkraw, pooled, dps, dg, dqan, dkvan, dqhn, dkhn, dpscale) = mixa_pre_bwd(
                    s['x1'], dy, _blocks_untransposed(dq_t), dk, dv, dpo, *mix_args)
            c3 = POOL_DIM + Q_RANK + KV_RANK
            dwin = mm_tn(hn, dz)
            G['a_w_in'][i] = jnp.concatenate([dwin[:, :c3], dwin[:, c3 + NOPE_DIM:c3 + QK_DIM]], axis=1)
            G['a_w_q_up'][i] = _unpad_heads(mm_tn(nq, dqraw), QK_DIM)
            dwkn = _unpad_heads(mm_tn(nkv, dkraw), NOPE_DIM).reshape(KV_RANK, HEADS, NOPE_DIM)
            dwv = _unpad_heads(mm_tn(nkv, dv), V_DIM).reshape(KV_RANK, HEADS, V_DIM)
            G['a_w_kv_up'][i] = jnp.concatenate([dwkn, dwv], axis=2).reshape(KV_RANK, HEADS * (NOPE_DIM + V_DIM))
            dwp = mm_tn(pooled, dps)
            G['a_w_pool'][i] = jnp.stack([dwp[g * POOL_GROUP:(g + 1) * POOL_GROUP, g * POOL_GROUP:(g + 1) * POOL_GROUP]
                                          for g in range(len(POOL_WINDOWS))])
            G['mix_norm'][l] = dg[0]
            G['a_q_a_norm'][i] = dqan[0]
            G['a_kv_a_norm'][i] = dkvan[0]
            G['a_q_head_norm'][i] = dqhn[0, :QK_DIM]
            G['a_k_head_norm'][i] = dkhn[0, :QK_DIM]
            G['a_pool_scale'][i] = dpscale[0]
        else:
            w_in, w_out = W['c_w_in'][i], W['c_w_out'][i]
            dy_in = dy
            dy, hn, dz, gated, dcw, dg = mixc_bwd(s['x1'], dy, s['z'], W['mix_norm'][l][None, :], w_in.T,
                                                  W['c_conv_w'][i].astype(F32), w_out.T)
            G['c_w_in'][i] = mm_tn(hn, dz)
            G['c_w_out'][i] = mm_tn(gated, dy_in)
            G['c_conv_w'][i] = dcw
            G['mix_norm'][l] = dg[0]
        dy = ffn_back('ffn1', l, s['x0'], s['g1'], s['u1'], dy)
    grad_x = dy[None]

    partial_small = pack_flat([[jnp.stack(G[n]) for n in REPLICATED]])
    land_first, sland = exchange_partials(partial_shards(late=False), partial_small)
    g_big = jnp.concatenate([share_with_sibling([sum_slots(land_first, BF)]), g_late], axis=0)
    g_small = sum_slots(sland, F32)

    def layer_groups(pre):
        first = [p[pre + n][:_last_done(n)] for n in SHARDED if _last_done(n) > 0]
        late = [p[pre + n][_last_done(n):] for n in SHARDED if p[n].shape[0] > _last_done(n)]
        return [first, late]

    outs = {}
    g32, delta, m2, v2 = adamw(pack_flat(layer_groups('')), g_big, pack_flat(layer_groups('m_')),
                               pack_flat(layer_groups('v_')))
    group_shapes = [[a.shape for a in group] for group in layer_groups('')]
    for kind, arr in (('grad_', g32), ('delta_', delta), ('new_m_', m2), ('new_v_', v2)):
        first, late = (iter(group) for group in unpack_flat(arr, group_shapes))
        for n in SHARDED:
            pieces = ([next(first)] if _last_done(n) > 0 else []) + (
                [next(late)] if p[n].shape[0] > _last_done(n) else [])
            outs[kind + n] = pieces[0] if len(pieces) == 1 else jnp.concatenate(pieces, axis=0)
    small_groups = [[p[n].shape for n in REPLICATED]]
    flat = lambda pre: pack_flat([[p[pre + n] for n in REPLICATED]])
    g32, delta, m2, v2 = adamw(flat(''), g_small, flat('m_'), flat('v_'))
    for kind, arr in (('grad_', g32), ('delta_', delta), ('new_m_', m2), ('new_v_', v2)):
        for n, a in zip(REPLICATED, unpack_flat(arr, small_groups)[0]):
            outs[kind + n] = a
    return (loss, grad_x, *[outs[k + n] for k in ('grad_', 'delta_', 'new_m_', 'new_v_') for n in WEIGHTS])
```

```python
import functools

import numpy as np
import jax
import jax.numpy as jnp
from jax import lax
from jax.experimental import pallas as pl
from jax.experimental.pallas import tpu as pltpu

BF, F32 = jnp.bfloat16, jnp.float32
MESH = pl.DeviceIdType.MESH
AXES = ("x", "y", "c")

NORM_EPS = 1e-6
DEPTH = 4
HEADS = 8
HEAD_SLOT = 128
QK_DIM, NOPE_DIM, ROPE_DIM, V_DIM = 96, 64, 32, 64
POOL_WINDOWS = (2, 4, 8, 16)
POOL_DIM, POOL_GROUP = 512, 128
Q_RANK, KV_RANK = 384, 256
ROPE_THETA = 10000.0
HALO = 16
ATTN_SCALE = QK_DIM ** -0.5
LOG2_E = 1.4426950408889634

ADAM_LR, ADAM_B1, ADAM_B2, ADAM_EPS, ADAM_WD, ADAM_STEP = 0.001, 0.9, 0.999, 1e-08, 0.01, 10

TM_FFN_FWD, TM_FFN_BWD, TF_FFN = 512, 256, 256
TM_MIX_FWD, TM_MIX_BWD = 512, 256
TM_CONV_FWD, TM_CONV_BWD = 256, 256
TQ_ATTN = 512
TM_MM = 512
TK_TN, BM_TN, BN_TN = 2048, 1024, 1536
FLAT_COLS = 1024
FLAT_ROW_ALIGN = 1024
SIBLING_CHUNKS = 16
LOCAL_CHUNKS = 16
GATHER_CHUNKS = 8
TR_FLAT = 256
VMEM_LIMIT = 56 * 1024 * 1024

WEIGHTS = ['ffn1_norm', 'ffn1_w_gate', 'ffn1_w_up', 'ffn1_w_down', 'mix_norm', 'ffn2_norm', 'ffn2_w_gate',
           'ffn2_w_up', 'ffn2_w_down', 'a_w_in', 'a_q_a_norm', 'a_w_q_up', 'a_kv_a_norm', 'a_w_kv_up',
           'a_q_head_norm', 'a_k_head_norm', 'a_w_pool', 'a_pool_scale', 'a_w_out', 'c_w_in', 'c_conv_w',
           'c_w_out']
COL_SHARDED = ('ffn1_w_gate', 'ffn1_w_up', 'ffn2_w_gate', 'ffn2_w_up', 'a_w_in', 'a_w_q_up', 'a_w_kv_up',
               'c_w_in', 'c_conv_w')
ROW_SHARDED = ('ffn1_w_down', 'ffn2_w_down', 'a_w_out', 'c_w_out')
SHARDED = tuple(n for n in WEIGHTS if n in COL_SHARDED or n in ROW_SHARDED)
REPLICATED = tuple(n for n in WEIGHTS if n not in SHARDED)
INPUTS = ['x'] + WEIGHTS + ['loss_target'] + ['m_' + n for n in WEIGHTS] + ['v_' + n for n in WEIGHTS]


def _dot(a, b):
    return jnp.dot(a, b, preferred_element_type=F32)


def _dot_nt(a, b):
    return lax.dot_general(a, b, (((1,), (1,)), ((), ())), preferred_element_type=F32)


def _dot_tn(a, b):
    return lax.dot_general(a, b, (((0,), (0,)), ((), ())), preferred_element_type=F32)


def _params(*sem):
    return pltpu.CompilerParams(dimension_semantics=sem or None, vmem_limit_bytes=VMEM_LIMIT)


def _tile(n, cap, unit):
    if n <= cap:
        return n
    best = None
    for t in range(unit, cap + 1, unit):
        if n % t == 0:
            best = t
    assert best is not None, (n, cap, unit)
    return best


def _rms(x, width=None):
    ms = jnp.sum(x * x, axis=-1, keepdims=True) * (1.0 / (width or x.shape[-1]))
    r = lax.rsqrt(ms + NORM_EPS)
    return x * r, r


def _rms_bwd(a, xhat, r, width=None):
    return r * (a - xhat * (jnp.sum(a * xhat, axis=-1, keepdims=True) * (1.0 / (width or a.shape[-1]))))


def _colsum(a):
    return jnp.sum(a, axis=0, keepdims=True)


def _accumulate(ref, first, value):
    @pl.when(first)
    def _():
        ref[...] = value

    @pl.when(jnp.logical_not(first))
    def _():
        ref[...] += value


def _rot_half(v):
    lane = lax.broadcasted_iota(jnp.int32, v.shape, 1)
    rot = jnp.where(lane < NOPE_DIM + ROPE_DIM // 2, -pltpu.roll(v, HEAD_SLOT - ROPE_DIM // 2, 1),
                    pltpu.roll(v, ROPE_DIM // 2, 1))
    return jnp.where((lane >= NOPE_DIM) & (lane < QK_DIM), rot, 0.0)


def _rope(v, cos, sin):
    return v * cos + _rot_half(v) * sin


def _rope_bwd(d, cos, sin):
    return d * cos - _rot_half(d * sin)


def _resident(arr):
    return pl.BlockSpec(arr.shape, lambda i: (0,) * arr.ndim, pipeline_mode=pl.Buffered(1))


def ffn_fwd(x, g, wg, wu, wd):
    T, D = x.shape
    F = wg.shape[1]
    tm, tf = _tile(T, TM_FFN_FWD, 8), _tile(F, TF_FFN, 128)
    nf = F // tf

    def body(x_ref, g_ref, wg_ref, wu_ref, wd_ref, y_ref, gg_ref, uu_ref, h_sc):
        xv = x_ref[...]
        xh, _ = _rms(xv)
        n = (xh * g_ref[...]).astype(BF)

        def projections(c):
            cols = slice(c * tf, (c + 1) * tf)
            return _dot(n, wg_ref[:, cols]), _dot(n, wu_ref[:, cols])

        ahead = projections(0)
        for c in range(nf):
            gg, uu = ahead
            if c + 1 < nf:
                ahead = projections(c + 1)
            cols = slice(c * tf, (c + 1) * tf)
            gg_ref[:, cols] = gg.astype(BF)
            uu_ref[:, cols] = uu.astype(BF)
            h_sc[:, cols] = (gg * jax.nn.sigmoid(gg) * uu).astype(BF)
        y_ref[...] = xv + 0.5 * _dot(h_sc[...], wd_ref[...])

    tok = lambda w: pl.BlockSpec((tm, w), lambda i: (i, 0))
    return pl.pallas_call(
        body, name="ffn_fwd", grid=(T // tm,),
        in_specs=[tok(D), _resident(g), _resident(wg), _resident(wu), _resident(wd)],
        out_specs=[tok(D), tok(F), tok(F)],
        out_shape=[jax.ShapeDtypeStruct((T, D), F32), jax.ShapeDtypeStruct((T, F), BF),
                   jax.ShapeDtypeStruct((T, F), BF)],
        scratch_shapes=[pltpu.VMEM((tm, F), BF)],
        compiler_params=_params("arbitrary"),
    )(x, g, wg, wu, wd)


def ffn_bwd(x, dy, g, gg, uu, wd_t, wg_t, wu_t):
    T, D = x.shape
    F = gg.shape[1]
    tm, tf = _tile(T, TM_FFN_BWD, 8), _tile(F, TF_FFN, 128)
    nf = F // tf

    def body(x_ref, dy_ref, g_ref, gg_ref, uu_ref, wdt_ref, wgt_ref, wut_ref,
             dx_ref, n_ref, dyh_ref, h_ref, dg_ref, du_ref, dgn_ref):
        xh, r = _rms(x_ref[...])
        n_ref[...] = (xh * g_ref[...]).astype(BF)
        dyv = dy_ref[...]
        dyh = (0.5 * dyv).astype(BF)
        dyh_ref[...] = dyh

        def hidden_grad(c):
            return _dot(dyh, wdt_ref[:, c * tf:(c + 1) * tf])

        ahead = hidden_grad(0)
        for c in range(nf):
            dh = ahead
            if c + 1 < nf:
                ahead = hidden_grad(c + 1)
            cols = slice(c * tf, (c + 1) * tf)
            gv = gg_ref[:, cols].astype(F32)
            uv = uu_ref[:, cols].astype(F32)
            sg = jax.nn.sigmoid(gv)
            silu = gv * sg
            h_ref[:, cols] = (silu * uv).astype(BF)
            du_ref[:, cols] = (dh * silu).astype(BF)
            dg_ref[:, cols] = (dh * uv * (sg * (1.0 + gv * (1.0 - sg)))).astype(BF)
        dn = _dot(dg_ref[...], wgt_ref[...]) + _dot(du_ref[...], wut_ref[...])
        dx_ref[...] = dyv + _rms_bwd(dn * g_ref[...], xh, r)
        _accumulate(dgn_ref, pl.program_id(0) == 0, _colsum(dn * xh))

    tok = lambda w: pl.BlockSpec((tm, w), lambda i: (i, 0))
    return pl.pallas_call(
        body, name="ffn_bwd", grid=(T // tm,),
        in_specs=[tok(D), tok(D), _resident(g), tok(F), tok(F), _resident(wd_t), _resident(wg_t), _resident(wu_t)],
        out_specs=[tok(D), tok(D), tok(D), tok(F), tok(F), tok(F), pl.BlockSpec((1, D), lambda i: (0, 0))],
        out_shape=[jax.ShapeDtypeStruct((T, D), F32), jax.ShapeDtypeStruct((T, D), BF),
                   jax.ShapeDtypeStruct((T, D), BF), jax.ShapeDtypeStruct((T, F), BF),
                   jax.ShapeDtypeStruct((T, F), BF), jax.ShapeDtypeStruct((T, F), BF),
                   jax.ShapeDtypeStruct((1, D), F32)],
        compiler_params=_params("arbitrary"),
    )(x, dy, g, gg, uu, wd_t, wg_t, wu_t)


def mm_tn(a, b):
    T, M = a.shape
    N = b.shape[1]
    tk, bm, bn = _tile(T, TK_TN, 16), _tile(M, BM_TN, 128), _tile(N, BN_TN, 128)

    def body(a_ref, b_ref, o_ref):
        part = _dot_tn(a_ref[...].astype(BF), b_ref[...].astype(BF))
        _accumulate(o_ref, pl.program_id(2) == 0, part)

    return pl.pallas_call(
        body, name="mm_tn", grid=(M // bm, N // bn, T // tk),
        in_specs=[pl.BlockSpec((tk, bm), lambda i, j, k: (k, i)), pl.BlockSpec((tk, bn), lambda i, j, k: (k, j))],
        out_specs=pl.BlockSpec((bm, bn), lambda i, j, k: (i, j)),
        out_shape=jax.ShapeDtypeStruct((M, N), F32),
        compiler_params=_params("arbitrary", "arbitrary", "arbitrary"),
    )(a, b)


def mm_multi(pairs, res=None, out_dtype=F32):
    T = pairs[0][0].shape[0]
    N = pairs[0][1].shape[1]
    tm = _tile(T, TM_MM, 16)
    n = len(pairs)

    def body(*refs):
        o_ref = refs[-1]
        acc = refs[2 * n][...] if res is not None else None
        for k in range(n):
            part = _dot(refs[k][...].astype(BF), refs[n + k][...])
            acc = part if acc is None else acc + part
        o_ref[...] = acc.astype(out_dtype)

    ins = [a for a, _ in pairs] + [w for _, w in pairs]
    specs = [pl.BlockSpec((tm, a.shape[1]), lambda i: (i, 0)) for a, _ in pairs]
    specs += [pl.BlockSpec(w.shape, lambda i: (0, 0)) for _, w in pairs]
    if res is not None:
        ins.append(res)
        specs.append(pl.BlockSpec((tm, N), lambda i: (i, 0)))
    return pl.pallas_call(
        body, name="mm_multi", grid=(T // tm,), in_specs=specs,
        out_specs=pl.BlockSpec((tm, N), lambda i: (i, 0)),
        out_shape=jax.ShapeDtypeStruct((T, N), out_dtype),
        compiler_params=_params("arbitrary"),
    )(*ins)


def _causal_mask(t, q_major):
    r = lax.broadcasted_iota(jnp.int32, (t, t), 0)
    c = lax.broadcasted_iota(jnp.int32, (t, t), 1)
    return (c <= r) if q_major else (r <= c)


def _blocks_transposed(a, t):
    T = a.shape[0]
    return jnp.transpose(a.reshape(T // t, t, HEADS, HEAD_SLOT), (2, 0, 3, 1))


def _blocks_untransposed(a):
    H, n, d, t = a.shape
    return jnp.transpose(a, (1, 3, 0, 2)).reshape(n * t, H * d)


def attn_fwd(q, k, v_t, gather=None):
    T = q.shape[0]
    t = _tile(T, TQ_ATTN, 128)
    nq = T // t

    def body(q_ref, k_ref, vt_ref, *rest):
        if gather is None:
            ot_ref, lse_ref = rest
        else:
            w_ref, ot_ref, lse_ref, all_ref, send_sems, recv_sems, local_sem = rest
            step = pl.program_id(0) * nq + pl.program_id(1)
            moving = lambda: _ShardGather(w_ref, all_ref, send_sems, recv_sems, local_sem)
            pl.when(step == 0)(lambda: moving().start())
            pl.when(step == (HEADS // 2) * nq)(lambda: moving().forward())
        i = pl.program_id(1)
        qv = q_ref[...]

        def update(st, j, m, l, acc):
            m2 = jnp.maximum(m, jnp.max(st, axis=0, keepdims=True))
            pt = jnp.exp2(st - m2)
            scale = jnp.exp2(m - m2)
            return (m2, scale * l + jnp.sum(pt, axis=0, keepdims=True),
                    scale * acc + _dot(vt_ref[j], pt.astype(BF)))

        def scores(j, masked):
            st = _dot_nt(k_ref[pl.ds(pl.multiple_of(j * t, t), t), :], qv)
            return jnp.where(_causal_mask(t, False), st, -jnp.inf) if masked else st

        def pair(j, carry, last_is_diagonal):
            s0, s1 = scores(j, False), scores(j + 1, last_is_diagonal)
            return update(s1, j + 1, *update(s0, j, *carry))

        init = (jnp.full((1, t), -1e30, F32), jnp.zeros((1, t), F32), jnp.zeros((HEAD_SLOT, t), F32))
        carry = lax.fori_loop(0, i // 2, lambda jj, c: pair(2 * jj, c, False), init)
        m, l, acc = lax.cond(i % 2 == 1, lambda c: pair(i - 1, c, True),
                             lambda c: update(scores(i, True), i, *c), carry)
        ot_ref[...] = (acc / l).astype(BF)
        lse_ref[...] = m + jnp.log2(l)
        if gather is not None:
            pl.when(step == HEADS * nq - 1)(lambda: moving().finish())

    operands = [q, k, v_t]
    in_specs = [pl.BlockSpec((t, HEAD_SLOT), lambda h, i: (i, h)), pl.BlockSpec((T, HEAD_SLOT), lambda h, i: (0, h)),
                pl.BlockSpec((None, nq, HEAD_SLOT, t), lambda h, i: (h, 0, 0, 0))]
    out_specs = [pl.BlockSpec((None, None, HEAD_SLOT, t), lambda h, i: (h, i, 0, 0)),
                 pl.BlockSpec((None, None, 1, t), lambda h, i: (h, i, 0, 0))]
    out_shape = [jax.ShapeDtypeStruct((HEADS, nq, HEAD_SLOT, t), BF), jax.ShapeDtypeStruct((HEADS, nq, 1, t), F32)]
    scratch = []
    if gather is not None:
        operands.append(gather)
        in_specs.append(pl.BlockSpec(memory_space=pl.ANY))
        out_specs.append(pl.BlockSpec(memory_space=pl.ANY))
        out_shape.append(jax.ShapeDtypeStruct((4, *gather.shape), gather.dtype))
        scratch = [pltpu.SemaphoreType.DMA((6,)), pltpu.SemaphoreType.DMA((6,)), pltpu.SemaphoreType.DMA]
    return pl.pallas_call(
        body, name="attn_fwd" if gather is None else "attn_fwd_gather", grid=(HEADS, nq),
        in_specs=in_specs, out_specs=out_specs, out_shape=out_shape, scratch_shapes=scratch,
        compiler_params=_params("arbitrary", "arbitrary"),
    )(*operands)


def attn_delta(o, do):
    T = o.shape[0]
    t = _tile(T, TQ_ATTN, 128)

    def body(o_ref, do_ref, delta_ref):
        for h in range(HEADS):
            lanes = slice(h * HEAD_SLOT, (h + 1) * HEAD_SLOT)
            prod = do_ref[:, lanes].astype(F32) * o_ref[:, lanes].astype(F32)
            delta_ref[h] = jnp.sum(prod.T, axis=0, keepdims=True)

    blk = pl.BlockSpec((t, HEADS * HEAD_SLOT), lambda i: (i, 0))
    return pl.pallas_call(
        body, name="attn_delta", grid=(T // t,), in_specs=[blk, blk],
        out_specs=pl.BlockSpec((HEADS, None, 1, t), lambda i: (0, i, 0, 0)),
        out_shape=jax.ShapeDtypeStruct((HEADS, T // t, 1, t), F32),
        compiler_params=_params("arbitrary"),
    )(o, do)


def attn_bwd(q, k, k_t, v, do, lse_rows, delta_rows, exchange=None):
    T = q.shape[0]
    t = _tile(T, TQ_ATTN, 128)
    nq = T // t

    def body(q_ref, k_ref, kt_ref, v_ref, do_ref, lse_ref, delta_ref, *rest):
        if exchange is None:
            dk_ref, dv_ref, dqt_ref = rest
        else:
            g_ref, dk_ref, dv_ref, dqt_ref, land_ref, send_sems, recv_sems, local_sem = rest
            first = (pl.program_id(0) == 0) & (pl.program_id(1) == 0)
            last = (pl.program_id(0) == HEADS - 1) & (pl.program_id(1) == nq - 1)
            pl.when(first)(lambda: _HalvesExchange(g_ref, land_ref, send_sems, recv_sems, local_sem).start())
        j = pl.program_id(1)
        kv, vv, ktv = k_ref[...], v_ref[...], kt_ref[...]

        @pl.when(j == 0)
        def _():
            dqt_ref[...] = jnp.zeros_like(dqt_ref)

        def block(i):
            return pl.ds(pl.multiple_of(i * t, t), t)

        def scores(i, masked):
            st = _dot_nt(kv, q_ref[block(i), :])
            return jnp.where(_causal_mask(t, False), st, -jnp.inf) if masked else st

        def add(carry, st, i):
            dk, dv = carry
            qv, dov = q_ref[block(i), :], do_ref[block(i), :]
            pt = jnp.exp2(st - lse_ref[pl.ds(i, 1), :])
            dst = (pt * (_dot_nt(vv, dov) - delta_ref[pl.ds(i, 1), :])).astype(BF)
            dqt_ref[i] += _dot(ktv, dst)
            return dk + _dot(dst, qv), dv + _dot(pt.astype(BF), dov)

        def pair(i, carry):
            s0, s1 = scores(i, False), scores(i + 1, False)
            return add(add(carry, s0, i), s1, i + 1)

        zero = jnp.zeros((t, HEAD_SLOT), F32)
        carry = add((zero, zero), scores(j, True), j)
        rest = nq - 1 - j
        carry = lax.fori_loop(0, rest // 2, lambda ii, c: pair(j + 1 + 2 * ii, c), carry)
        dk, dv = lax.cond(rest % 2 == 1, lambda c: add(c, scores(nq - 1, False), nq - 1), lambda c: c, carry)
        dk_ref[...] = dk * (1.0 / LOG2_E)
        dv_ref[...] = dv
        if exchange is not None:
            pl.when(last)(lambda: _HalvesExchange(g_ref, land_ref, send_sems, recv_sems, local_sem).wait())

    blk = pl.BlockSpec((t, HEAD_SLOT), lambda h, j: (j, h))
    full = pl.BlockSpec((T, HEAD_SLOT), lambda h, j: (0, h))
    rows = pl.BlockSpec((None, nq, t), lambda h, j: (h, 0, 0))
    operands = [q, k, k_t, v, do, lse_rows, delta_rows]
    in_specs = [full, blk, pl.BlockSpec((HEAD_SLOT, t), lambda h, j: (h, j)), blk, full, rows, rows]
    out_specs = [blk, blk, pl.BlockSpec((None, nq, HEAD_SLOT, t), lambda h, j: (h, 0, 0, 0))]
    out_shape = [jax.ShapeDtypeStruct((T, HEADS * HEAD_SLOT), F32)] * 2 + [
        jax.ShapeDtypeStruct((HEADS, nq, HEAD_SLOT, t), F32)]
    scratch = []
    if exchange is not None:
        operands.append(exchange)
        in_specs.append(pl.BlockSpec(memory_space=pl.ANY))
        out_specs.append(pl.BlockSpec(memory_space=pl.ANY))
        out_shape.append(jax.ShapeDtypeStruct((8, exchange.shape[1] // 2, exchange.shape[2]), exchange.dtype))
        scratch = [pltpu.SemaphoreType.DMA((7,)), pltpu.SemaphoreType.DMA((7,)), pltpu.SemaphoreType.DMA]
    return pl.pallas_call(
        body, name="attn_bwd" if exchange is None else "attn_bwd_exchange", grid=(HEADS, nq),
        in_specs=in_specs, out_specs=out_specs, out_shape=out_shape, scratch_shapes=scratch,
        compiler_params=_params("arbitrary", "arbitrary"),
    )(*operands)


def _prev_halo(tm):
    return lambda i: (jnp.maximum(i * (tm // HALO) - 1, 0), 0)


def _next_halo(tm, T):
    return lambda i: (jnp.minimum((i + 1) * (tm // HALO), T // HALO - 1), 0)


def _inv_count(row0, n, w):
    t = row0 + lax.broadcasted_iota(jnp.int32, (n, 1), 0)
    return 1.0 / jnp.minimum(t + 1, w).astype(F32)


def _pool_fwd(u_prev, u, row0):
    tm = u.shape[0]
    out = []
    for g, w in enumerate(POOL_WINDOWS):
        lanes = slice(g * POOL_GROUP, (g + 1) * POOL_GROUP)
        ue = jnp.concatenate([u_prev[:, lanes], u[:, lanes]], axis=0)
        s, step = ue, 1
        while step < w:
            s = s + pltpu.roll(s, step, 0)
            step *= 2
        out.append(s[HALO:, :] * _inv_count(row0, tm, w) - u[:, lanes])
    return out


def _pool_bwd(dp, dp_next, row0):
    tm = dp[0].shape[0]
    out = []
    for g, w in enumerate(POOL_WINDOWS):
        e = jnp.concatenate([dp[g] * _inv_count(row0, tm, w), dp_next[g] * (1.0 / w)], axis=0)
        n = tm + HALO
        s, step = e, 1
        while step < w:
            s = s + pltpu.roll(s, n - step, 0)
            step *= 2
        out.append(s[:tm, :] - dp[g])
    return out


def _mixa_front(x, xp, first, row0, g_ref, win_ref, qan_ref, wq_ref, kvan_ref, wkn_ref):
    xh, r = _rms(x)
    hn = (xh * g_ref[...]).astype(BF)
    z = _dot(hn, win_ref[...])
    xph, _ = _rms(xp)
    u_prev = _dot((xph * g_ref[...]).astype(BF), win_ref[:, :POOL_DIM]) * jnp.where(first, 0.0, 1.0)
    u = z[:, :POOL_DIM]
    pooled = _pool_fwd(u_prev, u, row0)
    c1, c2 = POOL_DIM + Q_RANK, POOL_DIM + Q_RANK + KV_RANK
    qh, rq = _rms(z[:, POOL_DIM:c1])
    nq = (qh * qan_ref[...]).astype(BF)
    kh, rk = _rms(z[:, c1:c2])
    nkv = (kh * kvan_ref[...]).astype(BF)
    qraw = _dot(nq, wq_ref[...])
    kraw = _dot(nkv, wkn_ref[...])
    krope = z[:, c2:c2 + HEAD_SLOT]
    return dict(xh=xh, r=r, hn=hn, pooled=pooled, qh=qh, rq=rq, nq=nq, kh=kh, rk=rk, nkv=nkv,
                qraw=qraw, kraw=kraw, krope=krope)


def mixa_pre_fwd(x, g, win, qan, wq, kvan, wkn, wv, qhn, khn, wpool, pscale, cos, sin):
    T, D = x.shape
    tm = _tile(T, TM_MIX_FWD, HALO)
    HS = HEADS * HEAD_SLOT

    def body(x_ref, xp_ref, g_ref, win_ref, qan_ref, wq_ref, kvan_ref, wkn_ref, wv_ref, qhn_ref, khn_ref,
             wpool_ref, pscale_ref, cos_ref, sin_ref, q_ref, k_ref, v_ref, po_ref):
        i = pl.program_id(0)
        a = _mixa_front(x_ref[...], xp_ref[...], i == 0, i * tm, g_ref, win_ref, qan_ref, wq_ref, kvan_ref, wkn_ref)
        for gi in range(len(POOL_WINDOWS)):
            lanes = slice(gi * POOL_GROUP, (gi + 1) * POOL_GROUP)
            po = _dot(a["pooled"][gi].astype(BF), wpool_ref[gi]) * pscale_ref[:, lanes]
            po_ref[:, lanes] = po.astype(BF)
        cosv, sinv = cos_ref[...], sin_ref[...]
        v_ref[...] = _dot(a["nkv"], wv_ref[...]).astype(BF)
        for h in range(HEADS):
            lanes = slice(h * HEAD_SLOT, (h + 1) * HEAD_SLOT)
            qn, _ = _rms(a["qraw"][:, lanes], QK_DIM)
            q_ref[:, lanes] = (_rope(qn * qhn_ref[...], cosv, sinv) * (ATTN_SCALE * LOG2_E)).astype(BF)
            kn, _ = _rms(a["kraw"][:, lanes] + a["krope"], QK_DIM)
            k_ref[:, lanes] = _rope(kn * khn_ref[...], cosv, sinv).astype(BF)

    tok = lambda w: pl.BlockSpec((tm, w), lambda i: (i, 0))
    whole = lambda arr: pl.BlockSpec(arr.shape, lambda i: (0,) * arr.ndim)
    return pl.pallas_call(
        body, name="mixa_pre_fwd", grid=(T // tm,),
        in_specs=[tok(D), pl.BlockSpec((HALO, D), _prev_halo(tm))] + [whole(a) for a in
                  (g, win, qan, wq, kvan, wkn, wv, qhn, khn, wpool, pscale)] + [tok(HEAD_SLOT), tok(HEAD_SLOT)],
        out_specs=[tok(HS), tok(HS), tok(HS), tok(POOL_DIM)],
        out_shape=[jax.ShapeDtypeStruct((T, HS), BF)] * 3 + [jax.ShapeDtypeStruct((T, POOL_DIM), BF)],
        compiler_params=_params("arbitrary"),
    )(x, x, g, win, qan, wq, kvan, wkn, wv, qhn, khn, wpool, pscale, cos, sin)


def mixa_pre_bwd(x, dy, dq, dk, dv, dpo, g, win, win_t, qan, wq, wq_t, kvan, wkn, wkn_t, wv_t, qhn, khn,
                 wpool, wpool_t, pscale, cos, sin, share=None):
    T, D = x.shape
    tm = _tile(T, TM_MIX_BWD, HALO)
    HS = HEADS * HEAD_SLOT
    ZW = win.shape[1]
    nt = T // tm

    def body(*refs):
        (x_ref, xp_ref, dy_ref, dq_ref, dk_ref, dv_ref, dpo_ref, dpon_ref, g_ref, win_ref, wint_ref, qan_ref,
         wq_ref, wqt_ref, kvan_ref, wkn_ref, wknt_ref, wvt_ref, qhn_ref, khn_ref, wpool_ref, wpoolt_ref,
         pscale_ref, cos_ref, sin_ref) = refs[:25]
        refs = refs[25:]
        if share is not None:
            p_ref, refs = refs[0], refs[1:]
            moving = lambda: _SiblingShare(p_ref, *refs[15:])
        (dx_ref, hn_ref, dz_ref, nq_ref, dqraw_ref, nkv_ref, dkraw_ref, pooled_ref, dps_ref,
         dg_ref, dqan_ref, dkvan_ref, dqhn_ref, dkhn_ref, dpscale_ref) = refs[:15]
        i = pl.program_id(0)
        first = i == 0
        if share is not None:
            pl.when(first)(lambda: moving().start())
        a = _mixa_front(x_ref[...], xp_ref[...], first, i * tm, g_ref, win_ref, qan_ref, wq_ref, kvan_ref, wkn_ref)
        cosv, sinv = cos_ref[...], sin_ref[...]
        hn_ref[...] = a["hn"]
        nq_ref[...] = a["nq"]
        nkv_ref[...] = a["nkv"]

        has_next = jnp.where(i == nt - 1, 0.0, 1.0)
        dpool, dpool_next, dpscale = [], [], []
        for gi in range(len(POOL_WINDOWS)):
            lanes = slice(gi * POOL_GROUP, (gi + 1) * POOL_GROUP)
            pooled = a["pooled"][gi].astype(BF)
            pooled_ref[:, lanes] = pooled
            dpo_g = dpo_ref[:, lanes]
            dpscale.append(_colsum(dpo_g * _dot(pooled, wpool_ref[gi])))
            dps = (dpo_g * pscale_ref[:, lanes]).astype(BF)
            dps_ref[:, lanes] = dps
            dpool.append(_dot(dps, wpoolt_ref[gi]))
            dps_n = (dpon_ref[:, lanes] * pscale_ref[:, lanes] * has_next).astype(BF)
            dpool_next.append(_dot(dps_n, wpoolt_ref[gi]))
        du = jnp.concatenate(_pool_bwd(dpool, dpool_next, i * tm), axis=1)
        _accumulate(dpscale_ref, first, jnp.concatenate(dpscale, axis=1))

        dqhn = jnp.zeros((1, HEAD_SLOT), F32)
        dkhn = jnp.zeros((1, HEAD_SLOT), F32)
        dkrope = jnp.zeros((tm, HEAD_SLOT), F32)
        for h in range(HEADS):
            lanes = slice(h * HEAD_SLOT, (h + 1) * HEAD_SLOT)
            qhat, rq = _rms(a["qraw"][:, lanes], QK_DIM)
            dqn = _rope_bwd(dq_ref[:, lanes] * ATTN_SCALE, cosv, sinv)
            dqhn = dqhn + _colsum(dqn * qhat)
            dqraw_ref[:, lanes] = _rms_bwd(dqn * qhn_ref[...], qhat, rq, QK_DIM).astype(BF)
            khat, rk = _rms(a["kraw"][:, lanes] + a["krope"], QK_DIM)
            dkn = _rope_bwd(dk_ref[:, lanes], cosv, sinv)
            dkhn = dkhn + _colsum(dkn * khat)
            dkraw = _rms_bwd(dkn * khn_ref[...], khat, rk, QK_DIM)
            dkrope = dkrope + dkraw
            dkraw_ref[:, lanes] = dkraw.astype(BF)
        _accumulate(dqhn_ref, first, dqhn)
        _accumulate(dkhn_ref, first, dkhn)

        dnq = _dot(dqraw_ref[...], wqt_ref[...])
        _accumulate(dqan_ref, first, _colsum(dnq * a["qh"]))
        dql = _rms_bwd(dnq * qan_ref[...], a["qh"], a["rq"])
        dnkv = _dot(dkraw_ref[...], wknt_ref[...]) + _dot(dv_ref[...].astype(BF), wvt_ref[...])
        _accumulate(dkvan_ref, first, _colsum(dnkv * a["kh"]))
        dkvl = _rms_bwd(dnkv * kvan_ref[...], a["kh"], a["rk"])

        dz = jnp.concatenate([du, dql, dkvl, dkrope], axis=1).astype(BF)
        dz_ref[...] = dz
        dhn = _dot(dz, wint_ref[...])
        _accumulate(dg_ref, first, _colsum(dhn * a["xh"]))
        dx_ref[...] = dy_ref[...] + _rms_bwd(dhn * g_ref[...], a["xh"], a["r"])
        if share is not None:
            pl.when(i == nt - 1)(lambda: moving().wait())

    tok = lambda w: pl.BlockSpec((tm, w), lambda i: (i, 0))
    whole = lambda arr: pl.BlockSpec(arr.shape, lambda i: (0,) * arr.ndim)
    row = lambda w: pl.BlockSpec((1, w), lambda i: (0, 0))
    weights = (g, win, win_t, qan, wq, wq_t, kvan, wkn, wkn_t, wv_t, qhn, khn, wpool, wpool_t, pscale)
    operands = [x, x, dy, dq, dk, dv, dpo, dpo, *weights, cos, sin]
    in_specs = ([tok(D), pl.BlockSpec((HALO, D), _prev_halo(tm)), tok(D), tok(HS), tok(HS), tok(HS), tok(POOL_DIM),
                 pl.BlockSpec((HALO, POOL_DIM), _next_halo(tm, T))] + [whole(a) for a in weights]
                + [tok(HEAD_SLOT), tok(HEAD_SLOT)])
    out_specs = [tok(D), tok(D), tok(ZW), tok(Q_RANK), tok(HS), tok(KV_RANK), tok(HS), tok(POOL_DIM), tok(POOL_DIM),
                 row(D), row(Q_RANK), row(KV_RANK), row(HEAD_SLOT), row(HEAD_SLOT), row(POOL_DIM)]
    out_shape = [jax.ShapeDtypeStruct((T, D), F32), jax.ShapeDtypeStruct((T, D), BF),
                 jax.ShapeDtypeStruct((T, ZW), BF), jax.ShapeDtypeStruct((T, Q_RANK), BF),
                 jax.ShapeDtypeStruct((T, HS), BF), jax.ShapeDtypeStruct((T, KV_RANK), BF),
                 jax.ShapeDtypeStruct((T, HS), BF), jax.ShapeDtypeStruct((T, POOL_DIM), BF),
                 jax.ShapeDtypeStruct((T, POOL_DIM), BF),
                 jax.ShapeDtypeStruct((1, D), F32), jax.ShapeDtypeStruct((1, Q_RANK), F32),
                 jax.ShapeDtypeStruct((1, KV_RANK), F32), jax.ShapeDtypeStruct((1, HEAD_SLOT), F32),
                 jax.ShapeDtypeStruct((1, HEAD_SLOT), F32), jax.ShapeDtypeStruct((1, POOL_DIM), F32)]
    scratch = []
    if share is not None:
        operands.append(share)
        in_specs.append(pl.BlockSpec(memory_space=pl.ANY))
        out_specs.append(pl.BlockSpec(memory_space=pl.ANY))
        out_shape.append(jax.ShapeDtypeStruct((2 * share.shape[0], share.shape[1]), share.dtype))
        scratch = [pltpu.SemaphoreType.DMA, pltpu.SemaphoreType.DMA, pltpu.SemaphoreType.DMA]
    return pl.pallas_call(
        body, name="mixa_pre_bwd" if share is None else "mixa_pre_bwd_share", grid=(nt,),
        in_specs=in_specs, out_specs=out_specs, out_shape=out_shape, scratch_shapes=scratch,
        compiler_params=_params("arbitrary"),
    )(*operands)


def _conv_taps(u_prev, u, cw_ref):
    ue = jnp.concatenate([u_prev, u], axis=0)
    u1 = pltpu.roll(ue, 1, 0)[HALO:, :]
    u2 = pltpu.roll(ue, 2, 0)[HALO:, :]
    return cw_ref[0:1, :] * u2 + cw_ref[1:2, :] * u1 + cw_ref[2:3, :] * u, u1, u2


def mixc_fwd(x, g, win, cw, wout):
    T, D = x.shape
    tm = _tile(T, TM_CONV_FWD, HALO)

    def body(x_ref, xp_ref, g_ref, win_ref, cw_ref, wout_ref, y_ref, z_ref):
        i = pl.program_id(0)
        xv = x_ref[...]
        xh, _ = _rms(xv)
        z = _dot((xh * g_ref[...]).astype(BF), win_ref[...])
        z_ref[...] = z.astype(BF)
        xph, _ = _rms(xp_ref[...])
        zp = _dot((xph * g_ref[...]).astype(BF), win_ref[:, D:])
        u_prev = zp[:, :D] * zp[:, D:] * jnp.where(i == 0, 0.0, 1.0)
        conv, _, _ = _conv_taps(u_prev, z[:, D:2 * D] * z[:, 2 * D:], cw_ref)
        y_ref[...] = xv + _dot((z[:, :D] * conv).astype(BF), wout_ref[...])

    tok = lambda w: pl.BlockSpec((tm, w), lambda i: (i, 0))
    whole = lambda arr: pl.BlockSpec(arr.shape, lambda i: (0,) * arr.ndim)
    return pl.pallas_call(
        body, name="mixc_fwd", grid=(T // tm,),
        in_specs=[tok(D), pl.BlockSpec((HALO, D), _prev_halo(tm)), whole(g), whole(win), whole(cw), whole(wout)],
        out_specs=[tok(D), tok(3 * D)],
        out_shape=[jax.ShapeDtypeStruct((T, D), F32), jax.ShapeDtypeStruct((T, 3 * D), BF)],
        compiler_params=_params("arbitrary"),
    )(x, x, g, win, cw, wout)


def mixc_bwd(x, dy, z, g, win_t, cw, wout_t):
    T, D = x.shape
    tm = _tile(T, TM_CONV_BWD, HALO)
    nt = T // tm

    def body(x_ref, dy_ref, dyn_ref, z_ref, zp_ref, zn_ref, g_ref, wint_ref, cw_ref, woutt_ref,
             dx_ref, hn_ref, dz_ref, v_ref, dcw_ref, dg_ref):
        i = pl.program_id(0)
        first = i == 0
        xh, r = _rms(x_ref[...])
        hn_ref[...] = (xh * g_ref[...]).astype(BF)
        zv = z_ref[...].astype(F32)
        gb, gc, hh = zv[:, :D], zv[:, D:2 * D], zv[:, 2 * D:]
        u = gc * hh
        zp = zp_ref[...].astype(F32)
        u_prev = zp[:, D:2 * D] * zp[:, 2 * D:] * jnp.where(first, 0.0, 1.0)
        conv, u1, u2 = _conv_taps(u_prev, u, cw_ref)
        v_ref[...] = (gb * conv).astype(BF)

        dv = _dot(dy_ref[...].astype(BF), woutt_ref[...])
        dconv = dv * gb
        dv_next = _dot(dyn_ref[...].astype(BF), woutt_ref[...])
        dconv_next = dv_next * zn_ref[:, :D].astype(F32) * jnp.where(i == nt - 1, 0.0, 1.0)
        de = jnp.concatenate([dconv, dconv_next], axis=0)
        n = tm + HALO
        du = (cw_ref[2:3, :] * dconv + cw_ref[1:2, :] * pltpu.roll(de, n - 1, 0)[:tm, :]
              + cw_ref[0:1, :] * pltpu.roll(de, n - 2, 0)[:tm, :])
        for tap, shifted in enumerate((u2, u1, u)):
            _accumulate(dcw_ref.at[tap:tap + 1, :], first, _colsum(dconv * shifted))
        dz = jnp.concatenate([dv * conv, du * hh, du * gc], axis=1).astype(BF)
        dz_ref[...] = dz
        dhn = _dot(dz, wint_ref[...])
        _accumulate(dg_ref, first, _colsum(dhn * xh))
        dx_ref[...] = dy_ref[...] + _rms_bwd(dhn * g_ref[...], xh, r)

    tok = lambda w: pl.BlockSpec((tm, w), lambda i: (i, 0))
    whole = lambda arr: pl.BlockSpec(arr.shape, lambda i: (0,) * arr.ndim)
    return pl.pallas_call(
        body, name="mixc_bwd", grid=(nt,),
        in_specs=[tok(D), tok(D), pl.BlockSpec((HALO, D), _next_halo(tm, T)), tok(3 * D),
                  pl.BlockSpec((HALO, 3 * D), _prev_halo(tm)), pl.BlockSpec((HALO, 3 * D), _next_halo(tm, T)),
                  whole(g), whole(win_t), whole(cw), whole(wout_t)],
        out_specs=[tok(D), tok(D), tok(3 * D), tok(D), pl.BlockSpec((3, D), lambda i: (0, 0)),
                   pl.BlockSpec((1, D), lambda i: (0, 0))],
        out_shape=[jax.ShapeDtypeStruct((T, D), F32), jax.ShapeDtypeStruct((T, D), BF),
                   jax.ShapeDtypeStruct((T, 3 * D), BF), jax.ShapeDtypeStruct((T, D), BF),
                   jax.ShapeDtypeStruct((3, D), F32), jax.ShapeDtypeStruct((1, D), F32)],
        compiler_params=_params("arbitrary"),
    )(x, dy, dy, z, z, z, g, win_t, cw, wout_t)


def loss_head(y, target):
    T, D = y.shape
    tm = _tile(T, TM_MM, 8)

    def body(y_ref, t_ref, sum_ref, dy_ref):
        err = y_ref[...] - t_ref[...]
        dy_ref[...] = err * (1.0 / D)
        part = jnp.sum(jnp.sum(err * err, axis=-1, keepdims=True) * (1.0 / D), axis=0, keepdims=True)
        _accumulate(sum_ref, pl.program_id(0) == 0, jnp.broadcast_to(part, sum_ref.shape))

    return pl.pallas_call(
        body, name="loss_head", grid=(T // tm,),
        in_specs=[pl.BlockSpec((tm, D), lambda i: (i, 0))] * 2,
        out_specs=[pl.BlockSpec((8, 128), lambda i: (0, 0)), pl.BlockSpec((tm, D), lambda i: (i, 0))],
        out_shape=[jax.ShapeDtypeStruct((8, 128), F32), jax.ShapeDtypeStruct((T, D), F32)],
        compiler_params=_params("arbitrary"),
    )(y, target)


def adamw(w, g, m, v):
    R, C = w.shape
    tr = _tile(R, TR_FLAT, 16)

    def body(w_ref, g_ref, m_ref, v_ref, g32_ref, d_ref, m2_ref, v2_ref):
        gv = g_ref[...].astype(F32)
        g32_ref[...] = gv
        m2 = ADAM_B1 * m_ref[...] + (1.0 - ADAM_B1) * gv
        v2 = ADAM_B2 * v_ref[...] + (1.0 - ADAM_B2) * (gv * gv)
        m2_ref[...] = m2
        v2_ref[...] = v2
        m_hat = m2 / (1.0 - ADAM_B1 ** ADAM_STEP)
        v_hat = v2 / (1.0 - ADAM_B2 ** ADAM_STEP)
        d_ref[...] = -ADAM_LR * (m_hat / (jnp.sqrt(v_hat) + ADAM_EPS) + ADAM_WD * w_ref[...])

    spec = pl.BlockSpec((tr, C), lambda i: (i, 0))
    return pl.pallas_call(
        body, name="adamw", grid=(R // tr,), in_specs=[spec] * 4, out_specs=[spec] * 4,
        out_shape=[jax.ShapeDtypeStruct((R, C), F32)] * 4,
        compiler_params=_params("arbitrary"),
    )(w, g, m, v)


def sum_slots(a, out_dtype):
    S, R, C = a.shape
    tr = _tile(R, TR_FLAT // 2, 16)

    def body(a_ref, o_ref):
        acc = a_ref[0].astype(F32)
        for s in range(1, S):
            acc = acc + a_ref[s].astype(F32)
        o_ref[...] = acc.astype(out_dtype)

    return pl.pallas_call(
        body, name="sum_slots", grid=(R // tr,),
        in_specs=[pl.BlockSpec((S, tr, C), lambda i: (0, i, 0))],
        out_specs=pl.BlockSpec((tr, C), lambda i: (i, 0)),
        out_shape=jax.ShapeDtypeStruct((R, C), out_dtype),
        compiler_params=_params("arbitrary"),
    )(a)


ANY = pl.BlockSpec(memory_space=pl.ANY)


def _place():
    return lax.axis_index("x"), lax.axis_index("y"), lax.axis_index("c")


class _LocalCopy:
    def __init__(self, src, dst, sem, rows):
        n = LOCAL_CHUNKS if rows % (16 * LOCAL_CHUNKS) == 0 else 1
        cr = rows // n
        self.parts = [pltpu.make_async_copy(src.at[pl.ds(q * cr, cr), :], dst.at[pl.ds(q * cr, cr), :], sem)
                      for q in range(n)]
        self.whole = pltpu.make_async_copy(src, dst, sem)

    def start(self):
        for part in self.parts:
            part.start()

    def wait(self):
        self.whole.wait()


class _HalvesExchange:
    def __init__(self, g_ref, land_ref, send_sems, recv_sems, local_sem):
        half = g_ref.shape[1] // 2
        x, y, c = _place()
        me = 4 * x + 2 * y + c
        self.peers = []
        for mask in range(1, 8):
            mx, my, mc = (mask >> 2) & 1, (mask >> 1) & 1, mask & 1
            self.peers.append(((1 - x) if mx else x, (1 - y) if my else y, (1 - c) if mc else c))

        def piece(px, py, pc):
            return g_ref.at[2 * px + py, pl.ds(pc * half, half), :]

        def copy(k, sender, to):
            return pltpu.make_async_remote_copy(
                src_ref=piece(*to), dst_ref=land_ref.at[sender], send_sem=send_sems.at[k], recv_sem=recv_sems.at[k],
                device_id=to, device_id_type=MESH)

        self.own = _LocalCopy(piece(x, y, c), land_ref.at[me], local_sem, half)
        self.sends = [copy(k, me, peer) for k, peer in enumerate(self.peers)]
        self.arrivals = [copy(k, 4 * px + 2 * py + pc, (x, y, c)) for k, (px, py, pc) in enumerate(self.peers)]

    def start(self):
        self.own.start()
        for cp in self.sends:
            cp.start()

    def wait(self):
        for cp in self.arrivals:
            cp.wait_recv()
        for cp in self.sends:
            cp.wait_send()
        self.own.wait()


class _ShardGather:
    def __init__(self, w_ref, out_ref, send_sems, recv_sems, local_sem):
        R = w_ref.shape[0]
        half = R // 2
        x, y, c = _place()
        me, sibling = (x, y, c), (x, y, 1 - c)
        chips = [(1 - x, y), (x, 1 - y), (1 - x, 1 - y)]

        def rows(px, py, pc):
            return out_ref.at[2 * px + py, pl.ds(pc * half, half), :]

        def copy(k, block, to, src=None):
            return pltpu.make_async_remote_copy(
                src_ref=rows(*block) if src is None else src, dst_ref=rows(*block),
                send_sem=send_sems.at[k], recv_sem=recv_sems.at[k], device_id=to, device_id_type=MESH)

        self.mine = _LocalCopy(w_ref, out_ref.at[2 * x + y], local_sem, R)
        self.first = [copy(j, me, (*chip, c), src=w_ref.at[pl.ds(c * half, half), :]) for j, chip in enumerate(chips)]
        self.landed = [copy(j, (*chip, c), me) for j, chip in enumerate(chips)]
        self.passed = [copy(3 + j, (*chip, c), sibling) for j, chip in enumerate(chips)]
        self.from_sibling = [copy(3 + j, (*chip, 1 - c), me) for j, chip in enumerate(chips)]

    def start(self):
        self.mine.start()
        for cp in self.first:
            cp.start()

    def forward(self):
        for landed, onward in zip(self.landed, self.passed):
            landed.wait_recv()
            onward.start()

    def finish(self):
        for cp in self.from_sibling:
            cp.wait_recv()
        for cp in self.first + self.passed:
            cp.wait_send()
        self.mine.wait()


def allgather_shards(w):
    R, C = w.shape
    half = R // 2
    n = GATHER_CHUNKS if half % (16 * GATHER_CHUNKS) == 0 else 1
    cr = half // n

    def body(w_ref, out_ref, send_sems, recv_sems, local_sem):
        x, y, c = _place()
        sibling = (x, y, 1 - c)
        chips = [(1 - x, y), (x, 1 - y), (1 - x, 1 - y)]

        def rows(px, py, pc, q):
            return out_ref.at[2 * px + py, pl.ds(pc * half + q * cr, cr), :]

        def copy(k, block, q, to, src=None):
            return pltpu.make_async_remote_copy(
                src_ref=rows(*block, q) if src is None else src, dst_ref=rows(*block, q),
                send_sem=send_sems.at[k * n + q], recv_sem=recv_sems.at[k * n + q], device_id=to, device_id_type=MESH)

        mine = _LocalCopy(w_ref, out_ref.at[2 * x + y], local_sem, R)
        mine.start()
        first = [copy(j, (x, y, c), q, (*chip, c), src=w_ref.at[pl.ds(c * half + q * cr, cr), :])
                 for q in range(n) for j, chip in enumerate(chips)]
        for cp in first:
            cp.start()
        passed = []
        for q in range(n):
            for j, chip in enumerate(chips):
                copy(j, (*chip, c), q, (x, y, c)).wait_recv()
                passed.append(copy(3 + j, (*chip, c), q, sibling))
                passed[-1].start()
        for q in range(n):
            for j, chip in enumerate(chips):
                copy(3 + j, (*chip, 1 - c), q, (x, y, c)).wait_recv()
        for cp in first + passed:
            cp.wait_send()
        mine.wait()

    return pl.pallas_call(
        body, name="allgather_shards", in_specs=[ANY], out_specs=ANY,
        out_shape=jax.ShapeDtypeStruct((4, R, C), w.dtype),
        scratch_shapes=[pltpu.SemaphoreType.DMA((6 * n,)), pltpu.SemaphoreType.DMA((6 * n,)), pltpu.SemaphoreType.DMA],
    )(w)


def exchange_partials(grads, small):
    _, R, C = grads.shape
    half = R // 2
    Rs = small.shape[0]

    def body(g_ref, s_ref, land_ref, sland_ref, send_sems, recv_sems, local_sems):
        x, y, c = _place()
        me = 4 * x + 2 * y + c
        big = _HalvesExchange(g_ref, land_ref, send_sems, recv_sems, local_sems.at[0])

        def little(k, sender, to):
            return pltpu.make_async_remote_copy(
                src_ref=s_ref, dst_ref=sland_ref.at[sender], send_sem=send_sems.at[7 + k],
                recv_sem=recv_sems.at[7 + k], device_id=to, device_id_type=MESH)

        own_small = pltpu.make_async_copy(s_ref, sland_ref.at[me], local_sems.at[1])
        own_small.start()
        sends = [little(k, me, peer) for k, peer in enumerate(big.peers)]
        for cp in sends:
            cp.start()
        big.start()
        for k, (px, py, pc) in enumerate(big.peers):
            little(k, 4 * px + 2 * py + pc, (x, y, c)).wait_recv()
        big.wait()
        for cp in sends:
            cp.wait_send()
        own_small.wait()

    return pl.pallas_call(
        body, name="exchange_partials", in_specs=[ANY, ANY], out_specs=[ANY, ANY],
        out_shape=[jax.ShapeDtypeStruct((8, half, C), grads.dtype), jax.ShapeDtypeStruct((8, Rs, C), small.dtype)],
        scratch_shapes=[pltpu.SemaphoreType.DMA((14,)), pltpu.SemaphoreType.DMA((14,)), pltpu.SemaphoreType.DMA((2,))],
    )(grads, small)


class _SiblingShare:
    def __init__(self, p_ref, out_ref, send_sem, recv_sem, local_sem):
        half = p_ref.shape[0]
        x, y, c = _place()
        n = SIBLING_CHUNKS if half % (16 * SIBLING_CHUNKS) == 0 else 1
        cr = half // n

        def rows(pc):
            return out_ref.at[pl.ds(pc * half, half), :]

        self.own = _LocalCopy(p_ref, rows(c), local_sem, half)
        self.sends = [pltpu.make_async_remote_copy(
            src_ref=p_ref.at[pl.ds(q * cr, cr), :], dst_ref=out_ref.at[pl.ds(c * half + q * cr, cr), :],
            send_sem=send_sem, recv_sem=recv_sem, device_id=(x, y, 1 - c), device_id_type=MESH) for q in range(n)]
        self.everything = pltpu.make_async_remote_copy(
            src_ref=p_ref, dst_ref=rows(1 - c), send_sem=send_sem, recv_sem=recv_sem, device_id=(x, y, c),
            device_id_type=MESH)

    def start(self):
        self.own.start()
        for cp in self.sends:
            cp.start()

    def wait(self):
        self.everything.wait_recv()
        self.everything.wait_send()
        self.own.wait()


def share_with_sibling(parts):
    C = parts[0].shape[1]
    halves = [p.shape[0] for p in parts]
    offsets = [2 * sum(halves[:i]) for i in range(len(parts))]

    def body(*refs):
        p_refs, out_ref = refs[:len(parts)], refs[len(parts)]
        send_sems, recv_sems, local_sems = refs[len(parts) + 1:]
        x, y, c = _place()
        pending = []
        for i, (p_ref, half, off) in enumerate(zip(p_refs, halves, offsets)):
            def rows(pc, half=half, off=off):
                return out_ref.at[pl.ds(off + pc * half, half), :]

            own = _LocalCopy(p_ref, rows(c), local_sems.at[i], half)
            own.start()
            n = SIBLING_CHUNKS if half % (16 * SIBLING_CHUNKS) == 0 else 1
            cr = half // n
            for q in range(n):
                pltpu.make_async_remote_copy(
                    src_ref=p_ref.at[pl.ds(q * cr, cr), :], dst_ref=out_ref.at[pl.ds(off + c * half + q * cr, cr), :],
                    send_sem=send_sems.at[i], recv_sem=recv_sems.at[i], device_id=(x, y, 1 - c),
                    device_id_type=MESH).start()
            pending.append((own, pltpu.make_async_remote_copy(
                src_ref=p_ref, dst_ref=rows(1 - c), send_sem=send_sems.at[i], recv_sem=recv_sems.at[i],
                device_id=(x, y, c), device_id_type=MESH)))
        for own, everything in pending:
            everything.wait_recv()
            everything.wait_send()
            own.wait()

    k = len(parts)
    return pl.pallas_call(
        body, name="share_with_sibling", in_specs=[ANY] * k, out_specs=ANY,
        out_shape=jax.ShapeDtypeStruct((2 * sum(halves), C), parts[0].dtype),
        scratch_shapes=[pltpu.SemaphoreType.DMA((k,)), pltpu.SemaphoreType.DMA((k,)), pltpu.SemaphoreType.DMA((k,))],
    )(*parts)


FLAT_SEG = 16 * FLAT_COLS


def _seg_rows(n):
    return -(-n // FLAT_SEG) * 16


def _flat_rows(sizes):
    rows = sum(_seg_rows(n) for n in sizes)
    return -(-rows // FLAT_ROW_ALIGN) * FLAT_ROW_ALIGN


def pack_flat(groups, lead=()):
    parts = []
    for arrays in groups:
        sizes = [int(np.prod(a.shape[len(lead):])) for a in arrays]
        used = 0
        for a, n in zip(arrays, sizes):
            rows = _seg_rows(n)
            flat = jnp.pad(a.reshape(*lead, n), [(0, 0)] * len(lead) + [(0, rows * FLAT_COLS - n)])
            parts.append(flat.reshape(*lead, rows, FLAT_COLS))
            used += rows
        if _flat_rows(sizes) > used:
            parts.append(jnp.zeros((*lead, _flat_rows(sizes) - used, FLAT_COLS), arrays[0].dtype))
    return jnp.concatenate(parts, axis=len(lead))


def unpack_flat(flat, group_shapes, lead=()):
    out, r0 = [], 0
    for shapes in group_shapes:
        arrays, start = [], r0
        for shp in shapes:
            n = int(np.prod(shp))
            rows = _seg_rows(n)
            seg = flat[..., r0:r0 + rows, :].reshape(*lead, rows * FLAT_COLS)[..., :n]
            arrays.append(seg.reshape(*lead, *shp))
            r0 += rows
        r0 = start + _flat_rows([int(np.prod(shp)) for shp in shapes])
        out.append(arrays)
    return out


def _f32_bits_as(a, dtype):
    return lax.bitcast_convert_type(a, dtype).reshape(*a.shape[:-1], -1)


def _f32_from_bits(a):
    k = 4 // a.dtype.itemsize
    if k > 1:
        a = a.reshape(*a.shape[:-1], a.shape[-1] // k, k)
    return lax.bitcast_convert_type(a, F32)


def _join_shards(name, a):
    if name in COL_SHARDED:
        return jnp.transpose(a, (1, 2, 0, 3)).reshape(a.shape[1], a.shape[2], 4 * a.shape[3])
    return jnp.transpose(a, (1, 0, 2, 3)).reshape(a.shape[1], 4 * a.shape[2], a.shape[3])


def _split_shards(name, a):
    L, K, N = a.shape
    if name in COL_SHARDED:
        return jnp.transpose(a.reshape(L, K, 4, N // 4), (2, 0, 1, 3))
    return jnp.transpose(a.reshape(L, 4, K // 4, N), (1, 0, 2, 3))


def _first_layers(name):
    return 0 if name.startswith('c_') else 1


def _last_done(name):
    return 0 if name.startswith('ffn2_') else _first_layers(name)


def _pad_heads(a, width):
    a = a.reshape(*a.shape[:-1], HEADS, width)
    a = jnp.pad(a, [(0, 0)] * (a.ndim - 1) + [(0, HEAD_SLOT - width)])
    return a.reshape(*a.shape[:-2], HEADS * HEAD_SLOT)


def _unpad_heads(a, width):
    a = a.reshape(*a.shape[:-1], HEADS, HEAD_SLOT)[..., :width]
    return a.reshape(*a.shape[:-2], HEADS * width)


def _rope_tables(T):
    pos = jnp.arange(T, dtype=F32)
    inv_freq = ROPE_THETA ** (-jnp.arange(0, ROPE_DIM, 2, dtype=F32) / ROPE_DIM)
    ang = pos[:, None] * inv_freq[None, :]
    cos, sin = jnp.cos(ang), jnp.sin(ang)
    pad = HEAD_SLOT - QK_DIM
    cos_t = jnp.concatenate([jnp.ones((T, NOPE_DIM), F32), cos, cos, jnp.zeros((T, pad), F32)], axis=1)
    sin_t = jnp.concatenate([jnp.zeros((T, NOPE_DIM), F32), sin, sin, jnp.zeros((T, pad), F32)], axis=1)
    return cos_t, sin_t


def _even_weights(W, i):
    c3 = POOL_DIM + Q_RANK + KV_RANK
    w_in = W['a_w_in'][i]
    D = w_in.shape[0]
    rope_cols = jnp.concatenate([jnp.zeros((D, NOPE_DIM), BF), w_in[:, c3:], jnp.zeros((D, HEAD_SLOT - QK_DIM), BF)], axis=1)
    win = jnp.concatenate([w_in[:, :c3], rope_cols], axis=1)
    wq = _pad_heads(W['a_w_q_up'][i], QK_DIM)
    kv = W['a_w_kv_up'][i].reshape(KV_RANK, HEADS, NOPE_DIM + V_DIM)
    wkn = _pad_heads(kv[:, :, :NOPE_DIM].reshape(KV_RANK, HEADS * NOPE_DIM), NOPE_DIM)
    wv = _pad_heads(kv[:, :, NOPE_DIM:].reshape(KV_RANK, HEADS * V_DIM), V_DIM)
    w_out = W['a_w_out'][i]
    wo_pool = w_out[:POOL_DIM]
    wo_attn = _pad_heads(w_out[POOL_DIM:].T, V_DIM).T
    wpool = W['a_w_pool'][i]
    pad = lambda a: jnp.pad(a, (0, HEAD_SLOT - QK_DIM))[None, :]
    return dict(win=win, win_t=win.T, wq=wq, wq_t=wq.T, wkn=wkn, wkn_t=wkn.T, wv=wv, wv_t=wv.T,
                wo_pool=wo_pool, wo_pool_t=wo_pool.T, wo_attn=wo_attn, wo_attn_t=wo_attn.T,
                wpool=wpool, wpool_t=jnp.transpose(wpool, (0, 2, 1)),
                qan=W['a_q_a_norm'][i][None, :], kvan=W['a_kv_a_norm'][i][None, :],
                qhn=pad(W['a_q_head_norm'][i]), khn=pad(W['a_k_head_norm'][i]),
                pscale=W['a_pool_scale'][i][None, :], g=W['mix_norm'][2 * i][None, :])


def kernel(*args):
    p = dict(zip(INPUTS, args))
    x0 = p['x'][0]
    target = p['loss_target'][0]
    T, D = x0.shape

    def wire(n, a):
        return _f32_bits_as(a, BF) if n == 'c_conv_w' else a.astype(BF)

    first = [(n, wire(n, p[n][:_first_layers(n)])) for n in SHARDED if _first_layers(n) > 0]
    late = [(n, wire(n, p[n][_first_layers(n):])) for n in SHARDED if p[n].shape[0] > _first_layers(n)]
    W = {n: [] for n in SHARDED}

    def receive(gathered, group):
        shapes = [[a.shape for _, a in group]]
        for (n, _), got in zip(group, unpack_flat(gathered, shapes, lead=(4,))[0]):
            whole = _join_shards(n, _f32_from_bits(got) if n == 'c_conv_w' else got)
            W[n] += [whole[i] for i in range(whole.shape[0])]

    receive(allgather_shards(pack_flat([[a for _, a in first]])), first)
    late_shards = pack_flat([[a for _, a in late]])
    for n in REPLICATED:
        W[n] = p[n]
    W['a_w_pool'] = p['a_w_pool'].astype(BF)
    cos, sin = _rope_tables(T)

    def ffn_weights(pre, l):
        wg, wu, wd = W[pre + '_w_gate'][l], W[pre + '_w_up'][l], W[pre + '_w_down'][l]
        return dict(g=W[pre + '_norm'][l][None, :], wg=wg, wu=wu, wd=wd, wg_t=wg.T, wu_t=wu.T, wd_t=wd.T)

    saved = []
    x = x0
    for l in range(DEPTH):
        s = dict(x0=x)
        f1 = ffn_weights('ffn1', l)
        x, s['g1'], s['u1'] = ffn_fwd(x, f1['g'], f1['wg'], f1['wu'], f1['wd'])
        s['x1'] = x
        if l % 2 == 0:
            e = _even_weights(W, l // 2)
            s['q'], s['k'], s['v'], s['po'] = mixa_pre_fwd(
                x, e['g'], e['win'], e['qan'], e['wq'], e['kvan'], e['wkn'], e['wv'], e['qhn'], e['khn'],
                e['wpool'], e['pscale'], cos, sin)
            v_t = _blocks_transposed(s['v'], _tile(T, TQ_ATTN, 128))
            if l == 0:
                o_t, s['lse'], gathered_late = attn_fwd(s['q'], s['k'], v_t, gather=late_shards)
                receive(gathered_late, late)
            else:
                o_t, s['lse'] = attn_fwd(s['q'], s['k'], v_t)
            s['o'] = _blocks_untransposed(o_t)
            x = mm_multi([(s['po'], e['wo_pool']), (s['o'], e['wo_attn'])], res=x)
        else:
            i = l // 2
            x, s['z'] = mixc_fwd(x, W['mix_norm'][l][None, :], W['c_w_in'][i], W['c_conv_w'][i].astype(F32),
                                 W['c_w_out'][i])
        s['x2'] = x
        f2 = ffn_weights('ffn2', l)
        x, s['g2'], s['u2'] = ffn_fwd(x, f2['g'], f2['wg'], f2['wu'], f2['wd'])
        saved.append(s)

    loss_sum, dy = loss_head(x, target)
    loss = lax.psum(0.5 * loss_sum[0, 0], AXES)

    G = {n: [None] * p[n].shape[0] for n in WEIGHTS}

    def ffn_back(pre, l, x_in, gg, uu, dy):
        f = ffn_weights(pre, l)
        dx, n, dyh, h, dgate, dup, dgn = ffn_bwd(x_in, dy, f['g'], gg, uu, f['wd_t'], f['wg_t'], f['wu_t'])
        G[pre + '_norm'][l] = dgn[0]
        G[pre + '_w_gate'][l] = mm_tn(n, dgate)
        G[pre + '_w_up'][l] = mm_tn(n, dup)
        G[pre + '_w_down'][l] = mm_tn(dyh, h).T
        return dx

    def partial_shards(late):
        arrays = [_split_shards(n, jnp.stack(G[n][_last_done(n):] if late else G[n][:_last_done(n)]))
                  for n in SHARDED if (p[n].shape[0] > _last_done(n) if late else _last_done(n) > 0)]
        return pack_flat([arrays], lead=(4,)).astype(BF)

    t_attn = _tile(T, TQ_ATTN, 128)
    for l in reversed(range(DEPTH)):
        s = saved[l]
        dy = ffn_back('ffn2', l, s['x2'], s['g2'], s['u2'], dy)
        i = l // 2
        if l % 2 == 0:
            e = _even_weights(W, i)
            G['a_w_out'][i] = jnp.concatenate(
                [mm_tn(s['po'], dy), _unpad_heads(mm_tn(s['o'], dy).T, V_DIM).T], axis=0)
            dpo = mm_multi([(dy, e['wo_pool_t'])])
            do = mm_multi([(dy, e['wo_attn_t'])], out_dtype=BF)
            as_rows = lambda a: a.reshape(HEADS, T // t_attn, t_attn)
            attn_args = (s['q'], s['k'], s['k'].T, s['v'], do, as_rows(s['lse']), as_rows(attn_delta(s['o'], do)))
            mix_args = (e['g'], e['win'], e['win_t'], e['qan'], e['wq'], e['wq_t'], e['kvan'], e['wkn'], e['wkn_t'],
                        e['wv_t'], e['qhn'], e['khn'], e['wpool'], e['wpool_t'], e['pscale'], cos, sin)
            if l == 0:
                dk, dv, dq_t, land_late = attn_bwd(*attn_args, exchange=partial_shards(late=True))
                (dy, hn, dz, nq, dqraw, nkv, dkraw, pooled, dps, dg, dqan, dkvan, dqhn, dkhn, dpscale,
                 g_late) = mixa_pre_bwd(s['x1'], dy, _blocks_untransposed(dq_t), dk, dv, dpo, *mix_args,
                                        share=sum_slots(land_late, BF))
            else:
                dk, dv, dq_t = attn_bwd(*attn_args)
                (dy, hn, dz, nq, dqraw, nkv, dkraw, pooled, dps, dg, dqan, dkvan, dqhn, dkhn, dpscale) = mixa_pre_bwd(
                    s['x1'], dy, _blocks_untransposed(dq_t), dk, dv, dpo, *mix_args)
            c3 = POOL_DIM + Q_RANK + KV_RANK
            dwin = mm_tn(hn, dz)
            G['a_w_in'][i] = jnp.concatenate([dwin[:, :c3], dwin[:, c3 + NOPE_DIM:c3 + QK_DIM]], axis=1)
            G['a_w_q_up'][i] = _unpad_heads(mm_tn(nq, dqraw), QK_DIM)
            dwkn = _unpad_heads(mm_tn(nkv, dkraw), NOPE_DIM).reshape(KV_RANK, HEADS, NOPE_DIM)
            dwv = _unpad_heads(mm_tn(nkv, dv), V_DIM).reshape(KV_RANK, HEADS, V_DIM)
            G['a_w_kv_up'][i] = jnp.concatenate([dwkn, dwv], axis=2).reshape(KV_RANK, HEADS * (NOPE_DIM + V_DIM))
            dwp = mm_tn(pooled, dps)
            G['a_w_pool'][i] = jnp.stack([dwp[g * POOL_GROUP:(g + 1) * POOL_GROUP, g * POOL_GROUP:(g + 1) * POOL_GROUP]
                                          for g in range(len(POOL_WINDOWS))])
            G['mix_norm'][l] = dg[0]
            G['a_q_a_norm'][i] = dqan[0]
            G['a_kv_a_norm'][i] = dkvan[0]
            G['a_q_head_norm'][i] = dqhn[0, :QK_DIM]
            G['a_k_head_norm'][i] = dkhn[0, :QK_DIM]
            G['a_pool_scale'][i] = dpscale[0]
        else:
            w_in, w_out = W['c_w_in'][i], W['c_w_out'][i]
            dy_in = dy
            dy, hn, dz, gated, dcw, dg = mixc_bwd(s['x1'], dy, s['z'], W['mix_norm'][l][None, :], w_in.T,
                                                  W['c_conv_w'][i].astype(F32), w_out.T)
            G['c_w_in'][i] = mm_tn(hn, dz)
            G['c_w_out'][i] = mm_tn(gated, dy_in)
            G['c_conv_w'][i] = dcw
            G['mix_norm'][l] = dg[0]
        dy = ffn_back('ffn1', l, s['x0'], s['g1'], s['u1'], dy)
    grad_x = dy[None]

    partial_small = pack_flat([[jnp.stack(G[n]) for n in REPLICATED]])
    land_first, sland = exchange_partials(partial_shards(late=False), partial_small)
    g_big = jnp.concatenate([share_with_sibling([sum_slots(land_first, BF)]), g_late], axis=0)
    g_small = sum_slots(sland, F32)

    def layer_groups(pre):
        first = [p[pre + n][:_last_done(n)] for n in SHARDED if _last_done(n) > 0]
        late = [p[pre + n][_last_done(n):] for n in SHARDED if p[n].shape[0] > _last_done(n)]
        return [first, late]

    outs = {}
    g32, delta, m2, v2 = adamw(pack_flat(layer_groups('')), g_big, pack_flat(layer_groups('m_')),
                               pack_flat(layer_groups('v_')))
    group_shapes = [[a.shape for a in group] for group in layer_groups('')]
    for kind, arr in (('grad_', g32), ('delta_', delta), ('new_m_', m2), ('new_v_', v2)):
        first, late = (iter(group) for group in unpack_flat(arr, group_shapes))
        for n in SHARDED:
            pieces = ([next(first)] if _last_done(n) > 0 else []) + (
                [next(late)] if p[n].shape[0] > _last_done(n) else [])
            outs[kind + n] = pieces[0] if len(pieces) == 1 else jnp.concatenate(pieces, axis=0)
    small_groups = [[p[n].shape for n in REPLICATED]]
    flat = lambda pre: pack_flat([[p[pre + n] for n in REPLICATED]])
    g32, delta, m2, v2 = adamw(flat(''), g_small, flat('m_'), flat('v_'))
    for kind, arr in (('grad_', g32), ('delta_', delta), ('new_m_', m2), ('new_v_', v2)):
        for n, a in zip(REPLICATED, unpack_flat(arr, small_groups)[0]):
            outs[kind + n] = a
    return (loss, grad_x, *[outs[k + n] for k in ('grad_', 'delta_', 'new_m_', 'new_v_') for n in WEIGHTS])
```
